```python
import math
import jax
import jax.numpy as jnp
from jax import lax
import numpy as np

D_MODEL = 2048
BATCH = 2
SEQ = 8192
DEPTH = 1
DEC_BATCH = 32
DEC_SEQ = 16
PAST_LEN = 2048

CHUNK = 64
Q_BLOCK = 128
H_A = 8
D_A = 64
D_VA = 2 * D_A
ROT_DIM = D_A // 4
ROPE_THETA = 500000.0
H_B = 8
D_K = 128
D_V = 128
CONV_W = 4
CONV_CH = 2 * H_B * D_K + H_B * D_V
N_GROUPS = 4
EXPERTS_PER_GROUP = 8
N_EXPERTS = N_GROUPS * EXPERTS_PER_GROUP
TOP_K_INNER = 2
D_EXPERT = 512
MOE_BLOCK = 128
EPS = 1e-6
SPLIT_SIZES = (H_A * 2 * D_A, H_A * 2 * D_A, H_A * D_VA, H_B * D_K, H_B * D_K, H_B * D_V, H_B * D_V, H_B, H_B, D_MODEL, D_MODEL)
N_PROJ = sum(SPLIT_SIZES)

kernel_name = 'hybrid_stream_diffattn_gdn_hmoe_step'

F32 = jnp.float32


def rms_norm(x, w):
    xf = x.astype(F32)
    y = xf * lax.rsqrt(jnp.mean(xf * xf, axis=-1, keepdims=True) + EPS) * w.astype(F32)
    return y.astype(x.dtype)


def l2_normalize(x):
    return x * lax.rsqrt(jnp.sum(x * x, axis=-1, keepdims=True) + EPS)


def rotary(x, pos):
    half = ROT_DIM // 2
    inv = ROPE_THETA ** (-jnp.arange(0, ROT_DIM, 2, dtype=F32) / ROT_DIM)
    ang = pos.astype(F32)[:, None] * inv[None, :]
    cos = jnp.cos(ang)[None, :, None, None, :]
    sin = jnp.sin(ang)[None, :, None, None, :]
    x1 = x[..., :half]
    x2 = x[..., half:ROT_DIM]
    return jnp.concatenate([x1 * cos - x2 * sin, x2 * cos + x1 * sin, x[..., ROT_DIM:]], axis=-1)


def diff_lambda(lq1, lk1, lq2, lk2, lam_init):
    return (jnp.exp(jnp.sum(lq1.astype(F32) * lk1.astype(F32)))
            - jnp.exp(jnp.sum(lq2.astype(F32) * lk2.astype(F32))) + lam_init)


def diff_attn_core(q, k, v, lam, mask):
    s = jnp.einsum('bqhmd,bkhmd->bmhqk', q, k) * (D_A ** -0.5)
    if mask is not None:
        s = jnp.where(mask, s, -jnp.inf)
    p = jax.nn.softmax(s, axis=-1)
    a = p[:, 0] - lam * p[:, 1]
    return jnp.einsum('bhqk,bkhe->bqhe', a, v)


def diff_attn_prompt(q, k, v, lam):
    bn, L = q.shape[0], q.shape[1]
    nblk = L // Q_BLOCK
    key_chunk = jnp.arange(L) // CHUNK
    qb = jnp.moveaxis(q.reshape((bn, nblk, Q_BLOCK) + q.shape[2:]), 1, 0)

    def one_block(args):
        qblk, b = args
        q_chunk = (b * Q_BLOCK + jnp.arange(Q_BLOCK)) // CHUNK
        mask = key_chunk[None, :] <= q_chunk[:, None]
        return diff_attn_core(qblk, k, v, lam, mask)

    o = lax.map(one_block, (qb, jnp.arange(nblk)))
    return jnp.moveaxis(o, 0, 1).reshape(bn, L, H_A, D_VA)


def gated_delta_chunked(q, k, v, g, beta, s0, chunk):
    bn, L, H = q.shape[0], q.shape[1], q.shape[2]
    dv = v.shape[-1]
    n = L // chunk

    def blocks(t):
        t = t.reshape((bn, n, chunk) + t.shape[2:])
        return jnp.moveaxis(jnp.moveaxis(t, 1, 0), 3, 2)

    qc, kc, vc, gc, bc = blocks(q), blocks(k), blocks(v), blocks(g), blocks(beta)
    gc = jnp.cumsum(gc, axis=-1)
    tri = jnp.tril(jnp.ones((chunk, chunk), dtype=bool))
    strict = jnp.tril(jnp.ones((chunk, chunk), dtype=bool), -1)
    gdiff = gc[..., :, None] - gc[..., None, :]
    decay = jnp.where(tri, jnp.exp(jnp.where(tri, gdiff, 0.0)), 0.0)
    kb = kc * bc[..., None]
    vb = vc * bc[..., None]
    m = jnp.where(strict, jnp.einsum('nbhid,nbhjd->nbhij', kb, kc) * decay, 0.0)
    a = m + jnp.eye(chunk, dtype=F32)
    u = lax.linalg.triangular_solve(a, vb, left_side=True, lower=True, unit_diagonal=True)
    w = lax.linalg.triangular_solve(a, kb * jnp.exp(gc)[..., None], left_side=True, lower=True, unit_diagonal=True)
    qk = jnp.where(tri, jnp.einsum('nbhid,nbhjd->nbhij', qc, kc) * decay, 0.0)
    qg = qc * jnp.exp(gc)[..., None]
    kg = kc * jnp.exp(gc[..., -1:] - gc)[..., None]
    glast = jnp.exp(gc[..., -1])

    def step(s, inp):
        u_i, w_i, qk_i, qg_i, kg_i, gl_i = inp
        v_new = u_i - jnp.einsum('bhcd,bhde->bhce', w_i, s)
        o_i = jnp.einsum('bhcd,bhde->bhce', qg_i, s) + jnp.einsum('bhij,bhje->bhie', qk_i, v_new)
        s = s * gl_i[..., None, None] + jnp.einsum('bhcd,bhce->bhde', kg_i, v_new)
        return s, o_i

    s_fin, o = lax.scan(step, s0, (u, w, qk, qg, kg, glast))
    o = jnp.moveaxis(jnp.moveaxis(o, 2, 3), 0, 1).reshape(bn, L, H, dv)
    return o, s_fin


def gated_delta_branch(qkv, z, b_raw, a_raw, conv_prev, s0, conv_w, a_log, dt_bias, norm_w, chunk):
    bn, L = qkv.shape[0], qkv.shape[1]
    xs = jnp.concatenate([conv_prev.astype(qkv.dtype), qkv], axis=1)
    conv = lax.conv_general_dilated(xs, conv_w[:, None, :].astype(qkv.dtype), window_strides=(1,),
                                    padding='VALID', dimension_numbers=('NWC', 'WIO', 'NWC'),
                                    feature_group_count=CONV_CH)
    new_conv = xs[:, xs.shape[1] - (CONV_W - 1):]
    act = jax.nn.silu(conv.astype(F32))
    q, k, v = jnp.split(act, [H_B * D_K, 2 * H_B * D_K], axis=-1)
    q = l2_normalize(q.reshape(bn, L, H_B, D_K)) * (D_K ** -0.5)
    k = l2_normalize(k.reshape(bn, L, H_B, D_K))
    v = v.reshape(bn, L, H_B, D_V)
    beta = jax.nn.sigmoid(b_raw.astype(F32))
    g = -jnp.exp(a_log.astype(F32)) * jax.nn.softplus(a_raw.astype(F32) + dt_bias.astype(F32))
    o, s_new = gated_delta_chunked(q, k, v, g, beta, s0.astype(F32), chunk)
    o = (o * lax.rsqrt(jnp.mean(o * o, axis=-1, keepdims=True) + EPS) * norm_w.astype(F32)
         * jax.nn.silu(z.reshape(bn, L, H_B, D_V).astype(F32)))
    return o.reshape(bn, L, H_B * D_V).astype(qkv.dtype), s_new, new_conv


def expert_apply(t, eid, weight, w_gate, w_up, w_down):
    T = t.shape[0]
    A = T * TOP_K_INNER
    n_blocks = -(-A // MOE_BLOCK) + N_EXPERTS
    P = n_blocks * MOE_BLOCK
    fe = eid.reshape(A)
    ft = jnp.arange(A, dtype=jnp.int32) // TOP_K_INNER
    fw = weight.reshape(A)
    order = jnp.argsort(fe)
    se, st, sw = fe[order], ft[order], fw[order]
    counts = jnp.bincount(fe, length=N_EXPERTS)
    pc = (counts + MOE_BLOCK - 1) // MOE_BLOCK * MOE_BLOCK
    pend = jnp.cumsum(pc)
    pstart = pend - pc
    cstart = jnp.cumsum(counts) - counts
    dest = pstart[se] + (jnp.arange(A) - cstart[se])
    buf_tok = jnp.zeros((P,), jnp.int32).at[dest].set(st)
    blk_exp = jnp.minimum(jnp.searchsorted(pend, jnp.arange(n_blocks) * MOE_BLOCK, side='right'), N_EXPERTS - 1)
    xb = t[buf_tok].reshape(n_blocks, MOE_BLOCK, t.shape[1])

    def expert_block(args):
        xblk, e = args
        return (jax.nn.silu(xblk @ w_gate[e]) * (xblk @ w_up[e])) @ w_down[e]

    yb = lax.map(expert_block, (xb, blk_exp)).reshape(P, t.shape[1])
    return jnp.zeros_like(t).at[st].add((yb[dest] * sw[:, None]).astype(t.dtype))


def hier_moe(h, w_gr, b_gr, w_er, b_er, w_eg, w_eu, w_ed):
    bn, L, D = h.shape
    t = h.reshape(bn * L, D)
    T = t.shape[0]
    glog = (t @ w_gr).astype(F32) + b_gr.astype(F32)
    gidx = jnp.argmax(glog, axis=-1)
    gw = jnp.take_along_axis(jax.nn.softmax(glog, axis=-1), gidx[:, None], axis=1)[:, 0]
    elog = ((t @ w_er).astype(F32) + b_er.astype(F32)).reshape(T, N_GROUPS, EXPERTS_PER_GROUP)
    elog_g = jnp.take_along_axis(elog, gidx[:, None, None], axis=1)[:, 0]
    tv, ti = lax.top_k(elog_g, TOP_K_INNER)
    weight = gw[:, None] * jax.nn.softmax(tv, axis=-1)
    eid = gidx[:, None].astype(jnp.int32) * EXPERTS_PER_GROUP + ti.astype(jnp.int32)
    return expert_apply(t, eid, weight, w_eg, w_eu, w_ed).reshape(bn, L, D)


def trunk_layer(x, pos, past_k, past_v, s0, conv_prev, gdn_chunk, layer, params):
    (norm_mix, w_in, lq1, lk1, lq2, lk2, subln_w, conv_w, a_log, dt_bias, gdn_norm_w,
     w_proj_a, w_proj_b, w_out, norm_ffn, w_gr, b_gr, w_er, b_er, w_eg, w_eu, w_ed) = params
    bn, L = x.shape[0], x.shape[1]
    dt = x.dtype
    h = rms_norm(x, norm_mix)
    proj = h @ w_in
    points = np.cumsum(SPLIT_SIZES)[:-1].tolist()
    aq, ak, av, bq, bk, bv, bz, bb, ba, ga, gb = jnp.split(proj, points, axis=-1)
    lam_init = 0.8 - 0.6 * math.exp(-0.3 * layer)
    lam = diff_lambda(lq1, lk1, lq2, lk2, lam_init)
    q = rotary(aq.astype(F32).reshape(bn, L, H_A, 2, D_A), pos)
    k = rotary(ak.astype(F32).reshape(bn, L, H_A, 2, D_A), pos)
    v = av.astype(F32).reshape(bn, L, H_A, D_VA)
    if past_k is None:
        o = diff_attn_prompt(q, k, v, lam)
    else:
        k_all = jnp.concatenate([past_k.astype(F32).reshape(bn, past_k.shape[1], H_A, 2, D_A), k], axis=1)
        v_all = jnp.concatenate([past_v.astype(F32), v], axis=1)
        o = diff_attn_core(q, k_all, v_all, lam, None)
    o_a = (rms_norm(o, subln_w) * (1.0 - lam_init)).reshape(bn, L, H_A * D_VA).astype(dt)
    o_b, s_new, conv_new = gated_delta_branch(jnp.concatenate([bq, bk, bv], axis=-1), bz, bb, ba,
                                              conv_prev, s0, conv_w, a_log, dt_bias, gdn_norm_w, gdn_chunk)
    merged = jax.nn.sigmoid(ga) * (o_a @ w_proj_a) + jax.nn.sigmoid(gb) * (o_b @ w_proj_b)
    x = x + merged @ w_out
    x = x + hier_moe(rms_norm(x, norm_ffn), w_gr, b_gr, w_er, b_er, w_eg, w_eu, w_ed)
    new_k = k.reshape(bn, L, H_A, 2 * D_A).astype(dt)
    new_v = v.astype(dt)
    return x, new_k, new_v, s_new.astype(s0.dtype), conv_new


def setup_inputs(seed: int = 0) -> dict:
    key = jax.random.key(seed)
    ks = jax.random.split(key, 32)

    def nrm(k, shape, scale):
        return jax.random.normal(k, shape, F32) * scale

    dt_init = jnp.exp(jax.random.uniform(ks[14], (DEPTH, H_B), F32, minval=math.log(1e-3), maxval=math.log(1e-1)))
    return {
        'x_prompt': nrm(ks[0], (BATCH, SEQ, D_MODEL), 1.0),
        'x_sample': nrm(ks[1], (DEC_BATCH, DEC_SEQ, D_MODEL), 1.0),
        'cache_k': nrm(ks[2], (DEPTH, DEC_BATCH, PAST_LEN, H_A, 2 * D_A), 1.0),
        'cache_v': nrm(ks[3], (DEPTH, DEC_BATCH, PAST_LEN, H_A, D_VA), 1.0),
        'state_delta': nrm(ks[4], (DEPTH, DEC_BATCH, H_B, D_K, D_V), 0.1),
        'state_conv': nrm(ks[5], (DEPTH, DEC_BATCH, CONV_W - 1, CONV_CH), 1.0),
        'norm_mix': 1.0 + nrm(ks[6], (DEPTH, D_MODEL), 0.05),
        'w_in': nrm(ks[7], (DEPTH, D_MODEL, N_PROJ), D_MODEL ** -0.5),
        'lambda_q1': nrm(ks[8], (DEPTH, D_A), 0.1),
        'lambda_k1': nrm(ks[9], (DEPTH, D_A), 0.1),
        'lambda_q2': nrm(ks[10], (DEPTH, D_A), 0.1),
        'lambda_k2': nrm(ks[11], (DEPTH, D_A), 0.1),
        'subln_w': 1.0 + nrm(ks[12], (DEPTH, D_VA), 0.05),
        'conv_w': nrm(ks[13], (DEPTH, CONV_W, CONV_CH), CONV_W ** -0.5),
        'a_log': jnp.log(jax.random.uniform(ks[15], (DEPTH, H_B), F32, minval=1.0, maxval=16.0)),
        'dt_bias': dt_init + jnp.log(-jnp.expm1(-dt_init)),
        'gdn_norm_w': 1.0 + nrm(ks[16], (DEPTH, D_V), 0.05),
        'w_proj_a': nrm(ks[17], (DEPTH, H_A * D_VA, D_MODEL), (H_A * D_VA) ** -0.5),
        'w_proj_b': nrm(ks[18], (DEPTH, H_B * D_V, D_MODEL), (H_B * D_V) ** -0.5),
        'w_out': nrm(ks[19], (DEPTH, D_MODEL, D_MODEL), D_MODEL ** -0.5),
        'norm_ffn': 1.0 + nrm(ks[20], (DEPTH, D_MODEL), 0.05),
        'w_group_router': nrm(ks[21], (DEPTH, D_MODEL, N_GROUPS), D_MODEL ** -0.5),
        'b_group_router': nrm(ks[22], (DEPTH, N_GROUPS), 0.01),
        'w_expert_router': nrm(ks[23], (DEPTH, D_MODEL, N_EXPERTS), D_MODEL ** -0.5),
        'b_expert_router': nrm(ks[24], (DEPTH, N_EXPERTS), 0.01),
        'w_exp_gate': nrm(ks[25], (DEPTH, N_EXPERTS, D_MODEL, D_EXPERT), D_MODEL ** -0.5),
        'w_exp_up': nrm(ks[26], (DEPTH, N_EXPERTS, D_MODEL, D_EXPERT), D_MODEL ** -0.5),
        'w_exp_down': nrm(ks[27], (DEPTH, N_EXPERTS, D_EXPERT, D_MODEL), D_EXPERT ** -0.5),
        'norm_final': 1.0 + nrm(ks[28], (D_MODEL,), 0.05),
    }


def reference(x_prompt, x_sample, cache_k, cache_v, state_delta, state_conv, norm_mix, w_in,
              lambda_q1, lambda_k1, lambda_q2, lambda_k2, subln_w, conv_w, a_log, dt_bias, gdn_norm_w,
              w_proj_a, w_proj_b, w_out, norm_ffn, w_group_router, b_group_router, w_expert_router,
              b_expert_router, w_exp_gate, w_exp_up, w_exp_down, norm_final):
    pos_p = jnp.arange(x_prompt.shape[1])
    pos_s = cache_k.shape[2] + jnp.arange(x_sample.shape[1])
    s0_p = jnp.zeros((x_prompt.shape[0], H_B, D_K, D_V), state_delta.dtype)
    c0_p = jnp.zeros((x_prompt.shape[0], CONV_W - 1, CONV_CH), state_conv.dtype)
    hp, hs = x_prompt, x_sample
    kp, vp, sp, cp, ksm, vsm, ssm, csm = [], [], [], [], [], [], [], []
    for l in range(DEPTH):
        params = (norm_mix[l], w_in[l], lambda_q1[l], lambda_k1[l], lambda_q2[l], lambda_k2[l], subln_w[l],
                  conv_w[l], a_log[l], dt_bias[l], gdn_norm_w[l], w_proj_a[l], w_proj_b[l], w_out[l],
                  norm_ffn[l], w_group_router[l], b_group_router[l], w_expert_router[l], b_expert_router[l],
                  w_exp_gate[l], w_exp_up[l], w_exp_down[l])
        hp, k1, v1, s1, c1 = trunk_layer(hp, pos_p, None, None, s0_p, c0_p, CHUNK, l, params)
        hs, k2, v2, s2, c2 = trunk_layer(hs, pos_s, cache_k[l], cache_v[l], state_delta[l], state_conv[l],
                                         x_sample.shape[1], l, params)
        kp.append(k1); vp.append(v1); sp.append(s1); cp.append(c1)
        ksm.append(k2); vsm.append(v2); ssm.append(s2); csm.append(c2)
    y_prompt = rms_norm(hp, norm_final)
    y_sample = rms_norm(hs, norm_final)
    return (y_prompt, y_sample, jnp.stack(kp), jnp.stack(vp), jnp.stack(sp), jnp.stack(cp),
            jnp.stack(ksm), jnp.stack(vsm), jnp.stack(ssm), jnp.stack(csm))
```

```python
import functools
import math

import jax
import jax.numpy as jnp
from jax import lax
from jax.experimental import pallas as pl
from jax.experimental.pallas import tpu as pltpu

F32 = jnp.float32
BF16 = jnp.bfloat16
I32 = jnp.int32

CHUNK = 64
H_A = 8
D_A = 64
D_VA = 2 * D_A
ROT_DIM = D_A // 4
ROPE_THETA = 500000.0
H_B = 8
D_K = 128
D_V = 128
CONV_W = 4
N_GROUPS = 4
EXPERTS_PER_GROUP = 8
TOP_K_INNER = 2
EPS = 1e-6

LANES = 128
SUBLANES = 8
V7X_VMEM_BYTES = 64 * 1024 * 1024
VMEM_LIMIT_BYTES = V7X_VMEM_BYTES - 16 * 1024 * 1024

HI = lax.Precision.HIGHEST
NT_DIMS = (((1,), (1,)), ((), ()))
TN_DIMS = (((0,), (0,)), ((), ()))


def _pick(n, cands):
    for c in cands:
        if n % c == 0:
            return c
    raise ValueError(f"no block size in {cands} divides {n}")


def _cparams(*sem):
    return pltpu.CompilerParams(dimension_semantics=sem, vmem_limit_bytes=VMEM_LIMIT_BYTES)


def _sigmoid(x):
    return 1.0 / (1.0 + jnp.exp(-x))


def _silu(x):
    return x * _sigmoid(x)


def _softplus(x):
    return jnp.maximum(x, 0.0) + jnp.log1p(jnp.exp(-jnp.abs(x)))


def _bdot(a, b):
    return jnp.dot(a.astype(BF16), b.astype(BF16), preferred_element_type=F32)


def _hdot(a, b):
    return jnp.dot(a, b, precision=HI, preferred_element_type=F32)


def _inproj_body(x_ref, nw_ref, cos_ref, sa_ref, sb_ref, w_ref, ws_ref,
                 o_ref, os_ref, k_ref, v_ref, h_scr):
    j = pl.program_id(1)

    @pl.when(j == 0)
    def _():
        x = x_ref[...]
        ms = jnp.mean(x * x, axis=-1, keepdims=True)
        h = (x * lax.rsqrt(ms + EPS) * nw_ref[...]).astype(BF16)
        h_scr[...] = h
        os_ref[...] = jnp.dot(h, ws_ref[...], preferred_element_type=F32)

    acc = jnp.dot(h_scr[...], w_ref[...], preferred_element_type=F32)

    @pl.when(j < 2)
    def _():
        cos = cos_ref[...]
        sa = sa_ref[...]
        sb = sb_ref[...]
        for c in range(acc.shape[1] // LANES):
            blk = acc[:, c * LANES:(c + 1) * LANES]
            rot = (blk * cos + pltpu.roll(blk, ROT_DIM // 2, 1) * sa
                   + pltpu.roll(blk, LANES - ROT_DIM // 2, 1) * sb)
            o_ref[:, c * LANES:(c + 1) * LANES] = rot

        @pl.when(j == 1)
        def _():
            k_ref[...] = o_ref[...]

    @pl.when(j >= 2)
    def _():
        o_ref[...] = acc

        @pl.when(j == 2)
        def _():
            v_ref[...] = acc


def _inproj(x_all, norm_w, cos_t, sa_t, sb_t, w_main, w_small):
    T, D = x_all.shape
    n_main = w_main.shape[1]
    bn = H_A * 2 * D_A
    assert n_main % bn == 0 and H_A * D_VA == bn
    bm = _pick(T, (512, 256, 128, 64, 32, 16, 8))
    grid = (T // bm, n_main // bn)
    return pl.pallas_call(
        _inproj_body,
        grid=grid,
        in_specs=[
            pl.BlockSpec((bm, D), lambda i, j: (i, 0)),
            pl.BlockSpec((1, D), lambda i, j: (0, 0)),
            pl.BlockSpec((bm, LANES), lambda i, j: (i, 0)),
            pl.BlockSpec((bm, LANES), lambda i, j: (i, 0)),
            pl.BlockSpec((bm, LANES), lambda i, j: (i, 0)),
            pl.BlockSpec((D, bn), lambda i, j: (0, j)),
            pl.BlockSpec((D, LANES), lambda i, j: (0, 0)),
        ],
        out_specs=[
            pl.BlockSpec((bm, bn), lambda i, j: (i, j)),
            pl.BlockSpec((bm, LANES), lambda i, j: (i, 0)),
            pl.BlockSpec((bm, bn), lambda i, j: (i, 0)),
            pl.BlockSpec((bm, bn), lambda i, j: (i, 0)),
        ],
        out_shape=[
            jax.ShapeDtypeStruct((T, n_main), F32),
            jax.ShapeDtypeStruct((T, LANES), F32),
            jax.ShapeDtypeStruct((T, bn), F32),
            jax.ShapeDtypeStruct((T, bn), F32),
        ],
        scratch_shapes=[pltpu.VMEM((bm, D), BF16)],
        compiler_params=_cparams("parallel", "arbitrary"),
        name="inproj",
    )(x_all, norm_w, cos_t, sa_t, sb_t, w_main, w_small)


def _diff_lambda(lamp, lam_init):
    s1 = jnp.sum(lamp[0:1] * lamp[1:2], axis=1, keepdims=True)
    s2 = jnp.sum(lamp[2:3] * lamp[3:4], axis=1, keepdims=True)
    return jnp.exp(s1) - jnp.exp(s2) + lam_init


def _subln(o, w, lam_init):
    ms = jnp.mean(o * o, axis=-1, keepdims=True)
    return o * lax.rsqrt(ms + EPS) * w * (1.0 - lam_init)


def _attn_prompt_body(lam_init, bq, bk, qi_ref, kj_ref, q_ref, k_ref, v_ref, lamp_ref, sw_ref,
                      oin_ref, o_ref, m_scr, l_scr, acc_scr):
    del oin_ref
    p = pl.program_id(2)
    qi = qi_ref[p]
    kj = kj_ref[p]

    @pl.when(kj == 0)
    def _():
        m_scr[...] = jnp.full(m_scr.shape, -jnp.inf, F32)
        l_scr[...] = jnp.zeros(l_scr.shape, F32)
        acc_scr[...] = jnp.zeros(acc_scr.shape, F32)

    q = (q_ref[...] * (D_A ** -0.5)).astype(BF16)
    k = k_ref[...].astype(BF16)
    v = v_ref[...].astype(BF16)
    row_chunk = (qi * bq + lax.broadcasted_iota(I32, (bq, 1), 0)) // CHUNK
    col_chunk = (kj * bk + lax.broadcasted_iota(I32, (1, bk), 1)) // CHUNK
    mask = col_chunk <= row_chunk
    for m in range(2):
        s = lax.dot_general(q[:, m * D_A:(m + 1) * D_A], k[:, m * D_A:(m + 1) * D_A], NT_DIMS,
                            preferred_element_type=F32)
        s = jnp.where(mask, s, -jnp.inf)
        m_prev = m_scr[m]
        m_new = jnp.maximum(m_prev, jnp.max(s, axis=1, keepdims=True))
        alpha = jnp.exp(m_prev - m_new)
        pexp = jnp.exp(s - m_new)
        l_scr[m] = alpha * l_scr[m] + jnp.sum(pexp, axis=1, keepdims=True)
        acc_scr[m] = alpha * acc_scr[m] + jnp.dot(pexp.astype(BF16), v, preferred_element_type=F32)
        m_scr[m] = m_new

    @pl.when(kj == qi)
    def _():
        lam = _diff_lambda(lamp_ref[...], lam_init)
        o = acc_scr[0] / l_scr[0] - lam * (acc_scr[1] / l_scr[1])
        o_ref[...] = _subln(o, sw_ref[...], lam_init).astype(o_ref.dtype)


def _attn_prompt(proj, o_a, lamp, subln_w, lam_init, bp, lp):
    bq = _pick(lp, (512, 256, 128, 64))
    bk = bq
    nq = lp // bq
    pairs = [(i, j) for i in range(nq) for j in range(i + 1)]
    qi = jnp.asarray([a for a, _ in pairs], I32)
    kj = jnp.asarray([b for _, b in pairs], I32)
    kcol = H_A
    grid_spec = pltpu.PrefetchScalarGridSpec(
        num_scalar_prefetch=2,
        grid=(bp, H_A, len(pairs)),
        in_specs=[
            pl.BlockSpec((bq, LANES), lambda b, h, p, qi, kj: (b * nq + qi[p], h)),
            pl.BlockSpec((bk, LANES), lambda b, h, p, qi, kj: (b * nq + kj[p], kcol + h)),
            pl.BlockSpec((bk, LANES), lambda b, h, p, qi, kj: (b * nq + kj[p], 2 * kcol + h)),
            pl.BlockSpec((4, D_A), lambda b, h, p, qi, kj: (0, 0)),
            pl.BlockSpec((1, D_VA), lambda b, h, p, qi, kj: (0, 0)),
            pl.BlockSpec(memory_space=pl.ANY),
        ],
        out_specs=pl.BlockSpec((bq, LANES), lambda b, h, p, qi, kj: (b * nq + qi[p], h)),
        scratch_shapes=[
            pltpu.VMEM((2, bq, 1), F32),
            pltpu.VMEM((2, bq, 1), F32),
            pltpu.VMEM((2, bq, D_VA), F32),
        ],
    )
    return pl.pallas_call(
        functools.partial(_attn_prompt_body, lam_init, bq, bk),
        grid_spec=grid_spec,
        out_shape=jax.ShapeDtypeStruct(o_a.shape, o_a.dtype),
        input_output_aliases={7: 0},
        compiler_params=_cparams("parallel", "parallel", "arbitrary"),
        name="attn_prompt",
    )(qi, kj, proj, proj, proj, lamp, subln_w, o_a)


def _attn_sample_body(lam_init, hg, q_ref, kn_ref, vn_ref, kc_ref, vc_ref, lamp_ref, sw_ref,
                      oin_ref, o_ref):
    del oin_ref
    lam = _diff_lambda(lamp_ref[...], lam_init)
    sw = sw_ref[...]
    for hh in range(hg):
        c0 = hh * D_VA
        q = (q_ref[:, c0:c0 + D_VA] * (D_A ** -0.5)).astype(BF16)
        kn = kn_ref[:, c0:c0 + D_VA].astype(BF16)
        vn = vn_ref[:, c0:c0 + D_VA].astype(BF16)
        kc = kc_ref[:, c0:c0 + D_VA].astype(BF16)
        vc = vc_ref[:, c0:c0 + D_VA].astype(BF16)
        outs = []
        for m in range(2):
            qm = q[:, m * D_A:(m + 1) * D_A]
            s_c = lax.dot_general(qm, kc[:, m * D_A:(m + 1) * D_A], NT_DIMS, preferred_element_type=F32)
            s_n = lax.dot_general(qm, kn[:, m * D_A:(m + 1) * D_A], NT_DIMS, preferred_element_type=F32)
            mx = jnp.maximum(jnp.max(s_c, axis=1, keepdims=True), jnp.max(s_n, axis=1, keepdims=True))
            p_c = jnp.exp(s_c - mx)
            p_n = jnp.exp(s_n - mx)
            den = jnp.sum(p_c, axis=1, keepdims=True) + jnp.sum(p_n, axis=1, keepdims=True)
            num = (jnp.dot(p_c.astype(BF16), vc, preferred_element_type=F32)
                   + jnp.dot(p_n.astype(BF16), vn, preferred_element_type=F32))
            outs.append(num / den)
        o = outs[0] - lam * outs[1]
        o_ref[:, c0:c0 + D_VA] = _subln(o, sw, lam_init).astype(o_ref.dtype)


def _attn_sample(proj, o_a, cache_k, cache_v, lamp, subln_w, lam_init, row0, bs, ls):
    past = cache_k.shape[1]
    hg = 4
    wcol = hg * D_VA
    ng = H_A // hg
    r0 = row0 // ls
    assert row0 % ls == 0 and ls % SUBLANES == 0
    return pl.pallas_call(
        functools.partial(_attn_sample_body, lam_init, hg),
        grid=(bs, ng),
        in_specs=[
            pl.BlockSpec((ls, wcol), lambda b, g: (r0 + b, g)),
            pl.BlockSpec((ls, wcol), lambda b, g: (r0 + b, ng + g)),
            pl.BlockSpec((ls, wcol), lambda b, g: (r0 + b, 2 * ng + g)),
            pl.BlockSpec((None, past, wcol), lambda b, g: (b, 0, g)),
            pl.BlockSpec((None, past, wcol), lambda b, g: (b, 0, g)),
            pl.BlockSpec((4, D_A), lambda b, g: (0, 0)),
            pl.BlockSpec((1, D_VA), lambda b, g: (0, 0)),
            pl.BlockSpec(memory_space=pl.ANY),
        ],
        out_specs=pl.BlockSpec((ls, wcol), lambda b, g: (r0 + b, g)),
        out_shape=jax.ShapeDtypeStruct(o_a.shape, o_a.dtype),
        input_output_aliases={7: 0},
        compiler_params=_cparams("parallel", "parallel"),
        name="attn_sample",
    )(proj, proj, proj, cache_k, cache_v, lamp, subln_w, o_a)


def _gdn_body(chunk, rows, pq_ref, pk_ref, pv_ref, pz_ref, sm_ref, smt_ref, cw_ref, alr_ref, dtr_ref,
              alc_ref, dtc_ref, nw_ref, s0_ref, c0_ref, oin_ref, o_ref, s_ref, cn_ref, ext_scr, act_scr):
    del oin_ref
    i = pl.program_id(1)
    nblk = pl.num_programs(1)
    hk = H_B * D_K
    hv = H_B * D_V
    n_chunks = rows // chunk
    n_steps = int(math.log2(chunk))
    assert 2 ** n_steps == chunk

    @pl.when(i == 0)
    def _():
        s_ref[...] = s0_ref[...]
        ext_scr[0:SUBLANES, :] = jnp.zeros((SUBLANES, ext_scr.shape[1]), F32)
        ext_scr[pl.ds(SUBLANES - (CONV_W - 1), CONV_W - 1), :] = c0_ref[...]

    ext_scr[pl.ds(SUBLANES, rows), 0:hk] = pq_ref[...]
    ext_scr[pl.ds(SUBLANES, rows), hk:2 * hk] = pk_ref[...]
    ext_scr[pl.ds(SUBLANES, rows), 2 * hk:2 * hk + hv] = pv_ref[...]

    @pl.when(i == nblk - 1)
    def _():
        cn_ref[...] = ext_scr[pl.ds(SUBLANES + rows - (CONV_W - 1), CONV_W - 1), :]

    conv = jnp.zeros((rows, ext_scr.shape[1]), F32)
    for w in range(CONV_W):
        conv = conv + ext_scr[pl.ds(SUBLANES - (CONV_W - 1) + w, rows), :] * cw_ref[w:w + 1, :]
    act_scr[...] = _silu(conv)
    ext_scr[0:SUBLANES, :] = ext_scr[pl.ds(rows, SUBLANES), :]

    sm = sm_ref[...]
    beta_all = _sigmoid(sm[:, 0:H_B])
    g_all = -jnp.exp(alr_ref[...]) * _softplus(sm[:, H_B:2 * H_B] + dtr_ref[...])
    smt = smt_ref[...]
    g_all_t = -jnp.exp(alc_ref[...]) * _softplus(smt[H_B:2 * H_B, :] + dtc_ref[...])

    ri = lax.broadcasted_iota(I32, (chunk, chunk), 0)
    ci = lax.broadcasted_iota(I32, (chunk, chunk), 1)
    tri = ri >= ci
    strict = ri > ci
    eye = (ri == ci).astype(F32)
    ltri = tri.astype(F32)
    utri = (ri <= ci).astype(F32)
    nw = nw_ref[...]

    for c in range(n_chunks):
        r0 = c * chunk
        gc_all = _hdot(ltri, g_all[r0:r0 + chunk, :])
        gr_all = _hdot(g_all_t[:, r0:r0 + chunk], utri)
        for h in range(H_B):
            q = act_scr[r0:r0 + chunk, h * D_K:(h + 1) * D_K]
            k = act_scr[r0:r0 + chunk, hk + h * D_K:hk + (h + 1) * D_K]
            v = act_scr[r0:r0 + chunk, 2 * hk + h * D_V:2 * hk + (h + 1) * D_V]
            q = q * lax.rsqrt(jnp.sum(q * q, axis=-1, keepdims=True) + EPS) * (D_K ** -0.5)
            k = k * lax.rsqrt(jnp.sum(k * k, axis=-1, keepdims=True) + EPS)
            beta = beta_all[r0:r0 + chunk, h:h + 1]
            gc = gc_all[:, h:h + 1]
            gr = gr_all[h:h + 1, :]
            g_last = gc[chunk - 1:chunk, :]
            decay = jnp.where(tri, jnp.exp(jnp.where(tri, gc - gr, 0.0)), 0.0)
            kb = k * beta
            vb = v * beta
            kk = lax.dot_general(kb.astype(BF16), k.astype(BF16), NT_DIMS, preferred_element_type=F32)
            nm = -jnp.where(strict, kk * decay, 0.0)
            inv = eye + nm
            pw = nm
            for _ in range(n_steps - 1):
                pw = _hdot(pw, pw)
                inv = inv + _hdot(inv, pw)
            eg = jnp.exp(gc)
            uw = _hdot(inv, jnp.concatenate([vb, kb * eg], axis=1))
            u = uw[:, 0:D_V]
            wmat = uw[:, D_V:D_V + D_K]
            qk = lax.dot_general(q.astype(BF16), k.astype(BF16), NT_DIMS, preferred_element_type=F32)
            qk = jnp.where(tri, qk * decay, 0.0)
            qg = q * eg
            kg = k * jnp.exp(g_last - gc)
            s = s_ref[h]
            v_new = u - _bdot(wmat, s)
            o = _bdot(qg, s) + _bdot(qk, v_new)
            s_ref[h] = s * jnp.exp(g_last) + lax.dot_general(
                kg.astype(BF16), v_new.astype(BF16), TN_DIMS, preferred_element_type=F32)
            z = pz_ref[r0:r0 + chunk, h * D_V:(h + 1) * D_V]
            o = o * lax.rsqrt(jnp.mean(o * o, axis=-1, keepdims=True) + EPS) * nw * _silu(z)
            o_ref[r0:r0 + chunk, h * D_V:(h + 1) * D_V] = o.astype(o_ref.dtype)


def _gdn(proj, small, o_b, s0, c0, conv_w, a_log, dt_bias, norm_w, row0, bn, ln, chunk):
    hk = H_B * D_K
    cch = 2 * hk + H_B * D_V
    rows = ln
    for cand in (2 * chunk, chunk):
        if cand % LANES == 0 and ln % cand == 0:
            rows = cand
            break
    assert rows % chunk == 0 and rows % SUBLANES == 0 and row0 % rows == 0
    nblk = ln // rows
    r0 = row0 // rows
    qcol = (H_A * 2 * D_A * 2 + H_A * D_VA) // hk
    small_t = jnp.swapaxes(small[row0:row0 + bn * ln, 0:2 * H_B].reshape(bn, ln, 2 * H_B), 1, 2)
    alr = a_log.reshape(1, H_B)
    dtr = dt_bias.reshape(1, H_B)
    alc = a_log.reshape(H_B, 1)
    dtc = dt_bias.reshape(H_B, 1)
    row_blk = lambda b, i: (r0 + b * nblk + i)
    return pl.pallas_call(
        functools.partial(_gdn_body, chunk, rows),
        grid=(bn, nblk),
        in_specs=[
            pl.BlockSpec((rows, hk), lambda b, i: (row_blk(b, i), qcol)),
            pl.BlockSpec((rows, hk), lambda b, i: (row_blk(b, i), qcol + 1)),
            pl.BlockSpec((rows, hk), lambda b, i: (row_blk(b, i), qcol + 2)),
            pl.BlockSpec((rows, hk), lambda b, i: (row_blk(b, i), qcol + 3)),
            pl.BlockSpec((rows, LANES), lambda b, i: (row_blk(b, i), 0)),
            pl.BlockSpec((None, 2 * H_B, rows), lambda b, i: (b, 0, i)),
            pl.BlockSpec((CONV_W, cch), lambda b, i: (0, 0)),
            pl.BlockSpec((1, H_B), lambda b, i: (0, 0)),
            pl.BlockSpec((1, H_B), lambda b, i: (0, 0)),
            pl.BlockSpec((H_B, 1), lambda b, i: (0, 0)),
            pl.BlockSpec((H_B, 1), lambda b, i: (0, 0)),
            pl.BlockSpec((1, D_V), lambda b, i: (0, 0)),
            pl.BlockSpec((None, H_B, D_K, D_V), lambda b, i: (b, 0, 0, 0)),
            pl.BlockSpec((None, CONV_W - 1, cch), lambda b, i: (b, 0, 0)),
            pl.BlockSpec(memory_space=pl.ANY),
        ],
        out_specs=[
            pl.BlockSpec((rows, hk), lambda b, i: (row_blk(b, i), 0)),
            pl.BlockSpec((None, H_B, D_K, D_V), lambda b, i: (b, 0, 0, 0)),
            pl.BlockSpec((None, CONV_W - 1, cch), lambda b, i: (b, 0, 0)),
        ],
        out_shape=[
            jax.ShapeDtypeStruct(o_b.shape, o_b.dtype),
            jax.ShapeDtypeStruct(s0.shape, F32),
            jax.ShapeDtypeStruct(c0.shape, F32),
        ],
        scratch_shapes=[
            pltpu.VMEM((rows + SUBLANES, cch), F32),
            pltpu.VMEM((rows, cch), F32),
        ],
        input_output_aliases={14: 0},
        compiler_params=_cparams("parallel", "arbitrary"),
        name="gdn",
    )(proj, proj, proj, proj, small, small_t, conv_w, alr, dtr, alc, dtc, norm_w, s0, c0, o_b)


def _merge_body(x_ref, oa_ref, ob_ref, ga0_ref, ga1_ref, gb0_ref, gb1_ref, wa_ref, wb_ref, wo_ref,
                nf_ref, wr_ref, br_ref, x2_ref, t_ref, lg_ref):
    ya = jnp.dot(oa_ref[...], wa_ref[...], preferred_element_type=F32)
    yb = jnp.dot(ob_ref[...], wb_ref[...], preferred_element_type=F32)
    half = ga0_ref.shape[1]
    m0 = _sigmoid(ga0_ref[...]) * ya[:, :half] + _sigmoid(gb0_ref[...]) * yb[:, :half]
    m1 = _sigmoid(ga1_ref[...]) * ya[:, half:] + _sigmoid(gb1_ref[...]) * yb[:, half:]
    merged = jnp.concatenate([m0, m1], axis=1).astype(BF16)
    x2 = x_ref[...] + jnp.dot(merged, wo_ref[...], preferred_element_type=F32)
    x2_ref[...] = x2
    ms = jnp.mean(x2 * x2, axis=-1, keepdims=True)
    t = x2 * lax.rsqrt(ms + EPS) * nf_ref[...]
    t_ref[...] = t
    lg_ref[...] = _hdot(t, wr_ref[...]) + br_ref[...]


def _merge(x_all, o_a, o_b, proj, wa, wb, wo, norm_ffn, w_router, b_router):
    T, D = x_all.shape
    half = D // 2
    gcol = (H_A * 2 * D_A * 2 + H_A * D_VA + 2 * H_B * D_K + 2 * H_B * D_V) // half
    bm = _pick(T, (256, 128, 64, 32, 16, 8))
    const = dict(pipeline_mode=pl.Buffered(1))
    return pl.pallas_call(
        _merge_body,
        grid=(T // bm,),
        in_specs=[
            pl.BlockSpec((bm, D), lambda i: (i, 0)),
            pl.BlockSpec((bm, o_a.shape[1]), lambda i: (i, 0)),
            pl.BlockSpec((bm, o_b.shape[1]), lambda i: (i, 0)),
            pl.BlockSpec((bm, half), lambda i: (i, gcol)),
            pl.BlockSpec((bm, half), lambda i: (i, gcol + 1)),
            pl.BlockSpec((bm, half), lambda i: (i, gcol + 2)),
            pl.BlockSpec((bm, half), lambda i: (i, gcol + 3)),
            pl.BlockSpec(wa.shape, lambda i: (0, 0), **const),
            pl.BlockSpec(wb.shape, lambda i: (0, 0), **const),
            pl.BlockSpec(wo.shape, lambda i: (0, 0), **const),
            pl.BlockSpec((1, D), lambda i: (0, 0)),
            pl.BlockSpec((D, LANES), lambda i: (0, 0), **const),
            pl.BlockSpec((1, LANES), lambda i: (0, 0)),
        ],
        out_specs=[
            pl.BlockSpec((bm, D), lambda i: (i, 0)),
            pl.BlockSpec((bm, D), lambda i: (i, 0)),
            pl.BlockSpec((bm, LANES), lambda i: (i, 0)),
        ],
        out_shape=[
            jax.ShapeDtypeStruct((T, D), F32),
            jax.ShapeDtypeStruct((T, D), F32),
            jax.ShapeDtypeStruct((T, LANES), F32),
        ],
        compiler_params=_cparams("parallel"),
        name="merge",
    )(x_all, o_a, o_b, proj, proj, proj, proj, wa, wb, wo, norm_ffn, w_router, b_router)


def _route_body(n_experts, lg_ref, mi_ref, mf_ref, cnt_ref):
    i = pl.program_id(0)

    @pl.when(i == 0)
    def _():
        cnt_ref[...] = jnp.zeros(cnt_ref.shape, F32)

    lg = lg_ref[...]
    bm = lg.shape[0]
    lane = lax.broadcasted_iota(I32, lg.shape, 1).astype(F32)
    big = jnp.float32(LANES)
    neg = -jnp.inf
    gl = jnp.where(lane < N_GROUPS, lg, neg)
    gmax = jnp.max(gl, axis=1, keepdims=True)
    gidx = jnp.min(jnp.where(gl == gmax, lane, big), axis=1, keepdims=True)
    gw = 1.0 / jnp.sum(jnp.exp(gl - gmax), axis=1, keepdims=True)
    e_lo = N_GROUPS + gidx * EXPERTS_PER_GROUP
    valid = (lane >= e_lo) & (lane < e_lo + EXPERTS_PER_GROUP)
    el = jnp.where(valid, lg, neg)
    v1 = jnp.max(el, axis=1, keepdims=True)
    i1 = jnp.min(jnp.where(el == v1, lane, big), axis=1, keepdims=True)
    el2 = jnp.where(lane == i1, neg, el)
    v2 = jnp.max(el2, axis=1, keepdims=True)
    i2 = jnp.min(jnp.where(el2 == v2, lane, big), axis=1, keepdims=True)
    e21 = jnp.exp(v2 - v1)
    w1 = gw / (1.0 + e21)
    w2 = gw * e21 / (1.0 + e21)
    oh1 = (lane == i1).astype(F32)
    oh2 = (lane == i2).astype(F32)
    oh = oh1 + oh2
    rr = lax.broadcasted_iota(I32, (bm, bm), 0)
    cc = lax.broadcasted_iota(I32, (bm, bm), 1)
    before = (cc < rr).astype(BF16)
    cum = jnp.dot(before, oh.astype(BF16), preferred_element_type=F32) + cnt_ref[...]
    rank1 = jnp.sum(cum * oh1, axis=1, keepdims=True)
    rank2 = jnp.sum(cum * oh2, axis=1, keepdims=True)
    cnt_ref[...] = cnt_ref[...] + jnp.sum(oh, axis=0, keepdims=True)
    mi = jnp.where(lane == 0, i1 - N_GROUPS, 0.0)
    mi = jnp.where(lane == 1, i2 - N_GROUPS, mi)
    mi = jnp.where(lane == 2, rank1, mi)
    mi = jnp.where(lane == 3, rank2, mi)
    mi_ref[...] = mi.astype(I32)
    mf_ref[...] = jnp.where(lane == 0, w1, jnp.where(lane == 1, w2, 0.0))
    del n_experts


def _route(logits, n_experts):
    T = logits.shape[0]
    bm = _pick(T, (512, 256, 128, 64, 32, 16, 8))
    return pl.pallas_call(
        functools.partial(_route_body, n_experts),
        grid=(T // bm,),
        in_specs=[pl.BlockSpec((bm, LANES), lambda i: (i, 0))],
        out_specs=[
            pl.BlockSpec((bm, LANES), lambda i: (i, 0)),
            pl.BlockSpec((bm, LANES), lambda i: (i, 0)),
            pl.BlockSpec((1, LANES), lambda i: (0, 0)),
        ],
        out_shape=[
            jax.ShapeDtypeStruct((T, LANES), I32),
            jax.ShapeDtypeStruct((T, LANES), F32),
            jax.ShapeDtypeStruct((1, LANES), F32),
        ],
        compiler_params=_cparams("arbitrary"),
        name="route",
    )(logits)


def _dispatch_body(bm, dest_ref, t_ref, xin_ref, xb_ref, sem):
    del xin_ref

    def row_copy(r, d):
        return pltpu.make_async_copy(t_ref.at[pl.ds(r, 1)], xb_ref.at[pl.ds(d, 1)], sem)

    def issue(r, carry):
        for k in range(TOP_K_INNER):
            row_copy(r, dest_ref[0, TOP_K_INNER * r + k]).start()
        return carry

    lax.fori_loop(0, bm, issue, 0)

    def drain(r, carry):
        for k in range(TOP_K_INNER):
            row_copy(r, dest_ref[0, TOP_K_INNER * r + k]).wait()
        return carry

    lax.fori_loop(0, bm, drain, 0)


def _dispatch(t, dest, xb_init):
    T, D = t.shape
    bm = _pick(T, (256, 128, 64, 32, 16, 8))
    nblk = T // bm
    dest3 = dest.reshape(nblk, 1, TOP_K_INNER * bm)
    return pl.pallas_call(
        functools.partial(_dispatch_body, bm),
        grid=(nblk,),
        in_specs=[
            pl.BlockSpec((None, 1, TOP_K_INNER * bm), lambda i: (i, 0, 0), memory_space=pltpu.SMEM),
            pl.BlockSpec((bm, D), lambda i: (i, 0)),
            pl.BlockSpec(memory_space=pl.ANY),
        ],
        out_specs=pl.BlockSpec(memory_space=pl.ANY),
        out_shape=jax.ShapeDtypeStruct(xb_init.shape, xb_init.dtype),
        scratch_shapes=[pltpu.SemaphoreType.DMA(())],
        input_output_aliases={2: 0},
        compiler_params=_cparams("arbitrary"),
        name="dispatch",
    )(dest3, t, xb_init)


def _expert_body(be_ref, nu_ref, x_ref, wg_ref, wu_ref, wd_ref, y_ref):
    i = pl.program_id(0)

    @pl.when(i < nu_ref[0])
    def _():
        x = x_ref[...].astype(BF16)
        g = jnp.dot(x, wg_ref[...], preferred_element_type=F32)
        u = jnp.dot(x, wu_ref[...], preferred_element_type=F32)
        hmid = (_silu(g) * u).astype(BF16)
        y_ref[...] = jnp.dot(hmid, wd_ref[...], preferred_element_type=F32)

    @pl.when(i >= nu_ref[0])
    def _():
        y_ref[...] = jnp.zeros(y_ref.shape, F32)


def _experts(xb, blk_exp, n_used, w_gate, w_up, w_down, blk):
    P, D = xb.shape
    de = w_gate.shape[2]
    n_blocks = P // blk
    last = lambda i, nu: jnp.minimum(i, nu[0] - 1)
    grid_spec = pltpu.PrefetchScalarGridSpec(
        num_scalar_prefetch=2,
        grid=(n_blocks,),
        in_specs=[
            pl.BlockSpec((blk, D), lambda i, be, nu: (last(i, nu), 0)),
            pl.BlockSpec((None, D, de), lambda i, be, nu: (be[last(i, nu)], 0, 0)),
            pl.BlockSpec((None, D, de), lambda i, be, nu: (be[last(i, nu)], 0, 0)),
            pl.BlockSpec((None, de, D), lambda i, be, nu: (be[last(i, nu)], 0, 0)),
        ],
        out_specs=pl.BlockSpec((blk, D), lambda i, be, nu: (i, 0)),
    )
    return pl.pallas_call(
        _expert_body,
        grid_spec=grid_spec,
        out_shape=jax.ShapeDtypeStruct((P, D), F32),
        compiler_params=_cparams("arbitrary"),
        name="experts",
    )(blk_exp, n_used, xb, w_gate, w_up, w_down)


def _combine_body(bm, final, dest_ref, x2_ref, mf_ref, nw_ref, yb_ref, o_ref, y0_scr, y1_scr, sem):
    bufs = (y0_scr, y1_scr)

    def row_copy(r, k):
        d = dest_ref[0, TOP_K_INNER * r + k]
        return pltpu.make_async_copy(yb_ref.at[pl.ds(d, 1)], bufs[k].at[pl.ds(r, 1)], sem)

    def issue(r, carry):
        for k in range(TOP_K_INNER):
            row_copy(r, k).start()
        return carry

    lax.fori_loop(0, bm, issue, 0)

    def drain(r, carry):
        for k in range(TOP_K_INNER):
            row_copy(r, k).wait()
        return carry

    lax.fori_loop(0, bm, drain, 0)

    mf = mf_ref[...]
    x3 = x2_ref[...] + y0_scr[...] * mf[:, 0:1] + y1_scr[...] * mf[:, 1:2]
    if final:
        ms = jnp.mean(x3 * x3, axis=-1, keepdims=True)
        x3 = x3 * lax.rsqrt(ms + EPS) * nw_ref[...]
    o_ref[...] = x3


def _combine(x2, yb, dest, mf, norm_final, final):
    T, D = x2.shape
    bm = _pick(T, (256, 128, 64, 32, 16, 8))
    nblk = T // bm
    dest3 = dest.reshape(nblk, 1, TOP_K_INNER * bm)
    return pl.pallas_call(
        functools.partial(_combine_body, bm, final),
        grid=(nblk,),
        in_specs=[
            pl.BlockSpec((None, 1, TOP_K_INNER * bm), lambda i: (i, 0, 0), memory_space=pltpu.SMEM),
            pl.BlockSpec((bm, D), lambda i: (i, 0)),
            pl.BlockSpec((bm, LANES), lambda i: (i, 0)),
            pl.BlockSpec((1, D), lambda i: (0, 0)),
            pl.BlockSpec(memory_space=pl.ANY),
        ],
        out_specs=pl.BlockSpec((bm, D), lambda i: (i, 0)),
        out_shape=jax.ShapeDtypeStruct((T, D), F32),
        scratch_shapes=[
            pltpu.VMEM((bm, D), F32),
            pltpu.VMEM((bm, D), F32),
            pltpu.SemaphoreType.DMA(()),
        ],
        compiler_params=_cparams("arbitrary"),
        name="combine",
    )(dest3, x2, mf, norm_final, yb)


def _rotary_tables(pos):
    half = ROT_DIM // 2
    inv = ROPE_THETA ** (-jnp.arange(0, ROT_DIM, 2, dtype=F32) / ROT_DIM)
    ang = pos.astype(F32)[:, None] * inv[None, :]
    cos = jnp.cos(ang)
    sin = jnp.sin(ang)
    n = pos.shape[0]
    ones = jnp.ones((n, D_A - ROT_DIM), F32)
    zeros = jnp.zeros((n, D_A - ROT_DIM), F32)
    zh = jnp.zeros((n, half), F32)
    cos64 = jnp.concatenate([cos, cos, ones], axis=1)
    sa64 = jnp.concatenate([zh, sin, zeros], axis=1)
    sb64 = jnp.concatenate([-sin, zh, zeros], axis=1)
    reps = LANES // D_A
    return jnp.tile(cos64, (1, reps)), jnp.tile(sa64, (1, reps)), jnp.tile(sb64, (1, reps))


def _moe_slots(mi, counts, n_experts, blk, n_assign):
    eid = mi[:, 0:TOP_K_INNER]
    rank = mi[:, TOP_K_INNER:2 * TOP_K_INNER]
    cnt = counts[0, N_GROUPS:N_GROUPS + n_experts].astype(I32)
    pc = (cnt + blk - 1) // blk * blk
    pend = jnp.cumsum(pc)
    pstart = pend - pc
    dest = pstart[eid] + rank
    n_blocks = -(-n_assign // blk) + n_experts
    blk_exp = jnp.minimum(jnp.searchsorted(pend, jnp.arange(n_blocks, dtype=I32) * blk, side='right'),
                          n_experts - 1).astype(I32)
    n_used = jnp.maximum(pend[-1] // blk, 1).astype(I32).reshape(1)
    return dest.astype(I32), blk_exp, n_used, n_blocks


def _layer(x_all, layer, seg, cache_k, cache_v, state_delta, state_conv, prm):
    (norm_mix, w_in, lq1, lk1, lq2, lk2, subln_w, conv_w, a_log, dt_bias, gdn_norm_w,
     w_proj_a, w_proj_b, w_out, norm_ffn, w_gr, b_gr, w_er, b_er, w_eg, w_eu, w_ed) = prm
    bp, lp, bs, ls, past = seg
    T, D = x_all.shape
    tp = bp * lp
    n_qkv = H_A * 2 * D_A * 2 + H_A * D_VA + 2 * H_B * D_K + 2 * H_B * D_V
    n_small = 2 * H_B
    lam_init = 0.8 - 0.6 * math.exp(-0.3 * layer)

    pos = jnp.concatenate([jnp.tile(jnp.arange(lp), bp), jnp.tile(past + jnp.arange(ls), bs)])
    cos_t, sa_t, sb_t = _rotary_tables(pos)
    w_main = jnp.concatenate([w_in[:, :n_qkv], w_in[:, n_qkv + n_small:]], axis=1).astype(BF16)
    w_small = jnp.pad(w_in[:, n_qkv:n_qkv + n_small], ((0, 0), (0, LANES - n_small))).astype(BF16)
    proj, small, k_new, v_new = _inproj(x_all, norm_mix.reshape(1, D), cos_t, sa_t, sb_t, w_main, w_small)

    lamp = jnp.stack([lq1, lk1, lq2, lk2]).astype(F32)
    sw = subln_w.reshape(1, D_VA)
    o_a = jnp.zeros((T, H_A * D_VA), BF16)
    o_a = _attn_prompt(proj, o_a, lamp, sw, lam_init, bp, lp)
    o_a = _attn_sample(proj, o_a, cache_k.reshape(bs, past, H_A * 2 * D_A),
                       cache_v.reshape(bs, past, H_A * D_VA), lamp, sw, lam_init, tp, bs, ls)

    cch = 2 * H_B * D_K + H_B * D_V
    o_b = jnp.zeros((T, H_B * D_V), BF16)
    nw = gdn_norm_w.reshape(1, D_V)
    o_b, s_p, c_p = _gdn(proj, small, o_b, jnp.zeros((bp, H_B, D_K, D_V), F32),
                         jnp.zeros((bp, CONV_W - 1, cch), F32), conv_w, a_log, dt_bias, nw,
                         0, bp, lp, CHUNK)
    o_b, s_s, c_s = _gdn(proj, small, o_b, state_delta, state_conv, conv_w, a_log, dt_bias, nw,
                         tp, bs, ls, ls)

    n_experts = w_er.shape[1]
    w_router = jnp.pad(jnp.concatenate([w_gr, w_er], axis=1), ((0, 0), (0, LANES - N_GROUPS - n_experts)))
    b_router = jnp.pad(jnp.concatenate([b_gr, b_er]), (0, LANES - N_GROUPS - n_experts)).reshape(1, LANES)
    x2, t, logits = _merge(x_all, o_a, o_b, proj, w_proj_a.astype(BF16), w_proj_b.astype(BF16),
                           w_out.astype(BF16), norm_ffn.reshape(1, D), w_router, b_router)

    blk = 128
    mi, mf, counts = _route(logits, n_experts)
    dest, blk_exp, n_used, n_blocks = _moe_slots(mi, counts, n_experts, blk, T * TOP_K_INNER)
    xb = _dispatch(t, dest, jnp.zeros((n_blocks * blk, D), F32))
    yb = _experts(xb, blk_exp, n_used, w_eg.astype(BF16), w_eu.astype(BF16), w_ed.astype(BF16), blk)
    return x2, yb, dest, mf, k_new, v_new, s_p, c_p, s_s, c_s


def kernel(x_prompt, x_sample, cache_k, cache_v, state_delta, state_conv, norm_mix, w_in, lambda_q1, lambda_k1, lambda_q2, lambda_k2, subln_w, conv_w, a_log, dt_bias, gdn_norm_w, w_proj_a, w_proj_b, w_out, norm_ffn, w_group_router, b_group_router, w_expert_router, b_expert_router, w_exp_gate, w_exp_up, w_exp_down, norm_final):
    bp, lp, D = x_prompt.shape
    bs, ls, _ = x_sample.shape
    depth = cache_k.shape[0]
    past = cache_k.shape[2]
    tp = bp * lp
    seg = (bp, lp, bs, ls, past)
    x_all = jnp.concatenate([x_prompt.reshape(tp, D), x_sample.reshape(bs * ls, D)], axis=0)
    kp, vp, sp, cp, ksm, vsm, ssm, csm = [], [], [], [], [], [], [], []
    y_all = None
    for l in range(depth):
        prm = (norm_mix[l], w_in[l], lambda_q1[l], lambda_k1[l], lambda_q2[l], lambda_k2[l], subln_w[l],
               conv_w[l], a_log[l], dt_bias[l], gdn_norm_w[l], w_proj_a[l], w_proj_b[l], w_out[l],
               norm_ffn[l], w_group_router[l], b_group_router[l], w_expert_router[l], b_expert_router[l],
               w_exp_gate[l], w_exp_up[l], w_exp_down[l])
        x2, yb, dest, mf, k_new, v_new, s_p, c_p, s_s, c_s = _layer(
            x_all, l, seg, cache_k[l], cache_v[l], state_delta[l], state_conv[l], prm)
        y_all = _combine(x2, yb, dest, mf, norm_final.reshape(1, D), l == depth - 1)
        x_all = y_all
        kp.append(k_new[:tp].reshape(bp, lp, H_A, 2 * D_A))
        vp.append(v_new[:tp].reshape(bp, lp, H_A, D_VA))
        ksm.append(k_new[tp:].reshape(bs, ls, H_A, 2 * D_A))
        vsm.append(v_new[tp:].reshape(bs, ls, H_A, D_VA))
        sp.append(s_p); cp.append(c_p); ssm.append(s_s); csm.append(c_s)
    y_prompt = y_all[:tp].reshape(bp, lp, D)
    y_sample = y_all[tp:].reshape(bs, ls, D)
    return (y_prompt, y_sample, jnp.stack(kp), jnp.stack(vp), jnp.stack(sp), jnp.stack(cp),
            jnp.stack(ksm), jnp.stack(vsm), jnp.stack(ssm), jnp.stack(csm))
```

```python
import functools
import math

import jax
import jax.numpy as jnp
from jax import lax
from jax.experimental import pallas as pl
from jax.experimental.pallas import tpu as pltpu

F32 = jnp.float32
BF16 = jnp.bfloat16
I32 = jnp.int32

CHUNK = 64
H_A = 8
D_A = 64
D_VA = 2 * D_A
ROT_DIM = D_A // 4
ROPE_THETA = 500000.0
H_B = 8
D_K = 128
D_V = 128
CONV_W = 4
N_GROUPS = 4
EXPERTS_PER_GROUP = 8
TOP_K_INNER = 2
EPS = 1e-6

LANES = 128
SUBLANES = 8
V7X_VMEM_BYTES = 64 * 1024 * 1024
VMEM_LIMIT_BYTES = V7X_VMEM_BYTES - 16 * 1024 * 1024

Q_SCALE = (D_A ** -0.5) * math.log2(math.e)
ONES_ROWS = 16

NT_DIMS = (((1,), (1,)), ((), ()))
TN_DIMS = (((0,), (0,)), ((), ()))


def _pick(n, cands):
    for c in cands:
        if n % c == 0:
            return c
    raise ValueError(f"no block size in {cands} divides {n}")


def _cparams(*sem):
    return pltpu.CompilerParams(dimension_semantics=sem, vmem_limit_bytes=VMEM_LIMIT_BYTES)


def _sigmoid(x):
    return 1.0 / (1.0 + jnp.exp(-x))


def _silu(x):
    return x * _sigmoid(x)


def _softplus(x):
    return jnp.maximum(x, 0.0) + jnp.log1p(jnp.exp(-jnp.abs(x)))


def _bdot(a, b):
    return jnp.dot(a.astype(BF16), b.astype(BF16), preferred_element_type=F32)


def _split2(a):
    hi = a.astype(BF16)
    lo = (a - hi.astype(F32)).astype(BF16)
    return hi, lo


def _dot3(a, b):
    ah, al = _split2(a)
    bh, bl = _split2(b)
    d = functools.partial(jnp.dot, preferred_element_type=F32)
    return d(ah, bh) + d(ah, bl) + d(al, bh)


def _dot_exact_mask(mask, x, mask_is_lhs):
    hi = x.astype(BF16)
    r1 = x - hi.astype(F32)
    mid = r1.astype(BF16)
    lo = (r1 - mid.astype(F32)).astype(BF16)
    mb = mask.astype(BF16)
    d = functools.partial(jnp.dot, preferred_element_type=F32)
    if mask_is_lhs:
        return d(mb, hi) + d(mb, mid) + d(mb, lo)
    return d(hi, mb) + d(mid, mb) + d(lo, mb)


def _inproj_body(x_ref, nw_ref, cos_ref, sa_ref, sb_ref, w_ref, ws_ref,
                 o_ref, os_ref, k_ref, v_ref, qb_ref, kb_ref, vt_ref, h_scr):
    j = pl.program_id(1)

    @pl.when(j == 0)
    def _():
        x = x_ref[...]
        ms = jnp.mean(x * x, axis=-1, keepdims=True)
        h = (x * lax.rsqrt(ms + EPS) * nw_ref[...]).astype(BF16)
        h_scr[...] = h
        os_ref[...] = jnp.dot(h, ws_ref[...], preferred_element_type=F32)

    acc = jnp.dot(h_scr[...], w_ref[...], preferred_element_type=F32)

    @pl.when(j < 2)
    def _():
        cos = cos_ref[...]
        sa = sa_ref[...]
        sb = sb_ref[...]
        for c in range(acc.shape[1] // LANES):
            blk = acc[:, c * LANES:(c + 1) * LANES]
            rot = (blk * cos + pltpu.roll(blk, ROT_DIM // 2, 1) * sa
                   + pltpu.roll(blk, LANES - ROT_DIM // 2, 1) * sb)
            o_ref[:, c * LANES:(c + 1) * LANES] = rot

        @pl.when(j == 0)
        def _():
            qb_ref[...] = (o_ref[...] * Q_SCALE).astype(BF16)

        @pl.when(j == 1)
        def _():
            k_ref[...] = o_ref[...]
            kb_ref[...] = o_ref[...].astype(BF16)

    @pl.when(j >= 2)
    def _():
        o_ref[...] = acc

        @pl.when(j == 2)
        def _():
            v_ref[...] = acc
            for h in range(H_A):
                vt_ref[h, 0:D_VA, :] = acc[:, h * D_VA:(h + 1) * D_VA].T.astype(BF16)
                vt_ref[h, D_VA:D_VA + ONES_ROWS, :] = jnp.ones((ONES_ROWS, acc.shape[0]), BF16)


def _inproj(x_all, norm_w, cos_t, sa_t, sb_t, w_main, w_small):
    T, D = x_all.shape
    n_main = w_main.shape[1]
    bn = H_A * 2 * D_A
    assert n_main % bn == 0 and H_A * D_VA == bn
    bm = _pick(T, (512, 256, 128))
    grid = (T // bm, n_main // bn)
    return pl.pallas_call(
        _inproj_body,
        grid=grid,
        in_specs=[
            pl.BlockSpec((bm, D), lambda i, j: (i, 0)),
            pl.BlockSpec((1, D), lambda i, j: (0, 0)),
            pl.BlockSpec((bm, LANES), lambda i, j: (i, 0)),
            pl.BlockSpec((bm, LANES), lambda i, j: (i, 0)),
            pl.BlockSpec((bm, LANES), lambda i, j: (i, 0)),
            pl.BlockSpec((D, bn), lambda i, j: (0, j)),
            pl.BlockSpec((D, LANES), lambda i, j: (0, 0)),
        ],
        out_specs=[
            pl.BlockSpec((bm, bn), lambda i, j: (i, j)),
            pl.BlockSpec((bm, LANES), lambda i, j: (i, 0)),
            pl.BlockSpec((bm, bn), lambda i, j: (i, 0)),
            pl.BlockSpec((bm, bn), lambda i, j: (i, 0)),
            pl.BlockSpec((bm, bn), lambda i, j: (i, 0)),
            pl.BlockSpec((bm, bn), lambda i, j: (i, 0)),
            pl.BlockSpec((H_A, D_VA + ONES_ROWS, bm), lambda i, j: (0, 0, i)),
        ],
        out_shape=[
            jax.ShapeDtypeStruct((T, n_main), F32),
            jax.ShapeDtypeStruct((T, LANES), F32),
            jax.ShapeDtypeStruct((T, bn), F32),
            jax.ShapeDtypeStruct((T, bn), F32),
            jax.ShapeDtypeStruct((T, bn), BF16),
            jax.ShapeDtypeStruct((T, bn), BF16),
            jax.ShapeDtypeStruct((H_A, D_VA + ONES_ROWS, T), BF16),
        ],
        scratch_shapes=[pltpu.VMEM((bm, D), BF16)],
        compiler_params=_cparams("parallel", "arbitrary"),
        name="inproj",
    )(x_all, norm_w, cos_t, sa_t, sb_t, w_main, w_small)


def _diff_lambda(lamp, lam_init):
    s1 = jnp.sum(lamp[0:1] * lamp[1:2], axis=1, keepdims=True)
    s2 = jnp.sum(lamp[2:3] * lamp[3:4], axis=1, keepdims=True)
    return jnp.exp(s1) - jnp.exp(s2) + lam_init


def _subln(o, w, lam_init):
    ms = jnp.mean(o * o, axis=-1, keepdims=True)
    return o * lax.rsqrt(ms + EPS) * w * (1.0 - lam_init)


def _attn_prompt_body(lam_init, bq, q_ref, k_ref, vt_ref, lamp_ref, swc_ref, oin_ref, o_ref,
                      m_scr, acc_scr):
    del oin_ref
    qi = pl.program_id(2)
    q = q_ref[...]
    lane = lax.broadcasted_iota(I32, q.shape, 1)
    qz = [jnp.where(lane < D_A, q, jnp.zeros_like(q)), jnp.where(lane >= D_A, q, jnp.zeros_like(q))]
    m_scr[...] = jnp.full(m_scr.shape, -jnp.inf, F32)
    acc_scr[...] = jnp.zeros(acc_scr.shape, F32)
    key_chunk = lax.broadcasted_iota(I32, (bq, bq), 0) // CHUNK
    qry_chunk = lax.broadcasted_iota(I32, (bq, bq), 1) // CHUNK
    diag_mask = key_chunk <= qry_chunk

    def block(j, masked):
        off = pl.multiple_of(j * bq, bq)
        kb = k_ref[pl.ds(off, bq), :]
        vb = vt_ref[:, pl.ds(off, bq)]
        for m in range(2):
            s = lax.dot_general(kb, qz[m], NT_DIMS, preferred_element_type=F32)
            if masked:
                s = jnp.where(diag_mask, s, -jnp.inf)
            m_prev = m_scr[m]
            m_new = jnp.maximum(m_prev, jnp.max(s, axis=0, keepdims=True))
            alpha = jnp.exp2(m_prev - m_new)
            p = jnp.exp2(s - m_new).astype(BF16)
            acc_scr[m] = alpha * acc_scr[m] + jnp.dot(vb, p, preferred_element_type=F32)
            m_scr[m] = m_new

    def full_block(j, carry):
        block(j, False)
        return carry

    lax.fori_loop(0, qi, full_block, 0)
    block(qi, True)

    lam = _diff_lambda(lamp_ref[...], lam_init)
    a0 = acc_scr[0]
    a1 = acc_scr[1]
    o_t = a0[0:D_VA] / a0[D_VA:D_VA + 1] - lam * (a1[0:D_VA] / a1[D_VA:D_VA + 1])
    ms = jnp.mean(o_t * o_t, axis=0, keepdims=True)
    o_t = o_t * lax.rsqrt(ms + EPS) * swc_ref[...] * (1.0 - lam_init)
    o_ref[...] = o_t.T.astype(o_ref.dtype)


def _attn_prompt(q_bf, k_bf, v_t, o_a, lamp, subln_w, lam_init, bp, lp):
    bq = _pick(lp, (512, 256, 128))
    nq = lp // bq
    vrows = v_t.shape[1]
    return pl.pallas_call(
        functools.partial(_attn_prompt_body, lam_init, bq),
        grid=(bp, H_A, nq),
        in_specs=[
            pl.BlockSpec((bq, LANES), lambda b, h, i: (b * nq + i, h)),
            pl.BlockSpec((lp, LANES), lambda b, h, i: (b, h)),
            pl.BlockSpec((None, vrows, lp), lambda b, h, i: (h, 0, b)),
            pl.BlockSpec((4, D_A), lambda b, h, i: (0, 0)),
            pl.BlockSpec((D_VA, 1), lambda b, h, i: (0, 0)),
            pl.BlockSpec(memory_space=pl.ANY),
        ],
        out_specs=pl.BlockSpec((bq, LANES), lambda b, h, i: (b * nq + i, h)),
        out_shape=jax.ShapeDtypeStruct(o_a.shape, o_a.dtype),
        scratch_shapes=[
            pltpu.VMEM((2, 1, bq), F32),
            pltpu.VMEM((2, vrows, bq), F32),
        ],
        input_output_aliases={5: 0},
        compiler_params=_cparams("parallel", "parallel", "arbitrary"),
        name="attn_prompt",
    )(q_bf, k_bf, v_t, lamp, subln_w.reshape(D_VA, 1), o_a)


def _attn_sample_body(lam_init, q_ref, kn_ref, vn_ref, kc_ref, vc_ref, lamp_ref, sw_ref,
                      oin_ref, o_ref, m_scr, l_scr, acc_scr):
    del oin_ref
    pj = pl.program_id(1)
    npos = kc_ref.shape[0] // H_A
    ls = q_ref.shape[0]

    @pl.when(pj == 0)
    def _():
        m_scr[...] = jnp.full(m_scr.shape, -jnp.inf, F32)
        l_scr[...] = jnp.zeros(l_scr.shape, F32)
        acc_scr[...] = jnp.zeros(acc_scr.shape, F32)

    lane = lax.broadcasted_iota(I32, (ls, D_VA), 1)

    def q_maps(h):
        q = q_ref[:, h * D_VA:(h + 1) * D_VA]
        return [jnp.where(lane < D_A, q, jnp.zeros_like(q)), jnp.where(lane >= D_A, q, jnp.zeros_like(q))]

    def update(idx, qz, keys, vals):
        s = lax.dot_general(qz, keys, NT_DIMS, preferred_element_type=F32)
        m_prev = m_scr[idx]
        m_new = jnp.maximum(m_prev, jnp.max(s, axis=1, keepdims=True))
        alpha = jnp.exp2(m_prev - m_new)
        p = jnp.exp2(s - m_new)
        l_scr[idx] = alpha * l_scr[idx] + jnp.sum(p, axis=1, keepdims=True)
        acc_scr[idx] = alpha * acc_scr[idx] + jnp.dot(p.astype(BF16), vals, preferred_element_type=F32)
        m_scr[idx] = m_new

    for h in range(H_A):
        kc = kc_ref[pl.ds(h, npos, stride=H_A), :].astype(BF16)
        vc = vc_ref[pl.ds(h, npos, stride=H_A), :].astype(BF16)
        qz = q_maps(h)
        for m in range(2):
            update(2 * h + m, qz[m], kc, vc)

    @pl.when(pj == pl.num_programs(1) - 1)
    def _():
        lam = _diff_lambda(lamp_ref[...], lam_init)
        sw = sw_ref[...]
        for h in range(H_A):
            kn = kn_ref[:, h * D_VA:(h + 1) * D_VA]
            vn = vn_ref[:, h * D_VA:(h + 1) * D_VA].astype(BF16)
            qz = q_maps(h)
            outs = []
            for m in range(2):
                update(2 * h + m, qz[m], kn, vn)
                outs.append(acc_scr[2 * h + m] / l_scr[2 * h + m])
            o = outs[0] - lam * outs[1]
            o_ref[:, h * D_VA:(h + 1) * D_VA] = _subln(o, sw, lam_init).astype(o_ref.dtype)


def _attn_sample(q_bf, k_bf, v_new, o_a, cache_k, cache_v, lamp, subln_w, lam_init, row0, bs, ls):
    past = cache_k.shape[1] // H_A
    pb = _pick(past, (1024, 512, 256, 128, 64))
    wcol = H_A * D_VA
    r0 = row0 // ls
    assert row0 % ls == 0 and ls % (2 * SUBLANES) == 0
    return pl.pallas_call(
        functools.partial(_attn_sample_body, lam_init),
        grid=(bs, past // pb),
        in_specs=[
            pl.BlockSpec((ls, wcol), lambda b, j: (r0 + b, 0)),
            pl.BlockSpec((ls, wcol), lambda b, j: (r0 + b, 0)),
            pl.BlockSpec((ls, wcol), lambda b, j: (r0 + b, 0)),
            pl.BlockSpec((None, pb * H_A, D_VA), lambda b, j: (b, j, 0)),
            pl.BlockSpec((None, pb * H_A, D_VA), lambda b, j: (b, j, 0)),
            pl.BlockSpec((4, D_A), lambda b, j: (0, 0)),
            pl.BlockSpec((1, D_VA), lambda b, j: (0, 0)),
            pl.BlockSpec(memory_space=pl.ANY),
        ],
        out_specs=pl.BlockSpec((ls, wcol), lambda b, j: (r0 + b, 0)),
        out_shape=jax.ShapeDtypeStruct(o_a.shape, o_a.dtype),
        scratch_shapes=[
            pltpu.VMEM((2 * H_A, ls, 1), F32),
            pltpu.VMEM((2 * H_A, ls, 1), F32),
            pltpu.VMEM((2 * H_A, ls, D_VA), F32),
        ],
        input_output_aliases={7: 0},
        compiler_params=_cparams("parallel", "arbitrary"),
        name="attn_sample",
    )(q_bf, k_bf, v_new, cache_k, cache_v, lamp, subln_w, o_a)


def _gdn_body(chunk, rows, pq_ref, pk_ref, pv_ref, pz_ref, sm_ref, smt_ref, cw_ref, alr_ref, dtr_ref,
              alc_ref, dtc_ref, nw_ref, s0_ref, c0_ref, oin_ref, o_ref, s_ref, cn_ref, ext_scr, act_scr):
    del oin_ref
    i = pl.program_id(1)
    nblk = pl.num_programs(1)
    hk = H_B * D_K
    hv = H_B * D_V
    n_chunks = rows // chunk
    n_steps = int(math.log2(chunk))
    assert 2 ** n_steps == chunk

    @pl.when(i == 0)
    def _():
        s_ref[...] = s0_ref[...]
        ext_scr[0:SUBLANES, :] = jnp.zeros((SUBLANES, ext_scr.shape[1]), F32)
        ext_scr[pl.ds(SUBLANES - (CONV_W - 1), CONV_W - 1), :] = c0_ref[...]

    ext_scr[pl.ds(SUBLANES, rows), 0:hk] = pq_ref[...]
    ext_scr[pl.ds(SUBLANES, rows), hk:2 * hk] = pk_ref[...]
    ext_scr[pl.ds(SUBLANES, rows), 2 * hk:2 * hk + hv] = pv_ref[...]

    @pl.when(i == nblk - 1)
    def _():
        cn_ref[...] = ext_scr[pl.ds(SUBLANES + rows - (CONV_W - 1), CONV_W - 1), :]

    conv = jnp.zeros((rows, ext_scr.shape[1]), F32)
    for w in range(CONV_W):
        conv = conv + ext_scr[pl.ds(SUBLANES - (CONV_W - 1) + w, rows), :] * cw_ref[w:w + 1, :]
    act_scr[...] = _silu(conv)
    ext_scr[0:SUBLANES, :] = ext_scr[pl.ds(rows, SUBLANES), :]

    sm = sm_ref[...]
    beta_all = _sigmoid(sm[:, 0:H_B])
    g_all = -jnp.exp(alr_ref[...]) * _softplus(sm[:, H_B:2 * H_B] + dtr_ref[...])
    smt = smt_ref[...]
    g_all_t = -jnp.exp(alc_ref[...]) * _softplus(smt[H_B:2 * H_B, :] + dtc_ref[...])

    ri = lax.broadcasted_iota(I32, (chunk, chunk), 0)
    ci = lax.broadcasted_iota(I32, (chunk, chunk), 1)
    tri = ri >= ci
    strict = ri > ci
    eye = (ri == ci).astype(F32)
    ltri = tri.astype(F32)
    utri = (ri <= ci).astype(F32)
    nw = nw_ref[...]

    units = [(c, h) for c in range(n_chunks) for h in range(H_B)]
    gcs = [_dot_exact_mask(ltri, g_all[c * chunk:(c + 1) * chunk, :], True) for c in range(n_chunks)]
    grs = [_dot_exact_mask(utri, g_all_t[:, c * chunk:(c + 1) * chunk], False) for c in range(n_chunks)]
    pre = []
    for c, h in units:
        r0 = c * chunk
        q = act_scr[r0:r0 + chunk, h * D_K:(h + 1) * D_K]
        k = act_scr[r0:r0 + chunk, hk + h * D_K:hk + (h + 1) * D_K]
        v = act_scr[r0:r0 + chunk, 2 * hk + h * D_V:2 * hk + (h + 1) * D_V]
        q = q * lax.rsqrt(jnp.sum(q * q, axis=-1, keepdims=True) + EPS) * (D_K ** -0.5)
        k = k * lax.rsqrt(jnp.sum(k * k, axis=-1, keepdims=True) + EPS)
        beta = beta_all[r0:r0 + chunk, h:h + 1]
        gc = gcs[c][:, h:h + 1]
        gr = grs[c][h:h + 1, :]
        g_last = gc[chunk - 1:chunk, :]
        decay = jnp.where(tri, jnp.exp(jnp.where(tri, gc - gr, 0.0)), 0.0)
        kb = k * beta
        eg = jnp.exp(gc)
        kbf = k.astype(BF16)
        kk = lax.dot_general(kb.astype(BF16), kbf, NT_DIMS, preferred_element_type=F32)
        qk = lax.dot_general(q.astype(BF16), kbf, NT_DIMS, preferred_element_type=F32)
        pre.append(dict(
            nm=-jnp.where(strict, kk * decay, 0.0),
            rhs=jnp.concatenate([v * beta, kb * eg], axis=1),
            qk=jnp.where(tri, qk * decay, 0.0).astype(BF16),
            qg=(q * eg).astype(BF16),
            kg=(k * jnp.exp(g_last - gc)).astype(BF16),
            gl=jnp.exp(g_last)))
    invs = [eye + p["nm"] for p in pre]
    pws = [p["nm"] for p in pre]
    for _ in range(n_steps - 1):
        pws = [_bdot(pw, pw) for pw in pws]
        invs = [inv + _bdot(inv, pw) for inv, pw in zip(invs, pws)]
    resid = [eye - _dot3(eye - p["nm"], inv) for p, inv in zip(pre, invs)]
    invs = [inv + _bdot(inv, r) for inv, r in zip(invs, resid)]
    uws = [_dot3(inv, p["rhs"]) for p, inv in zip(pre, invs)]

    for (c, h), p, uw in zip(units, pre, uws):
        r0 = c * chunk
        u = uw[:, 0:D_V]
        wmat = uw[:, D_V:D_V + D_K]
        s = s_ref[h]
        sb = s.astype(BF16)
        v_new = u - jnp.dot(wmat.astype(BF16), sb, preferred_element_type=F32)
        v_new_b = v_new.astype(BF16)
        o = (jnp.dot(p["qg"], sb, preferred_element_type=F32)
             + jnp.dot(p["qk"], v_new_b, preferred_element_type=F32))
        s_ref[h] = s * p["gl"] + lax.dot_general(p["kg"], v_new_b, TN_DIMS, preferred_element_type=F32)
        z = pz_ref[r0:r0 + chunk, h * D_V:(h + 1) * D_V]
        o = o * lax.rsqrt(jnp.mean(o * o, axis=-1, keepdims=True) + EPS) * nw * _silu(z)
        o_ref[r0:r0 + chunk, h * D_V:(h + 1) * D_V] = o.astype(o_ref.dtype)


def _gdn(proj, small, o_b, s0, c0, conv_w, a_log, dt_bias, norm_w, row0, bn, ln, chunk):
    hk = H_B * D_K
    cch = 2 * hk + H_B * D_V
    rows = ln
    for cand in (2 * chunk, chunk):
        if cand % LANES == 0 and ln % cand == 0:
            rows = cand
            break
    assert rows % chunk == 0 and rows % SUBLANES == 0 and row0 % rows == 0
    nblk = ln // rows
    r0 = row0 // rows
    qcol = (H_A * 2 * D_A * 2 + H_A * D_VA) // hk
    small_t = jnp.swapaxes(small[row0:row0 + bn * ln, 0:2 * H_B].reshape(bn, ln, 2 * H_B), 1, 2)
    alr = a_log.reshape(1, H_B)
    dtr = dt_bias.reshape(1, H_B)
    alc = a_log.reshape(H_B, 1)
    dtc = dt_bias.reshape(H_B, 1)
    row_blk = lambda b, i: (r0 + b * nblk + i)
    return pl.pallas_call(
        functools.partial(_gdn_body, chunk, rows),
        grid=(bn, nblk),
        in_specs=[
            pl.BlockSpec((rows, hk), lambda b, i: (row_blk(b, i), qcol)),
            pl.BlockSpec((rows, hk), lambda b, i: (row_blk(b, i), qcol + 1)),
            pl.BlockSpec((rows, hk), lambda b, i: (row_blk(b, i), qcol + 2)),
            pl.BlockSpec((rows, hk), lambda b, i: (row_blk(b, i), qcol + 3)),
            pl.BlockSpec((rows, LANES), lambda b, i: (row_blk(b, i), 0)),
            pl.BlockSpec((None, 2 * H_B, rows), lambda b, i: (b, 0, i)),
            pl.BlockSpec((CONV_W, cch), lambda b, i: (0, 0)),
            pl.BlockSpec((1, H_B), lambda b, i: (0, 0)),
            pl.BlockSpec((1, H_B), lambda b, i: (0, 0)),
            pl.BlockSpec((H_B, 1), lambda b, i: (0, 0)),
            pl.BlockSpec((H_B, 1), lambda b, i: (0, 0)),
            pl.BlockSpec((1, D_V), lambda b, i: (0, 0)),
            pl.BlockSpec((None, H_B, D_K, D_V), lambda b, i: (b, 0, 0, 0)),
            pl.BlockSpec((None, CONV_W - 1, cch), lambda b, i: (b, 0, 0)),
            pl.BlockSpec(memory_space=pl.ANY),
        ],
        out_specs=[
            pl.BlockSpec((rows, hk), lambda b, i: (row_blk(b, i), 0)),
            pl.BlockSpec((None, H_B, D_K, D_V), lambda b, i: (b, 0, 0, 0)),
            pl.BlockSpec((None, CONV_W - 1, cch), lambda b, i: (b, 0, 0)),
        ],
        out_shape=[
            jax.ShapeDtypeStruct(o_b.shape, o_b.dtype),
            jax.ShapeDtypeStruct(s0.shape, F32),
            jax.ShapeDtypeStruct(c0.shape, F32),
        ],
        scratch_shapes=[
            pltpu.VMEM((rows + SUBLANES, cch), F32),
            pltpu.VMEM((rows, cch), F32),
        ],
        input_output_aliases={14: 0},
        compiler_params=_cparams("parallel", "arbitrary"),
        name="gdn",
    )(proj, proj, proj, proj, small, small_t, conv_w, alr, dtr, alc, dtc, norm_w, s0, c0, o_b)


def _merge_body(x_ref, oa_ref, ob_ref, ga0_ref, ga1_ref, gb0_ref, gb1_ref, wa_ref, wb_ref, wo_ref,
                nf_ref, wr_ref, br_ref, x2_ref, t_ref, lg_ref):
    ya = jnp.dot(oa_ref[...], wa_ref[...], preferred_element_type=F32)
    yb = jnp.dot(ob_ref[...], wb_ref[...], preferred_element_type=F32)
    half = ga0_ref.shape[1]
    m0 = _sigmoid(ga0_ref[...]) * ya[:, :half] + _sigmoid(gb0_ref[...]) * yb[:, :half]
    m1 = _sigmoid(ga1_ref[...]) * ya[:, half:] + _sigmoid(gb1_ref[...]) * yb[:, half:]
    merged = jnp.concatenate([m0, m1], axis=1).astype(BF16)
    x2 = x_ref[...] + jnp.dot(merged, wo_ref[...], preferred_element_type=F32)
    x2_ref[...] = x2
    ms = jnp.mean(x2 * x2, axis=-1, keepdims=True)
    t = x2 * lax.rsqrt(ms + EPS) * nf_ref[...]
    t_ref[...] = t
    lg_ref[...] = _dot3(t, wr_ref[...]) + br_ref[...]


def _merge(x_all, o_a, o_b, proj, wa, wb, wo, norm_ffn, w_router, b_router):
    T, D = x_all.shape
    half = D // 2
    gcol = (H_A * 2 * D_A * 2 + H_A * D_VA + 2 * H_B * D_K + 2 * H_B * D_V) // half
    bm = _pick(T, (256, 128, 64, 32, 16, 8))
    const = dict(pipeline_mode=pl.Buffered(1))
    return pl.pallas_call(
        _merge_body,
        grid=(T // bm,),
        in_specs=[
            pl.BlockSpec((bm, D), lambda i: (i, 0)),
            pl.BlockSpec((bm, o_a.shape[1]), lambda i: (i, 0)),
            pl.BlockSpec((bm, o_b.shape[1]), lambda i: (i, 0)),
            pl.BlockSpec((bm, half), lambda i: (i, gcol)),
            pl.BlockSpec((bm, half), lambda i: (i, gcol + 1)),
            pl.BlockSpec((bm, half), lambda i: (i, gcol + 2)),
            pl.BlockSpec((bm, half), lambda i: (i, gcol + 3)),
            pl.BlockSpec(wa.shape, lambda i: (0, 0), **const),
            pl.BlockSpec(wb.shape, lambda i: (0, 0), **const),
            pl.BlockSpec(wo.shape, lambda i: (0, 0), **const),
            pl.BlockSpec((1, D), lambda i: (0, 0)),
            pl.BlockSpec((D, LANES), lambda i: (0, 0), **const),
            pl.BlockSpec((1, LANES), lambda i: (0, 0)),
        ],
        out_specs=[
            pl.BlockSpec((bm, D), lambda i: (i, 0)),
            pl.BlockSpec((bm, D), lambda i: (i, 0)),
            pl.BlockSpec((bm, LANES), lambda i: (i, 0)),
        ],
        out_shape=[
            jax.ShapeDtypeStruct((T, D), F32),
            jax.ShapeDtypeStruct((T, D), F32),
            jax.ShapeDtypeStruct((T, LANES), F32),
        ],
        compiler_params=_cparams("parallel"),
        name="merge",
    )(x_all, o_a, o_b, proj, proj, proj, proj, wa, wb, wo, norm_ffn, w_router, b_router)


def _route_body(n_experts, lg_ref, mi_ref, mf_ref, cnt_ref):
    i = pl.program_id(0)

    @pl.when(i == 0)
    def _():
        cnt_ref[...] = jnp.zeros(cnt_ref.shape, F32)

    lg = lg_ref[...]
    bm = lg.shape[0]
    lane = lax.broadcasted_iota(I32, lg.shape, 1).astype(F32)
    big = jnp.float32(LANES)
    neg = -jnp.inf
    gl = jnp.where(lane < N_GROUPS, lg, neg)
    gmax = jnp.max(gl, axis=1, keepdims=True)
    gidx = jnp.min(jnp.where(gl == gmax, lane, big), axis=1, keepdims=True)
    gw = 1.0 / jnp.sum(jnp.exp(gl - gmax), axis=1, keepdims=True)
    e_lo = N_GROUPS + gidx * EXPERTS_PER_GROUP
    valid = (lane >= e_lo) & (lane < e_lo + EXPERTS_PER_GROUP)
    el = jnp.where(valid, lg, neg)
    v1 = jnp.max(el, axis=1, keepdims=True)
    i1 = jnp.min(jnp.where(el == v1, lane, big), axis=1, keepdims=True)
    el2 = jnp.where(lane == i1, neg, el)
    v2 = jnp.max(el2, axis=1, keepdims=True)
    i2 = jnp.min(jnp.where(el2 == v2, lane, big), axis=1, keepdims=True)
    e21 = jnp.exp(v2 - v1)
    w1 = gw / (1.0 + e21)
    w2 = gw * e21 / (1.0 + e21)
    oh1 = (lane == i1).astype(F32)
    oh2 = (lane == i2).astype(F32)
    oh = oh1 + oh2
    rr = lax.broadcasted_iota(I32, (bm, bm), 0)
    cc = lax.broadcasted_iota(I32, (bm, bm), 1)
    before = (cc < rr).astype(BF16)
    cum = jnp.dot(before, oh.astype(BF16), preferred_element_type=F32) + cnt_ref[...]
    rank1 = jnp.sum(cum * oh1, axis=1, keepdims=True)
    rank2 = jnp.sum(cum * oh2, axis=1, keepdims=True)
    cnt_ref[...] = cnt_ref[...] + jnp.sum(oh, axis=0, keepdims=True)
    mi = jnp.where(lane == 0, i1 - N_GROUPS, 0.0)
    mi = jnp.where(lane == 1, i2 - N_GROUPS, mi)
    mi = jnp.where(lane == 2, rank1, mi)
    mi = jnp.where(lane == 3, rank2, mi)
    mi_ref[...] = mi.astype(I32)
    mf_ref[...] = jnp.where(lane == 0, w1, jnp.where(lane == 1, w2, 0.0))
    del n_experts


def _route(logits, n_experts):
    T = logits.shape[0]
    bm = _pick(T, (512, 256, 128, 64, 32, 16, 8))
    return pl.pallas_call(
        functools.partial(_route_body, n_experts),
        grid=(T // bm,),
        in_specs=[pl.BlockSpec((bm, LANES), lambda i: (i, 0))],
        out_specs=[
            pl.BlockSpec((bm, LANES), lambda i: (i, 0)),
            pl.BlockSpec((bm, LANES), lambda i: (i, 0)),
            pl.BlockSpec((1, LANES), lambda i: (0, 0)),
        ],
        out_shape=[
            jax.ShapeDtypeStruct((T, LANES), I32),
            jax.ShapeDtypeStruct((T, LANES), F32),
            jax.ShapeDtypeStruct((1, LANES), F32),
        ],
        compiler_params=_cparams("arbitrary"),
        name="route",
    )(logits)


def _dispatch_body(bm, dest_ref, t_ref, xin_ref, xb_ref, sem):
    del xin_ref

    def row_copy(r, d):
        return pltpu.make_async_copy(t_ref.at[pl.ds(r, 1)], xb_ref.at[pl.ds(d, 1)], sem)

    def issue(r, carry):
        for k in range(TOP_K_INNER):
            row_copy(r, dest_ref[0, TOP_K_INNER * r + k]).start()
        return carry

    lax.fori_loop(0, bm, issue, 0)

    def drain(r, carry):
        for k in range(TOP_K_INNER):
            row_copy(r, dest_ref[0, TOP_K_INNER * r + k]).wait()
        return carry

    lax.fori_loop(0, bm, drain, 0)


def _dispatch(t, dest, xb_init):
    T, D = t.shape
    bm = _pick(T, (256, 128, 64, 32, 16, 8))
    nblk = T // bm
    dest3 = dest.reshape(nblk, 1, TOP_K_INNER * bm)
    return pl.pallas_call(
        functools.partial(_dispatch_body, bm),
        grid=(nblk,),
        in_specs=[
            pl.BlockSpec((None, 1, TOP_K_INNER * bm), lambda i: (i, 0, 0), memory_space=pltpu.SMEM),
            pl.BlockSpec((bm, D), lambda i: (i, 0)),
            pl.BlockSpec(memory_space=pl.ANY),
        ],
        out_specs=pl.BlockSpec(memory_space=pl.ANY),
        out_shape=jax.ShapeDtypeStruct(xb_init.shape, xb_init.dtype),
        scratch_shapes=[pltpu.SemaphoreType.DMA(())],
        input_output_aliases={2: 0},
        compiler_params=_cparams("arbitrary"),
        name="dispatch",
    )(dest3, t, xb_init)


def _expert_body(be_ref, nu_ref, x_ref, wg_ref, wu_ref, wd_ref, y_ref):
    i = pl.program_id(0)

    @pl.when(i < nu_ref[0])
    def _():
        x = x_ref[...].astype(BF16)
        g = jnp.dot(x, wg_ref[...], preferred_element_type=F32)
        u = jnp.dot(x, wu_ref[...], preferred_element_type=F32)
        hmid = (_silu(g) * u).astype(BF16)
        y_ref[...] = jnp.dot(hmid, wd_ref[...], preferred_element_type=F32)

    @pl.when(i >= nu_ref[0])
    def _():
        y_ref[...] = jnp.zeros(y_ref.shape, F32)


def _experts(xb, blk_exp, n_used, w_gate, w_up, w_down, blk):
    P, D = xb.shape
    de = w_gate.shape[2]
    n_blocks = P // blk
    last = lambda i, nu: jnp.minimum(i, nu[0] - 1)
    grid_spec = pltpu.PrefetchScalarGridSpec(
        num_scalar_prefetch=2,
        grid=(n_blocks,),
        in_specs=[
            pl.BlockSpec((blk, D), lambda i, be, nu: (last(i, nu), 0)),
            pl.BlockSpec((None, D, de), lambda i, be, nu: (be[last(i, nu)], 0, 0)),
            pl.BlockSpec((None, D, de), lambda i, be, nu: (be[last(i, nu)], 0, 0)),
            pl.BlockSpec((None, de, D), lambda i, be, nu: (be[last(i, nu)], 0, 0)),
        ],
        out_specs=pl.BlockSpec((blk, D), lambda i, be, nu: (i, 0)),
    )
    return pl.pallas_call(
        _expert_body,
        grid_spec=grid_spec,
        out_shape=jax.ShapeDtypeStruct((P, D), F32),
        compiler_params=_cparams("arbitrary"),
        name="experts",
    )(blk_exp, n_used, xb, w_gate, w_up, w_down)


def _combine_body(bm, final, dest_ref, x2_ref, mf_ref, nw_ref, yb_ref, o_ref, y0_scr, y1_scr, sem):
    bufs = (y0_scr, y1_scr)

    def row_copy(r, k):
        d = dest_ref[0, TOP_K_INNER * r + k]
        return pltpu.make_async_copy(yb_ref.at[pl.ds(d, 1)], bufs[k].at[pl.ds(r, 1)], sem)

    def issue(r, carry):
        for k in range(TOP_K_INNER):
            row_copy(r, k).start()
        return carry

    lax.fori_loop(0, bm, issue, 0)

    def drain(r, carry):
        for k in range(TOP_K_INNER):
            row_copy(r, k).wait()
        return carry

    lax.fori_loop(0, bm, drain, 0)

    mf = mf_ref[...]
    x3 = x2_ref[...] + y0_scr[...] * mf[:, 0:1] + y1_scr[...] * mf[:, 1:2]
    if final:
        ms = jnp.mean(x3 * x3, axis=-1, keepdims=True)
        x3 = x3 * lax.rsqrt(ms + EPS) * nw_ref[...]
    o_ref[...] = x3


def _combine(x2, yb, dest, mf, norm_final, final):
    T, D = x2.shape
    bm = _pick(T, (256, 128, 64, 32, 16, 8))
    nblk = T // bm
    dest3 = dest.reshape(nblk, 1, TOP_K_INNER * bm)
    return pl.pallas_call(
        functools.partial(_combine_body, bm, final),
        grid=(nblk,),
        in_specs=[
            pl.BlockSpec((None, 1, TOP_K_INNER * bm), lambda i: (i, 0, 0), memory_space=pltpu.SMEM),
            pl.BlockSpec((bm, D), lambda i: (i, 0)),
            pl.BlockSpec((bm, LANES), lambda i: (i, 0)),
            pl.BlockSpec((1, D), lambda i: (0, 0)),
            pl.BlockSpec(memory_space=pl.ANY),
        ],
        out_specs=pl.BlockSpec((bm, D), lambda i: (i, 0)),
        out_shape=jax.ShapeDtypeStruct((T, D), F32),
        scratch_shapes=[
            pltpu.VMEM((bm, D), F32),
            pltpu.VMEM((bm, D), F32),
            pltpu.SemaphoreType.DMA(()),
        ],
        compiler_params=_cparams("arbitrary"),
        name="combine",
    )(dest3, x2, mf, norm_final, yb)


def _rotary_tables(pos):
    half = ROT_DIM // 2
    inv = ROPE_THETA ** (-jnp.arange(0, ROT_DIM, 2, dtype=F32) / ROT_DIM)
    ang = pos.astype(F32)[:, None] * inv[None, :]
    cos = jnp.cos(ang)
    sin = jnp.sin(ang)
    n = pos.shape[0]
    ones = jnp.ones((n, D_A - ROT_DIM), F32)
    zeros = jnp.zeros((n, D_A - ROT_DIM), F32)
    zh = jnp.zeros((n, half), F32)
    cos64 = jnp.concatenate([cos, cos, ones], axis=1)
    sa64 = jnp.concatenate([zh, sin, zeros], axis=1)
    sb64 = jnp.concatenate([-sin, zh, zeros], axis=1)
    reps = LANES // D_A
    return jnp.tile(cos64, (1, reps)), jnp.tile(sa64, (1, reps)), jnp.tile(sb64, (1, reps))


def _moe_slots(mi, counts, n_experts, blk, n_assign):
    eid = mi[:, 0:TOP_K_INNER]
    rank = mi[:, TOP_K_INNER:2 * TOP_K_INNER]
    cnt = counts[0, N_GROUPS:N_GROUPS + n_experts].astype(I32)
    pc = (cnt + blk - 1) // blk * blk
    pend = jnp.cumsum(pc)
    pstart = pend - pc
    dest = pstart[eid] + rank
    n_blocks = -(-n_assign // blk) + n_experts
    blk_exp = jnp.minimum(jnp.searchsorted(pend, jnp.arange(n_blocks, dtype=I32) * blk, side='right'),
                          n_experts - 1).astype(I32)
    n_used = jnp.maximum(pend[-1] // blk, 1).astype(I32).reshape(1)
    return dest.astype(I32), blk_exp, n_used, n_blocks


def _layer(x_all, layer, seg, cache_k, cache_v, state_delta, state_conv, prm):
    (norm_mix, w_in, lq1, lk1, lq2, lk2, subln_w, conv_w, a_log, dt_bias, gdn_norm_w,
     w_proj_a, w_proj_b, w_out, norm_ffn, w_gr, b_gr, w_er, b_er, w_eg, w_eu, w_ed) = prm
    bp, lp, bs, ls, past = seg
    T, D = x_all.shape
    tp = bp * lp
    n_qkv = H_A * 2 * D_A * 2 + H_A * D_VA + 2 * H_B * D_K + 2 * H_B * D_V
    n_small = 2 * H_B
    lam_init = 0.8 - 0.6 * math.exp(-0.3 * layer)

    pos = jnp.concatenate([jnp.tile(jnp.arange(lp), bp), jnp.tile(past + jnp.arange(ls), bs)])
    cos_t, sa_t, sb_t = _rotary_tables(pos)
    w_main = jnp.concatenate([w_in[:, :n_qkv], w_in[:, n_qkv + n_small:]], axis=1).astype(BF16)
    w_small = jnp.pad(w_in[:, n_qkv:n_qkv + n_small], ((0, 0), (0, LANES - n_small))).astype(BF16)
    proj, small, k_new, v_new, q_bf, k_bf, v_t = _inproj(
        x_all, norm_mix.reshape(1, D), cos_t, sa_t, sb_t, w_main, w_small)

    lamp = jnp.stack([lq1, lk1, lq2, lk2]).astype(F32)
    sw = subln_w.reshape(1, D_VA)
    o_a = jnp.zeros((T, H_A * D_VA), BF16)
    o_a = _attn_prompt(q_bf, k_bf, v_t, o_a, lamp, sw, lam_init, bp, lp)
    o_a = _attn_sample(q_bf, k_bf, v_new, o_a, cache_k.reshape(bs, past * H_A, 2 * D_A),
                       cache_v.reshape(bs, past * H_A, D_VA), lamp, sw, lam_init, tp, bs, ls)

    cch = 2 * H_B * D_K + H_B * D_V
    o_b = jnp.zeros((T, H_B * D_V), BF16)
    nw = gdn_norm_w.reshape(1, D_V)
    o_b, s_p, c_p = _gdn(proj, small, o_b, jnp.zeros((bp, H_B, D_K, D_V), F32),
                         jnp.zeros((bp, CONV_W - 1, cch), F32), conv_w, a_log, dt_bias, nw,
                         0, bp, lp, CHUNK)
    o_b, s_s, c_s = _gdn(proj, small, o_b, state_delta, state_conv, conv_w, a_log, dt_bias, nw,
                         tp, bs, ls, ls)

    n_experts = w_er.shape[1]
    w_router = jnp.pad(jnp.concatenate([w_gr, w_er], axis=1), ((0, 0), (0, LANES - N_GROUPS - n_experts)))
    b_router = jnp.pad(jnp.concatenate([b_gr, b_er]), (0, LANES - N_GROUPS - n_experts)).reshape(1, LANES)
    x2, t, logits = _merge(x_all, o_a, o_b, proj, w_proj_a.astype(BF16), w_proj_b.astype(BF16),
                           w_out.astype(BF16), norm_ffn.reshape(1, D), w_router, b_router)

    blk = 128
    mi, mf, counts = _route(logits, n_experts)
    dest, blk_exp, n_used, n_blocks = _moe_slots(mi, counts, n_experts, blk, T * TOP_K_INNER)
    xb = _dispatch(t, dest, jnp.zeros((n_blocks * blk, D), F32))
    yb = _experts(xb, blk_exp, n_used, w_eg.astype(BF16), w_eu.astype(BF16), w_ed.astype(BF16), blk)
    return x2, yb, dest, mf, k_new, v_new, s_p, c_p, s_s, c_s


def kernel(x_prompt, x_sample, cache_k, cache_v, state_delta, state_conv, norm_mix, w_in, lambda_q1, lambda_k1, lambda_q2, lambda_k2, subln_w, conv_w, a_log, dt_bias, gdn_norm_w, w_proj_a, w_proj_b, w_out, norm_ffn, w_group_router, b_group_router, w_expert_router, b_expert_router, w_exp_gate, w_exp_up, w_exp_down, norm_final):
    bp, lp, D = x_prompt.shape
    bs, ls, _ = x_sample.shape
    depth = cache_k.shape[0]
    past = cache_k.shape[2]
    tp = bp * lp
    seg = (bp, lp, bs, ls, past)
    x_all = jnp.concatenate([x_prompt.reshape(tp, D), x_sample.reshape(bs * ls, D)], axis=0)
    kp, vp, sp, cp, ksm, vsm, ssm, csm = [], [], [], [], [], [], [], []
    y_all = None
    for l in range(depth):
        prm = (norm_mix[l], w_in[l], lambda_q1[l], lambda_k1[l], lambda_q2[l], lambda_k2[l], subln_w[l],
               conv_w[l], a_log[l], dt_bias[l], gdn_norm_w[l], w_proj_a[l], w_proj_b[l], w_out[l],
               norm_ffn[l], w_group_router[l], b_group_router[l], w_expert_router[l], b_expert_router[l],
               w_exp_gate[l], w_exp_up[l], w_exp_down[l])
        x2, yb, dest, mf, k_new, v_new, s_p, c_p, s_s, c_s = _layer(
            x_all, l, seg, cache_k[l], cache_v[l], state_delta[l], state_conv[l], prm)
        y_all = _combine(x2, yb, dest, mf, norm_final.reshape(1, D), l == depth - 1)
        x_all = y_all
        kp.append(k_new[:tp].reshape(bp, lp, H_A, 2 * D_A))
        vp.append(v_new[:tp].reshape(bp, lp, H_A, D_VA))
        ksm.append(k_new[tp:].reshape(bs, ls, H_A, 2 * D_A))
        vsm.append(v_new[tp:].reshape(bs, ls, H_A, D_VA))
        sp.append(s_p); cp.append(c_p); ssm.append(s_s); csm.append(c_s)
    y_prompt = y_all[:tp].reshape(bp, lp, D)
    y_sample = y_all[tp:].reshape(bs, ls, D)
    return (y_prompt, y_sample, jnp.stack(kp), jnp.stack(vp), jnp.stack(sp), jnp.stack(cp),
            jnp.stack(ksm), jnp.stack(vsm), jnp.stack(ssm), jnp.stack(csm))
```

```python
import functools
import math

import jax
import jax.numpy as jnp
from jax import lax
from jax.experimental import pallas as pl
from jax.experimental.pallas import tpu as pltpu

F32 = jnp.float32
BF16 = jnp.bfloat16
I32 = jnp.int32

CHUNK = 64
H_A = 8
D_A = 64
D_VA = 2 * D_A
ROT_DIM = D_A // 4
ROPE_THETA = 500000.0
H_B = 8
D_K = 128
D_V = 128
CONV_W = 4
N_GROUPS = 4
EXPERTS_PER_GROUP = 8
TOP_K_INNER = 2
EPS = 1e-6

LANES = 128
SUBLANES = 8
V7X_VMEM_BYTES = 64 * 1024 * 1024
VMEM_LIMIT_BYTES = V7X_VMEM_BYTES - 16 * 1024 * 1024

Q_SCALE = (D_A ** -0.5) * math.log2(math.e)
ONES_ROWS = 16

NT_DIMS = (((1,), (1,)), ((), ()))
TN_DIMS = (((0,), (0,)), ((), ()))


def _pick(n, cands):
    for c in cands:
        if n % c == 0:
            return c
    raise ValueError(f"no block size in {cands} divides {n}")


def _cparams(*sem):
    return pltpu.CompilerParams(dimension_semantics=sem, vmem_limit_bytes=VMEM_LIMIT_BYTES)


def _sigmoid(x):
    return 1.0 / (1.0 + jnp.exp(-x))


def _silu(x):
    return x * _sigmoid(x)


def _softplus(x):
    return jnp.maximum(x, 0.0) + jnp.log1p(jnp.exp(-jnp.abs(x)))


def _bdot(a, b):
    return jnp.dot(a.astype(BF16), b.astype(BF16), preferred_element_type=F32)


def _split2(a):
    hi = a.astype(BF16)
    lo = (a - hi.astype(F32)).astype(BF16)
    return hi, lo


def _dot3(a, b):
    ah, al = _split2(a)
    bh, bl = _split2(b)
    d = functools.partial(jnp.dot, preferred_element_type=F32)
    return d(ah, bh) + d(ah, bl) + d(al, bh)


def _dot_exact_mask(mask, x, mask_is_lhs):
    hi = x.astype(BF16)
    r1 = x - hi.astype(F32)
    mid = r1.astype(BF16)
    lo = (r1 - mid.astype(F32)).astype(BF16)
    mb = mask.astype(BF16)
    d = functools.partial(jnp.dot, preferred_element_type=F32)
    if mask_is_lhs:
        return d(mb, hi) + d(mb, mid) + d(mb, lo)
    return d(hi, mb) + d(mid, mb) + d(lo, mb)


def _inproj_body(x_ref, nw_ref, cos_ref, sa_ref, sb_ref, w_ref, ws_ref,
                 o_ref, os_ref, k_ref, v_ref, qb_ref, kb_ref, vt_ref, h_scr):
    j = pl.program_id(1)

    @pl.when(j == 0)
    def _():
        x = x_ref[...]
        ms = jnp.mean(x * x, axis=-1, keepdims=True)
        h = (x * lax.rsqrt(ms + EPS) * nw_ref[...]).astype(BF16)
        h_scr[...] = h
        os_ref[...] = jnp.dot(h, ws_ref[...], preferred_element_type=F32)

    acc = jnp.dot(h_scr[...], w_ref[...], preferred_element_type=F32)

    @pl.when(j < 2)
    def _():
        cos = cos_ref[...]
        sa = sa_ref[...]
        sb = sb_ref[...]
        for c in range(acc.shape[1] // LANES):
            blk = acc[:, c * LANES:(c + 1) * LANES]
            rot = (blk * cos + pltpu.roll(blk, ROT_DIM // 2, 1) * sa
                   + pltpu.roll(blk, LANES - ROT_DIM // 2, 1) * sb)
            o_ref[:, c * LANES:(c + 1) * LANES] = rot

        @pl.when(j == 0)
        def _():
            qb_ref[...] = (o_ref[...] * Q_SCALE).astype(BF16)

        @pl.when(j == 1)
        def _():
            k_ref[...] = o_ref[...]
            kb_ref[...] = o_ref[...].astype(BF16)

    @pl.when(j >= 2)
    def _():
        o_ref[...] = acc

        @pl.when(j == 2)
        def _():
            v_ref[...] = acc
            for h in range(H_A):
                vt_ref[h, 0:D_VA, :] = acc[:, h * D_VA:(h + 1) * D_VA].T.astype(BF16)
                vt_ref[h, D_VA:D_VA + ONES_ROWS, :] = jnp.ones((ONES_ROWS, acc.shape[0]), BF16)


def _inproj(x_all, norm_w, cos_t, sa_t, sb_t, w_main, w_small):
    T, D = x_all.shape
    n_main = w_main.shape[1]
    bn = H_A * 2 * D_A
    assert n_main % bn == 0 and H_A * D_VA == bn
    bm = _pick(T, (512, 256, 128))
    grid = (T // bm, n_main // bn)
    return pl.pallas_call(
        _inproj_body,
        grid=grid,
        in_specs=[
            pl.BlockSpec((bm, D), lambda i, j: (i, 0)),
            pl.BlockSpec((1, D), lambda i, j: (0, 0)),
            pl.BlockSpec((bm, LANES), lambda i, j: (i, 0)),
            pl.BlockSpec((bm, LANES), lambda i, j: (i, 0)),
            pl.BlockSpec((bm, LANES), lambda i, j: (i, 0)),
            pl.BlockSpec((D, bn), lambda i, j: (0, j)),
            pl.BlockSpec((D, LANES), lambda i, j: (0, 0)),
        ],
        out_specs=[
            pl.BlockSpec((bm, bn), lambda i, j: (i, j)),
            pl.BlockSpec((bm, LANES), lambda i, j: (i, 0)),
            pl.BlockSpec((bm, bn), lambda i, j: (i, 0)),
            pl.BlockSpec((bm, bn), lambda i, j: (i, 0)),
            pl.BlockSpec((bm, bn), lambda i, j: (i, 0)),
            pl.BlockSpec((bm, bn), lambda i, j: (i, 0)),
            pl.BlockSpec((H_A, D_VA + ONES_ROWS, bm), lambda i, j: (0, 0, i)),
        ],
        out_shape=[
            jax.ShapeDtypeStruct((T, n_main), F32),
            jax.ShapeDtypeStruct((T, LANES), F32),
            jax.ShapeDtypeStruct((T, bn), F32),
            jax.ShapeDtypeStruct((T, bn), F32),
            jax.ShapeDtypeStruct((T, bn), BF16),
            jax.ShapeDtypeStruct((T, bn), BF16),
            jax.ShapeDtypeStruct((H_A, D_VA + ONES_ROWS, T), BF16),
        ],
        scratch_shapes=[pltpu.VMEM((bm, D), BF16)],
        compiler_params=_cparams("parallel", "arbitrary"),
        name="inproj",
    )(x_all, norm_w, cos_t, sa_t, sb_t, w_main, w_small)


def _diff_lambda(lamp, lam_init):
    s1 = jnp.sum(lamp[0:1] * lamp[1:2], axis=1, keepdims=True)
    s2 = jnp.sum(lamp[2:3] * lamp[3:4], axis=1, keepdims=True)
    return jnp.exp(s1) - jnp.exp(s2) + lam_init


def _subln(o, w, lam_init):
    ms = jnp.mean(o * o, axis=-1, keepdims=True)
    return o * lax.rsqrt(ms + EPS) * w * (1.0 - lam_init)


def _attn_prompt_body(lam_init, bq, q_ref, k_ref, vt_ref, lamp_ref, swc_ref, oin_ref, o_ref,
                      m_scr, acc_scr, s_scr):
    del oin_ref
    qi = pl.program_id(2)
    q = q_ref[...]
    lane = lax.broadcasted_iota(I32, q.shape, 1)
    qz = [jnp.where(lane < D_A, q, jnp.zeros_like(q)), jnp.where(lane >= D_A, q, jnp.zeros_like(q))]
    m_scr[...] = jnp.full(m_scr.shape, -jnp.inf, F32)
    acc_scr[...] = jnp.zeros(acc_scr.shape, F32)
    key_chunk = lax.broadcasted_iota(I32, (bq, bq), 0) // CHUNK
    qry_chunk = lax.broadcasted_iota(I32, (bq, bq), 1) // CHUNK
    diag_mask = key_chunk <= qry_chunk

    def scores(j, slot):
        kb = k_ref[pl.ds(pl.multiple_of(j * bq, bq), bq), :]
        for m in range(2):
            s_scr[slot, m] = lax.dot_general(kb, qz[m], NT_DIMS, preferred_element_type=F32)

    def accumulate(j, slot, masked):
        vb = vt_ref[:, pl.ds(pl.multiple_of(j * bq, bq), bq)]
        for m in range(2):
            s = s_scr[slot, m]
            if masked:
                s = jnp.where(diag_mask, s, -jnp.inf)
            m_prev = m_scr[m]
            m_new = jnp.maximum(m_prev, jnp.max(s, axis=0, keepdims=True))
            alpha = jnp.exp2(m_prev - m_new)
            p = jnp.exp2(s - m_new).astype(BF16)
            acc_scr[m] = alpha * acc_scr[m] + jnp.dot(vb, p, preferred_element_type=F32)
            m_scr[m] = m_new

    scores(0, 0)

    def two_full_blocks(t, carry):
        j = 2 * t
        scores(j + 1, 1)
        accumulate(j, 0, False)
        scores(j + 2, 0)
        accumulate(j + 1, 1, False)
        return carry

    lax.fori_loop(0, qi // 2, two_full_blocks, 0)
    j0 = 2 * (qi // 2)

    @pl.when(j0 == qi)
    def _():
        accumulate(qi, 0, True)

    @pl.when(j0 != qi)
    def _():
        scores(qi, 1)
        accumulate(j0, 0, False)
        accumulate(qi, 1, True)

    lam = _diff_lambda(lamp_ref[...], lam_init)
    a0 = acc_scr[0]
    a1 = acc_scr[1]
    o_t = a0[0:D_VA] / a0[D_VA:D_VA + 1] - lam * (a1[0:D_VA] / a1[D_VA:D_VA + 1])
    ms = jnp.mean(o_t * o_t, axis=0, keepdims=True)
    o_t = o_t * lax.rsqrt(ms + EPS) * swc_ref[...] * (1.0 - lam_init)
    o_ref[...] = o_t.T.astype(o_ref.dtype)


def _attn_prompt(q_bf, k_bf, v_t, o_a, lamp, subln_w, lam_init, bp, lp):
    bq = _pick(lp, (512, 256, 128))
    nq = lp // bq
    vrows = v_t.shape[1]
    return pl.pallas_call(
        functools.partial(_attn_prompt_body, lam_init, bq),
        grid=(bp, H_A, nq),
        in_specs=[
            pl.BlockSpec((bq, LANES), lambda b, h, i: (b * nq + i, h)),
            pl.BlockSpec((lp, LANES), lambda b, h, i: (b, h)),
            pl.BlockSpec((None, vrows, lp), lambda b, h, i: (h, 0, b)),
            pl.BlockSpec((4, D_A), lambda b, h, i: (0, 0)),
            pl.BlockSpec((D_VA, 1), lambda b, h, i: (0, 0)),
            pl.BlockSpec(memory_space=pl.ANY),
        ],
        out_specs=pl.BlockSpec((bq, LANES), lambda b, h, i: (b * nq + i, h)),
        out_shape=jax.ShapeDtypeStruct(o_a.shape, o_a.dtype),
        scratch_shapes=[
            pltpu.VMEM((2, 1, bq), F32),
            pltpu.VMEM((2, vrows, bq), F32),
            pltpu.VMEM((2, 2, bq, bq), F32),
        ],
        input_output_aliases={5: 0},
        compiler_params=_cparams("parallel", "parallel", "arbitrary"),
        name="attn_prompt",
    )(q_bf, k_bf, v_t, lamp, subln_w.reshape(D_VA, 1), o_a)


def _attn_sample_body(lam_init, q_ref, kn_ref, vn_ref, kc_ref, vc_ref, lamp_ref, sw_ref,
                      oin_ref, o_ref, m_scr, l_scr, acc_scr):
    del oin_ref
    pj = pl.program_id(1)
    npos = kc_ref.shape[0] // H_A
    ls = q_ref.shape[0]

    @pl.when(pj == 0)
    def _():
        m_scr[...] = jnp.full(m_scr.shape, -jnp.inf, F32)
        l_scr[...] = jnp.zeros(l_scr.shape, F32)
        acc_scr[...] = jnp.zeros(acc_scr.shape, F32)

    lane = lax.broadcasted_iota(I32, (ls, D_VA), 1)

    def q_maps(h):
        q = q_ref[:, h * D_VA:(h + 1) * D_VA]
        return [jnp.where(lane < D_A, q, jnp.zeros_like(q)), jnp.where(lane >= D_A, q, jnp.zeros_like(q))]

    def update(idx, qz, keys, vals):
        s = lax.dot_general(qz, keys, NT_DIMS, preferred_element_type=F32)
        m_prev = m_scr[idx]
        m_new = jnp.maximum(m_prev, jnp.max(s, axis=1, keepdims=True))
        alpha = jnp.exp2(m_prev - m_new)
        p = jnp.exp2(s - m_new)
        l_scr[idx] = alpha * l_scr[idx] + jnp.sum(p, axis=1, keepdims=True)
        acc_scr[idx] = alpha * acc_scr[idx] + jnp.dot(p.astype(BF16), vals, preferred_element_type=F32)
        m_scr[idx] = m_new

    for h in range(H_A):
        kc = kc_ref[pl.ds(h, npos, stride=H_A), :].astype(BF16)
        vc = vc_ref[pl.ds(h, npos, stride=H_A), :].astype(BF16)
        qz = q_maps(h)
        for m in range(2):
            update(2 * h + m, qz[m], kc, vc)

    @pl.when(pj == pl.num_programs(1) - 1)
    def _():
        lam = _diff_lambda(lamp_ref[...], lam_init)
        sw = sw_ref[...]
        for h in range(H_A):
            kn = kn_ref[:, h * D_VA:(h + 1) * D_VA]
            vn = vn_ref[:, h * D_VA:(h + 1) * D_VA].astype(BF16)
            qz = q_maps(h)
            outs = []
            for m in range(2):
                update(2 * h + m, qz[m], kn, vn)
                outs.append(acc_scr[2 * h + m] / l_scr[2 * h + m])
            o = outs[0] - lam * outs[1]
            o_ref[:, h * D_VA:(h + 1) * D_VA] = _subln(o, sw, lam_init).astype(o_ref.dtype)


def _attn_sample(q_bf, k_bf, v_new, o_a, cache_k, cache_v, lamp, subln_w, lam_init, row0, bs, ls):
    past = cache_k.shape[1] // H_A
    pb = _pick(past, (1024, 512, 256, 128, 64))
    wcol = H_A * D_VA
    r0 = row0 // ls
    assert row0 % ls == 0 and ls % (2 * SUBLANES) == 0
    return pl.pallas_call(
        functools.partial(_attn_sample_body, lam_init),
        grid=(bs, past // pb),
        in_specs=[
            pl.BlockSpec((ls, wcol), lambda b, j: (r0 + b, 0)),
            pl.BlockSpec((ls, wcol), lambda b, j: (r0 + b, 0)),
            pl.BlockSpec((ls, wcol), lambda b, j: (r0 + b, 0)),
            pl.BlockSpec((None, pb * H_A, D_VA), lambda b, j: (b, j, 0)),
            pl.BlockSpec((None, pb * H_A, D_VA), lambda b, j: (b, j, 0)),
            pl.BlockSpec((4, D_A), lambda b, j: (0, 0)),
            pl.BlockSpec((1, D_VA), lambda b, j: (0, 0)),
            pl.BlockSpec(memory_space=pl.ANY),
        ],
        out_specs=pl.BlockSpec((ls, wcol), lambda b, j: (r0 + b, 0)),
        out_shape=jax.ShapeDtypeStruct(o_a.shape, o_a.dtype),
        scratch_shapes=[
            pltpu.VMEM((2 * H_A, ls, 1), F32),
            pltpu.VMEM((2 * H_A, ls, 1), F32),
            pltpu.VMEM((2 * H_A, ls, D_VA), F32),
        ],
        input_output_aliases={7: 0},
        compiler_params=_cparams("parallel", "arbitrary"),
        name="attn_sample",
    )(q_bf, k_bf, v_new, cache_k, cache_v, lamp, subln_w, o_a)


def _gdn_body(chunk, rows, pq_ref, pk_ref, pv_ref, pz_ref, sm_ref, smt_ref, cw_ref, alr_ref, dtr_ref,
              alc_ref, dtc_ref, nw_ref, s0_ref, c0_ref, oin_ref, o_ref, s_ref, cn_ref, ext_scr, act_scr):
    del oin_ref
    i = pl.program_id(1)
    nblk = pl.num_programs(1)
    hk = H_B * D_K
    hv = H_B * D_V
    n_chunks = rows // chunk
    n_steps = int(math.log2(chunk))
    assert 2 ** n_steps == chunk

    @pl.when(i == 0)
    def _():
        s_ref[...] = s0_ref[...]
        ext_scr[0:SUBLANES, :] = jnp.zeros((SUBLANES, ext_scr.shape[1]), F32)
        ext_scr[pl.ds(SUBLANES - (CONV_W - 1), CONV_W - 1), :] = c0_ref[...]

    ext_scr[pl.ds(SUBLANES, rows), 0:hk] = pq_ref[...]
    ext_scr[pl.ds(SUBLANES, rows), hk:2 * hk] = pk_ref[...]
    ext_scr[pl.ds(SUBLANES, rows), 2 * hk:2 * hk + hv] = pv_ref[...]

    @pl.when(i == nblk - 1)
    def _():
        cn_ref[...] = ext_scr[pl.ds(SUBLANES + rows - (CONV_W - 1), CONV_W - 1), :]

    conv = jnp.zeros((rows, ext_scr.shape[1]), F32)
    for w in range(CONV_W):
        conv = conv + ext_scr[pl.ds(SUBLANES - (CONV_W - 1) + w, rows), :] * cw_ref[w:w + 1, :]
    act_scr[...] = _silu(conv)
    ext_scr[0:SUBLANES, :] = ext_scr[pl.ds(rows, SUBLANES), :]

    sm = sm_ref[...]
    beta_all = _sigmoid(sm[:, 0:H_B])
    g_all = -jnp.exp(alr_ref[...]) * _softplus(sm[:, H_B:2 * H_B] + dtr_ref[...])
    smt = smt_ref[...]
    g_all_t = -jnp.exp(alc_ref[...]) * _softplus(smt[H_B:2 * H_B, :] + dtc_ref[...])

    ri = lax.broadcasted_iota(I32, (chunk, chunk), 0)
    ci = lax.broadcasted_iota(I32, (chunk, chunk), 1)
    tri = ri >= ci
    strict = ri > ci
    eye = (ri == ci).astype(F32)
    ltri = tri.astype(F32)
    utri = (ri <= ci).astype(F32)
    nw = nw_ref[...]

    units = [(c, h) for c in range(n_chunks) for h in range(H_B)]
    gcs = [_dot_exact_mask(ltri, g_all[c * chunk:(c + 1) * chunk, :], True) for c in range(n_chunks)]
    grs = [_dot_exact_mask(utri, g_all_t[:, c * chunk:(c + 1) * chunk], False) for c in range(n_chunks)]
    pre = []
    for c, h in units:
        r0 = c * chunk
        q = act_scr[r0:r0 + chunk, h * D_K:(h + 1) * D_K]
        k = act_scr[r0:r0 + chunk, hk + h * D_K:hk + (h + 1) * D_K]
        v = act_scr[r0:r0 + chunk, 2 * hk + h * D_V:2 * hk + (h + 1) * D_V]
        q = q * lax.rsqrt(jnp.sum(q * q, axis=-1, keepdims=True) + EPS) * (D_K ** -0.5)
        k = k * lax.rsqrt(jnp.sum(k * k, axis=-1, keepdims=True) + EPS)
        beta = beta_all[r0:r0 + chunk, h:h + 1]
        gc = gcs[c][:, h:h + 1]
        gr = grs[c][h:h + 1, :]
        g_last = gc[chunk - 1:chunk, :]
        decay = jnp.where(tri, jnp.exp(jnp.where(tri, gc - gr, 0.0)), 0.0)
        kb = k * beta
        eg = jnp.exp(gc)
        kbf = k.astype(BF16)
        kk = lax.dot_general(kb.astype(BF16), kbf, NT_DIMS, preferred_element_type=F32)
        qk = lax.dot_general(q.astype(BF16), kbf, NT_DIMS, preferred_element_type=F32)
        pre.append(dict(
            nm=-jnp.where(strict, kk * decay, 0.0),
            rhs=jnp.concatenate([v * beta, kb * eg], axis=1),
            qk=jnp.where(tri, qk * decay, 0.0).astype(BF16),
            qg=(q * eg).astype(BF16),
            kg=(k * jnp.exp(g_last - gc)).astype(BF16),
            gl=jnp.exp(g_last)))
    invs = [eye + p["nm"] for p in pre]
    pws = [p["nm"] for p in pre]
    for _ in range(n_steps - 1):
        pws = [_bdot(pw, pw) for pw in pws]
        invs = [inv + _bdot(inv, pw) for inv, pw in zip(invs, pws)]
    resid = [eye - _dot3(eye - p["nm"], inv) for p, inv in zip(pre, invs)]
    invs = [inv + _bdot(inv, r) for inv, r in zip(invs, resid)]
    uws = [_dot3(inv, p["rhs"]) for p, inv in zip(pre, invs)]

    for (c, h), p, uw in zip(units, pre, uws):
        r0 = c * chunk
        u = uw[:, 0:D_V]
        wmat = uw[:, D_V:D_V + D_K]
        s = s_ref[h]
        sb = s.astype(BF16)
        v_new = u - jnp.dot(wmat.astype(BF16), sb, preferred_element_type=F32)
        v_new_b = v_new.astype(BF16)
        o = (jnp.dot(p["qg"], sb, preferred_element_type=F32)
             + jnp.dot(p["qk"], v_new_b, preferred_element_type=F32))
        s_ref[h] = s * p["gl"] + lax.dot_general(p["kg"], v_new_b, TN_DIMS, preferred_element_type=F32)
        z = pz_ref[r0:r0 + chunk, h * D_V:(h + 1) * D_V]
        o = o * lax.rsqrt(jnp.mean(o * o, axis=-1, keepdims=True) + EPS) * nw * _silu(z)
        o_ref[r0:r0 + chunk, h * D_V:(h + 1) * D_V] = o.astype(o_ref.dtype)


def _gdn(proj, small, o_b, s0, c0, conv_w, a_log, dt_bias, norm_w, row0, bn, ln, chunk):
    hk = H_B * D_K
    cch = 2 * hk + H_B * D_V
    rows = ln
    for cand in (2 * chunk, chunk):
        if cand % LANES == 0 and ln % cand == 0:
            rows = cand
            break
    assert rows % chunk == 0 and rows % SUBLANES == 0 and row0 % rows == 0
    nblk = ln // rows
    r0 = row0 // rows
    qcol = (H_A * 2 * D_A * 2 + H_A * D_VA) // hk
    small_t = jnp.swapaxes(small[row0:row0 + bn * ln, 0:2 * H_B].reshape(bn, ln, 2 * H_B), 1, 2)
    alr = a_log.reshape(1, H_B)
    dtr = dt_bias.reshape(1, H_B)
    alc = a_log.reshape(H_B, 1)
    dtc = dt_bias.reshape(H_B, 1)
    row_blk = lambda b, i: (r0 + b * nblk + i)
    return pl.pallas_call(
        functools.partial(_gdn_body, chunk, rows),
        grid=(bn, nblk),
        in_specs=[
            pl.BlockSpec((rows, hk), lambda b, i: (row_blk(b, i), qcol)),
            pl.BlockSpec((rows, hk), lambda b, i: (row_blk(b, i), qcol + 1)),
            pl.BlockSpec((rows, hk), lambda b, i: (row_blk(b, i), qcol + 2)),
            pl.BlockSpec((rows, hk), lambda b, i: (row_blk(b, i), qcol + 3)),
            pl.BlockSpec((rows, LANES), lambda b, i: (row_blk(b, i), 0)),
            pl.BlockSpec((None, 2 * H_B, rows), lambda b, i: (b, 0, i)),
            pl.BlockSpec((CONV_W, cch), lambda b, i: (0, 0)),
            pl.BlockSpec((1, H_B), lambda b, i: (0, 0)),
            pl.BlockSpec((1, H_B), lambda b, i: (0, 0)),
            pl.BlockSpec((H_B, 1), lambda b, i: (0, 0)),
            pl.BlockSpec((H_B, 1), lambda b, i: (0, 0)),
            pl.BlockSpec((1, D_V), lambda b, i: (0, 0)),
            pl.BlockSpec((None, H_B, D_K, D_V), lambda b, i: (b, 0, 0, 0)),
            pl.BlockSpec((None, CONV_W - 1, cch), lambda b, i: (b, 0, 0)),
            pl.BlockSpec(memory_space=pl.ANY),
        ],
        out_specs=[
            pl.BlockSpec((rows, hk), lambda b, i: (row_blk(b, i), 0)),
            pl.BlockSpec((None, H_B, D_K, D_V), lambda b, i: (b, 0, 0, 0)),
            pl.BlockSpec((None, CONV_W - 1, cch), lambda b, i: (b, 0, 0)),
        ],
        out_shape=[
            jax.ShapeDtypeStruct(o_b.shape, o_b.dtype),
            jax.ShapeDtypeStruct(s0.shape, F32),
            jax.ShapeDtypeStruct(c0.shape, F32),
        ],
        scratch_shapes=[
            pltpu.VMEM((rows + SUBLANES, cch), F32),
            pltpu.VMEM((rows, cch), F32),
        ],
        input_output_aliases={14: 0},
        compiler_params=_cparams("parallel", "arbitrary"),
        name="gdn",
    )(proj, proj, proj, proj, small, small_t, conv_w, alr, dtr, alc, dtc, norm_w, s0, c0, o_b)


def _merge_body(x_ref, oa_ref, ob_ref, ga0_ref, ga1_ref, gb0_ref, gb1_ref, wa_ref, wb_ref, wo_ref,
                nf_ref, wr_ref, br_ref, x2_ref, t_ref, lg_ref):
    ya = jnp.dot(oa_ref[...], wa_ref[...], preferred_element_type=F32)
    yb = jnp.dot(ob_ref[...], wb_ref[...], preferred_element_type=F32)
    half = ga0_ref.shape[1]
    m0 = _sigmoid(ga0_ref[...]) * ya[:, :half] + _sigmoid(gb0_ref[...]) * yb[:, :half]
    m1 = _sigmoid(ga1_ref[...]) * ya[:, half:] + _sigmoid(gb1_ref[...]) * yb[:, half:]
    merged = jnp.concatenate([m0, m1], axis=1).astype(BF16)
    x2 = x_ref[...] + jnp.dot(merged, wo_ref[...], preferred_element_type=F32)
    x2_ref[...] = x2
    ms = jnp.mean(x2 * x2, axis=-1, keepdims=True)
    t = x2 * lax.rsqrt(ms + EPS) * nf_ref[...]
    t_ref[...] = t
    lg_ref[...] = _dot3(t, wr_ref[...]) + br_ref[...]


def _merge(x_all, o_a, o_b, proj, wa, wb, wo, norm_ffn, w_router, b_router):
    T, D = x_all.shape
    half = D // 2
    gcol = (H_A * 2 * D_A * 2 + H_A * D_VA + 2 * H_B * D_K + 2 * H_B * D_V) // half
    bm = _pick(T, (256, 128, 64, 32, 16, 8))
    const = dict(pipeline_mode=pl.Buffered(1))
    return pl.pallas_call(
        _merge_body,
        grid=(T // bm,),
        in_specs=[
            pl.BlockSpec((bm, D), lambda i: (i, 0)),
            pl.BlockSpec((bm, o_a.shape[1]), lambda i: (i, 0)),
            pl.BlockSpec((bm, o_b.shape[1]), lambda i: (i, 0)),
            pl.BlockSpec((bm, half), lambda i: (i, gcol)),
            pl.BlockSpec((bm, half), lambda i: (i, gcol + 1)),
            pl.BlockSpec((bm, half), lambda i: (i, gcol + 2)),
            pl.BlockSpec((bm, half), lambda i: (i, gcol + 3)),
            pl.BlockSpec(wa.shape, lambda i: (0, 0), **const),
            pl.BlockSpec(wb.shape, lambda i: (0, 0), **const),
            pl.BlockSpec(wo.shape, lambda i: (0, 0), **const),
            pl.BlockSpec((1, D), lambda i: (0, 0)),
            pl.BlockSpec((D, LANES), lambda i: (0, 0), **const),
            pl.BlockSpec((1, LANES), lambda i: (0, 0)),
        ],
        out_specs=[
            pl.BlockSpec((bm, D), lambda i: (i, 0)),
            pl.BlockSpec((bm, D), lambda i: (i, 0)),
            pl.BlockSpec((bm, LANES), lambda i: (i, 0)),
        ],
        out_shape=[
            jax.ShapeDtypeStruct((T, D), F32),
            jax.ShapeDtypeStruct((T, D), F32),
            jax.ShapeDtypeStruct((T, LANES), F32),
        ],
        compiler_params=_cparams("parallel"),
        name="merge",
    )(x_all, o_a, o_b, proj, proj, proj, proj, wa, wb, wo, norm_ffn, w_router, b_router)


def _route_body(n_experts, lg_ref, mi_ref, mf_ref, cnt_ref):
    i = pl.program_id(0)

    @pl.when(i == 0)
    def _():
        cnt_ref[...] = jnp.zeros(cnt_ref.shape, F32)

    lg = lg_ref[...]
    bm = lg.shape[0]
    lane = lax.broadcasted_iota(I32, lg.shape, 1).astype(F32)
    big = jnp.float32(LANES)
    neg = -jnp.inf
    gl = jnp.where(lane < N_GROUPS, lg, neg)
    gmax = jnp.max(gl, axis=1, keepdims=True)
    gidx = jnp.min(jnp.where(gl == gmax, lane, big), axis=1, keepdims=True)
    gw = 1.0 / jnp.sum(jnp.exp(gl - gmax), axis=1, keepdims=True)
    e_lo = N_GROUPS + gidx * EXPERTS_PER_GROUP
    valid = (lane >= e_lo) & (lane < e_lo + EXPERTS_PER_GROUP)
    el = jnp.where(valid, lg, neg)
    v1 = jnp.max(el, axis=1, keepdims=True)
    i1 = jnp.min(jnp.where(el == v1, lane, big), axis=1, keepdims=True)
    el2 = jnp.where(lane == i1, neg, el)
    v2 = jnp.max(el2, axis=1, keepdims=True)
    i2 = jnp.min(jnp.where(el2 == v2, lane, big), axis=1, keepdims=True)
    e21 = jnp.exp(v2 - v1)
    w1 = gw / (1.0 + e21)
    w2 = gw * e21 / (1.0 + e21)
    oh1 = (lane == i1).astype(F32)
    oh2 = (lane == i2).astype(F32)
    oh = oh1 + oh2
    rr = lax.broadcasted_iota(I32, (bm, bm), 0)
    cc = lax.broadcasted_iota(I32, (bm, bm), 1)
    before = (cc < rr).astype(BF16)
    cum = jnp.dot(before, oh.astype(BF16), preferred_element_type=F32) + cnt_ref[...]
    rank1 = jnp.sum(cum * oh1, axis=1, keepdims=True)
    rank2 = jnp.sum(cum * oh2, axis=1, keepdims=True)
    cnt_ref[...] = cnt_ref[...] + jnp.sum(oh, axis=0, keepdims=True)
    mi = jnp.where(lane == 0, i1 - N_GROUPS, 0.0)
    mi = jnp.where(lane == 1, i2 - N_GROUPS, mi)
    mi = jnp.where(lane == 2, rank1, mi)
    mi = jnp.where(lane == 3, rank2, mi)
    mi_ref[...] = mi.astype(I32)
    mf_ref[...] = jnp.where(lane == 0, w1, jnp.where(lane == 1, w2, 0.0))
    del n_experts


def _route(logits, n_experts):
    T = logits.shape[0]
    bm = _pick(T, (512, 256, 128, 64, 32, 16, 8))
    return pl.pallas_call(
        functools.partial(_route_body, n_experts),
        grid=(T // bm,),
        in_specs=[pl.BlockSpec((bm, LANES), lambda i: (i, 0))],
        out_specs=[
            pl.BlockSpec((bm, LANES), lambda i: (i, 0)),
            pl.BlockSpec((bm, LANES), lambda i: (i, 0)),
            pl.BlockSpec((1, LANES), lambda i: (0, 0)),
        ],
        out_shape=[
            jax.ShapeDtypeStruct((T, LANES), I32),
            jax.ShapeDtypeStruct((T, LANES), F32),
            jax.ShapeDtypeStruct((1, LANES), F32),
        ],
        compiler_params=_cparams("arbitrary"),
        name="route",
    )(logits)


def _dispatch_body(bm, blk, n_experts, ps_ref, pn_ref, nu_ref, dest_ref, t_ref, xb_ref,
                   zero_scr, sem, zsem):
    i = pl.program_id(0)
    n_blocks = xb_ref.shape[0] // blk

    def pad_rows(act):
        def per_expert(e, carry):
            def per_row(r, c2):
                act(pltpu.make_async_copy(zero_scr.at[pl.ds(0, 1)], xb_ref.at[pl.ds(ps_ref[e] + r, 1)], zsem))
                return c2
            return lax.fori_loop(0, pn_ref[e], per_row, carry)
        lax.fori_loop(0, n_experts, per_expert, 0)

    def tail_blocks(act):
        def per_block(b, carry):
            act(pltpu.make_async_copy(zero_scr, xb_ref.at[pl.ds(pl.multiple_of(b * blk, blk), blk)], zsem))
            return carry
        lax.fori_loop(nu_ref[0], n_blocks, per_block, 0)

    @pl.when(i == 0)
    def _():
        zero_scr[...] = jnp.zeros(zero_scr.shape, zero_scr.dtype)
        pad_rows(lambda cp: cp.start())
        tail_blocks(lambda cp: cp.start())

    def row_copy(r, d):
        return pltpu.make_async_copy(t_ref.at[pl.ds(r, 1)], xb_ref.at[pl.ds(d, 1)], sem)

    def issue(r, carry):
        for k in range(TOP_K_INNER):
            row_copy(r, dest_ref[0, TOP_K_INNER * r + k]).start()
        return carry

    lax.fori_loop(0, bm, issue, 0)

    def drain(r, carry):
        for k in range(TOP_K_INNER):
            row_copy(r, dest_ref[0, TOP_K_INNER * r + k]).wait()
        return carry

    lax.fori_loop(0, bm, drain, 0)

    @pl.when(i == 0)
    def _():
        pad_rows(lambda cp: cp.wait())
        tail_blocks(lambda cp: cp.wait())


def _dispatch(t, dest, pad_start, pad_len, n_used, n_rows, blk):
    T, D = t.shape
    bm = _pick(T, (256, 128, 64, 32, 16, 8))
    nblk = T // bm
    dest3 = dest.reshape(nblk, 1, TOP_K_INNER * bm)
    grid_spec = pltpu.PrefetchScalarGridSpec(
        num_scalar_prefetch=3,
        grid=(nblk,),
        in_specs=[
            pl.BlockSpec((None, 1, TOP_K_INNER * bm), lambda i, ps, pn, nu: (i, 0, 0),
                         memory_space=pltpu.SMEM),
            pl.BlockSpec((bm, D), lambda i, ps, pn, nu: (i, 0)),
        ],
        out_specs=pl.BlockSpec(memory_space=pl.ANY),
        scratch_shapes=[
            pltpu.VMEM((blk, D), t.dtype),
            pltpu.SemaphoreType.DMA(()),
            pltpu.SemaphoreType.DMA(()),
        ],
    )
    return pl.pallas_call(
        functools.partial(_dispatch_body, bm, blk, pad_start.shape[0]),
        grid_spec=grid_spec,
        out_shape=jax.ShapeDtypeStruct((n_rows, D), t.dtype),
        compiler_params=_cparams("arbitrary"),
        name="dispatch",
    )(pad_start, pad_len, n_used, dest3, t)


def _expert_body(be_ref, nu_ref, x_ref, wg_ref, wu_ref, wd_ref, y_ref, wg_scr, wu_scr, wd_scr):
    i = pl.program_id(0)

    @pl.when(i < nu_ref[0])
    def _():
        @pl.when(jnp.logical_or(i == 0, be_ref[i] != be_ref[jnp.maximum(i - 1, 0)]))
        def _():
            wg_scr[...] = wg_ref[...].astype(BF16)
            wu_scr[...] = wu_ref[...].astype(BF16)
            wd_scr[...] = wd_ref[...].astype(BF16)

        x = x_ref[...].astype(BF16)
        g = jnp.dot(x, wg_scr[...], preferred_element_type=F32)
        u = jnp.dot(x, wu_scr[...], preferred_element_type=F32)
        hmid = (_silu(g) * u).astype(BF16)
        y_ref[...] = jnp.dot(hmid, wd_scr[...], preferred_element_type=F32)

    @pl.when(i >= nu_ref[0])
    def _():
        y_ref[...] = jnp.zeros(y_ref.shape, F32)


def _experts(xb, blk_exp, n_used, w_gate, w_up, w_down, blk):
    P, D = xb.shape
    de = w_gate.shape[2]
    n_blocks = P // blk
    last = lambda i, nu: jnp.minimum(i, nu[0] - 1)
    grid_spec = pltpu.PrefetchScalarGridSpec(
        num_scalar_prefetch=2,
        grid=(n_blocks,),
        in_specs=[
            pl.BlockSpec((blk, D), lambda i, be, nu: (last(i, nu), 0)),
            pl.BlockSpec((None, D, de), lambda i, be, nu: (be[last(i, nu)], 0, 0)),
            pl.BlockSpec((None, D, de), lambda i, be, nu: (be[last(i, nu)], 0, 0)),
            pl.BlockSpec((None, de, D), lambda i, be, nu: (be[last(i, nu)], 0, 0)),
        ],
        out_specs=pl.BlockSpec((blk, D), lambda i, be, nu: (i, 0)),
        scratch_shapes=[
            pltpu.VMEM((D, de), BF16),
            pltpu.VMEM((D, de), BF16),
            pltpu.VMEM((de, D), BF16),
        ],
    )
    return pl.pallas_call(
        _expert_body,
        grid_spec=grid_spec,
        out_shape=jax.ShapeDtypeStruct((P, D), F32),
        compiler_params=_cparams("arbitrary"),
        name="experts",
    )(blk_exp, n_used, xb, w_gate, w_up, w_down)


def _combine_body(bm, final, dest_ref, x2_ref, mf_ref, nw_ref, yb_ref, o_ref, y0_scr, y1_scr, sem):
    bufs = (y0_scr, y1_scr)

    def row_copy(r, k):
        d = dest_ref[0, TOP_K_INNER * r + k]
        return pltpu.make_async_copy(yb_ref.at[pl.ds(d, 1)], bufs[k].at[pl.ds(r, 1)], sem)

    def issue(r, carry):
        for k in range(TOP_K_INNER):
            row_copy(r, k).start()
        return carry

    lax.fori_loop(0, bm, issue, 0)

    def drain(r, carry):
        for k in range(TOP_K_INNER):
            row_copy(r, k).wait()
        return carry

    lax.fori_loop(0, bm, drain, 0)

    mf = mf_ref[...]
    x3 = x2_ref[...] + y0_scr[...] * mf[:, 0:1] + y1_scr[...] * mf[:, 1:2]
    if final:
        ms = jnp.mean(x3 * x3, axis=-1, keepdims=True)
        x3 = x3 * lax.rsqrt(ms + EPS) * nw_ref[...]
    o_ref[...] = x3


def _combine(x2, yb, dest, mf, norm_final, final):
    T, D = x2.shape
    bm = _pick(T, (256, 128, 64, 32, 16, 8))
    nblk = T // bm
    dest3 = dest.reshape(nblk, 1, TOP_K_INNER * bm)
    return pl.pallas_call(
        functools.partial(_combine_body, bm, final),
        grid=(nblk,),
        in_specs=[
            pl.BlockSpec((None, 1, TOP_K_INNER * bm), lambda i: (i, 0, 0), memory_space=pltpu.SMEM),
            pl.BlockSpec((bm, D), lambda i: (i, 0)),
            pl.BlockSpec((bm, LANES), lambda i: (i, 0)),
            pl.BlockSpec((1, D), lambda i: (0, 0)),
            pl.BlockSpec(memory_space=pl.ANY),
        ],
        out_specs=pl.BlockSpec((bm, D), lambda i: (i, 0)),
        out_shape=jax.ShapeDtypeStruct((T, D), F32),
        scratch_shapes=[
            pltpu.VMEM((bm, D), F32),
            pltpu.VMEM((bm, D), F32),
            pltpu.SemaphoreType.DMA(()),
        ],
        compiler_params=_cparams("arbitrary"),
        name="combine",
    )(dest3, x2, mf, norm_final, yb)


def _rotary_tables(pos):
    half = ROT_DIM // 2
    inv = ROPE_THETA ** (-jnp.arange(0, ROT_DIM, 2, dtype=F32) / ROT_DIM)
    ang = pos.astype(F32)[:, None] * inv[None, :]
    cos = jnp.cos(ang)
    sin = jnp.sin(ang)
    n = pos.shape[0]
    ones = jnp.ones((n, D_A - ROT_DIM), F32)
    zeros = jnp.zeros((n, D_A - ROT_DIM), F32)
    zh = jnp.zeros((n, half), F32)
    cos64 = jnp.concatenate([cos, cos, ones], axis=1)
    sa64 = jnp.concatenate([zh, sin, zeros], axis=1)
    sb64 = jnp.concatenate([-sin, zh, zeros], axis=1)
    reps = LANES // D_A
    return jnp.tile(cos64, (1, reps)), jnp.tile(sa64, (1, reps)), jnp.tile(sb64, (1, reps))


def _moe_slots(mi, counts, n_experts, blk, n_assign):
    eid = mi[:, 0:TOP_K_INNER]
    rank = mi[:, TOP_K_INNER:2 * TOP_K_INNER]
    cnt = counts[0, N_GROUPS:N_GROUPS + n_experts].astype(I32)
    pc = (cnt + blk - 1) // blk * blk
    pend = jnp.cumsum(pc)
    pstart = pend - pc
    dest = pstart[eid] + rank
    n_blocks = -(-n_assign // blk) + n_experts
    blk_start = jnp.arange(n_blocks, dtype=I32) * blk
    blk_exp = jnp.minimum(jnp.sum((pend[None, :] <= blk_start[:, None]).astype(I32), axis=1), n_experts - 1)
    n_used = jnp.maximum(pend[-1] // blk, 1).astype(I32).reshape(1)
    return dest.astype(I32), blk_exp.astype(I32), n_used, n_blocks, pstart + cnt, pc - cnt


def _layer(x_all, layer, seg, cache_k, cache_v, state_delta, state_conv, prm):
    (norm_mix, w_in, lq1, lk1, lq2, lk2, subln_w, conv_w, a_log, dt_bias, gdn_norm_w,
     w_proj_a, w_proj_b, w_out, norm_ffn, w_gr, b_gr, w_er, b_er, w_eg, w_eu, w_ed) = prm
    bp, lp, bs, ls, past = seg
    T, D = x_all.shape
    tp = bp * lp
    n_qkv = H_A * 2 * D_A * 2 + H_A * D_VA + 2 * H_B * D_K + 2 * H_B * D_V
    n_small = 2 * H_B
    lam_init = 0.8 - 0.6 * math.exp(-0.3 * layer)

    pos = jnp.concatenate([jnp.tile(jnp.arange(lp), bp), jnp.tile(past + jnp.arange(ls), bs)])
    cos_t, sa_t, sb_t = _rotary_tables(pos)
    w_main = jnp.concatenate([w_in[:, :n_qkv], w_in[:, n_qkv + n_small:]], axis=1).astype(BF16)
    w_small = jnp.pad(w_in[:, n_qkv:n_qkv + n_small], ((0, 0), (0, LANES - n_small))).astype(BF16)
    proj, small, k_new, v_new, q_bf, k_bf, v_t = _inproj(
        x_all, norm_mix.reshape(1, D), cos_t, sa_t, sb_t, w_main, w_small)

    lamp = jnp.stack([lq1, lk1, lq2, lk2]).astype(F32)
    sw = subln_w.reshape(1, D_VA)
    o_a = jnp.zeros((T, H_A * D_VA), BF16)
    o_a = _attn_prompt(q_bf, k_bf, v_t, o_a, lamp, sw, lam_init, bp, lp)
    o_a = _attn_sample(q_bf, k_bf, v_new, o_a, cache_k.reshape(bs, past * H_A, 2 * D_A),
                       cache_v.reshape(bs, past * H_A, D_VA), lamp, sw, lam_init, tp, bs, ls)

    cch = 2 * H_B * D_K + H_B * D_V
    o_b = jnp.zeros((T, H_B * D_V), BF16)
    nw = gdn_norm_w.reshape(1, D_V)
    o_b, s_p, c_p = _gdn(proj, small, o_b, jnp.zeros((bp, H_B, D_K, D_V), F32),
                         jnp.zeros((bp, CONV_W - 1, cch), F32), conv_w, a_log, dt_bias, nw,
                         0, bp, lp, CHUNK)
    o_b, s_s, c_s = _gdn(proj, small, o_b, state_delta, state_conv, conv_w, a_log, dt_bias, nw,
                         tp, bs, ls, ls)

    n_experts = w_er.shape[1]
    w_router = jnp.pad(jnp.concatenate([w_gr, w_er], axis=1), ((0, 0), (0, LANES - N_GROUPS - n_experts)))
    b_router = jnp.pad(jnp.concatenate([b_gr, b_er]), (0, LANES - N_GROUPS - n_experts)).reshape(1, LANES)
    x2, t, logits = _merge(x_all, o_a, o_b, proj, w_proj_a.astype(BF16), w_proj_b.astype(BF16),
                           w_out.astype(BF16), norm_ffn.reshape(1, D), w_router, b_router)

    blk = 128
    mi, mf, counts = _route(logits, n_experts)
    dest, blk_exp, n_used, n_blocks, pad_start, pad_len = _moe_slots(
        mi, counts, n_experts, blk, T * TOP_K_INNER)
    xb = _dispatch(t, dest, pad_start.astype(I32), pad_len.astype(I32), n_used, n_blocks * blk, blk)
    yb = _experts(xb, blk_exp, n_used, w_eg, w_eu, w_ed, blk)
    return x2, yb, dest, mf, k_new, v_new, s_p, c_p, s_s, c_s


def kernel(x_prompt, x_sample, cache_k, cache_v, state_delta, state_conv, norm_mix, w_in, lambda_q1, lambda_k1, lambda_q2, lambda_k2, subln_w, conv_w, a_log, dt_bias, gdn_norm_w, w_proj_a, w_proj_b, w_out, norm_ffn, w_group_router, b_group_router, w_expert_router, b_expert_router, w_exp_gate, w_exp_up, w_exp_down, norm_final):
    bp, lp, D = x_prompt.shape
    bs, ls, _ = x_sample.shape
    depth = cache_k.shape[0]
    past = cache_k.shape[2]
    tp = bp * lp
    seg = (bp, lp, bs, ls, past)
    x_all = jnp.concatenate([x_prompt.reshape(tp, D), x_sample.reshape(bs * ls, D)], axis=0)
    kp, vp, sp, cp, ksm, vsm, ssm, csm = [], [], [], [], [], [], [], []
    y_all = None
    for l in range(depth):
        prm = (norm_mix[l], w_in[l], lambda_q1[l], lambda_k1[l], lambda_q2[l], lambda_k2[l], subln_w[l],
               conv_w[l], a_log[l], dt_bias[l], gdn_norm_w[l], w_proj_a[l], w_proj_b[l], w_out[l],
               norm_ffn[l], w_group_router[l], b_group_router[l], w_expert_router[l], b_expert_router[l],
               w_exp_gate[l], w_exp_up[l], w_exp_down[l])
        x2, yb, dest, mf, k_new, v_new, s_p, c_p, s_s, c_s = _layer(
            x_all, l, seg, cache_k[l], cache_v[l], state_delta[l], state_conv[l], prm)
        y_all = _combine(x2, yb, dest, mf, norm_final.reshape(1, D), l == depth - 1)
        x_all = y_all
        kp.append(k_new[:tp].reshape(bp, lp, H_A, 2 * D_A))
        vp.append(v_new[:tp].reshape(bp, lp, H_A, D_VA))
        ksm.append(k_new[tp:].reshape(bs, ls, H_A, 2 * D_A))
        vsm.append(v_new[tp:].reshape(bs, ls, H_A, D_VA))
        sp.append(s_p); cp.append(c_p); ssm.append(s_s); csm.append(c_s)
    y_prompt = y_all[:tp].reshape(bp, lp, D)
    y_sample = y_all[tp:].reshape(bs, ls, D)
    return (y_prompt, y_sample, jnp.stack(kp), jnp.stack(vp), jnp.stack(sp), jnp.stack(cp),
            jnp.stack(ksm), jnp.stack(vsm), jnp.stack(ssm), jnp.stack(csm))
```

```python
import collections
import functools
import math

import jax
import jax.numpy as jnp
from jax import lax
from jax.experimental import pallas as pl
from jax.experimental.pallas import tpu as pltpu

F32 = jnp.float32
BF16 = jnp.bfloat16
I32 = jnp.int32

CHUNK = 64
H_A = 8
D_A = 64
D_VA = 2 * D_A
ROT_DIM = D_A // 4
ROPE_THETA = 500000.0
H_B = 8
D_K = 128
D_V = 128
CONV_W = 4
N_GROUPS = 4
EXPERTS_PER_GROUP = 8
TOP_K_INNER = 2
EPS = 1e-6

LANES = 128
SUBLANES = 8
V7X_VMEM_BYTES = 64 * 1024 * 1024
VMEM_LIMIT_BYTES = V7X_VMEM_BYTES - 8 * 1024 * 1024

Q_SCALE = (D_A ** -0.5) * math.log2(math.e)
ONES_ROWS = 16
INPROJ_BN = 512
MOE_ROWS = 128

NT_DIMS = (((1,), (1,)), ((), ()))
TN_DIMS = (((0,), (0,)), ((), ()))

A_COLS = H_A * 2 * D_A
V_COLS = H_A * D_VA
G_COLS = 2 * H_B * D_K + H_B * D_V
Z_COLS = H_B * D_V


def _pick(n, cands):
    for c in cands:
        if n % c == 0:
            return c
    raise ValueError(f"no block size in {cands} divides {n}")


def _cparams(*sem):
    return pltpu.CompilerParams(dimension_semantics=sem, vmem_limit_bytes=VMEM_LIMIT_BYTES)


def _sigmoid(x):
    return 1.0 / (1.0 + jnp.exp(-x))


def _silu(x):
    return x * _sigmoid(x)


def _softplus(x):
    return jnp.maximum(x, 0.0) + jnp.log1p(jnp.exp(-jnp.abs(x)))


def _bdot(a, b):
    return jnp.dot(a.astype(BF16), b.astype(BF16), preferred_element_type=F32)


def _split2(a):
    hi = a.astype(BF16)
    lo = (a - hi.astype(F32)).astype(BF16)
    return hi, lo


def _dot3(a, b):
    ah, al = _split2(a)
    bh, bl = _split2(b)
    d = functools.partial(jnp.dot, preferred_element_type=F32)
    return d(ah, bh) + d(ah, bl) + d(al, bh)


def _dot_exact_mask(mask, x, mask_is_lhs):
    hi = x.astype(BF16)
    r1 = x - hi.astype(F32)
    mid = r1.astype(BF16)
    lo = (r1 - mid.astype(F32)).astype(BF16)
    mb = mask.astype(BF16)
    d = functools.partial(jnp.dot, preferred_element_type=F32)
    if mask_is_lhs:
        return d(mb, hi) + d(mb, mid) + d(mb, lo)
    return d(hi, mb) + d(mid, mb) + d(lo, mb)


InProj = collections.namedtuple("InProj", "small k_new v_new q_bf k_bf v_t gq zg")


def _inproj_body(o_k, o_v, o_g, o_z, x_ref, nw_ref, cos_ref, sa_ref, sb_ref, w_ref, ws_ref,
                 os_ref, k_ref, v_ref, qb_ref, kb_ref, vt_ref, gq_ref, zg_ref, h_scr):
    j = pl.program_id(1)

    @pl.when(j == 0)
    def _():
        x = x_ref[...]
        ms = jnp.mean(x * x, axis=-1, keepdims=True)
        h = (x * lax.rsqrt(ms + EPS) * nw_ref[...]).astype(BF16)
        h_scr[...] = h
        os_ref[...] = jnp.dot(h, ws_ref[...], preferred_element_type=F32)

    acc = jnp.dot(h_scr[...], w_ref[...], preferred_element_type=F32)
    n_lane_blocks = acc.shape[1] // LANES

    def rotated(c):
        blk = acc[:, c * LANES:(c + 1) * LANES]
        return (blk * cos_ref[...] + pltpu.roll(blk, ROT_DIM // 2, 1) * sa_ref[...]
                + pltpu.roll(blk, LANES - ROT_DIM // 2, 1) * sb_ref[...])

    @pl.when(j < o_k)
    def _():
        for c in range(n_lane_blocks):
            qb_ref[:, c * LANES:(c + 1) * LANES] = (rotated(c) * Q_SCALE).astype(BF16)

    @pl.when(jnp.logical_and(j >= o_k, j < o_v))
    def _():
        for c in range(n_lane_blocks):
            rot = rotated(c)
            k_ref[:, c * LANES:(c + 1) * LANES] = rot
            kb_ref[:, c * LANES:(c + 1) * LANES] = rot.astype(BF16)

    @pl.when(jnp.logical_and(j >= o_v, j < o_g))
    def _():
        v_ref[...] = acc
        for hh in range(acc.shape[1] // D_VA):
            vt_ref[hh, 0:D_VA, :] = acc[:, hh * D_VA:(hh + 1) * D_VA].T.astype(BF16)
            vt_ref[hh, D_VA:D_VA + ONES_ROWS, :] = jnp.ones((ONES_ROWS, acc.shape[0]), BF16)

    @pl.when(jnp.logical_and(j >= o_g, j < o_z))
    def _():
        gq_ref[...] = acc

    @pl.when(j >= o_z)
    def _():
        zg_ref[...] = acc.astype(BF16)


def _inproj(x, norm_w, tables, w_main, w_small):
    T, D = x.shape
    bn = INPROJ_BN
    n_main = w_main.shape[1]
    zg_cols = Z_COLS + 2 * D
    assert A_COLS % bn == 0 and V_COLS % bn == 0 and G_COLS % bn == 0 and zg_cols % bn == 0
    assert bn % D_VA == 0 and n_main == 2 * A_COLS + V_COLS + G_COLS + zg_cols
    nb_a, nb_v, nb_g, nb_z = A_COLS // bn, V_COLS // bn, G_COLS // bn, zg_cols // bn
    o_k, o_v = nb_a, 2 * nb_a
    o_g = o_v + nb_v
    o_z = o_g + nb_g
    bm = _pick(T, (1024, 512, 256, 128))
    hpb = bn // D_VA

    def col(lo, n):
        return lambda i, j: (i, jnp.clip(j - lo, 0, n - 1))

    cos_t, sa_t, sb_t = tables
    outs = pl.pallas_call(
        functools.partial(_inproj_body, o_k, o_v, o_g, o_z),
        grid=(T // bm, n_main // bn),
        in_specs=[
            pl.BlockSpec((bm, D), lambda i, j: (i, 0), pipeline_mode=pl.Buffered(1)),
            pl.BlockSpec((1, D), lambda i, j: (0, 0)),
            pl.BlockSpec((bm, LANES), lambda i, j: (i, 0)),
            pl.BlockSpec((bm, LANES), lambda i, j: (i, 0)),
            pl.BlockSpec((bm, LANES), lambda i, j: (i, 0)),
            pl.BlockSpec((D, bn), lambda i, j: (0, j)),
            pl.BlockSpec((D, LANES), lambda i, j: (0, 0)),
        ],
        out_specs=[
            pl.BlockSpec((bm, LANES), lambda i, j: (i, 0)),
            pl.BlockSpec((bm, bn), col(o_k, nb_a)),
            pl.BlockSpec((bm, bn), col(o_v, nb_v)),
            pl.BlockSpec((bm, bn), col(0, nb_a)),
            pl.BlockSpec((bm, bn), col(o_k, nb_a)),
            pl.BlockSpec((hpb, D_VA + ONES_ROWS, bm), lambda i, j: (jnp.clip(j - o_v, 0, nb_v - 1), 0, i)),
            pl.BlockSpec((bm, bn), col(o_g, nb_g)),
            pl.BlockSpec((bm, bn), col(o_z, nb_z)),
        ],
        out_shape=[
            jax.ShapeDtypeStruct((T, LANES), F32),
            jax.ShapeDtypeStruct((T, A_COLS), F32),
            jax.ShapeDtypeStruct((T, V_COLS), F32),
            jax.ShapeDtypeStruct((T, A_COLS), BF16),
            jax.ShapeDtypeStruct((T, A_COLS), BF16),
            jax.ShapeDtypeStruct((H_A, D_VA + ONES_ROWS, T), BF16),
            jax.ShapeDtypeStruct((T, G_COLS), F32),
            jax.ShapeDtypeStruct((T, zg_cols), BF16),
        ],
        scratch_shapes=[pltpu.VMEM((bm, D), BF16)],
        compiler_params=_cparams("parallel", "arbitrary"),
        name="inproj",
    )(x, norm_w, cos_t, sa_t, sb_t, w_main, w_small)
    return InProj(*outs)


def _diff_lambda(lamp, lam_init):
    s1 = jnp.sum(lamp[0:1] * lamp[1:2], axis=1, keepdims=True)
    s2 = jnp.sum(lamp[2:3] * lamp[3:4], axis=1, keepdims=True)
    return jnp.exp(s1) - jnp.exp(s2) + lam_init


def _subln(o, w, lam_init):
    ms = jnp.mean(o * o, axis=-1, keepdims=True)
    return o * lax.rsqrt(ms + EPS) * w * (1.0 - lam_init)


def _attn_prompt_body(lam_init, bq, q_ref, k_ref, vt_ref, lamp_ref, swc_ref, o_ref,
                      m_scr, acc_scr, s_scr):
    qi = pl.program_id(2)
    q = q_ref[...]
    lane = lax.broadcasted_iota(I32, q.shape, 1)
    qz = [jnp.where(lane < D_A, q, jnp.zeros_like(q)), jnp.where(lane >= D_A, q, jnp.zeros_like(q))]
    m_scr[...] = jnp.full(m_scr.shape, -jnp.inf, F32)
    acc_scr[...] = jnp.zeros(acc_scr.shape, F32)
    key_chunk = lax.broadcasted_iota(I32, (bq, bq), 0) // CHUNK
    qry_chunk = lax.broadcasted_iota(I32, (bq, bq), 1) // CHUNK
    diag_mask = key_chunk <= qry_chunk

    def scores(j, slot):
        kb = k_ref[pl.ds(pl.multiple_of(j * bq, bq), bq), :]
        for m in range(2):
            s_scr[slot, m] = lax.dot_general(kb, qz[m], NT_DIMS, preferred_element_type=F32)

    def accumulate(j, slot, masked):
        vb = vt_ref[:, pl.ds(pl.multiple_of(j * bq, bq), bq)]
        for m in range(2):
            s = s_scr[slot, m]
            if masked:
                s = jnp.where(diag_mask, s, -jnp.inf)
            m_prev = m_scr[m]
            m_new = jnp.maximum(m_prev, jnp.max(s, axis=0, keepdims=True))
            alpha = jnp.exp2(m_prev - m_new)
            p = jnp.exp2(s - m_new).astype(BF16)
            acc_scr[m] = alpha * acc_scr[m] + jnp.dot(vb, p, preferred_element_type=F32)
            m_scr[m] = m_new

    scores(0, 0)

    def two_full_blocks(t, carry):
        j = 2 * t
        scores(j + 1, 1)
        accumulate(j, 0, False)
        scores(j + 2, 0)
        accumulate(j + 1, 1, False)
        return carry

    lax.fori_loop(0, qi // 2, two_full_blocks, 0)
    j0 = 2 * (qi // 2)

    @pl.when(j0 == qi)
    def _():
        accumulate(qi, 0, True)

    @pl.when(j0 != qi)
    def _():
        scores(qi, 1)
        accumulate(j0, 0, False)
        accumulate(qi, 1, True)

    lam = _diff_lambda(lamp_ref[...], lam_init)
    a0 = acc_scr[0]
    a1 = acc_scr[1]
    o_t = a0[0:D_VA] / a0[D_VA:D_VA + 1] - lam * (a1[0:D_VA] / a1[D_VA:D_VA + 1])
    ms = jnp.mean(o_t * o_t, axis=0, keepdims=True)
    o_t = o_t * lax.rsqrt(ms + EPS) * swc_ref[...] * (1.0 - lam_init)
    o_ref[...] = o_t.T.astype(o_ref.dtype)


def _attn_prompt(q_bf, k_bf, v_t, lamp, subln_w, lam_init, bp, lp):
    bq = _pick(lp, (512, 256, 128))
    nq = lp // bq
    vrows = v_t.shape[1]
    return pl.pallas_call(
        functools.partial(_attn_prompt_body, lam_init, bq),
        grid=(bp, H_A, nq),
        in_specs=[
            pl.BlockSpec((bq, LANES), lambda b, h, i: (b * nq + i, h)),
            pl.BlockSpec((lp, LANES), lambda b, h, i: (b, h)),
            pl.BlockSpec((None, vrows, lp), lambda b, h, i: (h, 0, b)),
            pl.BlockSpec((4, D_A), lambda b, h, i: (0, 0)),
            pl.BlockSpec((D_VA, 1), lambda b, h, i: (0, 0)),
        ],
        out_specs=pl.BlockSpec((bq, LANES), lambda b, h, i: (b * nq + i, h)),
        out_shape=jax.ShapeDtypeStruct((bp * lp, V_COLS), BF16),
        scratch_shapes=[
            pltpu.VMEM((2, 1, bq), F32),
            pltpu.VMEM((2, vrows, bq), F32),
            pltpu.VMEM((2, 2, bq, bq), F32),
        ],
        compiler_params=_cparams("parallel", "parallel", "arbitrary"),
        name="attn_prompt",
    )(q_bf, k_bf, v_t, lamp, subln_w.reshape(D_VA, 1))


def _attn_sample_body(lam_init, q_ref, kn_ref, vn_ref, kc_ref, vc_ref, lamp_ref, sw_ref, o_ref,
                      m_scr, l_scr, acc_scr):
    pj = pl.program_id(1)
    npos = kc_ref.shape[0] // H_A
    ls = q_ref.shape[0]

    @pl.when(pj == 0)
    def _():
        m_scr[...] = jnp.full(m_scr.shape, -jnp.inf, F32)
        l_scr[...] = jnp.zeros(l_scr.shape, F32)
        acc_scr[...] = jnp.zeros(acc_scr.shape, F32)

    lane = lax.broadcasted_iota(I32, (ls, D_VA), 1)

    def q_maps(h):
        q = q_ref[:, h * D_VA:(h + 1) * D_VA]
        return [jnp.where(lane < D_A, q, jnp.zeros_like(q)), jnp.where(lane >= D_A, q, jnp.zeros_like(q))]

    def update(idx, qz, keys, vals):
        s = lax.dot_general(qz, keys, NT_DIMS, preferred_element_type=F32)
        m_prev = m_scr[idx]
        m_new = jnp.maximum(m_prev, jnp.max(s, axis=1, keepdims=True))
        alpha = jnp.exp2(m_prev - m_new)
        p = jnp.exp2(s - m_new)
        l_scr[idx] = alpha * l_scr[idx] + jnp.sum(p, axis=1, keepdims=True)
        acc_scr[idx] = alpha * acc_scr[idx] + jnp.dot(p.astype(BF16), vals, preferred_element_type=F32)
        m_scr[idx] = m_new

    for h in range(H_A):
        kc = kc_ref[pl.ds(h, npos, stride=H_A), :].astype(BF16)
        vc = vc_ref[pl.ds(h, npos, stride=H_A), :].astype(BF16)
        qz = q_maps(h)
        for m in range(2):
            update(2 * h + m, qz[m], kc, vc)

    @pl.when(pj == pl.num_programs(1) - 1)
    def _():
        lam = _diff_lambda(lamp_ref[...], lam_init)
        sw = sw_ref[...]
        for h in range(H_A):
            kn = kn_ref[:, h * D_VA:(h + 1) * D_VA]
            vn = vn_ref[:, h * D_VA:(h + 1) * D_VA].astype(BF16)
            qz = q_maps(h)
            outs = []
            for m in range(2):
                update(2 * h + m, qz[m], kn, vn)
                outs.append(acc_scr[2 * h + m] / l_scr[2 * h + m])
            o = outs[0] - lam * outs[1]
            o_ref[:, h * D_VA:(h + 1) * D_VA] = _subln(o, sw, lam_init).astype(o_ref.dtype)


def _attn_sample(q_bf, k_bf, v_new, cache_k, cache_v, lamp, subln_w, lam_init, bs, ls):
    past = cache_k.shape[1] // H_A
    pb = _pick(past, (1024, 512, 256, 128, 64))
    assert ls % (2 * SUBLANES) == 0
    return pl.pallas_call(
        functools.partial(_attn_sample_body, lam_init),
        grid=(bs, past // pb),
        in_specs=[
            pl.BlockSpec((ls, A_COLS), lambda b, j: (b, 0)),
            pl.BlockSpec((ls, A_COLS), lambda b, j: (b, 0)),
            pl.BlockSpec((ls, V_COLS), lambda b, j: (b, 0)),
            pl.BlockSpec((None, pb * H_A, D_VA), lambda b, j: (b, j, 0)),
            pl.BlockSpec((None, pb * H_A, D_VA), lambda b, j: (b, j, 0)),
            pl.BlockSpec((4, D_A), lambda b, j: (0, 0)),
            pl.BlockSpec((1, D_VA), lambda b, j: (0, 0)),
        ],
        out_specs=pl.BlockSpec((ls, V_COLS), lambda b, j: (b, 0)),
        out_shape=jax.ShapeDtypeStruct((bs * ls, V_COLS), BF16),
        scratch_shapes=[
            pltpu.VMEM((2 * H_A, ls, 1), F32),
            pltpu.VMEM((2 * H_A, ls, 1), F32),
            pltpu.VMEM((2 * H_A, ls, D_VA), F32),
        ],
        compiler_params=_cparams("parallel", "arbitrary"),
        name="attn_sample",
    )(q_bf, k_bf, v_new, cache_k, cache_v, lamp, subln_w)


def _gdn_body(chunk, rows, pq_ref, pk_ref, pv_ref, pz_ref, sm_ref, smt_ref, cw_ref, alr_ref, dtr_ref,
              alc_ref, dtc_ref, nw_ref, s0_ref, c0_ref, o_ref, s_ref, cn_ref, ext_scr, act_scr):
    i = pl.program_id(1)
    nblk = pl.num_programs(1)
    hk = H_B * D_K
    hv = H_B * D_V
    n_chunks = rows // chunk
    n_steps = int(math.log2(chunk))
    assert 2 ** n_steps == chunk

    @pl.when(i == 0)
    def _():
        s_ref[...] = s0_ref[...]
        ext_scr[0:SUBLANES, :] = jnp.zeros((SUBLANES, ext_scr.shape[1]), F32)
        ext_scr[pl.ds(SUBLANES - (CONV_W - 1), CONV_W - 1), :] = c0_ref[...]

    ext_scr[pl.ds(SUBLANES, rows), 0:hk] = pq_ref[...]
    ext_scr[pl.ds(SUBLANES, rows), hk:2 * hk] = pk_ref[...]
    ext_scr[pl.ds(SUBLANES, rows), 2 * hk:2 * hk + hv] = pv_ref[...]

    @pl.when(i == nblk - 1)
    def _():
        cn_ref[...] = ext_scr[pl.ds(SUBLANES + rows - (CONV_W - 1), CONV_W - 1), :]

    conv = jnp.zeros((rows, ext_scr.shape[1]), F32)
    for w in range(CONV_W):
        conv = conv + ext_scr[pl.ds(SUBLANES - (CONV_W - 1) + w, rows), :] * cw_ref[w:w + 1, :]
    act_scr[...] = _silu(conv)
    ext_scr[0:SUBLANES, :] = ext_scr[pl.ds(rows, SUBLANES), :]

    sm = sm_ref[...]
    beta_all = _sigmoid(sm[:, 0:H_B])
    g_all = -jnp.exp(alr_ref[...]) * _softplus(sm[:, H_B:2 * H_B] + dtr_ref[...])
    smt = smt_ref[...]
    g_all_t = -jnp.exp(alc_ref[...]) * _softplus(smt[H_B:2 * H_B, :] + dtc_ref[...])

    ri = lax.broadcasted_iota(I32, (chunk, chunk), 0)
    ci = lax.broadcasted_iota(I32, (chunk, chunk), 1)
    tri = ri >= ci
    strict = ri > ci
    eye = (ri == ci).astype(F32)
    ltri = tri.astype(F32)
    utri = (ri <= ci).astype(F32)
    nw = nw_ref[...]

    units = [(c, h) for c in range(n_chunks) for h in range(H_B)]
    gcs = [_dot_exact_mask(ltri, g_all[c * chunk:(c + 1) * chunk, :], True) for c in range(n_chunks)]
    grs = [_dot_exact_mask(utri, g_all_t[:, c * chunk:(c + 1) * chunk], False) for c in range(n_chunks)]
    pre = []
    for c, h in units:
        r0 = c * chunk
        q = act_scr[r0:r0 + chunk, h * D_K:(h + 1) * D_K]
        k = act_scr[r0:r0 + chunk, hk + h * D_K:hk + (h + 1) * D_K]
        v = act_scr[r0:r0 + chunk, 2 * hk + h * D_V:2 * hk + (h + 1) * D_V]
        q = q * lax.rsqrt(jnp.sum(q * q, axis=-1, keepdims=True) + EPS) * (D_K ** -0.5)
        k = k * lax.rsqrt(jnp.sum(k * k, axis=-1, keepdims=True) + EPS)
        beta = beta_all[r0:r0 + chunk, h:h + 1]
        gc = gcs[c][:, h:h + 1]
        gr = grs[c][h:h + 1, :]
        g_last = gc[chunk - 1:chunk, :]
        decay = jnp.where(tri, jnp.exp(jnp.where(tri, gc - gr, 0.0)), 0.0)
        kb = k * beta
        eg = jnp.exp(gc)
        kbf = k.astype(BF16)
        kk = lax.dot_general(kb.astype(BF16), kbf, NT_DIMS, preferred_element_type=F32)
        qk = lax.dot_general(q.astype(BF16), kbf, NT_DIMS, preferred_element_type=F32)
        pre.append(dict(
            nm=-jnp.where(strict, kk * decay, 0.0),
            rhs=jnp.concatenate([v * beta, kb * eg], axis=1),
            qk=jnp.where(tri, qk * decay, 0.0).astype(BF16),
            qg=(q * eg).astype(BF16),
            kg=(k * jnp.exp(g_last - gc)).astype(BF16),
            gl=jnp.exp(g_last)))
    invs = [eye + p["nm"] for p in pre]
    pws = [p["nm"] for p in pre]
    for _ in range(n_steps - 1):
        pws = [_bdot(pw, pw) for pw in pws]
        invs = [inv + _bdot(inv, pw) for inv, pw in zip(invs, pws)]
    resid = [eye - _dot3(eye - p["nm"], inv) for p, inv in zip(pre, invs)]
    invs = [inv + _bdot(inv, r) for inv, r in zip(invs, resid)]
    uws = [_dot3(inv, p["rhs"]) for p, inv in zip(pre, invs)]

    for (c, h), p, uw in zip(units, pre, uws):
        r0 = c * chunk
        u = uw[:, 0:D_V]
        wmat = uw[:, D_V:D_V + D_K]
        s = s_ref[h]
        sb = s.astype(BF16)
        v_new = u - jnp.dot(wmat.astype(BF16), sb, preferred_element_type=F32)
        v_new_b = v_new.astype(BF16)
        o = (jnp.dot(p["qg"], sb, preferred_element_type=F32)
             + jnp.dot(p["qk"], v_new_b, preferred_element_type=F32))
        s_ref[h] = s * p["gl"] + lax.dot_general(p["kg"], v_new_b, TN_DIMS, preferred_element_type=F32)
        z = pz_ref[r0:r0 + chunk, h * D_V:(h + 1) * D_V].astype(F32)
        o = o * lax.rsqrt(jnp.mean(o * o, axis=-1, keepdims=True) + EPS) * nw * _silu(z)
        o_ref[r0:r0 + chunk, h * D_V:(h + 1) * D_V] = o.astype(o_ref.dtype)


def _gdn(gq, zg, small, s0, c0, conv_w, a_log, dt_bias, norm_w, bn, ln, chunk):
    hk = H_B * D_K
    assert H_B * D_V == hk and Z_COLS == hk
    rows = ln
    for cand in (2 * chunk, chunk):
        if cand % LANES == 0 and ln % cand == 0:
            rows = cand
            break
    assert rows % chunk == 0 and rows % SUBLANES == 0
    nblk = ln // rows
    small_t = jnp.swapaxes(small[:, 0:2 * H_B].reshape(bn, ln, 2 * H_B), 1, 2)
    alr = a_log.reshape(1, H_B)
    dtr = dt_bias.reshape(1, H_B)
    alc = a_log.reshape(H_B, 1)
    dtc = dt_bias.reshape(H_B, 1)
    row_blk = lambda b, i: b * nblk + i
    return pl.pallas_call(
        functools.partial(_gdn_body, chunk, rows),
        grid=(bn, nblk),
        in_specs=[
            pl.BlockSpec((rows, hk), lambda b, i: (row_blk(b, i), 0)),
            pl.BlockSpec((rows, hk), lambda b, i: (row_blk(b, i), 1)),
            pl.BlockSpec((rows, hk), lambda b, i: (row_blk(b, i), 2)),
            pl.BlockSpec((rows, hk), lambda b, i: (row_blk(b, i), 0)),
            pl.BlockSpec((rows, LANES), lambda b, i: (row_blk(b, i), 0)),
            pl.BlockSpec((None, 2 * H_B, rows), lambda b, i: (b, 0, i)),
            pl.BlockSpec((CONV_W, G_COLS), lambda b, i: (0, 0)),
            pl.BlockSpec((1, H_B), lambda b, i: (0, 0)),
            pl.BlockSpec((1, H_B), lambda b, i: (0, 0)),
            pl.BlockSpec((H_B, 1), lambda b, i: (0, 0)),
            pl.BlockSpec((H_B, 1), lambda b, i: (0, 0)),
            pl.BlockSpec((1, D_V), lambda b, i: (0, 0)),
            pl.BlockSpec((None, H_B, D_K, D_V), lambda b, i: (b, 0, 0, 0)),
            pl.BlockSpec((None, CONV_W - 1, G_COLS), lambda b, i: (b, 0, 0)),
        ],
        out_specs=[
            pl.BlockSpec((rows, hk), lambda b, i: (row_blk(b, i), 0)),
            pl.BlockSpec((None, H_B, D_K, D_V), lambda b, i: (b, 0, 0, 0)),
            pl.BlockSpec((None, CONV_W - 1, G_COLS), lambda b, i: (b, 0, 0)),
        ],
        out_shape=[
            jax.ShapeDtypeStruct((bn * ln, hk), BF16),
            jax.ShapeDtypeStruct(s0.shape, F32),
            jax.ShapeDtypeStruct(c0.shape, F32),
        ],
        scratch_shapes=[
            pltpu.VMEM((rows + SUBLANES, G_COLS), F32),
            pltpu.VMEM((rows, G_COLS), F32),
        ],
        compiler_params=_cparams("parallel", "arbitrary"),
        name="gdn",
    )(gq, gq, gq, zg, small, small_t, conv_w, alr, dtr, alc, dtc, norm_w, s0, c0)


def _merge_body(x_ref, oa_ref, ob_ref, ga0_ref, ga1_ref, gb0_ref, gb1_ref, wa_ref, wb_ref, wo_ref,
                nf_ref, wr_ref, br_ref, x2_ref, t_ref, lg_ref):
    ya = jnp.dot(oa_ref[...], wa_ref[...], preferred_element_type=F32)
    yb = jnp.dot(ob_ref[...], wb_ref[...], preferred_element_type=F32)
    half = ga0_ref.shape[1]
    gate = lambda ref: _sigmoid(ref[...].astype(F32))
    m0 = gate(ga0_ref) * ya[:, :half] + gate(gb0_ref) * yb[:, :half]
    m1 = gate(ga1_ref) * ya[:, half:] + gate(gb1_ref) * yb[:, half:]
    merged = jnp.concatenate([m0, m1], axis=1).astype(BF16)
    x2 = x_ref[...] + jnp.dot(merged, wo_ref[...], preferred_element_type=F32)
    x2_ref[...] = x2
    ms = jnp.mean(x2 * x2, axis=-1, keepdims=True)
    t = x2 * lax.rsqrt(ms + EPS) * nf_ref[...]
    t_ref[...] = t
    lg_ref[...] = _dot3(t, wr_ref[...]) + br_ref[...]


def _merge(x, o_a, o_b, zg, wa, wb, wo, norm_ffn, w_router, b_router):
    T, D = x.shape
    half = D // 2
    assert Z_COLS % half == 0
    gcol = Z_COLS // half
    bm = _pick(T, (256, 128, 64, 32, 16, 8))
    const = dict(pipeline_mode=pl.Buffered(1))
    return pl.pallas_call(
        _merge_body,
        grid=(T // bm,),
        in_specs=[
            pl.BlockSpec((bm, D), lambda i: (i, 0)),
            pl.BlockSpec((bm, o_a.shape[1]), lambda i: (i, 0)),
            pl.BlockSpec((bm, o_b.shape[1]), lambda i: (i, 0)),
            pl.BlockSpec((bm, half), lambda i: (i, gcol)),
            pl.BlockSpec((bm, half), lambda i: (i, gcol + 1)),
            pl.BlockSpec((bm, half), lambda i: (i, gcol + 2)),
            pl.BlockSpec((bm, half), lambda i: (i, gcol + 3)),
            pl.BlockSpec(wa.shape, lambda i: (0, 0), **const),
            pl.BlockSpec(wb.shape, lambda i: (0, 0), **const),
            pl.BlockSpec(wo.shape, lambda i: (0, 0), **const),
            pl.BlockSpec((1, D), lambda i: (0, 0)),
            pl.BlockSpec((D, LANES), lambda i: (0, 0), **const),
            pl.BlockSpec((1, LANES), lambda i: (0, 0)),
        ],
        out_specs=[
            pl.BlockSpec((bm, D), lambda i: (i, 0)),
            pl.BlockSpec((bm, D), lambda i: (i, 0)),
            pl.BlockSpec((bm, LANES), lambda i: (i, 0)),
        ],
        out_shape=[
            jax.ShapeDtypeStruct((T, D), F32),
            jax.ShapeDtypeStruct((T, D), F32),
            jax.ShapeDtypeStruct((T, LANES), F32),
        ],
        compiler_params=_cparams("parallel"),
        name="merge",
    )(x, o_a, o_b, zg, zg, zg, zg, wa, wb, wo, norm_ffn, w_router, b_router)


def _route_body(lg_ref, mi_ref, mf_ref, cnt_ref):
    i = pl.program_id(0)

    @pl.when(i == 0)
    def _():
        cnt_ref[...] = jnp.zeros(cnt_ref.shape, F32)

    lg = lg_ref[...]
    bm = lg.shape[0]
    lane = lax.broadcasted_iota(I32, lg.shape, 1).astype(F32)
    big = jnp.float32(LANES)
    neg = -jnp.inf
    gl = jnp.where(lane < N_GROUPS, lg, neg)
    gmax = jnp.max(gl, axis=1, keepdims=True)
    gidx = jnp.min(jnp.where(gl == gmax, lane, big), axis=1, keepdims=True)
    gw = 1.0 / jnp.sum(jnp.exp(gl - gmax), axis=1, keepdims=True)
    e_lo = N_GROUPS + gidx * EXPERTS_PER_GROUP
    valid = (lane >= e_lo) & (lane < e_lo + EXPERTS_PER_GROUP)
    el = jnp.where(valid, lg, neg)
    v1 = jnp.max(el, axis=1, keepdims=True)
    i1 = jnp.min(jnp.where(el == v1, lane, big), axis=1, keepdims=True)
    el2 = jnp.where(lane == i1, neg, el)
    v2 = jnp.max(el2, axis=1, keepdims=True)
    i2 = jnp.min(jnp.where(el2 == v2, lane, big), axis=1, keepdims=True)
    e21 = jnp.exp(v2 - v1)
    w1 = gw / (1.0 + e21)
    w2 = gw * e21 / (1.0 + e21)
    oh1 = (lane == i1).astype(F32)
    oh2 = (lane == i2).astype(F32)
    oh = oh1 + oh2
    rr = lax.broadcasted_iota(I32, (bm, bm), 0)
    cc = lax.broadcasted_iota(I32, (bm, bm), 1)
    before = (cc < rr).astype(BF16)
    cum = jnp.dot(before, oh.astype(BF16), preferred_element_type=F32) + cnt_ref[...]
    rank1 = jnp.sum(cum * oh1, axis=1, keepdims=True)
    rank2 = jnp.sum(cum * oh2, axis=1, keepdims=True)
    cnt_ref[...] = cnt_ref[...] + jnp.sum(oh, axis=0, keepdims=True)
    mi = jnp.where(lane == 0, i1 - N_GROUPS, 0.0)
    mi = jnp.where(lane == 1, i2 - N_GROUPS, mi)
    mi = jnp.where(lane == 2, rank1, mi)
    mi = jnp.where(lane == 3, rank2, mi)
    mi_ref[...] = mi.astype(I32)
    mf_ref[...] = jnp.where(lane == 0, w1, jnp.where(lane == 1, w2, 0.0))


def _route(logits):
    T = logits.shape[0]
    bm = _pick(T, (512, 256, 128, 64, 32, 16, 8))
    return pl.pallas_call(
        _route_body,
        grid=(T // bm,),
        in_specs=[pl.BlockSpec((bm, LANES), lambda i: (i, 0))],
        out_specs=[
            pl.BlockSpec((bm, LANES), lambda i: (i, 0)),
            pl.BlockSpec((bm, LANES), lambda i: (i, 0)),
            pl.BlockSpec((1, LANES), lambda i: (0, 0)),
        ],
        out_shape=[
            jax.ShapeDtypeStruct((T, LANES), I32),
            jax.ShapeDtypeStruct((T, LANES), F32),
            jax.ShapeDtypeStruct((1, LANES), F32),
        ],
        compiler_params=_cparams("arbitrary"),
        name="route",
    )(logits)


def _dispatch_body(bm, blk, n_experts, nbp, ps_ref, pn_ref, nu_ref, dest_ref, tp_ref, ts_ref, xb_ref,
                   zero_scr, sem, zsem):
    i = pl.program_id(0)
    n_blocks = xb_ref.shape[0] // blk

    def pad_rows(act):
        def per_expert(e, carry):
            def per_row(r, c2):
                act(pltpu.make_async_copy(zero_scr.at[pl.ds(0, 1)], xb_ref.at[pl.ds(ps_ref[e] + r, 1)], zsem))
                return c2
            return lax.fori_loop(0, pn_ref[e], per_row, carry)
        lax.fori_loop(0, n_experts, per_expert, 0)

    def tail_blocks(act):
        def per_block(b, carry):
            act(pltpu.make_async_copy(zero_scr, xb_ref.at[pl.ds(pl.multiple_of(b * blk, blk), blk)], zsem))
            return carry
        lax.fori_loop(nu_ref[0], n_blocks, per_block, 0)

    @pl.when(i == 0)
    def _():
        zero_scr[...] = jnp.zeros(zero_scr.shape, zero_scr.dtype)
        pad_rows(lambda cp: cp.start())
        tail_blocks(lambda cp: cp.start())

    def send_rows(t_ref):
        def row_copy(r, k):
            d = dest_ref[0, TOP_K_INNER * r + k]
            return pltpu.make_async_copy(t_ref.at[pl.ds(r, 1)], xb_ref.at[pl.ds(d, 1)], sem)

        def issue(r, carry):
            for k in range(TOP_K_INNER):
                row_copy(r, k).start()
            return carry

        def drain(r, carry):
            for k in range(TOP_K_INNER):
                row_copy(r, k).wait()
            return carry

        lax.fori_loop(0, bm, issue, 0)
        lax.fori_loop(0, bm, drain, 0)

    @pl.when(i < nbp)
    def _():
        send_rows(tp_ref)

    @pl.when(i >= nbp)
    def _():
        send_rows(ts_ref)

    @pl.when(i == 0)
    def _():
        pad_rows(lambda cp: cp.wait())
        tail_blocks(lambda cp: cp.wait())


def _dispatch(t_p, t_s, dest, pad_start, pad_len, n_used, n_rows, blk):
    (tp, D), ts = t_p.shape, t_s.shape[0]
    bm = _pick(math.gcd(tp, ts), (256, 128, 64, 32, 16, 8))
    nbp, nbs = tp // bm, ts // bm
    dest3 = dest.reshape(nbp + nbs, 1, TOP_K_INNER * bm)
    grid_spec = pltpu.PrefetchScalarGridSpec(
        num_scalar_prefetch=3,
        grid=(nbp + nbs,),
        in_specs=[
            pl.BlockSpec((None, 1, TOP_K_INNER * bm), lambda i, ps, pn, nu: (i, 0, 0),
                         memory_space=pltpu.SMEM),
            pl.BlockSpec((bm, D), lambda i, ps, pn, nu: (jnp.minimum(i, nbp - 1), 0)),
            pl.BlockSpec((bm, D), lambda i, ps, pn, nu: (jnp.maximum(i - nbp, 0), 0)),
        ],
        out_specs=pl.BlockSpec(memory_space=pl.ANY),
        scratch_shapes=[
            pltpu.VMEM((blk, D), t_p.dtype),
            pltpu.SemaphoreType.DMA(()),
            pltpu.SemaphoreType.DMA(()),
        ],
    )
    return pl.pallas_call(
        functools.partial(_dispatch_body, bm, blk, pad_start.shape[0], nbp),
        grid_spec=grid_spec,
        out_shape=jax.ShapeDtypeStruct((n_rows, D), t_p.dtype),
        compiler_params=_cparams("arbitrary"),
        name="dispatch",
    )(pad_start, pad_len, n_used, dest3, t_p, t_s)


def _expert_body(be_ref, nu_ref, x_ref, wg_ref, wu_ref, wd_ref, y_ref, wg_scr, wu_scr, wd_scr):
    i = pl.program_id(0)

    @pl.when(i < nu_ref[0])
    def _():
        @pl.when(jnp.logical_or(i == 0, be_ref[i] != be_ref[jnp.maximum(i - 1, 0)]))
        def _():
            wg_scr[...] = wg_ref[...].astype(BF16)
            wu_scr[...] = wu_ref[...].astype(BF16)
            wd_scr[...] = wd_ref[...].astype(BF16)

        x = x_ref[...].astype(BF16)
        g = jnp.dot(x, wg_scr[...], preferred_element_type=F32)
        u = jnp.dot(x, wu_scr[...], preferred_element_type=F32)
        hmid = (_silu(g) * u).astype(BF16)
        y_ref[...] = jnp.dot(hmid, wd_scr[...], preferred_element_type=F32)

    @pl.when(i >= nu_ref[0])
    def _():
        y_ref[...] = jnp.zeros(y_ref.shape, F32)


def _experts(xb, blk_exp, n_used, w_gate, w_up, w_down, blk):
    P, D = xb.shape
    de = w_gate.shape[2]
    n_blocks = P // blk
    last = lambda i, nu: jnp.minimum(i, nu[0] - 1)
    grid_spec = pltpu.PrefetchScalarGridSpec(
        num_scalar_prefetch=2,
        grid=(n_blocks,),
        in_specs=[
            pl.BlockSpec((blk, D), lambda i, be, nu: (last(i, nu), 0)),
            pl.BlockSpec((None, D, de), lambda i, be, nu: (be[last(i, nu)], 0, 0)),
            pl.BlockSpec((None, D, de), lambda i, be, nu: (be[last(i, nu)], 0, 0)),
            pl.BlockSpec((None, de, D), lambda i, be, nu: (be[last(i, nu)], 0, 0)),
        ],
        out_specs=pl.BlockSpec((blk, D), lambda i, be, nu: (i, 0)),
        scratch_shapes=[
            pltpu.VMEM((D, de), BF16),
            pltpu.VMEM((D, de), BF16),
            pltpu.VMEM((de, D), BF16),
        ],
    )
    return pl.pallas_call(
        _expert_body,
        grid_spec=grid_spec,
        out_shape=jax.ShapeDtypeStruct((P, D), F32),
        compiler_params=_cparams("arbitrary"),
        name="experts",
    )(blk_exp, n_used, xb, w_gate, w_up, w_down)


def _combine_body(bm, final, dest_ref, x2_ref, mf_ref, nw_ref, yb_ref, o_ref, y0_scr, y1_scr, sem):
    bufs = (y0_scr, y1_scr)

    def row_copy(r, k):
        d = dest_ref[0, TOP_K_INNER * r + k]
        return pltpu.make_async_copy(yb_ref.at[pl.ds(d, 1)], bufs[k].at[pl.ds(r, 1)], sem)

    def issue(r, carry):
        for k in range(TOP_K_INNER):
            row_copy(r, k).start()
        return carry

    lax.fori_loop(0, bm, issue, 0)

    def drain(r, carry):
        for k in range(TOP_K_INNER):
            row_copy(r, k).wait()
        return carry

    lax.fori_loop(0, bm, drain, 0)

    mf = mf_ref[...]
    x3 = x2_ref[...] + y0_scr[...] * mf[:, 0:1] + y1_scr[...] * mf[:, 1:2]
    if final:
        ms = jnp.mean(x3 * x3, axis=-1, keepdims=True)
        x3 = x3 * lax.rsqrt(ms + EPS) * nw_ref[...]
    o_ref[...] = x3


def _combine(x2, yb, dest, mf, norm_final, final):
    T, D = x2.shape
    bm = _pick(T, (256, 128, 64, 32, 16, 8))
    nblk = T // bm
    dest3 = dest.reshape(nblk, 1, TOP_K_INNER * bm)
    return pl.pallas_call(
        functools.partial(_combine_body, bm, final),
        grid=(nblk,),
        in_specs=[
            pl.BlockSpec((None, 1, TOP_K_INNER * bm), lambda i: (i, 0, 0), memory_space=pltpu.SMEM),
            pl.BlockSpec((bm, D), lambda i: (i, 0)),
            pl.BlockSpec((bm, LANES), lambda i: (i, 0)),
            pl.BlockSpec((1, D), lambda i: (0, 0)),
            pl.BlockSpec(memory_space=pl.ANY),
        ],
        out_specs=pl.BlockSpec((bm, D), lambda i: (i, 0)),
        out_shape=jax.ShapeDtypeStruct((T, D), F32),
        scratch_shapes=[
            pltpu.VMEM((bm, D), F32),
            pltpu.VMEM((bm, D), F32),
            pltpu.SemaphoreType.DMA(()),
        ],
        compiler_params=_cparams("arbitrary"),
        name="combine",
    )(dest3, x2, mf, norm_final, yb)


def _rotary_tables(pos):
    half = ROT_DIM // 2
    inv = ROPE_THETA ** (-jnp.arange(0, ROT_DIM, 2, dtype=F32) / ROT_DIM)
    ang = pos.astype(F32)[:, None] * inv[None, :]
    cos = jnp.cos(ang)
    sin = jnp.sin(ang)
    n = pos.shape[0]
    ones = jnp.ones((n, D_A - ROT_DIM), F32)
    zeros = jnp.zeros((n, D_A - ROT_DIM), F32)
    zh = jnp.zeros((n, half), F32)
    cos64 = jnp.concatenate([cos, cos, ones], axis=1)
    sa64 = jnp.concatenate([zh, sin, zeros], axis=1)
    sb64 = jnp.concatenate([-sin, zh, zeros], axis=1)
    reps = LANES // D_A
    return jnp.tile(cos64, (1, reps)), jnp.tile(sa64, (1, reps)), jnp.tile(sb64, (1, reps))


def _moe_slots(mi, counts, n_experts, blk, n_assign):
    eid = mi[:, 0:TOP_K_INNER]
    rank = mi[:, TOP_K_INNER:2 * TOP_K_INNER]
    cnt = counts[0, N_GROUPS:N_GROUPS + n_experts].astype(I32)
    pc = (cnt + blk - 1) // blk * blk
    pend = jnp.cumsum(pc)
    pstart = pend - pc
    dest = (pstart[eid] + rank).astype(I32)
    n_blocks = -(-n_assign // blk) + n_experts
    blk_start = jnp.arange(n_blocks, dtype=I32) * blk
    blk_exp = jnp.minimum(jnp.sum((pend[None, :] <= blk_start[:, None]).astype(I32), axis=1), n_experts - 1)
    n_used = jnp.maximum(pend[-1] // blk, 1).astype(I32).reshape(1)
    return dest, blk_exp.astype(I32), n_used, n_blocks, (pstart + cnt).astype(I32), (pc - cnt).astype(I32)


def _layer(xp, xs, layer, dims, cache_k, cache_v, state_delta, state_conv, prm, norm_final, final):
    (norm_mix, w_in, lq1, lk1, lq2, lk2, subln_w, conv_w, a_log, dt_bias, gdn_norm_w,
     w_proj_a, w_proj_b, w_out, norm_ffn, w_gr, b_gr, w_er, b_er, w_eg, w_eu, w_ed) = prm
    bp, lp, bs, ls, past = dims
    D = xp.shape[1]
    tp = bp * lp
    n_qkv = 2 * A_COLS + V_COLS + G_COLS + Z_COLS
    n_small = 2 * H_B
    lam_init = 0.8 - 0.6 * math.exp(-0.3 * layer)

    w_main = jnp.concatenate([w_in[:, :n_qkv], w_in[:, n_qkv + n_small:]], axis=1).astype(BF16)
    w_small = jnp.pad(w_in[:, n_qkv:n_qkv + n_small], ((0, 0), (0, LANES - n_small))).astype(BF16)
    nm = norm_mix.reshape(1, D)
    pp = _inproj(xp, nm, _rotary_tables(jnp.tile(jnp.arange(lp), bp)), w_main, w_small)
    ps = _inproj(xs, nm, _rotary_tables(jnp.tile(past + jnp.arange(ls), bs)), w_main, w_small)

    lamp = jnp.stack([lq1, lk1, lq2, lk2]).astype(F32)
    sw = subln_w.reshape(1, D_VA)
    oa_p = _attn_prompt(pp.q_bf, pp.k_bf, pp.v_t, lamp, sw, lam_init, bp, lp)
    oa_s = _attn_sample(ps.q_bf, ps.k_bf, ps.v_new, cache_k.reshape(bs, past * H_A, 2 * D_A),
                        cache_v.reshape(bs, past * H_A, D_VA), lamp, sw, lam_init, bs, ls)

    nw = gdn_norm_w.reshape(1, D_V)
    ob_p, s_p, c_p = _gdn(pp.gq, pp.zg, pp.small, jnp.zeros((bp, H_B, D_K, D_V), F32),
                          jnp.zeros((bp, CONV_W - 1, G_COLS), F32), conv_w, a_log, dt_bias, nw,
                          bp, lp, CHUNK)
    ob_s, s_s, c_s = _gdn(ps.gq, ps.zg, ps.small, state_delta, state_conv, conv_w, a_log, dt_bias, nw,
                          bs, ls, ls)

    n_experts = w_er.shape[1]
    w_router = jnp.pad(jnp.concatenate([w_gr, w_er], axis=1), ((0, 0), (0, LANES - N_GROUPS - n_experts)))
    b_router = jnp.pad(jnp.concatenate([b_gr, b_er]), (0, LANES - N_GROUPS - n_experts)).reshape(1, LANES)
    mw = (w_proj_a.astype(BF16), w_proj_b.astype(BF16), w_out.astype(BF16), norm_ffn.reshape(1, D),
          w_router, b_router)
    x2_p, t_p, lg_p = _merge(xp, oa_p, ob_p, pp.zg, *mw)
    x2_s, t_s, lg_s = _merge(xs, oa_s, ob_s, ps.zg, *mw)

    blk = MOE_ROWS
    mi, mf, counts = _route(jnp.concatenate([lg_p, lg_s], axis=0))
    n_tok = tp + bs * ls
    dest, blk_exp, n_used, n_blocks, pad_start, pad_len = _moe_slots(
        mi, counts, n_experts, blk, n_tok * TOP_K_INNER)
    xb = _dispatch(t_p, t_s, dest, pad_start, pad_len, n_used, n_blocks * blk, blk)
    yb = _experts(xb, blk_exp, n_used, w_eg, w_eu, w_ed, blk)
    nf = norm_final.reshape(1, D)
    y_p = _combine(x2_p, yb, dest[:tp], mf[:tp], nf, final)
    y_s = _combine(x2_s, yb, dest[tp:], mf[tp:], nf, final)
    return y_p, y_s, pp.k_new, pp.v_new, s_p, c_p, ps.k_new, ps.v_new, s_s, c_s


def kernel(x_prompt, x_sample, cache_k, cache_v, state_delta, state_conv, norm_mix, w_in, lambda_q1, lambda_k1, lambda_q2, lambda_k2, subln_w, conv_w, a_log, dt_bias, gdn_norm_w, w_proj_a, w_proj_b, w_out, norm_ffn, w_group_router, b_group_router, w_expert_router, b_expert_router, w_exp_gate, w_exp_up, w_exp_down, norm_final):
    bp, lp, D = x_prompt.shape
    bs, ls, _ = x_sample.shape
    depth = cache_k.shape[0]
    past = cache_k.shape[2]
    dims = (bp, lp, bs, ls, past)
    xp = x_prompt.reshape(bp * lp, D)
    xs = x_sample.reshape(bs * ls, D)
    kp, vp, sp, cp, ksm, vsm, ssm, csm = [], [], [], [], [], [], [], []
    for l in range(depth):
        prm = (norm_mix[l], w_in[l], lambda_q1[l], lambda_k1[l], lambda_q2[l], lambda_k2[l], subln_w[l],
               conv_w[l], a_log[l], dt_bias[l], gdn_norm_w[l], w_proj_a[l], w_proj_b[l], w_out[l],
               norm_ffn[l], w_group_router[l], b_group_router[l], w_expert_router[l], b_expert_router[l],
               w_exp_gate[l], w_exp_up[l], w_exp_down[l])
        xp, xs, k_p, v_p, s_p, c_p, k_s, v_s, s_s, c_s = _layer(
            xp, xs, l, dims, cache_k[l], cache_v[l], state_delta[l], state_conv[l], prm,
            norm_final, l == depth - 1)
        kp.append(k_p.reshape(bp, lp, H_A, 2 * D_A))
        vp.append(v_p.reshape(bp, lp, H_A, D_VA))
        ksm.append(k_s.reshape(bs, ls, H_A, 2 * D_A))
        vsm.append(v_s.reshape(bs, ls, H_A, D_VA))
        sp.append(s_p)
        cp.append(c_p)
        ssm.append(s_s)
        csm.append(c_s)
    return (xp.reshape(bp, lp, D), xs.reshape(bs, ls, D), jnp.stack(kp), jnp.stack(vp), jnp.stack(sp),
            jnp.stack(cp), jnp.stack(ksm), jnp.stack(vsm), jnp.stack(ssm), jnp.stack(csm))
```

```python
import collections
import functools
import math

import jax
import jax.numpy as jnp
from jax import lax
from jax.experimental import pallas as pl
from jax.experimental.pallas import tpu as pltpu

F32 = jnp.float32
BF16 = jnp.bfloat16
I32 = jnp.int32

CHUNK = 64
H_A = 8
D_A = 64
D_VA = 2 * D_A
ROT_DIM = D_A // 4
ROPE_THETA = 500000.0
H_B = 8
D_K = 128
D_V = 128
CONV_W = 4
N_GROUPS = 4
EXPERTS_PER_GROUP = 8
TOP_K_INNER = 2
EPS = 1e-6

LANES = 128
SUBLANES = 8
V7X_VMEM_BYTES = 64 * 1024 * 1024
VMEM_LIMIT_BYTES = V7X_VMEM_BYTES - 8 * 1024 * 1024

Q_SCALE = (D_A ** -0.5) * math.log2(math.e)
ONES_ROWS = 16
INPROJ_BN = 512
MOE_ROWS = 128
DMA_UNROLL = 8

NT_DIMS = (((1,), (1,)), ((), ()))
TN_DIMS = (((0,), (0,)), ((), ()))

A_COLS = H_A * 2 * D_A
V_COLS = H_A * D_VA
G_COLS = 2 * H_B * D_K + H_B * D_V
Z_COLS = H_B * D_V


def _pick(n, cands):
    for c in cands:
        if n % c == 0:
            return c
    raise ValueError(f"no block size in {cands} divides {n}")


def _cparams(*sem):
    return pltpu.CompilerParams(dimension_semantics=sem, vmem_limit_bytes=VMEM_LIMIT_BYTES)


def _sigmoid(x):
    return 1.0 / (1.0 + jnp.exp(-x))


def _silu(x):
    return x * _sigmoid(x)


def _softplus(x):
    return jnp.maximum(x, 0.0) + jnp.log1p(jnp.exp(-jnp.abs(x)))


def _bdot(a, b):
    return jnp.dot(a.astype(BF16), b.astype(BF16), preferred_element_type=F32)


def _split2(a):
    hi = a.astype(BF16)
    lo = (a - hi.astype(F32)).astype(BF16)
    return hi, lo


def _dot3(a, b):
    ah, al = _split2(a)
    bh, bl = _split2(b)
    d = functools.partial(jnp.dot, preferred_element_type=F32)
    return d(ah, bh) + d(ah, bl) + d(al, bh)


def _dot_exact_mask(mask, x, mask_is_lhs):
    hi = x.astype(BF16)
    r1 = x - hi.astype(F32)
    mid = r1.astype(BF16)
    lo = (r1 - mid.astype(F32)).astype(BF16)
    mb = mask.astype(BF16)
    d = functools.partial(jnp.dot, preferred_element_type=F32)
    if mask_is_lhs:
        return d(mb, hi) + d(mb, mid) + d(mb, lo)
    return d(hi, mb) + d(mid, mb) + d(lo, mb)


InProj = collections.namedtuple("InProj", "small k_new v_new q_bf k_bf v_t gq zg")


def _inproj_body(o_k, o_v, o_g, o_z, x_ref, nw_ref, cos_ref, sa_ref, sb_ref, w_ref, ws_ref,
                 os_ref, k_ref, v_ref, qb_ref, kb_ref, vt_ref, gq_ref, zg_ref, h_scr):
    j = pl.program_id(1)

    @pl.when(j == 0)
    def _():
        x = x_ref[...]
        ms = jnp.mean(x * x, axis=-1, keepdims=True)
        h = (x * lax.rsqrt(ms + EPS) * nw_ref[...]).astype(BF16)
        h_scr[...] = h
        os_ref[...] = jnp.dot(h, ws_ref[...], preferred_element_type=F32)

    acc = jnp.dot(h_scr[...], w_ref[...], preferred_element_type=F32)
    n_lane_blocks = acc.shape[1] // LANES

    def rotated(c):
        blk = acc[:, c * LANES:(c + 1) * LANES]
        return (blk * cos_ref[...] + pltpu.roll(blk, ROT_DIM // 2, 1) * sa_ref[...]
                + pltpu.roll(blk, LANES - ROT_DIM // 2, 1) * sb_ref[...])

    @pl.when(j < o_k)
    def _():
        for c in range(n_lane_blocks):
            qb_ref[:, c * LANES:(c + 1) * LANES] = (rotated(c) * Q_SCALE).astype(BF16)

    bm = acc.shape[0]
    hpb = n_lane_blocks
    for jj in range(o_v - o_k):
        @pl.when(j == o_k + jj)
        def _(jj=jj):
            for c in range(hpb):
                rot = rotated(c)
                k_ref[pl.ds(jj * hpb + c, bm, stride=H_A), :] = rot
                kb_ref[:, c * LANES:(c + 1) * LANES] = rot.astype(BF16)

    for jj in range(o_g - o_v):
        @pl.when(j == o_v + jj)
        def _(jj=jj):
            for c in range(hpb):
                blk = acc[:, c * D_VA:(c + 1) * D_VA]
                v_ref[pl.ds(jj * hpb + c, bm, stride=H_A), :] = blk
                vt_ref[c, 0:D_VA, :] = blk.T.astype(BF16)
                vt_ref[c, D_VA:D_VA + ONES_ROWS, :] = jnp.ones((ONES_ROWS, bm), BF16)

    @pl.when(jnp.logical_and(j >= o_g, j < o_z))
    def _():
        gq_ref[...] = acc

    @pl.when(j >= o_z)
    def _():
        zg_ref[...] = acc.astype(BF16)


def _inproj(x, norm_w, tables, w_main, w_small):
    T, D = x.shape
    bn = INPROJ_BN
    n_main = w_main.shape[1]
    zg_cols = Z_COLS + 2 * D
    assert A_COLS % bn == 0 and V_COLS % bn == 0 and G_COLS % bn == 0 and zg_cols % bn == 0
    assert bn % D_VA == 0 and D_VA == LANES and n_main == 2 * A_COLS + V_COLS + G_COLS + zg_cols
    nb_a, nb_v, nb_g, nb_z = A_COLS // bn, V_COLS // bn, G_COLS // bn, zg_cols // bn
    o_k, o_v = nb_a, 2 * nb_a
    o_g = o_v + nb_v
    o_z = o_g + nb_g
    bm = _pick(T, (1024, 512, 256, 128))
    hpb = bn // D_VA

    def col(lo, n):
        return lambda i, j: (i, jnp.clip(j - lo, 0, n - 1))

    cos_t, sa_t, sb_t = tables
    outs = pl.pallas_call(
        functools.partial(_inproj_body, o_k, o_v, o_g, o_z),
        grid=(T // bm, n_main // bn),
        in_specs=[
            pl.BlockSpec((bm, D), lambda i, j: (i, 0), pipeline_mode=pl.Buffered(1)),
            pl.BlockSpec((1, D), lambda i, j: (0, 0)),
            pl.BlockSpec((bm, LANES), lambda i, j: (i, 0)),
            pl.BlockSpec((bm, LANES), lambda i, j: (i, 0)),
            pl.BlockSpec((bm, LANES), lambda i, j: (i, 0)),
            pl.BlockSpec((D, bn), lambda i, j: (0, j)),
            pl.BlockSpec((D, LANES), lambda i, j: (0, 0)),
        ],
        out_specs=[
            pl.BlockSpec((bm, LANES), lambda i, j: (i, 0)),
            pl.BlockSpec((bm * H_A, D_VA), lambda i, j: (i, 0)),
            pl.BlockSpec((bm * H_A, D_VA), lambda i, j: (i, 0)),
            pl.BlockSpec((bm, bn), col(0, nb_a)),
            pl.BlockSpec((bm, bn), col(o_k, nb_a)),
            pl.BlockSpec((hpb, D_VA + ONES_ROWS, bm), lambda i, j: (jnp.clip(j - o_v, 0, nb_v - 1), 0, i)),
            pl.BlockSpec((bm, bn), col(o_g, nb_g)),
            pl.BlockSpec((bm, bn), col(o_z, nb_z)),
        ],
        out_shape=[
            jax.ShapeDtypeStruct((T, LANES), F32),
            jax.ShapeDtypeStruct((T * H_A, 2 * D_A), F32),
            jax.ShapeDtypeStruct((T * H_A, D_VA), F32),
            jax.ShapeDtypeStruct((T, A_COLS), BF16),
            jax.ShapeDtypeStruct((T, A_COLS), BF16),
            jax.ShapeDtypeStruct((H_A, D_VA + ONES_ROWS, T), BF16),
            jax.ShapeDtypeStruct((T, G_COLS), F32),
            jax.ShapeDtypeStruct((T, zg_cols), BF16),
        ],
        scratch_shapes=[pltpu.VMEM((bm, D), BF16)],
        compiler_params=_cparams("parallel", "arbitrary"),
        name="inproj",
    )(x, norm_w, cos_t, sa_t, sb_t, w_main, w_small)
    return InProj(*outs)


def _diff_lambda(lamp, lam_init):
    s1 = jnp.sum(lamp[0:1] * lamp[1:2], axis=1, keepdims=True)
    s2 = jnp.sum(lamp[2:3] * lamp[3:4], axis=1, keepdims=True)
    return jnp.exp(s1) - jnp.exp(s2) + lam_init


def _subln(o, w, lam_init):
    ms = jnp.mean(o * o, axis=-1, keepdims=True)
    return o * lax.rsqrt(ms + EPS) * w * (1.0 - lam_init)


def _attn_prompt_body(lam_init, bq, q_ref, k_ref, vt_ref, lamp_ref, swc_ref, o_ref,
                      m_scr, acc_scr, s_scr):
    qi = pl.program_id(2)
    q = q_ref[...]
    lane = lax.broadcasted_iota(I32, q.shape, 1)
    qz = [jnp.where(lane < D_A, q, jnp.zeros_like(q)), jnp.where(lane >= D_A, q, jnp.zeros_like(q))]
    m_scr[...] = jnp.full(m_scr.shape, -jnp.inf, F32)
    acc_scr[...] = jnp.zeros(acc_scr.shape, F32)
    key_chunk = lax.broadcasted_iota(I32, (bq, bq), 0) // CHUNK
    qry_chunk = lax.broadcasted_iota(I32, (bq, bq), 1) // CHUNK
    diag_mask = key_chunk <= qry_chunk

    def scores(j, slot):
        kb = k_ref[pl.ds(pl.multiple_of(j * bq, bq), bq), :]
        for m in range(2):
            s_scr[slot, m] = lax.dot_general(kb, qz[m], NT_DIMS, preferred_element_type=F32)

    def accumulate(j, slot, masked):
        vb = vt_ref[:, pl.ds(pl.multiple_of(j * bq, bq), bq)]
        for m in range(2):
            s = s_scr[slot, m]
            if masked:
                s = jnp.where(diag_mask, s, -jnp.inf)
            m_prev = m_scr[m]
            m_new = jnp.maximum(m_prev, jnp.max(s, axis=0, keepdims=True))
            alpha = jnp.exp2(m_prev - m_new)
            p = jnp.exp2(s - m_new).astype(BF16)
            acc_scr[m] = alpha * acc_scr[m] + jnp.dot(vb, p, preferred_element_type=F32)
            m_scr[m] = m_new

    scores(0, 0)

    def two_full_blocks(t, carry):
        j = 2 * t
        scores(j + 1, 1)
        accumulate(j, 0, False)
        scores(j + 2, 0)
        accumulate(j + 1, 1, False)
        return carry

    lax.fori_loop(0, qi // 2, two_full_blocks, 0)
    j0 = 2 * (qi // 2)

    @pl.when(j0 == qi)
    def _():
        accumulate(qi, 0, True)

    @pl.when(j0 != qi)
    def _():
        scores(qi, 1)
        accumulate(j0, 0, False)
        accumulate(qi, 1, True)

    lam = _diff_lambda(lamp_ref[...], lam_init)
    a0 = acc_scr[0]
    a1 = acc_scr[1]
    o_t = a0[0:D_VA] / a0[D_VA:D_VA + 1] - lam * (a1[0:D_VA] / a1[D_VA:D_VA + 1])
    ms = jnp.mean(o_t * o_t, axis=0, keepdims=True)
    o_t = o_t * lax.rsqrt(ms + EPS) * swc_ref[...] * (1.0 - lam_init)
    o_ref[...] = o_t.T.astype(o_ref.dtype)


def _attn_prompt(q_bf, k_bf, v_t, lamp, subln_w, lam_init, bp, lp):
    bq = _pick(lp, (512, 256, 128))
    nq = lp // bq
    vrows = v_t.shape[1]
    return pl.pallas_call(
        functools.partial(_attn_prompt_body, lam_init, bq),
        grid=(bp, H_A, nq),
        in_specs=[
            pl.BlockSpec((bq, LANES), lambda b, h, i: (b * nq + i, h)),
            pl.BlockSpec((lp, LANES), lambda b, h, i: (b, h)),
            pl.BlockSpec((None, vrows, lp), lambda b, h, i: (h, 0, b)),
            pl.BlockSpec((4, D_A), lambda b, h, i: (0, 0)),
            pl.BlockSpec((D_VA, 1), lambda b, h, i: (0, 0)),
        ],
        out_specs=pl.BlockSpec((bq, LANES), lambda b, h, i: (b * nq + i, h)),
        out_shape=jax.ShapeDtypeStruct((bp * lp, V_COLS), BF16),
        scratch_shapes=[
            pltpu.VMEM((2, 1, bq), F32),
            pltpu.VMEM((2, vrows, bq), F32),
            pltpu.VMEM((2, 2, bq, bq), F32),
        ],
        compiler_params=_cparams("parallel", "parallel", "arbitrary"),
        name="attn_prompt",
    )(q_bf, k_bf, v_t, lamp, subln_w.reshape(D_VA, 1))


def _attn_sample_body(lam_init, q_ref, kn_ref, vn_ref, kc_ref, vc_ref, lamp_ref, sw_ref, o_ref,
                      m_scr, l_scr, acc_scr):
    pj = pl.program_id(1)
    npos = kc_ref.shape[0] // H_A
    ls = q_ref.shape[0]

    @pl.when(pj == 0)
    def _():
        m_scr[...] = jnp.full(m_scr.shape, -jnp.inf, F32)
        l_scr[...] = jnp.zeros(l_scr.shape, F32)
        acc_scr[...] = jnp.zeros(acc_scr.shape, F32)

    lane = lax.broadcasted_iota(I32, (ls, D_VA), 1)

    def q_maps(h):
        q = q_ref[:, h * D_VA:(h + 1) * D_VA]
        return jnp.concatenate([jnp.where(lane < D_A, q, jnp.zeros_like(q)),
                                jnp.where(lane >= D_A, q, jnp.zeros_like(q))], axis=0)

    def update(h, qz, keys, vals):
        s = lax.dot_general(qz, keys, NT_DIMS, preferred_element_type=F32)
        m_prev = m_scr[h]
        m_new = jnp.maximum(m_prev, jnp.max(s, axis=1, keepdims=True))
        alpha = jnp.exp2(m_prev - m_new)
        p = jnp.exp2(s - m_new)
        l_scr[h] = alpha * l_scr[h] + jnp.sum(p, axis=1, keepdims=True)
        acc_scr[h] = alpha * acc_scr[h] + jnp.dot(p.astype(BF16), vals, preferred_element_type=F32)
        m_scr[h] = m_new

    for h in range(H_A):
        kc = kc_ref[pl.ds(h, npos, stride=H_A), :].astype(BF16)
        vc = vc_ref[pl.ds(h, npos, stride=H_A), :].astype(BF16)
        update(h, q_maps(h), kc, vc)

    @pl.when(pj == pl.num_programs(1) - 1)
    def _():
        lam = _diff_lambda(lamp_ref[...], lam_init)
        sw = sw_ref[...]
        for h in range(H_A):
            kn = kn_ref[:, h * D_VA:(h + 1) * D_VA]
            vn = vn_ref[pl.ds(h, ls, stride=H_A), :].astype(BF16)
            update(h, q_maps(h), kn, vn)
            both = acc_scr[h] / l_scr[h]
            o = both[0:ls] - lam * both[ls:2 * ls]
            o_ref[:, h * D_VA:(h + 1) * D_VA] = _subln(o, sw, lam_init).astype(o_ref.dtype)


def _attn_sample(q_bf, k_bf, v_new, cache_k, cache_v, lamp, subln_w, lam_init, bs, ls):
    past = cache_k.shape[1] // H_A
    pb = _pick(past, (1024, 512, 256, 128, 64))
    assert ls % (2 * SUBLANES) == 0
    return pl.pallas_call(
        functools.partial(_attn_sample_body, lam_init),
        grid=(bs, past // pb),
        in_specs=[
            pl.BlockSpec((ls, A_COLS), lambda b, j: (b, 0)),
            pl.BlockSpec((ls, A_COLS), lambda b, j: (b, 0)),
            pl.BlockSpec((ls * H_A, D_VA), lambda b, j: (b, 0)),
            pl.BlockSpec((None, pb * H_A, D_VA), lambda b, j: (b, j, 0)),
            pl.BlockSpec((None, pb * H_A, D_VA), lambda b, j: (b, j, 0)),
            pl.BlockSpec((4, D_A), lambda b, j: (0, 0)),
            pl.BlockSpec((1, D_VA), lambda b, j: (0, 0)),
        ],
        out_specs=pl.BlockSpec((ls, V_COLS), lambda b, j: (b, 0)),
        out_shape=jax.ShapeDtypeStruct((bs * ls, V_COLS), BF16),
        scratch_shapes=[
            pltpu.VMEM((H_A, 2 * ls, 1), F32),
            pltpu.VMEM((H_A, 2 * ls, 1), F32),
            pltpu.VMEM((H_A, 2 * ls, D_VA), F32),
        ],
        compiler_params=_cparams("parallel", "arbitrary"),
        name="attn_sample",
    )(q_bf, k_bf, v_new, cache_k, cache_v, lamp, subln_w)


def _gdn_body(chunk, rows, pq_ref, pk_ref, pv_ref, pz_ref, sm_ref, smt_ref, cw_ref, alr_ref, dtr_ref,
              alc_ref, dtc_ref, nw_ref, s0_ref, c0_ref, o_ref, s_ref, cn_ref, ext_scr, act_scr):
    i = pl.program_id(1)
    nblk = pl.num_programs(1)
    hk = H_B * D_K
    hv = H_B * D_V
    n_chunks = rows // chunk
    n_steps = int(math.log2(chunk))
    assert 2 ** n_steps == chunk

    @pl.when(i == 0)
    def _():
        s_ref[...] = s0_ref[...]
        ext_scr[0:SUBLANES, :] = jnp.zeros((SUBLANES, ext_scr.shape[1]), F32)
        ext_scr[pl.ds(SUBLANES - (CONV_W - 1), CONV_W - 1), :] = c0_ref[...]

    ext_scr[pl.ds(SUBLANES, rows), 0:hk] = pq_ref[...]
    ext_scr[pl.ds(SUBLANES, rows), hk:2 * hk] = pk_ref[...]
    ext_scr[pl.ds(SUBLANES, rows), 2 * hk:2 * hk + hv] = pv_ref[...]

    @pl.when(i == nblk - 1)
    def _():
        cn_ref[...] = ext_scr[pl.ds(SUBLANES + rows - (CONV_W - 1), CONV_W - 1), :]

    conv = jnp.zeros((rows, ext_scr.shape[1]), F32)
    for w in range(CONV_W):
        conv = conv + ext_scr[pl.ds(SUBLANES - (CONV_W - 1) + w, rows), :] * cw_ref[w:w + 1, :]
    act_scr[...] = _silu(conv)
    ext_scr[0:SUBLANES, :] = ext_scr[pl.ds(rows, SUBLANES), :]

    sm = sm_ref[...]
    beta_all = _sigmoid(sm[:, 0:H_B])
    g_all = -jnp.exp(alr_ref[...]) * _softplus(sm[:, H_B:2 * H_B] + dtr_ref[...])
    smt = smt_ref[...]
    g_all_t = -jnp.exp(alc_ref[...]) * _softplus(smt[H_B:2 * H_B, :] + dtc_ref[...])

    ri = lax.broadcasted_iota(I32, (chunk, chunk), 0)
    ci = lax.broadcasted_iota(I32, (chunk, chunk), 1)
    tri = ri >= ci
    strict = ri > ci
    eye = (ri == ci).astype(F32)
    ltri = tri.astype(F32)
    utri = (ri <= ci).astype(F32)
    nw = nw_ref[...]

    units = [(c, h) for c in range(n_chunks) for h in range(H_B)]
    gcs = [_dot_exact_mask(ltri, g_all[c * chunk:(c + 1) * chunk, :], True) for c in range(n_chunks)]
    grs = [_dot_exact_mask(utri, g_all_t[:, c * chunk:(c + 1) * chunk], False) for c in range(n_chunks)]
    pre = []
    for c, h in units:
        r0 = c * chunk
        q = act_scr[r0:r0 + chunk, h * D_K:(h + 1) * D_K]
        k = act_scr[r0:r0 + chunk, hk + h * D_K:hk + (h + 1) * D_K]
        v = act_scr[r0:r0 + chunk, 2 * hk + h * D_V:2 * hk + (h + 1) * D_V]
        q = q * lax.rsqrt(jnp.sum(q * q, axis=-1, keepdims=True) + EPS) * (D_K ** -0.5)
        k = k * lax.rsqrt(jnp.sum(k * k, axis=-1, keepdims=True) + EPS)
        beta = beta_all[r0:r0 + chunk, h:h + 1]
        gc = gcs[c][:, h:h + 1]
        gr = grs[c][h:h + 1, :]
        g_last = gc[chunk - 1:chunk, :]
        decay = jnp.where(tri, jnp.exp(jnp.where(tri, gc - gr, 0.0)), 0.0)
        kb = k * beta
        eg = jnp.exp(gc)
        kbf = k.astype(BF16)
        kk = lax.dot_general(kb.astype(BF16), kbf, NT_DIMS, preferred_element_type=F32)
        qk = lax.dot_general(q.astype(BF16), kbf, NT_DIMS, preferred_element_type=F32)
        pre.append(dict(
            nm=-jnp.where(strict, kk * decay, 0.0),
            rhs=jnp.concatenate([v * beta, kb * eg], axis=1),
            qk=jnp.where(tri, qk * decay, 0.0).astype(BF16),
            qg=(q * eg).astype(BF16),
            kg=(k * jnp.exp(g_last - gc)).astype(BF16),
            gl=jnp.exp(g_last)))
    invs = [eye + p["nm"] for p in pre]
    pws = [p["nm"] for p in pre]
    for _ in range(n_steps - 1):
        pws = [_bdot(pw, pw) for pw in pws]
        invs = [inv + _bdot(inv, pw) for inv, pw in zip(invs, pws)]
    resid = [eye - _dot3(eye - p["nm"], inv) for p, inv in zip(pre, invs)]
    invs = [inv + _bdot(inv, r) for inv, r in zip(invs, resid)]
    uws = [_dot3(inv, p["rhs"]) for p, inv in zip(pre, invs)]

    for (c, h), p, uw in zip(units, pre, uws):
        r0 = c * chunk
        u = uw[:, 0:D_V]
        wmat = uw[:, D_V:D_V + D_K]
        s = s_ref[h]
        sb = s.astype(BF16)
        v_new = u - jnp.dot(wmat.astype(BF16), sb, preferred_element_type=F32)
        v_new_b = v_new.astype(BF16)
        o = (jnp.dot(p["qg"], sb, preferred_element_type=F32)
             + jnp.dot(p["qk"], v_new_b, preferred_element_type=F32))
        s_ref[h] = s * p["gl"] + lax.dot_general(p["kg"], v_new_b, TN_DIMS, preferred_element_type=F32)
        z = pz_ref[r0:r0 + chunk, h * D_V:(h + 1) * D_V].astype(F32)
        o = o * lax.rsqrt(jnp.mean(o * o, axis=-1, keepdims=True) + EPS) * nw * _silu(z)
        o_ref[r0:r0 + chunk, h * D_V:(h + 1) * D_V] = o.astype(o_ref.dtype)


def _gdn(gq, zg, small, s0, c0, conv_w, a_log, dt_bias, norm_w, bn, ln, chunk):
    hk = H_B * D_K
    assert H_B * D_V == hk and Z_COLS == hk
    rows = ln
    for cand in (2 * chunk, chunk):
        if cand % LANES == 0 and ln % cand == 0:
            rows = cand
            break
    assert rows % chunk == 0 and rows % SUBLANES == 0
    nblk = ln // rows
    small_t = jnp.swapaxes(small[:, 0:2 * H_B].reshape(bn, ln, 2 * H_B), 1, 2)
    alr = a_log.reshape(1, H_B)
    dtr = dt_bias.reshape(1, H_B)
    alc = a_log.reshape(H_B, 1)
    dtc = dt_bias.reshape(H_B, 1)
    row_blk = lambda b, i: b * nblk + i
    return pl.pallas_call(
        functools.partial(_gdn_body, chunk, rows),
        grid=(bn, nblk),
        in_specs=[
            pl.BlockSpec((rows, hk), lambda b, i: (row_blk(b, i), 0)),
            pl.BlockSpec((rows, hk), lambda b, i: (row_blk(b, i), 1)),
            pl.BlockSpec((rows, hk), lambda b, i: (row_blk(b, i), 2)),
            pl.BlockSpec((rows, hk), lambda b, i: (row_blk(b, i), 0)),
            pl.BlockSpec((rows, LANES), lambda b, i: (row_blk(b, i), 0)),
            pl.BlockSpec((None, 2 * H_B, rows), lambda b, i: (b, 0, i)),
            pl.BlockSpec((CONV_W, G_COLS), lambda b, i: (0, 0)),
            pl.BlockSpec((1, H_B), lambda b, i: (0, 0)),
            pl.BlockSpec((1, H_B), lambda b, i: (0, 0)),
            pl.BlockSpec((H_B, 1), lambda b, i: (0, 0)),
            pl.BlockSpec((H_B, 1), lambda b, i: (0, 0)),
            pl.BlockSpec((1, D_V), lambda b, i: (0, 0)),
            pl.BlockSpec((None, H_B, D_K, D_V), lambda b, i: (b, 0, 0, 0)),
            pl.BlockSpec((None, CONV_W - 1, G_COLS), lambda b, i: (b, 0, 0)),
        ],
        out_specs=[
            pl.BlockSpec((rows, hk), lambda b, i: (row_blk(b, i), 0)),
            pl.BlockSpec((None, H_B, D_K, D_V), lambda b, i: (b, 0, 0, 0)),
            pl.BlockSpec((None, CONV_W - 1, G_COLS), lambda b, i: (b, 0, 0)),
        ],
        out_shape=[
            jax.ShapeDtypeStruct((bn * ln, hk), BF16),
            jax.ShapeDtypeStruct(s0.shape, F32),
            jax.ShapeDtypeStruct(c0.shape, F32),
        ],
        scratch_shapes=[
            pltpu.VMEM((rows + SUBLANES, G_COLS), F32),
            pltpu.VMEM((rows, G_COLS), F32),
        ],
        compiler_params=_cparams("parallel", "arbitrary"),
        name="gdn",
    )(gq, gq, gq, zg, small, small_t, conv_w, alr, dtr, alc, dtc, norm_w, s0, c0)


def _merge_body(x_ref, oa_ref, ob_ref, ga0_ref, ga1_ref, gb0_ref, gb1_ref, wa_ref, wb_ref, wo_ref,
                nf_ref, wr_ref, br_ref, x2_ref, t_ref, lg_ref):
    ya = jnp.dot(oa_ref[...], wa_ref[...], preferred_element_type=F32)
    yb = jnp.dot(ob_ref[...], wb_ref[...], preferred_element_type=F32)
    half = ga0_ref.shape[1]
    gate = lambda ref: _sigmoid(ref[...].astype(F32))
    m0 = gate(ga0_ref) * ya[:, :half] + gate(gb0_ref) * yb[:, :half]
    m1 = gate(ga1_ref) * ya[:, half:] + gate(gb1_ref) * yb[:, half:]
    merged = jnp.concatenate([m0, m1], axis=1).astype(BF16)
    x2 = x_ref[...] + jnp.dot(merged, wo_ref[...], preferred_element_type=F32)
    x2_ref[...] = x2
    ms = jnp.mean(x2 * x2, axis=-1, keepdims=True)
    t = x2 * lax.rsqrt(ms + EPS) * nf_ref[...]
    t_ref[...] = t
    lg_ref[...] = _dot3(t, wr_ref[...]) + br_ref[...]


def _merge(x, o_a, o_b, zg, wa, wb, wo, norm_ffn, w_router, b_router):
    T, D = x.shape
    half = D // 2
    assert Z_COLS % half == 0
    gcol = Z_COLS // half
    bm = _pick(T, (256, 128, 64, 32, 16, 8))
    const = dict(pipeline_mode=pl.Buffered(1))
    return pl.pallas_call(
        _merge_body,
        grid=(T // bm,),
        in_specs=[
            pl.BlockSpec((bm, D), lambda i: (i, 0)),
            pl.BlockSpec((bm, o_a.shape[1]), lambda i: (i, 0)),
            pl.BlockSpec((bm, o_b.shape[1]), lambda i: (i, 0)),
            pl.BlockSpec((bm, half), lambda i: (i, gcol)),
            pl.BlockSpec((bm, half), lambda i: (i, gcol + 1)),
            pl.BlockSpec((bm, half), lambda i: (i, gcol + 2)),
            pl.BlockSpec((bm, half), lambda i: (i, gcol + 3)),
            pl.BlockSpec(wa.shape, lambda i: (0, 0), **const),
            pl.BlockSpec(wb.shape, lambda i: (0, 0), **const),
            pl.BlockSpec(wo.shape, lambda i: (0, 0), **const),
            pl.BlockSpec((1, D), lambda i: (0, 0)),
            pl.BlockSpec((D, LANES), lambda i: (0, 0), **const),
            pl.BlockSpec((1, LANES), lambda i: (0, 0)),
        ],
        out_specs=[
            pl.BlockSpec((bm, D), lambda i: (i, 0)),
            pl.BlockSpec((bm, D), lambda i: (i, 0)),
            pl.BlockSpec((bm, LANES), lambda i: (i, 0)),
        ],
        out_shape=[
            jax.ShapeDtypeStruct((T, D), F32),
            jax.ShapeDtypeStruct((T, D), F32),
            jax.ShapeDtypeStruct((T, LANES), F32),
        ],
        compiler_params=_cparams("parallel"),
        name="merge",
    )(x, o_a, o_b, zg, zg, zg, zg, wa, wb, wo, norm_ffn, w_router, b_router)


def _route_body(lg_ref, mi_ref, mf_ref, cnt_ref):
    i = pl.program_id(0)

    @pl.when(i == 0)
    def _():
        cnt_ref[...] = jnp.zeros(cnt_ref.shape, F32)

    lg = lg_ref[...]
    bm = lg.shape[0]
    lane = lax.broadcasted_iota(I32, lg.shape, 1).astype(F32)
    big = jnp.float32(LANES)
    neg = -jnp.inf
    gl = jnp.where(lane < N_GROUPS, lg, neg)
    gmax = jnp.max(gl, axis=1, keepdims=True)
    gidx = jnp.min(jnp.where(gl == gmax, lane, big), axis=1, keepdims=True)
    gw = 1.0 / jnp.sum(jnp.exp(gl - gmax), axis=1, keepdims=True)
    e_lo = N_GROUPS + gidx * EXPERTS_PER_GROUP
    valid = (lane >= e_lo) & (lane < e_lo + EXPERTS_PER_GROUP)
    el = jnp.where(valid, lg, neg)
    v1 = jnp.max(el, axis=1, keepdims=True)
    i1 = jnp.min(jnp.where(el == v1, lane, big), axis=1, keepdims=True)
    el2 = jnp.where(lane == i1, neg, el)
    v2 = jnp.max(el2, axis=1, keepdims=True)
    i2 = jnp.min(jnp.where(el2 == v2, lane, big), axis=1, keepdims=True)
    e21 = jnp.exp(v2 - v1)
    w1 = gw / (1.0 + e21)
    w2 = gw * e21 / (1.0 + e21)
    oh1 = (lane == i1).astype(F32)
    oh2 = (lane == i2).astype(F32)
    oh = oh1 + oh2
    rr = lax.broadcasted_iota(I32, (bm, bm), 0)
    cc = lax.broadcasted_iota(I32, (bm, bm), 1)
    before = (cc < rr).astype(BF16)
    cum = jnp.dot(before, oh.astype(BF16), preferred_element_type=F32) + cnt_ref[...]
    rank1 = jnp.sum(cum * oh1, axis=1, keepdims=True)
    rank2 = jnp.sum(cum * oh2, axis=1, keepdims=True)
    cnt_ref[...] = cnt_ref[...] + jnp.sum(oh, axis=0, keepdims=True)
    mi = jnp.where(lane == 0, i1 - N_GROUPS, 0.0)
    mi = jnp.where(lane == 1, i2 - N_GROUPS, mi)
    mi = jnp.where(lane == 2, rank1, mi)
    mi = jnp.where(lane == 3, rank2, mi)
    mi_ref[...] = mi.astype(I32)
    mf_ref[...] = jnp.where(lane == 0, w1, jnp.where(lane == 1, w2, 0.0))


def _route(logits):
    T = logits.shape[0]
    bm = _pick(T, (512, 256, 128, 64, 32, 16, 8))
    return pl.pallas_call(
        _route_body,
        grid=(T // bm,),
        in_specs=[pl.BlockSpec((bm, LANES), lambda i: (i, 0))],
        out_specs=[
            pl.BlockSpec((bm, LANES), lambda i: (i, 0)),
            pl.BlockSpec((bm, LANES), lambda i: (i, 0)),
            pl.BlockSpec((1, LANES), lambda i: (0, 0)),
        ],
        out_shape=[
            jax.ShapeDtypeStruct((T, LANES), I32),
            jax.ShapeDtypeStruct((T, LANES), F32),
            jax.ShapeDtypeStruct((1, LANES), F32),
        ],
        compiler_params=_cparams("arbitrary"),
        name="route",
    )(logits)


def _dispatch_body(bm, blk, n_experts, nbp, ps_ref, pn_ref, nu_ref, dest_ref, tp_ref, ts_ref, xb_ref,
                   zero_scr, sem, zsem):
    i = pl.program_id(0)
    n_blocks = xb_ref.shape[0] // blk

    def pad_rows(act):
        def per_expert(e, carry):
            def per_row(r, c2):
                act(pltpu.make_async_copy(zero_scr.at[pl.ds(0, 1)], xb_ref.at[pl.ds(ps_ref[e] + r, 1)], zsem))
                return c2
            return lax.fori_loop(0, pn_ref[e], per_row, carry)
        lax.fori_loop(0, n_experts, per_expert, 0)

    def tail_blocks(act):
        def per_block(b, carry):
            act(pltpu.make_async_copy(zero_scr, xb_ref.at[pl.ds(pl.multiple_of(b * blk, blk), blk)], zsem))
            return carry
        lax.fori_loop(nu_ref[0], n_blocks, per_block, 0)

    @pl.when(i == 0)
    def _():
        zero_scr[...] = jnp.zeros(zero_scr.shape, zero_scr.dtype)
        pad_rows(lambda cp: cp.start())
        tail_blocks(lambda cp: cp.start())

    def send_rows(t_ref):
        def row_copy(r, k):
            d = dest_ref[0, TOP_K_INNER * r + k]
            return pltpu.make_async_copy(t_ref.at[pl.ds(r, 1)], xb_ref.at[pl.ds(d, 1)], sem)

        def issue(r, carry):
            for k in range(TOP_K_INNER):
                row_copy(r, k).start()
            return carry

        def drain(r, carry):
            for k in range(TOP_K_INNER):
                row_copy(r, k).wait()
            return carry

        lax.fori_loop(0, bm, issue, 0, unroll=DMA_UNROLL)
        lax.fori_loop(0, bm, drain, 0, unroll=DMA_UNROLL)

    @pl.when(i < nbp)
    def _():
        send_rows(tp_ref)

    @pl.when(i >= nbp)
    def _():
        send_rows(ts_ref)

    @pl.when(i == 0)
    def _():
        pad_rows(lambda cp: cp.wait())
        tail_blocks(lambda cp: cp.wait())


def _dispatch(t_p, t_s, dest, pad_start, pad_len, n_used, n_rows, blk):
    (tp, D), ts = t_p.shape, t_s.shape[0]
    bm = _pick(math.gcd(tp, ts), (256, 128, 64, 32, 16, 8))
    nbp, nbs = tp // bm, ts // bm
    dest3 = dest.reshape(nbp + nbs, 1, TOP_K_INNER * bm)
    grid_spec = pltpu.PrefetchScalarGridSpec(
        num_scalar_prefetch=3,
        grid=(nbp + nbs,),
        in_specs=[
            pl.BlockSpec((None, 1, TOP_K_INNER * bm), lambda i, ps, pn, nu: (i, 0, 0),
                         memory_space=pltpu.SMEM),
            pl.BlockSpec((bm, D), lambda i, ps, pn, nu: (jnp.minimum(i, nbp - 1), 0)),
            pl.BlockSpec((bm, D), lambda i, ps, pn, nu: (jnp.maximum(i - nbp, 0), 0)),
        ],
        out_specs=pl.BlockSpec(memory_space=pl.ANY),
        scratch_shapes=[
            pltpu.VMEM((blk, D), t_p.dtype),
            pltpu.SemaphoreType.DMA(()),
            pltpu.SemaphoreType.DMA(()),
        ],
    )
    return pl.pallas_call(
        functools.partial(_dispatch_body, bm, blk, pad_start.shape[0], nbp),
        grid_spec=grid_spec,
        out_shape=jax.ShapeDtypeStruct((n_rows, D), t_p.dtype),
        compiler_params=_cparams("arbitrary"),
        name="dispatch",
    )(pad_start, pad_len, n_used, dest3, t_p, t_s)


def _expert_body(be_ref, nu_ref, x_ref, wg_ref, wu_ref, wd_ref, y_ref, wg_scr, wu_scr, wd_scr):
    i = pl.program_id(0)

    @pl.when(i < nu_ref[0])
    def _():
        @pl.when(jnp.logical_or(i == 0, be_ref[i] != be_ref[jnp.maximum(i - 1, 0)]))
        def _():
            wg_scr[...] = wg_ref[...].astype(BF16)
            wu_scr[...] = wu_ref[...].astype(BF16)
            wd_scr[...] = wd_ref[...].astype(BF16)

        x = x_ref[...].astype(BF16)
        g = jnp.dot(x, wg_scr[...], preferred_element_type=F32)
        u = jnp.dot(x, wu_scr[...], preferred_element_type=F32)
        hmid = (_silu(g) * u).astype(BF16)
        y_ref[...] = jnp.dot(hmid, wd_scr[...], preferred_element_type=F32)

    @pl.when(i >= nu_ref[0])
    def _():
        y_ref[...] = jnp.zeros(y_ref.shape, F32)


def _experts(xb, blk_exp, n_used, w_gate, w_up, w_down, blk):
    P, D = xb.shape
    de = w_gate.shape[2]
    n_blocks = P // blk
    last = lambda i, nu: jnp.minimum(i, nu[0] - 1)
    grid_spec = pltpu.PrefetchScalarGridSpec(
        num_scalar_prefetch=2,
        grid=(n_blocks,),
        in_specs=[
            pl.BlockSpec((blk, D), lambda i, be, nu: (last(i, nu), 0)),
            pl.BlockSpec((None, D, de), lambda i, be, nu: (be[last(i, nu)], 0, 0)),
            pl.BlockSpec((None, D, de), lambda i, be, nu: (be[last(i, nu)], 0, 0)),
            pl.BlockSpec((None, de, D), lambda i, be, nu: (be[last(i, nu)], 0, 0)),
        ],
        out_specs=pl.BlockSpec((blk, D), lambda i, be, nu: (i, 0)),
        scratch_shapes=[
            pltpu.VMEM((D, de), BF16),
            pltpu.VMEM((D, de), BF16),
            pltpu.VMEM((de, D), BF16),
        ],
    )
    return pl.pallas_call(
        _expert_body,
        grid_spec=grid_spec,
        out_shape=jax.ShapeDtypeStruct((P, D), F32),
        compiler_params=_cparams("arbitrary"),
        name="experts",
    )(blk_exp, n_used, xb, w_gate, w_up, w_down)


def _combine_body(bm, final, dest_ref, x2_ref, mf_ref, nw_ref, yb_ref, o_ref, y0_scr, y1_scr, sem):
    bufs = (y0_scr, y1_scr)

    def row_copy(r, k):
        d = dest_ref[0, TOP_K_INNER * r + k]
        return pltpu.make_async_copy(yb_ref.at[pl.ds(d, 1)], bufs[k].at[pl.ds(r, 1)], sem)

    def issue(r, carry):
        for k in range(TOP_K_INNER):
            row_copy(r, k).start()
        return carry

    lax.fori_loop(0, bm, issue, 0, unroll=DMA_UNROLL)

    def drain(r, carry):
        for k in range(TOP_K_INNER):
            row_copy(r, k).wait()
        return carry

    lax.fori_loop(0, bm, drain, 0, unroll=DMA_UNROLL)

    mf = mf_ref[...]
    x3 = x2_ref[...] + y0_scr[...] * mf[:, 0:1] + y1_scr[...] * mf[:, 1:2]
    if final:
        ms = jnp.mean(x3 * x3, axis=-1, keepdims=True)
        x3 = x3 * lax.rsqrt(ms + EPS) * nw_ref[...]
    o_ref[...] = x3


def _combine(x2, yb, dest, mf, norm_final, final):
    T, D = x2.shape
    bm = _pick(T, (256, 128, 64, 32, 16, 8))
    nblk = T // bm
    dest3 = dest.reshape(nblk, 1, TOP_K_INNER * bm)
    return pl.pallas_call(
        functools.partial(_combine_body, bm, final),
        grid=(nblk,),
        in_specs=[
            pl.BlockSpec((None, 1, TOP_K_INNER * bm), lambda i: (i, 0, 0), memory_space=pltpu.SMEM),
            pl.BlockSpec((bm, D), lambda i: (i, 0)),
            pl.BlockSpec((bm, LANES), lambda i: (i, 0)),
            pl.BlockSpec((1, D), lambda i: (0, 0)),
            pl.BlockSpec(memory_space=pl.ANY),
        ],
        out_specs=pl.BlockSpec((bm, D), lambda i: (i, 0)),
        out_shape=jax.ShapeDtypeStruct((T, D), F32),
        scratch_shapes=[
            pltpu.VMEM((bm, D), F32),
            pltpu.VMEM((bm, D), F32),
            pltpu.SemaphoreType.DMA(()),
        ],
        compiler_params=_cparams("arbitrary"),
        name="combine",
    )(dest3, x2, mf, norm_final, yb)


def _rotary_tables(pos):
    half = ROT_DIM // 2
    inv = ROPE_THETA ** (-jnp.arange(0, ROT_DIM, 2, dtype=F32) / ROT_DIM)
    ang = pos.astype(F32)[:, None] * inv[None, :]
    cos = jnp.cos(ang)
    sin = jnp.sin(ang)
    n = pos.shape[0]
    ones = jnp.ones((n, D_A - ROT_DIM), F32)
    zeros = jnp.zeros((n, D_A - ROT_DIM), F32)
    zh = jnp.zeros((n, half), F32)
    cos64 = jnp.concatenate([cos, cos, ones], axis=1)
    sa64 = jnp.concatenate([zh, sin, zeros], axis=1)
    sb64 = jnp.concatenate([-sin, zh, zeros], axis=1)
    reps = LANES // D_A
    return jnp.tile(cos64, (1, reps)), jnp.tile(sa64, (1, reps)), jnp.tile(sb64, (1, reps))


def _moe_slots(mi, counts, n_experts, blk, n_assign):
    eid = mi[:, 0:TOP_K_INNER]
    rank = mi[:, TOP_K_INNER:2 * TOP_K_INNER]
    cnt = counts[0, N_GROUPS:N_GROUPS + n_experts].astype(I32)
    pc = (cnt + blk - 1) // blk * blk
    pend = jnp.cumsum(pc)
    pstart = pend - pc
    dest = (pstart[eid] + rank).astype(I32)
    n_blocks = -(-n_assign // blk) + n_experts
    blk_start = jnp.arange(n_blocks, dtype=I32) * blk
    blk_exp = jnp.minimum(jnp.sum((pend[None, :] <= blk_start[:, None]).astype(I32), axis=1), n_experts - 1)
    n_used = jnp.maximum(pend[-1] // blk, 1).astype(I32).reshape(1)
    return dest, blk_exp.astype(I32), n_used, n_blocks, (pstart + cnt).astype(I32), (pc - cnt).astype(I32)


def _layer(xp, xs, layer, dims, cache_k, cache_v, state_delta, state_conv, prm, norm_final, final):
    (norm_mix, w_in, lq1, lk1, lq2, lk2, subln_w, conv_w, a_log, dt_bias, gdn_norm_w,
     w_proj_a, w_proj_b, w_out, norm_ffn, w_gr, b_gr, w_er, b_er, w_eg, w_eu, w_ed) = prm
    bp, lp, bs, ls, past = dims
    D = xp.shape[1]
    tp = bp * lp
    n_qkv = 2 * A_COLS + V_COLS + G_COLS + Z_COLS
    n_small = 2 * H_B
    lam_init = 0.8 - 0.6 * math.exp(-0.3 * layer)

    w_main = jnp.concatenate([w_in[:, :n_qkv], w_in[:, n_qkv + n_small:]], axis=1).astype(BF16)
    w_small = jnp.pad(w_in[:, n_qkv:n_qkv + n_small], ((0, 0), (0, LANES - n_small))).astype(BF16)
    nm = norm_mix.reshape(1, D)
    pp = _inproj(xp, nm, _rotary_tables(jnp.tile(jnp.arange(lp), bp)), w_main, w_small)
    ps = _inproj(xs, nm, _rotary_tables(jnp.tile(past + jnp.arange(ls), bs)), w_main, w_small)

    lamp = jnp.stack([lq1, lk1, lq2, lk2]).astype(F32)
    sw = subln_w.reshape(1, D_VA)
    oa_p = _attn_prompt(pp.q_bf, pp.k_bf, pp.v_t, lamp, sw, lam_init, bp, lp)
    oa_s = _attn_sample(ps.q_bf, ps.k_bf, ps.v_new, cache_k.reshape(bs, past * H_A, 2 * D_A),
                        cache_v.reshape(bs, past * H_A, D_VA), lamp, sw, lam_init, bs, ls)

    nw = gdn_norm_w.reshape(1, D_V)
    ob_p, s_p, c_p = _gdn(pp.gq, pp.zg, pp.small, jnp.zeros((bp, H_B, D_K, D_V), F32),
                          jnp.zeros((bp, CONV_W - 1, G_COLS), F32), conv_w, a_log, dt_bias, nw,
                          bp, lp, CHUNK)
    ob_s, s_s, c_s = _gdn(ps.gq, ps.zg, ps.small, state_delta, state_conv, conv_w, a_log, dt_bias, nw,
                          bs, ls, ls)

    n_experts = w_er.shape[1]
    w_router = jnp.pad(jnp.concatenate([w_gr, w_er], axis=1), ((0, 0), (0, LANES - N_GROUPS - n_experts)))
    b_router = jnp.pad(jnp.concatenate([b_gr, b_er]), (0, LANES - N_GROUPS - n_experts)).reshape(1, LANES)
    mw = (w_proj_a.astype(BF16), w_proj_b.astype(BF16), w_out.astype(BF16), norm_ffn.reshape(1, D),
          w_router, b_router)
    x2_p, t_p, lg_p = _merge(xp, oa_p, ob_p, pp.zg, *mw)
    x2_s, t_s, lg_s = _merge(xs, oa_s, ob_s, ps.zg, *mw)

    blk = MOE_ROWS
    mi, mf, counts = _route(jnp.concatenate([lg_p, lg_s], axis=0))
    n_tok = tp + bs * ls
    dest, blk_exp, n_used, n_blocks, pad_start, pad_len = _moe_slots(
        mi, counts, n_experts, blk, n_tok * TOP_K_INNER)
    xb = _dispatch(t_p, t_s, dest, pad_start, pad_len, n_used, n_blocks * blk, blk)
    yb = _experts(xb, blk_exp, n_used, w_eg, w_eu, w_ed, blk)
    nf = norm_final.reshape(1, D)
    y_p = _combine(x2_p, yb, dest[:tp], mf[:tp], nf, final)
    y_s = _combine(x2_s, yb, dest[tp:], mf[tp:], nf, final)
    return y_p, y_s, pp.k_new, pp.v_new, s_p, c_p, ps.k_new, ps.v_new, s_s, c_s


def kernel(x_prompt, x_sample, cache_k, cache_v, state_delta, state_conv, norm_mix, w_in, lambda_q1, lambda_k1, lambda_q2, lambda_k2, subln_w, conv_w, a_log, dt_bias, gdn_norm_w, w_proj_a, w_proj_b, w_out, norm_ffn, w_group_router, b_group_router, w_expert_router, b_expert_router, w_exp_gate, w_exp_up, w_exp_down, norm_final):
    bp, lp, D = x_prompt.shape
    bs, ls, _ = x_sample.shape
    depth = cache_k.shape[0]
    past = cache_k.shape[2]
    dims = (bp, lp, bs, ls, past)
    xp = x_prompt.reshape(bp * lp, D)
    xs = x_sample.reshape(bs * ls, D)
    kp, vp, sp, cp, ksm, vsm, ssm, csm = [], [], [], [], [], [], [], []
    for l in range(depth):
        prm = (norm_mix[l], w_in[l], lambda_q1[l], lambda_k1[l], lambda_q2[l], lambda_k2[l], subln_w[l],
               conv_w[l], a_log[l], dt_bias[l], gdn_norm_w[l], w_proj_a[l], w_proj_b[l], w_out[l],
               norm_ffn[l], w_group_router[l], b_group_router[l], w_expert_router[l], b_expert_router[l],
               w_exp_gate[l], w_exp_up[l], w_exp_down[l])
        xp, xs, k_p, v_p, s_p, c_p, k_s, v_s, s_s, c_s = _layer(
            xp, xs, l, dims, cache_k[l], cache_v[l], state_delta[l], state_conv[l], prm,
            norm_final, l == depth - 1)
        kp.append(k_p.reshape(bp, lp, H_A, 2 * D_A))
        vp.append(v_p.reshape(bp, lp, H_A, D_VA))
        ksm.append(k_s.reshape(bs, ls, H_A, 2 * D_A))
        vsm.append(v_s.reshape(bs, ls, H_A, D_VA))
        sp.append(s_p)
        cp.append(c_p)
        ssm.append(s_s)
        csm.append(c_s)
    return (xp.reshape(bp, lp, D), xs.reshape(bs, ls, D), jnp.stack(kp), jnp.stack(vp), jnp.stack(sp),
            jnp.stack(cp), jnp.stack(ksm), jnp.stack(vsm), jnp.stack(ssm), jnp.stack(csm))
```

```python
import collections
import functools
import math

import jax
import jax.numpy as jnp
from jax import lax
from jax.experimental import pallas as pl
from jax.experimental.pallas import tpu as pltpu

F32 = jnp.float32
BF16 = jnp.bfloat16
I32 = jnp.int32

CHUNK = 64
H_A = 8
D_A = 64
D_VA = 2 * D_A
ROT_DIM = D_A // 4
ROPE_THETA = 500000.0
H_B = 8
D_K = 128
D_V = 128
CONV_W = 4
N_GROUPS = 4
EXPERTS_PER_GROUP = 8
TOP_K_INNER = 2
EPS = 1e-6

LANES = 128
SUBLANES = 8
V7X_VMEM_BYTES = 64 * 1024 * 1024
VMEM_LIMIT_BYTES = V7X_VMEM_BYTES - 8 * 1024 * 1024

Q_SCALE = (D_A ** -0.5) * math.log2(math.e)
ONES_ROWS = 16
INPROJ_BN = 512
MOE_ROWS = 128
DMA_UNROLL = 8

NT_DIMS = (((1,), (1,)), ((), ()))
TN_DIMS = (((0,), (0,)), ((), ()))

A_COLS = H_A * 2 * D_A
V_COLS = H_A * D_VA
G_COLS = 2 * H_B * D_K + H_B * D_V
Z_COLS = H_B * D_V


def _pick(n, cands):
    for c in cands:
        if n % c == 0:
            return c
    raise ValueError(f"no block size in {cands} divides {n}")


def _cparams(*sem):
    return pltpu.CompilerParams(dimension_semantics=sem, vmem_limit_bytes=VMEM_LIMIT_BYTES)


def _sigmoid(x):
    return 1.0 / (1.0 + jnp.exp(-x))


def _silu(x):
    return x * _sigmoid(x)


def _softplus(x):
    return jnp.maximum(x, 0.0) + jnp.log1p(jnp.exp(-jnp.abs(x)))


def _bdot(a, b):
    return jnp.dot(a.astype(BF16), b.astype(BF16), preferred_element_type=F32)


def _split2(a):
    hi = a.astype(BF16)
    lo = (a - hi.astype(F32)).astype(BF16)
    return hi, lo


def _dot3(a, b):
    ah, al = _split2(a)
    bh, bl = _split2(b)
    d = functools.partial(jnp.dot, preferred_element_type=F32)
    return d(ah, bh) + d(ah, bl) + d(al, bh)


def _dot_exact_mask(mask, x, mask_is_lhs):
    hi = x.astype(BF16)
    r1 = x - hi.astype(F32)
    mid = r1.astype(BF16)
    lo = (r1 - mid.astype(F32)).astype(BF16)
    mb = mask.astype(BF16)
    d = functools.partial(jnp.dot, preferred_element_type=F32)
    if mask_is_lhs:
        return d(mb, hi) + d(mb, mid) + d(mb, lo)
    return d(hi, mb) + d(mid, mb) + d(lo, mb)


InProj = collections.namedtuple("InProj", "small k_new v_new q_bf k_bf v_t gq zg")


def _inproj_body(o_k, o_v, o_g, o_z, nb_wa, x_ref, nw_ref, cos_ref, sa_ref, sb_ref, wa_ref, wb_ref, ws_ref,
                 os_ref, k_ref, v_ref, qb_ref, kb_ref, vt_ref, gq_ref, zg_ref, h_scr):
    j = pl.program_id(1)
    bm = h_scr.shape[0]
    half = wa_ref.shape[1] // 2
    lpb = half // LANES

    @pl.when(j == 0)
    def _():
        x = x_ref[...]
        ms = jnp.mean(x * x, axis=-1, keepdims=True)
        h = (x * lax.rsqrt(ms + EPS) * nw_ref[...]).astype(BF16)
        h_scr[...] = h
        os_ref[...] = jnp.dot(h, ws_ref[...], preferred_element_type=F32)

    def halves(w_ref):
        return [jnp.dot(h_scr[...], w_ref[:, c * half:(c + 1) * half], preferred_element_type=F32)
                for c in range(2)]

    def lane_blocks(w_ref):
        for c2, acc in enumerate(halves(w_ref)):
            for c in range(lpb):
                yield c2 * lpb + c, acc[:, c * LANES:(c + 1) * LANES]

    def rotated(blk):
        return (blk * cos_ref[...] + pltpu.roll(blk, ROT_DIM // 2, 1) * sa_ref[...]
                + pltpu.roll(blk, LANES - ROT_DIM // 2, 1) * sb_ref[...])

    @pl.when(j < o_k)
    def _():
        for lb, blk in lane_blocks(wa_ref):
            qb_ref[:, lb * LANES:(lb + 1) * LANES] = (rotated(blk) * Q_SCALE).astype(BF16)

    hpb = 2 * lpb
    for jj in range(o_v - o_k):
        @pl.when(j == o_k + jj)
        def _(jj=jj):
            for lb, blk in lane_blocks(wa_ref):
                rot = rotated(blk)
                k_ref[pl.ds(jj * hpb + lb, bm, stride=H_A), :] = rot
                kb_ref[:, lb * LANES:(lb + 1) * LANES] = rot.astype(BF16)

    for jj in range(o_g - o_v):
        @pl.when(j == o_v + jj)
        def _(jj=jj):
            for lb, blk in lane_blocks(wa_ref):
                v_ref[pl.ds(jj * hpb + lb, bm, stride=H_A), :] = blk
                vt_ref[lb, 0:D_VA, :] = blk.T.astype(BF16)
                vt_ref[lb, D_VA:D_VA + ONES_ROWS, :] = jnp.ones((ONES_ROWS, bm), BF16)

    @pl.when(jnp.logical_and(j >= o_g, j < o_z))
    def _():
        for c, acc in enumerate(halves(wa_ref)):
            gq_ref[:, c * half:(c + 1) * half] = acc

    @pl.when(jnp.logical_and(j >= o_z, j < nb_wa))
    def _():
        for c, acc in enumerate(halves(wa_ref)):
            zg_ref[:, c * half:(c + 1) * half] = acc.astype(BF16)

    @pl.when(j >= nb_wa)
    def _():
        for c, acc in enumerate(halves(wb_ref)):
            zg_ref[:, c * half:(c + 1) * half] = acc.astype(BF16)


def _inproj(x, norm_w, tables, w_a, w_b, w_small):
    T, D = x.shape
    bn = INPROJ_BN
    zg_cols = Z_COLS + 2 * D
    assert A_COLS % bn == 0 and V_COLS % bn == 0 and G_COLS % bn == 0 and Z_COLS % bn == 0
    assert (bn // 2) % D_VA == 0 and D_VA == LANES and (2 * D) % bn == 0
    assert w_a.shape[1] == 2 * A_COLS + V_COLS + G_COLS + Z_COLS and w_b.shape[1] == 2 * D
    nb_a, nb_v, nb_g, nb_z = A_COLS // bn, V_COLS // bn, G_COLS // bn, zg_cols // bn
    o_k, o_v = nb_a, 2 * nb_a
    o_g = o_v + nb_v
    o_z = o_g + nb_g
    nb_wa, nb_wb = w_a.shape[1] // bn, w_b.shape[1] // bn
    bm = _pick(T, (1024, 512, 256, 128))
    hpb = bn // D_VA

    def col(lo, n):
        return lambda i, j: (i, jnp.clip(j - lo, 0, n - 1))

    cos_t, sa_t, sb_t = tables
    outs = pl.pallas_call(
        functools.partial(_inproj_body, o_k, o_v, o_g, o_z, nb_wa),
        grid=(T // bm, nb_wa + nb_wb),
        in_specs=[
            pl.BlockSpec((bm, D), lambda i, j: (i, 0), pipeline_mode=pl.Buffered(1)),
            pl.BlockSpec((1, D), lambda i, j: (0, 0)),
            pl.BlockSpec((bm, LANES), lambda i, j: (i, 0)),
            pl.BlockSpec((bm, LANES), lambda i, j: (i, 0)),
            pl.BlockSpec((bm, LANES), lambda i, j: (i, 0)),
            pl.BlockSpec((D, bn), lambda i, j: (0, jnp.minimum(j, nb_wa - 1))),
            pl.BlockSpec((D, bn), lambda i, j: (0, jnp.maximum(j - nb_wa, 0))),
            pl.BlockSpec((D, LANES), lambda i, j: (0, 0)),
        ],
        out_specs=[
            pl.BlockSpec((bm, LANES), lambda i, j: (i, 0)),
            pl.BlockSpec((bm * H_A, D_VA), lambda i, j: (i, 0)),
            pl.BlockSpec((bm * H_A, D_VA), lambda i, j: (i, 0)),
            pl.BlockSpec((bm, bn), col(0, nb_a)),
            pl.BlockSpec((bm, bn), col(o_k, nb_a)),
            pl.BlockSpec((hpb, D_VA + ONES_ROWS, bm), lambda i, j: (jnp.clip(j - o_v, 0, nb_v - 1), 0, i)),
            pl.BlockSpec((bm, bn), col(o_g, nb_g)),
            pl.BlockSpec((bm, bn), col(o_z, nb_z)),
        ],
        out_shape=[
            jax.ShapeDtypeStruct((T, LANES), F32),
            jax.ShapeDtypeStruct((T * H_A, 2 * D_A), F32),
            jax.ShapeDtypeStruct((T * H_A, D_VA), F32),
            jax.ShapeDtypeStruct((T, A_COLS), BF16),
            jax.ShapeDtypeStruct((T, A_COLS), BF16),
            jax.ShapeDtypeStruct((H_A, D_VA + ONES_ROWS, T), BF16),
            jax.ShapeDtypeStruct((T, G_COLS), F32),
            jax.ShapeDtypeStruct((T, zg_cols), BF16),
        ],
        scratch_shapes=[pltpu.VMEM((bm, D), BF16)],
        compiler_params=_cparams("parallel", "arbitrary"),
        name="inproj",
    )(x, norm_w, cos_t, sa_t, sb_t, w_a, w_b, w_small)
    return InProj(*outs)


def _diff_lambda(lamp, lam_init):
    s1 = jnp.sum(lamp[0:1] * lamp[1:2], axis=1, keepdims=True)
    s2 = jnp.sum(lamp[2:3] * lamp[3:4], axis=1, keepdims=True)
    return jnp.exp(s1) - jnp.exp(s2) + lam_init


def _subln(o, w, lam_init):
    ms = jnp.mean(o * o, axis=-1, keepdims=True)
    return o * lax.rsqrt(ms + EPS) * w * (1.0 - lam_init)


def _attn_prompt_body(lam_init, bq, q_ref, k_ref, vt_ref, lamp_ref, swc_ref, o_ref,
                      m_scr, acc_scr, s_scr):
    qi = pl.program_id(2)
    q = q_ref[...]
    lane = lax.broadcasted_iota(I32, q.shape, 1)
    qz = [jnp.where(lane < D_A, q, jnp.zeros_like(q)), jnp.where(lane >= D_A, q, jnp.zeros_like(q))]
    m_scr[...] = jnp.full(m_scr.shape, -jnp.inf, F32)
    acc_scr[...] = jnp.zeros(acc_scr.shape, F32)
    key_chunk = lax.broadcasted_iota(I32, (bq, bq), 0) // CHUNK
    qry_chunk = lax.broadcasted_iota(I32, (bq, bq), 1) // CHUNK
    diag_mask = key_chunk <= qry_chunk

    def scores(j, slot):
        kb = k_ref[pl.ds(pl.multiple_of(j * bq, bq), bq), :]
        for m in range(2):
            s_scr[slot, m] = lax.dot_general(kb, qz[m], NT_DIMS, preferred_element_type=F32)

    def accumulate(j, slot, masked):
        vb = vt_ref[:, pl.ds(pl.multiple_of(j * bq, bq), bq)]
        for m in range(2):
            s = s_scr[slot, m]
            if masked:
                s = jnp.where(diag_mask, s, -jnp.inf)
            m_prev = m_scr[m]
            m_new = jnp.maximum(m_prev, jnp.max(s, axis=0, keepdims=True))
            alpha = jnp.exp2(m_prev - m_new)
            p = jnp.exp2(s - m_new).astype(BF16)
            acc_scr[m] = alpha * acc_scr[m] + jnp.dot(vb, p, preferred_element_type=F32)
            m_scr[m] = m_new

    scores(0, 0)

    def two_full_blocks(t, carry):
        j = 2 * t
        scores(j + 1, 1)
        accumulate(j, 0, False)
        scores(j + 2, 0)
        accumulate(j + 1, 1, False)
        return carry

    lax.fori_loop(0, qi // 2, two_full_blocks, 0)
    j0 = 2 * (qi // 2)

    @pl.when(j0 == qi)
    def _():
        accumulate(qi, 0, True)

    @pl.when(j0 != qi)
    def _():
        scores(qi, 1)
        accumulate(j0, 0, False)
        accumulate(qi, 1, True)

    lam = _diff_lambda(lamp_ref[...], lam_init)
    a0 = acc_scr[0]
    a1 = acc_scr[1]
    o_t = a0[0:D_VA] / a0[D_VA:D_VA + 1] - lam * (a1[0:D_VA] / a1[D_VA:D_VA + 1])
    ms = jnp.mean(o_t * o_t, axis=0, keepdims=True)
    o_t = o_t * lax.rsqrt(ms + EPS) * swc_ref[...] * (1.0 - lam_init)
    o_ref[...] = o_t.T.astype(o_ref.dtype)


def _attn_prompt(q_bf, k_bf, v_t, lamp, subln_w, lam_init, bp, lp):
    bq = _pick(lp, (512, 256, 128))
    nq = lp // bq
    vrows = v_t.shape[1]
    return pl.pallas_call(
        functools.partial(_attn_prompt_body, lam_init, bq),
        grid=(bp, H_A, nq),
        in_specs=[
            pl.BlockSpec((bq, LANES), lambda b, h, i: (b * nq + i, h)),
            pl.BlockSpec((lp, LANES), lambda b, h, i: (b, h)),
            pl.BlockSpec((None, vrows, lp), lambda b, h, i: (h, 0, b)),
            pl.BlockSpec((4, D_A), lambda b, h, i: (0, 0)),
            pl.BlockSpec((D_VA, 1), lambda b, h, i: (0, 0)),
        ],
        out_specs=pl.BlockSpec((bq, LANES), lambda b, h, i: (b * nq + i, h)),
        out_shape=jax.ShapeDtypeStruct((bp * lp, V_COLS), BF16),
        scratch_shapes=[
            pltpu.VMEM((2, 1, bq), F32),
            pltpu.VMEM((2, vrows, bq), F32),
            pltpu.VMEM((2, 2, bq, bq), F32),
        ],
        compiler_params=_cparams("parallel", "parallel", "arbitrary"),
        name="attn_prompt",
    )(q_bf, k_bf, v_t, lamp, subln_w.reshape(D_VA, 1))


def _attn_sample_body(lam_init, q_ref, kn_ref, vn_ref, kc_ref, vc_ref, lamp_ref, sw_ref, o_ref,
                      m_scr, l_scr, acc_scr):
    pj = pl.program_id(1)
    npos = kc_ref.shape[0] // H_A
    ls = q_ref.shape[0]

    @pl.when(pj == 0)
    def _():
        m_scr[...] = jnp.full(m_scr.shape, -jnp.inf, F32)
        l_scr[...] = jnp.zeros(l_scr.shape, F32)
        acc_scr[...] = jnp.zeros(acc_scr.shape, F32)

    lane = lax.broadcasted_iota(I32, (ls, D_VA), 1)

    def q_maps(h):
        q = q_ref[:, h * D_VA:(h + 1) * D_VA]
        return jnp.concatenate([jnp.where(lane < D_A, q, jnp.zeros_like(q)),
                                jnp.where(lane >= D_A, q, jnp.zeros_like(q))], axis=0)

    def update(h, qz, keys, vals):
        s = lax.dot_general(qz, keys, NT_DIMS, preferred_element_type=F32)
        m_prev = m_scr[h]
        m_new = jnp.maximum(m_prev, jnp.max(s, axis=1, keepdims=True))
        alpha = jnp.exp2(m_prev - m_new)
        p = jnp.exp2(s - m_new)
        l_scr[h] = alpha * l_scr[h] + jnp.sum(p, axis=1, keepdims=True)
        acc_scr[h] = alpha * acc_scr[h] + jnp.dot(p.astype(BF16), vals, preferred_element_type=F32)
        m_scr[h] = m_new

    for h in range(H_A):
        kc = kc_ref[pl.ds(h, npos, stride=H_A), :].astype(BF16)
        vc = vc_ref[pl.ds(h, npos, stride=H_A), :].astype(BF16)
        update(h, q_maps(h), kc, vc)

    @pl.when(pj == pl.num_programs(1) - 1)
    def _():
        lam = _diff_lambda(lamp_ref[...], lam_init)
        sw = sw_ref[...]
        for h in range(H_A):
            kn = kn_ref[:, h * D_VA:(h + 1) * D_VA]
            vn = vn_ref[pl.ds(h, ls, stride=H_A), :].astype(BF16)
            update(h, q_maps(h), kn, vn)
            both = acc_scr[h] / l_scr[h]
            o = both[0:ls] - lam * both[ls:2 * ls]
            o_ref[:, h * D_VA:(h + 1) * D_VA] = _subln(o, sw, lam_init).astype(o_ref.dtype)


def _attn_sample(q_bf, k_bf, v_new, cache_k, cache_v, lamp, subln_w, lam_init, bs, ls):
    past = cache_k.shape[1] // H_A
    pb = _pick(past, (1024, 512, 256, 128, 64))
    assert ls % (2 * SUBLANES) == 0
    return pl.pallas_call(
        functools.partial(_attn_sample_body, lam_init),
        grid=(bs, past // pb),
        in_specs=[
            pl.BlockSpec((ls, A_COLS), lambda b, j: (b, 0)),
            pl.BlockSpec((ls, A_COLS), lambda b, j: (b, 0)),
            pl.BlockSpec((ls * H_A, D_VA), lambda b, j: (b, 0)),
            pl.BlockSpec((None, pb * H_A, D_VA), lambda b, j: (b, j, 0)),
            pl.BlockSpec((None, pb * H_A, D_VA), lambda b, j: (b, j, 0)),
            pl.BlockSpec((4, D_A), lambda b, j: (0, 0)),
            pl.BlockSpec((1, D_VA), lambda b, j: (0, 0)),
        ],
        out_specs=pl.BlockSpec((ls, V_COLS), lambda b, j: (b, 0)),
        out_shape=jax.ShapeDtypeStruct((bs * ls, V_COLS), BF16),
        scratch_shapes=[
            pltpu.VMEM((H_A, 2 * ls, 1), F32),
            pltpu.VMEM((H_A, 2 * ls, 1), F32),
            pltpu.VMEM((H_A, 2 * ls, D_VA), F32),
        ],
        compiler_params=_cparams("parallel", "arbitrary"),
        name="attn_sample",
    )(q_bf, k_bf, v_new, cache_k, cache_v, lamp, subln_w)


def _gdn_body(chunk, rows, pq_ref, pk_ref, pv_ref, pz_ref, sm_ref, smt_ref, cw_ref, alr_ref, dtr_ref,
              alc_ref, dtc_ref, nw_ref, s0_ref, c0_ref, o_ref, s_ref, cn_ref, ext_scr, act_scr):
    i = pl.program_id(1)
    nblk = pl.num_programs(1)
    hk = H_B * D_K
    hv = H_B * D_V
    n_chunks = rows // chunk
    n_steps = int(math.log2(chunk))
    assert 2 ** n_steps == chunk

    @pl.when(i == 0)
    def _():
        s_ref[...] = s0_ref[...]
        ext_scr[0:SUBLANES, :] = jnp.zeros((SUBLANES, ext_scr.shape[1]), F32)
        ext_scr[pl.ds(SUBLANES - (CONV_W - 1), CONV_W - 1), :] = c0_ref[...]

    ext_scr[pl.ds(SUBLANES, rows), 0:hk] = pq_ref[...]
    ext_scr[pl.ds(SUBLANES, rows), hk:2 * hk] = pk_ref[...]
    ext_scr[pl.ds(SUBLANES, rows), 2 * hk:2 * hk + hv] = pv_ref[...]

    @pl.when(i == nblk - 1)
    def _():
        cn_ref[...] = ext_scr[pl.ds(SUBLANES + rows - (CONV_W - 1), CONV_W - 1), :]

    conv = jnp.zeros((rows, ext_scr.shape[1]), F32)
    for w in range(CONV_W):
        conv = conv + ext_scr[pl.ds(SUBLANES - (CONV_W - 1) + w, rows), :] * cw_ref[w:w + 1, :]
    act_scr[...] = _silu(conv)
    ext_scr[0:SUBLANES, :] = ext_scr[pl.ds(rows, SUBLANES), :]

    sm = sm_ref[...]
    beta_all = _sigmoid(sm[:, 0:H_B])
    g_all = -jnp.exp(alr_ref[...]) * _softplus(sm[:, H_B:2 * H_B] + dtr_ref[...])
    smt = smt_ref[...]
    g_all_t = -jnp.exp(alc_ref[...]) * _softplus(smt[H_B:2 * H_B, :] + dtc_ref[...])

    ri = lax.broadcasted_iota(I32, (chunk, chunk), 0)
    ci = lax.broadcasted_iota(I32, (chunk, chunk), 1)
    tri = ri >= ci
    strict = ri > ci
    eye = (ri == ci).astype(F32)
    ltri = tri.astype(F32)
    utri = (ri <= ci).astype(F32)
    nw = nw_ref[...]

    units = [(c, h) for c in range(n_chunks) for h in range(H_B)]
    gcs = [_dot_exact_mask(ltri, g_all[c * chunk:(c + 1) * chunk, :], True) for c in range(n_chunks)]
    grs = [_dot_exact_mask(utri, g_all_t[:, c * chunk:(c + 1) * chunk], False) for c in range(n_chunks)]
    pre = []
    for c, h in units:
        r0 = c * chunk
        q = act_scr[r0:r0 + chunk, h * D_K:(h + 1) * D_K]
        k = act_scr[r0:r0 + chunk, hk + h * D_K:hk + (h + 1) * D_K]
        v = act_scr[r0:r0 + chunk, 2 * hk + h * D_V:2 * hk + (h + 1) * D_V]
        q = q * lax.rsqrt(jnp.sum(q * q, axis=-1, keepdims=True) + EPS) * (D_K ** -0.5)
        k = k * lax.rsqrt(jnp.sum(k * k, axis=-1, keepdims=True) + EPS)
        beta = beta_all[r0:r0 + chunk, h:h + 1]
        gc = gcs[c][:, h:h + 1]
        gr = grs[c][h:h + 1, :]
        g_last = gc[chunk - 1:chunk, :]
        decay = jnp.where(tri, jnp.exp(jnp.where(tri, gc - gr, 0.0)), 0.0)
        kb = k * beta
        eg = jnp.exp(gc)
        kbf = k.astype(BF16)
        kk = lax.dot_general(kb.astype(BF16), kbf, NT_DIMS, preferred_element_type=F32)
        qk = lax.dot_general(q.astype(BF16), kbf, NT_DIMS, preferred_element_type=F32)
        pre.append(dict(
            nm=-jnp.where(strict, kk * decay, 0.0),
            rhs=jnp.concatenate([v * beta, kb * eg], axis=1),
            qk=jnp.where(tri, qk * decay, 0.0).astype(BF16),
            qg=(q * eg).astype(BF16),
            kg=(k * jnp.exp(g_last - gc)).astype(BF16),
            gl=jnp.exp(g_last)))
    invs = [eye + p["nm"] for p in pre]
    pws = [p["nm"] for p in pre]
    for _ in range(n_steps - 1):
        pws = [_bdot(pw, pw) for pw in pws]
        invs = [inv + _bdot(inv, pw) for inv, pw in zip(invs, pws)]
    uws = [_bdot(inv, p["rhs"]) for p, inv in zip(pre, invs)]

    for (c, h), p, uw in zip(units, pre, uws):
        r0 = c * chunk
        u = uw[:, 0:D_V]
        wmat = uw[:, D_V:D_V + D_K]
        s = s_ref[h]
        sb = s.astype(BF16)
        v_new = u - jnp.dot(wmat.astype(BF16), sb, preferred_element_type=F32)
        v_new_b = v_new.astype(BF16)
        o = (jnp.dot(p["qg"], sb, preferred_element_type=F32)
             + jnp.dot(p["qk"], v_new_b, preferred_element_type=F32))
        s_ref[h] = s * p["gl"] + lax.dot_general(p["kg"], v_new_b, TN_DIMS, preferred_element_type=F32)
        z = pz_ref[r0:r0 + chunk, h * D_V:(h + 1) * D_V].astype(F32)
        o = o * lax.rsqrt(jnp.mean(o * o, axis=-1, keepdims=True) + EPS) * nw * _silu(z)
        o_ref[r0:r0 + chunk, h * D_V:(h + 1) * D_V] = o.astype(o_ref.dtype)


def _gdn(gq, zg, small, s0, c0, conv_w, a_log, dt_bias, norm_w, bn, ln, chunk):
    hk = H_B * D_K
    assert H_B * D_V == hk and Z_COLS == hk
    rows = ln
    for cand in (2 * chunk, chunk):
        if cand % LANES == 0 and ln % cand == 0:
            rows = cand
            break
    assert rows % chunk == 0 and rows % SUBLANES == 0
    nblk = ln // rows
    small_t = jnp.swapaxes(small[:, 0:2 * H_B].reshape(bn, ln, 2 * H_B), 1, 2)
    alr = a_log.reshape(1, H_B)
    dtr = dt_bias.reshape(1, H_B)
    alc = a_log.reshape(H_B, 1)
    dtc = dt_bias.reshape(H_B, 1)
    row_blk = lambda b, i: b * nblk + i
    return pl.pallas_call(
        functools.partial(_gdn_body, chunk, rows),
        grid=(bn, nblk),
        in_specs=[
            pl.BlockSpec((rows, hk), lambda b, i: (row_blk(b, i), 0)),
            pl.BlockSpec((rows, hk), lambda b, i: (row_blk(b, i), 1)),
            pl.BlockSpec((rows, hk), lambda b, i: (row_blk(b, i), 2)),
            pl.BlockSpec((rows, hk), lambda b, i: (row_blk(b, i), 0)),
            pl.BlockSpec((rows, LANES), lambda b, i: (row_blk(b, i), 0)),
            pl.BlockSpec((None, 2 * H_B, rows), lambda b, i: (b, 0, i)),
            pl.BlockSpec((CONV_W, G_COLS), lambda b, i: (0, 0)),
            pl.BlockSpec((1, H_B), lambda b, i: (0, 0)),
            pl.BlockSpec((1, H_B), lambda b, i: (0, 0)),
            pl.BlockSpec((H_B, 1), lambda b, i: (0, 0)),
            pl.BlockSpec((H_B, 1), lambda b, i: (0, 0)),
            pl.BlockSpec((1, D_V), lambda b, i: (0, 0)),
            pl.BlockSpec((None, H_B, D_K, D_V), lambda b, i: (b, 0, 0, 0)),
            pl.BlockSpec((None, CONV_W - 1, G_COLS), lambda b, i: (b, 0, 0)),
        ],
        out_specs=[
            pl.BlockSpec((rows, hk), lambda b, i: (row_blk(b, i), 0)),
            pl.BlockSpec((None, H_B, D_K, D_V), lambda b, i: (b, 0, 0, 0)),
            pl.BlockSpec((None, CONV_W - 1, G_COLS), lambda b, i: (b, 0, 0)),
        ],
        out_shape=[
            jax.ShapeDtypeStruct((bn * ln, hk), BF16),
            jax.ShapeDtypeStruct(s0.shape, F32),
            jax.ShapeDtypeStruct(c0.shape, F32),
        ],
        scratch_shapes=[
            pltpu.VMEM((rows + SUBLANES, G_COLS), F32),
            pltpu.VMEM((rows, G_COLS), F32),
        ],
        compiler_params=_cparams("parallel", "arbitrary"),
        name="gdn",
    )(gq, gq, gq, zg, small, small_t, conv_w, alr, dtr, alc, dtc, norm_w, s0, c0)


def _merge_body(x_ref, oa_ref, ob_ref, ga0_ref, ga1_ref, gb0_ref, gb1_ref, wa_ref, wb_ref, wo_ref,
                nf_ref, wr_ref, br_ref, x2_ref, t_ref, lg_ref):
    ya = jnp.dot(oa_ref[...], wa_ref[...], preferred_element_type=F32)
    yb = jnp.dot(ob_ref[...], wb_ref[...], preferred_element_type=F32)
    half = ga0_ref.shape[1]
    gate = lambda ref: _sigmoid(ref[...].astype(F32))
    m0 = gate(ga0_ref) * ya[:, :half] + gate(gb0_ref) * yb[:, :half]
    m1 = gate(ga1_ref) * ya[:, half:] + gate(gb1_ref) * yb[:, half:]
    merged = jnp.concatenate([m0, m1], axis=1).astype(BF16)
    x2 = x_ref[...] + jnp.dot(merged, wo_ref[...], preferred_element_type=F32)
    x2_ref[...] = x2
    ms = jnp.mean(x2 * x2, axis=-1, keepdims=True)
    t = x2 * lax.rsqrt(ms + EPS) * nf_ref[...]
    t_ref[...] = t
    lg_ref[...] = _dot3(t, wr_ref[...]) + br_ref[...]


def _merge(x, o_a, o_b, zg, wa, wb, wo, norm_ffn, w_router, b_router):
    T, D = x.shape
    half = D // 2
    assert Z_COLS % half == 0
    gcol = Z_COLS // half
    bm = _pick(T, (256, 128, 64, 32, 16, 8))
    const = dict(pipeline_mode=pl.Buffered(1))
    return pl.pallas_call(
        _merge_body,
        grid=(T // bm,),
        in_specs=[
            pl.BlockSpec((bm, D), lambda i: (i, 0)),
            pl.BlockSpec((bm, o_a.shape[1]), lambda i: (i, 0)),
            pl.BlockSpec((bm, o_b.shape[1]), lambda i: (i, 0)),
            pl.BlockSpec((bm, half), lambda i: (i, gcol)),
            pl.BlockSpec((bm, half), lambda i: (i, gcol + 1)),
            pl.BlockSpec((bm, half), lambda i: (i, gcol + 2)),
            pl.BlockSpec((bm, half), lambda i: (i, gcol + 3)),
            pl.BlockSpec(wa.shape, lambda i: (0, 0), **const),
            pl.BlockSpec(wb.shape, lambda i: (0, 0), **const),
            pl.BlockSpec(wo.shape, lambda i: (0, 0), **const),
            pl.BlockSpec((1, D), lambda i: (0, 0)),
            pl.BlockSpec((D, LANES), lambda i: (0, 0), **const),
            pl.BlockSpec((1, LANES), lambda i: (0, 0)),
        ],
        out_specs=[
            pl.BlockSpec((bm, D), lambda i: (i, 0)),
            pl.BlockSpec((bm, D), lambda i: (i, 0)),
            pl.BlockSpec((bm, LANES), lambda i: (i, 0)),
        ],
        out_shape=[
            jax.ShapeDtypeStruct((T, D), F32),
            jax.ShapeDtypeStruct((T, D), F32),
            jax.ShapeDtypeStruct((T, LANES), F32),
        ],
        compiler_params=_cparams("parallel"),
        name="merge",
    )(x, o_a, o_b, zg, zg, zg, zg, wa, wb, wo, norm_ffn, w_router, b_router)


def _route_body(lg_ref, mi_ref, mf_ref, cnt_ref):
    i = pl.program_id(0)

    @pl.when(i == 0)
    def _():
        cnt_ref[...] = jnp.zeros(cnt_ref.shape, F32)

    lg = lg_ref[...]
    bm = lg.shape[0]
    lane = lax.broadcasted_iota(I32, lg.shape, 1).astype(F32)
    big = jnp.float32(LANES)
    neg = -jnp.inf
    gl = jnp.where(lane < N_GROUPS, lg, neg)
    gmax = jnp.max(gl, axis=1, keepdims=True)
    gidx = jnp.min(jnp.where(gl == gmax, lane, big), axis=1, keepdims=True)
    gw = 1.0 / jnp.sum(jnp.exp(gl - gmax), axis=1, keepdims=True)
    e_lo = N_GROUPS + gidx * EXPERTS_PER_GROUP
    valid = (lane >= e_lo) & (lane < e_lo + EXPERTS_PER_GROUP)
    el = jnp.where(valid, lg, neg)
    v1 = jnp.max(el, axis=1, keepdims=True)
    i1 = jnp.min(jnp.where(el == v1, lane, big), axis=1, keepdims=True)
    el2 = jnp.where(lane == i1, neg, el)
    v2 = jnp.max(el2, axis=1, keepdims=True)
    i2 = jnp.min(jnp.where(el2 == v2, lane, big), axis=1, keepdims=True)
    e21 = jnp.exp(v2 - v1)
    w1 = gw / (1.0 + e21)
    w2 = gw * e21 / (1.0 + e21)
    oh1 = (lane == i1).astype(F32)
    oh2 = (lane == i2).astype(F32)
    oh = oh1 + oh2
    rr = lax.broadcasted_iota(I32, (bm, bm), 0)
    cc = lax.broadcasted_iota(I32, (bm, bm), 1)
    before = (cc < rr).astype(BF16)
    cum = jnp.dot(before, oh.astype(BF16), preferred_element_type=F32) + cnt_ref[...]
    rank1 = jnp.sum(cum * oh1, axis=1, keepdims=True)
    rank2 = jnp.sum(cum * oh2, axis=1, keepdims=True)
    cnt_ref[...] = cnt_ref[...] + jnp.sum(oh, axis=0, keepdims=True)
    mi = jnp.where(lane == 0, i1 - N_GROUPS, 0.0)
    mi = jnp.where(lane == 1, i2 - N_GROUPS, mi)
    mi = jnp.where(lane == 2, rank1, mi)
    mi = jnp.where(lane == 3, rank2, mi)
    mi_ref[...] = mi.T[0:SUBLANES].astype(I32)
    mf_ref[...] = jnp.where(lane == 0, w1, jnp.where(lane == 1, w2, 0.0))


def _route(logits):
    T = logits.shape[0]
    bm = _pick(T, (512, 256, 128, 64, 32, 16, 8))
    return pl.pallas_call(
        _route_body,
        grid=(T // bm,),
        in_specs=[pl.BlockSpec((bm, LANES), lambda i: (i, 0))],
        out_specs=[
            pl.BlockSpec((SUBLANES, bm), lambda i: (0, i)),
            pl.BlockSpec((bm, LANES), lambda i: (i, 0)),
            pl.BlockSpec((1, LANES), lambda i: (0, 0)),
        ],
        out_shape=[
            jax.ShapeDtypeStruct((SUBLANES, T), I32),
            jax.ShapeDtypeStruct((T, LANES), F32),
            jax.ShapeDtypeStruct((1, LANES), F32),
        ],
        compiler_params=_cparams("arbitrary"),
        name="route",
    )(logits)


def _dispatch_body(bm, blk, n_experts, nbp, ps_ref, pn_ref, nu_ref, dest_ref, tp_ref, ts_ref, xb_ref,
                   zero_scr, sem, zsem):
    i = pl.program_id(0)
    n_blocks = xb_ref.shape[0] // blk

    def pad_rows(act):
        def per_expert(e, carry):
            def per_row(r, c2):
                act(pltpu.make_async_copy(zero_scr.at[pl.ds(0, 1)], xb_ref.at[pl.ds(ps_ref[e] + r, 1)], zsem))
                return c2
            return lax.fori_loop(0, pn_ref[e], per_row, carry)
        lax.fori_loop(0, n_experts, per_expert, 0)

    def tail_blocks(act):
        def per_block(b, carry):
            act(pltpu.make_async_copy(zero_scr, xb_ref.at[pl.ds(pl.multiple_of(b * blk, blk), blk)], zsem))
            return carry
        lax.fori_loop(nu_ref[0], n_blocks, per_block, 0)

    @pl.when(i == 0)
    def _():
        zero_scr[...] = jnp.zeros(zero_scr.shape, zero_scr.dtype)
        pad_rows(lambda cp: cp.start())
        tail_blocks(lambda cp: cp.start())

    def send_rows(t_ref):
        def row_copy(r, k):
            d = dest_ref[0, k * bm + r]
            return pltpu.make_async_copy(t_ref.at[pl.ds(r, 1)], xb_ref.at[pl.ds(d, 1)], sem)

        def issue(r, carry):
            for k in range(TOP_K_INNER):
                row_copy(r, k).start()
            return carry

        def drain(r, carry):
            for k in range(TOP_K_INNER):
                row_copy(r, k).wait()
            return carry

        lax.fori_loop(0, bm, issue, 0, unroll=DMA_UNROLL)
        lax.fori_loop(0, bm, drain, 0, unroll=DMA_UNROLL)

    @pl.when(i < nbp)
    def _():
        send_rows(tp_ref)

    @pl.when(i >= nbp)
    def _():
        send_rows(ts_ref)

    @pl.when(i == 0)
    def _():
        pad_rows(lambda cp: cp.wait())
        tail_blocks(lambda cp: cp.wait())


def _dispatch(t_p, t_s, dest, pad_start, pad_len, n_used, n_rows, blk):
    (tp, D), ts = t_p.shape, t_s.shape[0]
    bm = _pick(math.gcd(tp, ts), (256, 128, 64, 32, 16, 8))
    nbp, nbs = tp // bm, ts // bm
    dest3 = _dest_blocks(dest, bm)
    grid_spec = pltpu.PrefetchScalarGridSpec(
        num_scalar_prefetch=3,
        grid=(nbp + nbs,),
        in_specs=[
            pl.BlockSpec((None, 1, TOP_K_INNER * bm), lambda i, ps, pn, nu: (i, 0, 0),
                         memory_space=pltpu.SMEM),
            pl.BlockSpec((bm, D), lambda i, ps, pn, nu: (jnp.minimum(i, nbp - 1), 0)),
            pl.BlockSpec((bm, D), lambda i, ps, pn, nu: (jnp.maximum(i - nbp, 0), 0)),
        ],
        out_specs=pl.BlockSpec(memory_space=pl.ANY),
        scratch_shapes=[
            pltpu.VMEM((blk, D), t_p.dtype),
            pltpu.SemaphoreType.DMA(()),
            pltpu.SemaphoreType.DMA(()),
        ],
    )
    return pl.pallas_call(
        functools.partial(_dispatch_body, bm, blk, pad_start.shape[0], nbp),
        grid_spec=grid_spec,
        out_shape=jax.ShapeDtypeStruct((n_rows, D), t_p.dtype),
        compiler_params=_cparams("arbitrary"),
        name="dispatch",
    )(pad_start, pad_len, n_used, dest3, t_p, t_s)


def _expert_body(be_ref, nu_ref, x_ref, wg_ref, wu_ref, wd_ref, y_ref, wg_scr, wu_scr, wd_scr):
    i = pl.program_id(0)

    @pl.when(i < nu_ref[0])
    def _():
        @pl.when(jnp.logical_or(i == 0, be_ref[i] != be_ref[jnp.maximum(i - 1, 0)]))
        def _():
            wg_scr[...] = wg_ref[...].astype(BF16)
            wu_scr[...] = wu_ref[...].astype(BF16)
            wd_scr[...] = wd_ref[...].astype(BF16)

        x = x_ref[...].astype(BF16)
        g = jnp.dot(x, wg_scr[...], preferred_element_type=F32)
        u = jnp.dot(x, wu_scr[...], preferred_element_type=F32)
        hmid = (_silu(g) * u).astype(BF16)
        y_ref[...] = jnp.dot(hmid, wd_scr[...], preferred_element_type=F32)

    @pl.when(i >= nu_ref[0])
    def _():
        y_ref[...] = jnp.zeros(y_ref.shape, F32)


def _experts(xb, blk_exp, n_used, w_gate, w_up, w_down, blk):
    P, D = xb.shape
    de = w_gate.shape[2]
    n_blocks = P // blk
    last = lambda i, nu: jnp.minimum(i, nu[0] - 1)
    grid_spec = pltpu.PrefetchScalarGridSpec(
        num_scalar_prefetch=2,
        grid=(n_blocks,),
        in_specs=[
            pl.BlockSpec((blk, D), lambda i, be, nu: (last(i, nu), 0)),
            pl.BlockSpec((None, D, de), lambda i, be, nu: (be[last(i, nu)], 0, 0)),
            pl.BlockSpec((None, D, de), lambda i, be, nu: (be[last(i, nu)], 0, 0)),
            pl.BlockSpec((None, de, D), lambda i, be, nu: (be[last(i, nu)], 0, 0)),
        ],
        out_specs=pl.BlockSpec((blk, D), lambda i, be, nu: (i, 0)),
        scratch_shapes=[
            pltpu.VMEM((D, de), BF16),
            pltpu.VMEM((D, de), BF16),
            pltpu.VMEM((de, D), BF16),
        ],
    )
    return pl.pallas_call(
        _expert_body,
        grid_spec=grid_spec,
        out_shape=jax.ShapeDtypeStruct((P, D), F32),
        compiler_params=_cparams("arbitrary"),
        name="experts",
    )(blk_exp, n_used, xb, w_gate, w_up, w_down)


def _combine_body(bm, final, dest_ref, x2_ref, mf_ref, nw_ref, yb_ref, o_ref, y0_scr, y1_scr, sem):
    bufs = (y0_scr, y1_scr)

    def row_copy(r, k):
        d = dest_ref[0, k * bm + r]
        return pltpu.make_async_copy(yb_ref.at[pl.ds(d, 1)], bufs[k].at[pl.ds(r, 1)], sem)

    def issue(r, carry):
        for k in range(TOP_K_INNER):
            row_copy(r, k).start()
        return carry

    lax.fori_loop(0, bm, issue, 0, unroll=DMA_UNROLL)

    def drain(r, carry):
        for k in range(TOP_K_INNER):
            row_copy(r, k).wait()
        return carry

    lax.fori_loop(0, bm, drain, 0, unroll=DMA_UNROLL)

    mf = mf_ref[...]
    x3 = x2_ref[...] + y0_scr[...] * mf[:, 0:1] + y1_scr[...] * mf[:, 1:2]
    if final:
        ms = jnp.mean(x3 * x3, axis=-1, keepdims=True)
        x3 = x3 * lax.rsqrt(ms + EPS) * nw_ref[...]
    o_ref[...] = x3


def _combine(x2, yb, dest, mf, norm_final, final):
    T, D = x2.shape
    bm = _pick(T, (256, 128, 64, 32, 16, 8))
    nblk = T // bm
    dest3 = _dest_blocks(dest, bm)
    return pl.pallas_call(
        functools.partial(_combine_body, bm, final),
        grid=(nblk,),
        in_specs=[
            pl.BlockSpec((None, 1, TOP_K_INNER * bm), lambda i: (i, 0, 0), memory_space=pltpu.SMEM),
            pl.BlockSpec((bm, D), lambda i: (i, 0)),
            pl.BlockSpec((bm, LANES), lambda i: (i, 0)),
            pl.BlockSpec((1, D), lambda i: (0, 0)),
            pl.BlockSpec(memory_space=pl.ANY),
        ],
        out_specs=pl.BlockSpec((bm, D), lambda i: (i, 0)),
        out_shape=jax.ShapeDtypeStruct((T, D), F32),
        scratch_shapes=[
            pltpu.VMEM((bm, D), F32),
            pltpu.VMEM((bm, D), F32),
            pltpu.SemaphoreType.DMA(()),
        ],
        compiler_params=_cparams("arbitrary"),
        name="combine",
    )(dest3, x2, mf, norm_final, yb)


def _rotary_tables(pos):
    half = ROT_DIM // 2
    inv = ROPE_THETA ** (-jnp.arange(0, ROT_DIM, 2, dtype=F32) / ROT_DIM)
    ang = pos.astype(F32)[:, None] * inv[None, :]
    cos = jnp.cos(ang)
    sin = jnp.sin(ang)
    n = pos.shape[0]
    ones = jnp.ones((n, D_A - ROT_DIM), F32)
    zeros = jnp.zeros((n, D_A - ROT_DIM), F32)
    zh = jnp.zeros((n, half), F32)
    cos64 = jnp.concatenate([cos, cos, ones], axis=1)
    sa64 = jnp.concatenate([zh, sin, zeros], axis=1)
    sb64 = jnp.concatenate([-sin, zh, zeros], axis=1)
    reps = LANES // D_A
    return jnp.tile(cos64, (1, reps)), jnp.tile(sa64, (1, reps)), jnp.tile(sb64, (1, reps))


def _dest_blocks(dest, bm):
    nblk = dest.shape[1] // bm
    return jnp.swapaxes(dest.reshape(TOP_K_INNER, nblk, bm), 0, 1).reshape(nblk, 1, TOP_K_INNER * bm)


def _moe_slots(mi, counts, n_experts, blk, n_assign):
    eid = mi[0:TOP_K_INNER]
    rank = mi[TOP_K_INNER:2 * TOP_K_INNER]
    cnt = counts[0, N_GROUPS:N_GROUPS + n_experts].astype(I32)
    pc = (cnt + blk - 1) // blk * blk
    pend = jnp.cumsum(pc)
    pstart = pend - pc
    dest = (pstart[eid] + rank).astype(I32)
    n_blocks = -(-n_assign // blk) + n_experts
    blk_start = jnp.arange(n_blocks, dtype=I32) * blk
    blk_exp = jnp.minimum(jnp.sum((pend[None, :] <= blk_start[:, None]).astype(I32), axis=1), n_experts - 1)
    n_used = jnp.maximum(pend[-1] // blk, 1).astype(I32).reshape(1)
    return dest, blk_exp.astype(I32), n_used, n_blocks, (pstart + cnt).astype(I32), (pc - cnt).astype(I32)


def _layer(xp, xs, layer, dims, cache_k, cache_v, state_delta, state_conv, prm, norm_final, final):
    (norm_mix, w_in, lq1, lk1, lq2, lk2, subln_w, conv_w, a_log, dt_bias, gdn_norm_w,
     w_proj_a, w_proj_b, w_out, norm_ffn, w_gr, b_gr, w_er, b_er, w_eg, w_eu, w_ed) = prm
    bp, lp, bs, ls, past = dims
    D = xp.shape[1]
    tp = bp * lp
    n_qkv = 2 * A_COLS + V_COLS + G_COLS + Z_COLS
    n_small = 2 * H_B
    lam_init = 0.8 - 0.6 * math.exp(-0.3 * layer)

    w_a = w_in[:, :n_qkv].astype(BF16)
    w_b = w_in[:, n_qkv + n_small:].astype(BF16)
    w_small = jnp.pad(w_in[:, n_qkv:n_qkv + n_small], ((0, 0), (0, LANES - n_small))).astype(BF16)
    nm = norm_mix.reshape(1, D)
    pp = _inproj(xp, nm, _rotary_tables(jnp.tile(jnp.arange(lp), bp)), w_a, w_b, w_small)
    ps = _inproj(xs, nm, _rotary_tables(jnp.tile(past + jnp.arange(ls), bs)), w_a, w_b, w_small)

    lamp = jnp.stack([lq1, lk1, lq2, lk2]).astype(F32)
    sw = subln_w.reshape(1, D_VA)
    oa_p = _attn_prompt(pp.q_bf, pp.k_bf, pp.v_t, lamp, sw, lam_init, bp, lp)
    oa_s = _attn_sample(ps.q_bf, ps.k_bf, ps.v_new, cache_k.reshape(bs, past * H_A, 2 * D_A),
                        cache_v.reshape(bs, past * H_A, D_VA), lamp, sw, lam_init, bs, ls)

    nw = gdn_norm_w.reshape(1, D_V)
    ob_p, s_p, c_p = _gdn(pp.gq, pp.zg, pp.small, jnp.zeros((bp, H_B, D_K, D_V), F32),
                          jnp.zeros((bp, CONV_W - 1, G_COLS), F32), conv_w, a_log, dt_bias, nw,
                          bp, lp, CHUNK)
    ob_s, s_s, c_s = _gdn(ps.gq, ps.zg, ps.small, state_delta, state_conv, conv_w, a_log, dt_bias, nw,
                          bs, ls, ls)

    n_experts = w_er.shape[1]
    w_router = jnp.pad(jnp.concatenate([w_gr, w_er], axis=1), ((0, 0), (0, LANES - N_GROUPS - n_experts)))
    b_router = jnp.pad(jnp.concatenate([b_gr, b_er]), (0, LANES - N_GROUPS - n_experts)).reshape(1, LANES)
    mw = (w_proj_a.astype(BF16), w_proj_b.astype(BF16), w_out.astype(BF16), norm_ffn.reshape(1, D),
          w_router, b_router)
    x2_p, t_p, lg_p = _merge(xp, oa_p, ob_p, pp.zg, *mw)
    x2_s, t_s, lg_s = _merge(xs, oa_s, ob_s, ps.zg, *mw)

    blk = MOE_ROWS
    mi, mf, counts = _route(jnp.concatenate([lg_p, lg_s], axis=0))
    n_tok = tp + bs * ls
    dest, blk_exp, n_used, n_blocks, pad_start, pad_len = _moe_slots(
        mi, counts, n_experts, blk, n_tok * TOP_K_INNER)
    xb = _dispatch(t_p, t_s, dest, pad_start, pad_len, n_used, n_blocks * blk, blk)
    yb = _experts(xb, blk_exp, n_used, w_eg, w_eu, w_ed, blk)
    nf = norm_final.reshape(1, D)
    y_p = _combine(x2_p, yb, dest[:, :tp], mf[:tp], nf, final)
    y_s = _combine(x2_s, yb, dest[:, tp:], mf[tp:], nf, final)
    return y_p, y_s, pp.k_new, pp.v_new, s_p, c_p, ps.k_new, ps.v_new, s_s, c_s


def kernel(x_prompt, x_sample, cache_k, cache_v, state_delta, state_conv, norm_mix, w_in, lambda_q1, lambda_k1, lambda_q2, lambda_k2, subln_w, conv_w, a_log, dt_bias, gdn_norm_w, w_proj_a, w_proj_b, w_out, norm_ffn, w_group_router, b_group_router, w_expert_router, b_expert_router, w_exp_gate, w_exp_up, w_exp_down, norm_final):
    bp, lp, D = x_prompt.shape
    bs, ls, _ = x_sample.shape
    depth = cache_k.shape[0]
    past = cache_k.shape[2]
    dims = (bp, lp, bs, ls, past)
    xp = x_prompt.reshape(bp * lp, D)
    xs = x_sample.reshape(bs * ls, D)
    kp, vp, sp, cp, ksm, vsm, ssm, csm = [], [], [], [], [], [], [], []
    for l in range(depth):
        prm = (norm_mix[l], w_in[l], lambda_q1[l], lambda_k1[l], lambda_q2[l], lambda_k2[l], subln_w[l],
               conv_w[l], a_log[l], dt_bias[l], gdn_norm_w[l], w_proj_a[l], w_proj_b[l], w_out[l],
               norm_ffn[l], w_group_router[l], b_group_router[l], w_expert_router[l], b_expert_router[l],
               w_exp_gate[l], w_exp_up[l], w_exp_down[l])
        xp, xs, k_p, v_p, s_p, c_p, k_s, v_s, s_s, c_s = _layer(
            xp, xs, l, dims, cache_k[l], cache_v[l], state_delta[l], state_conv[l], prm,
            norm_final, l == depth - 1)
        kp.append(k_p.reshape(bp, lp, H_A, 2 * D_A))
        vp.append(v_p.reshape(bp, lp, H_A, D_VA))
        ksm.append(k_s.reshape(bs, ls, H_A, 2 * D_A))
        vsm.append(v_s.reshape(bs, ls, H_A, D_VA))
        sp.append(s_p)
        cp.append(c_p)
        ssm.append(s_s)
        csm.append(c_s)
    return (xp.reshape(bp, lp, D), xs.reshape(bs, ls, D), jnp.stack(kp), jnp.stack(vp), jnp.stack(sp),
            jnp.stack(cp), jnp.stack(ksm), jnp.stack(vsm), jnp.stack(ssm), jnp.stack(csm))
```

```python
import collections
import functools
import math

import jax
import jax.numpy as jnp
from jax import lax
from jax.experimental import pallas as pl
from jax.experimental.pallas import tpu as pltpu

F32 = jnp.float32
BF16 = jnp.bfloat16
I32 = jnp.int32

CHUNK = 64
GDN_CHUNK = 128
H_A = 8
D_A = 64
D_VA = 2 * D_A
ROT_DIM = D_A // 4
ROPE_THETA = 500000.0
H_B = 8
D_K = 128
D_V = 128
CONV_W = 4
N_GROUPS = 4
EXPERTS_PER_GROUP = 8
TOP_K_INNER = 2
EPS = 1e-6

LANES = 128
SUBLANES = 8
V7X_VMEM_BYTES = 64 * 1024 * 1024
VMEM_LIMIT_BYTES = V7X_VMEM_BYTES - 8 * 1024 * 1024

Q_SCALE = (D_A ** -0.5) * math.log2(math.e)
ONES_ROWS = 16
INPROJ_BN = 512
MOE_ROWS = 128
DMA_UNROLL = 8

NT_DIMS = (((1,), (1,)), ((), ()))
TN_DIMS = (((0,), (0,)), ((), ()))

A_COLS = H_A * 2 * D_A
V_COLS = H_A * D_VA
G_COLS = 2 * H_B * D_K + H_B * D_V
Z_COLS = H_B * D_V


def _pick(n, cands):
    for c in cands:
        if n % c == 0:
            return c
    raise ValueError(f"no block size in {cands} divides {n}")


def _cparams(*sem):
    return pltpu.CompilerParams(dimension_semantics=sem, vmem_limit_bytes=VMEM_LIMIT_BYTES)


def _sigmoid(x):
    return 1.0 / (1.0 + jnp.exp(-x))


def _silu(x):
    return x * _sigmoid(x)


def _softplus(x):
    return jnp.maximum(x, 0.0) + jnp.log1p(jnp.exp(-jnp.abs(x)))


def _bdot(a, b):
    return jnp.dot(a.astype(BF16), b.astype(BF16), preferred_element_type=F32)


def _split2(a):
    hi = a.astype(BF16)
    lo = (a - hi.astype(F32)).astype(BF16)
    return hi, lo


def _dot3(a, b):
    ah, al = _split2(a)
    bh, bl = _split2(b)
    d = functools.partial(jnp.dot, preferred_element_type=F32)
    return d(ah, bh) + d(ah, bl) + d(al, bh)


def _dot_exact_mask(mask, x, mask_is_lhs):
    hi = x.astype(BF16)
    r1 = x - hi.astype(F32)
    mid = r1.astype(BF16)
    lo = (r1 - mid.astype(F32)).astype(BF16)
    mb = mask.astype(BF16)
    d = functools.partial(jnp.dot, preferred_element_type=F32)
    if mask_is_lhs:
        return d(mb, hi) + d(mb, mid) + d(mb, lo)
    return d(hi, mb) + d(mid, mb) + d(lo, mb)


InProj = collections.namedtuple("InProj", "small k_new v_new q_bf k_bf v_t gq zg")


def _inproj_body(o_k, o_v, o_g, o_z, nb_wa, x_ref, nw_ref, cos_ref, sa_ref, sb_ref, wa_ref, wb_ref, ws_ref,
                 os_ref, k_ref, v_ref, qb_ref, kb_ref, vt_ref, gq_ref, zg_ref, h_scr):
    j = pl.program_id(1)
    bm = h_scr.shape[0]
    half = wa_ref.shape[1] // 2
    lpb = half // LANES

    @pl.when(j == 0)
    def _():
        x = x_ref[...]
        ms = jnp.mean(x * x, axis=-1, keepdims=True)
        h = (x * lax.rsqrt(ms + EPS) * nw_ref[...]).astype(BF16)
        h_scr[...] = h
        os_ref[...] = jnp.dot(h, ws_ref[...], preferred_element_type=F32)

    def halves(w_ref):
        return [jnp.dot(h_scr[...], w_ref[:, c * half:(c + 1) * half], preferred_element_type=F32)
                for c in range(2)]

    def lane_blocks(w_ref):
        for c2, acc in enumerate(halves(w_ref)):
            for c in range(lpb):
                yield c2 * lpb + c, acc[:, c * LANES:(c + 1) * LANES]

    def rotated(blk):
        return (blk * cos_ref[...] + pltpu.roll(blk, ROT_DIM // 2, 1) * sa_ref[...]
                + pltpu.roll(blk, LANES - ROT_DIM // 2, 1) * sb_ref[...])

    @pl.when(j < o_k)
    def _():
        for lb, blk in lane_blocks(wa_ref):
            qb_ref[:, lb * LANES:(lb + 1) * LANES] = (rotated(blk) * Q_SCALE).astype(BF16)

    hpb = 2 * lpb
    for jj in range(o_v - o_k):
        @pl.when(j == o_k + jj)
        def _(jj=jj):
            for lb, blk in lane_blocks(wa_ref):
                rot = rotated(blk)
                k_ref[pl.ds(jj * hpb + lb, bm, stride=H_A), :] = rot
                kb_ref[:, lb * LANES:(lb + 1) * LANES] = rot.astype(BF16)

    for jj in range(o_g - o_v):
        @pl.when(j == o_v + jj)
        def _(jj=jj):
            for lb, blk in lane_blocks(wa_ref):
                v_ref[pl.ds(jj * hpb + lb, bm, stride=H_A), :] = blk
                vt_ref[lb, 0:D_VA, :] = blk.T.astype(BF16)
                vt_ref[lb, D_VA:D_VA + ONES_ROWS, :] = jnp.ones((ONES_ROWS, bm), BF16)

    @pl.when(jnp.logical_and(j >= o_g, j < o_z))
    def _():
        for c, acc in enumerate(halves(wa_ref)):
            gq_ref[:, c * half:(c + 1) * half] = acc

    @pl.when(jnp.logical_and(j >= o_z, j < nb_wa))
    def _():
        for c, acc in enumerate(halves(wa_ref)):
            zg_ref[:, c * half:(c + 1) * half] = acc.astype(BF16)

    @pl.when(j >= nb_wa)
    def _():
        for c, acc in enumerate(halves(wb_ref)):
            zg_ref[:, c * half:(c + 1) * half] = acc.astype(BF16)


def _inproj(x, norm_w, tables, w_a, w_b, w_small):
    T, D = x.shape
    bn = INPROJ_BN
    zg_cols = Z_COLS + 2 * D
    assert A_COLS % bn == 0 and V_COLS % bn == 0 and G_COLS % bn == 0 and Z_COLS % bn == 0
    assert (bn // 2) % D_VA == 0 and D_VA == LANES and (2 * D) % bn == 0
    assert w_a.shape[1] == 2 * A_COLS + V_COLS + G_COLS + Z_COLS and w_b.shape[1] == 2 * D
    nb_a, nb_v, nb_g, nb_z = A_COLS // bn, V_COLS // bn, G_COLS // bn, zg_cols // bn
    o_k, o_v = nb_a, 2 * nb_a
    o_g = o_v + nb_v
    o_z = o_g + nb_g
    nb_wa, nb_wb = w_a.shape[1] // bn, w_b.shape[1] // bn
    bm = _pick(T, (1024, 512, 256, 128))
    hpb = bn // D_VA

    def col(lo, n):
        return lambda i, j: (i, jnp.clip(j - lo, 0, n - 1))

    cos_t, sa_t, sb_t = tables
    outs = pl.pallas_call(
        functools.partial(_inproj_body, o_k, o_v, o_g, o_z, nb_wa),
        grid=(T // bm, nb_wa + nb_wb),
        in_specs=[
            pl.BlockSpec((bm, D), lambda i, j: (i, 0), pipeline_mode=pl.Buffered(1)),
            pl.BlockSpec((1, D), lambda i, j: (0, 0)),
            pl.BlockSpec((bm, LANES), lambda i, j: (i, 0)),
            pl.BlockSpec((bm, LANES), lambda i, j: (i, 0)),
            pl.BlockSpec((bm, LANES), lambda i, j: (i, 0)),
            pl.BlockSpec((D, bn), lambda i, j: (0, jnp.minimum(j, nb_wa - 1))),
            pl.BlockSpec((D, bn), lambda i, j: (0, jnp.maximum(j - nb_wa, 0))),
            pl.BlockSpec((D, LANES), lambda i, j: (0, 0)),
        ],
        out_specs=[
            pl.BlockSpec((bm, LANES), lambda i, j: (i, 0)),
            pl.BlockSpec((bm * H_A, D_VA), lambda i, j: (i, 0)),
            pl.BlockSpec((bm * H_A, D_VA), lambda i, j: (i, 0)),
            pl.BlockSpec((bm, bn), col(0, nb_a)),
            pl.BlockSpec((bm, bn), col(o_k, nb_a)),
            pl.BlockSpec((hpb, D_VA + ONES_ROWS, bm), lambda i, j: (jnp.clip(j - o_v, 0, nb_v - 1), 0, i)),
            pl.BlockSpec((bm, bn), col(o_g, nb_g)),
            pl.BlockSpec((bm, bn), col(o_z, nb_z)),
        ],
        out_shape=[
            jax.ShapeDtypeStruct((T, LANES), F32),
            jax.ShapeDtypeStruct((T * H_A, 2 * D_A), F32),
            jax.ShapeDtypeStruct((T * H_A, D_VA), F32),
            jax.ShapeDtypeStruct((T, A_COLS), BF16),
            jax.ShapeDtypeStruct((T, A_COLS), BF16),
            jax.ShapeDtypeStruct((H_A, D_VA + ONES_ROWS, T), BF16),
            jax.ShapeDtypeStruct((T, G_COLS), F32),
            jax.ShapeDtypeStruct((T, zg_cols), BF16),
        ],
        scratch_shapes=[pltpu.VMEM((bm, D), BF16)],
        compiler_params=_cparams("parallel", "arbitrary"),
        name="inproj",
    )(x, norm_w, cos_t, sa_t, sb_t, w_a, w_b, w_small)
    return InProj(*outs)


def _diff_lambda(lamp, lam_init):
    s1 = jnp.sum(lamp[0:1] * lamp[1:2], axis=1, keepdims=True)
    s2 = jnp.sum(lamp[2:3] * lamp[3:4], axis=1, keepdims=True)
    return jnp.exp(s1) - jnp.exp(s2) + lam_init


def _subln(o, w, lam_init):
    ms = jnp.mean(o * o, axis=-1, keepdims=True)
    return o * lax.rsqrt(ms + EPS) * w * (1.0 - lam_init)


def _attn_prompt_body(lam_init, bq, q_ref, k_ref, vt_ref, lamp_ref, swc_ref, o_ref,
                      m_scr, acc_scr, s_scr):
    qi = pl.program_id(2)
    q = q_ref[...]
    lane = lax.broadcasted_iota(I32, q.shape, 1)
    qz = [jnp.where(lane < D_A, q, jnp.zeros_like(q)), jnp.where(lane >= D_A, q, jnp.zeros_like(q))]
    m_scr[...] = jnp.full(m_scr.shape, -jnp.inf, F32)
    acc_scr[...] = jnp.zeros(acc_scr.shape, F32)
    qry_pos = lax.broadcasted_iota(I32, (1, bq), 1)

    def scores(j, slot):
        kb = k_ref[pl.ds(pl.multiple_of(j * bq, bq), bq), :]
        for m in range(2):
            s_scr[slot, m] = lax.dot_general(kb, qz[m], NT_DIMS, preferred_element_type=F32)

    def accumulate(j, slot, masked):
        vb = vt_ref[:, pl.ds(pl.multiple_of(j * bq, bq), bq)]
        for m in range(2):
            if masked:
                s = jnp.concatenate(
                    [jnp.where(qry_pos >= c * CHUNK, s_scr[slot, m, c * CHUNK:(c + 1) * CHUNK, :], -jnp.inf)
                     for c in range(bq // CHUNK)], axis=0)
            else:
                s = s_scr[slot, m]
            m_prev = m_scr[m]
            m_new = jnp.maximum(m_prev, jnp.max(s, axis=0, keepdims=True))
            alpha = jnp.exp2(m_prev - m_new)
            p = jnp.exp2(s - m_new).astype(BF16)
            acc_scr[m] = alpha * acc_scr[m] + jnp.dot(vb, p, preferred_element_type=F32)
            m_scr[m] = m_new

    scores(0, 0)

    def two_full_blocks(t, carry):
        j = 2 * t
        scores(j + 1, 1)
        accumulate(j, 0, False)
        scores(j + 2, 0)
        accumulate(j + 1, 1, False)
        return carry

    lax.fori_loop(0, qi // 2, two_full_blocks, 0)
    j0 = 2 * (qi // 2)

    @pl.when(j0 == qi)
    def _():
        accumulate(qi, 0, True)

    @pl.when(j0 != qi)
    def _():
        scores(qi, 1)
        accumulate(j0, 0, False)
        accumulate(qi, 1, True)

    lam = _diff_lambda(lamp_ref[...], lam_init)
    a0 = acc_scr[0]
    a1 = acc_scr[1]
    o_t = a0[0:D_VA] / a0[D_VA:D_VA + 1] - lam * (a1[0:D_VA] / a1[D_VA:D_VA + 1])
    ms = jnp.mean(o_t * o_t, axis=0, keepdims=True)
    o_t = o_t * lax.rsqrt(ms + EPS) * swc_ref[...] * (1.0 - lam_init)
    o_ref[...] = o_t.T.astype(o_ref.dtype)


def _attn_prompt(q_bf, k_bf, v_t, lamp, subln_w, lam_init, bp, lp):
    bq = _pick(lp, (512, 256, 128))
    nq = lp // bq
    vrows = v_t.shape[1]
    return pl.pallas_call(
        functools.partial(_attn_prompt_body, lam_init, bq),
        grid=(bp, H_A, nq),
        in_specs=[
            pl.BlockSpec((bq, LANES), lambda b, h, i: (b * nq + i, h)),
            pl.BlockSpec((lp, LANES), lambda b, h, i: (b, h)),
            pl.BlockSpec((None, vrows, lp), lambda b, h, i: (h, 0, b)),
            pl.BlockSpec((4, D_A), lambda b, h, i: (0, 0)),
            pl.BlockSpec((D_VA, 1), lambda b, h, i: (0, 0)),
        ],
        out_specs=pl.BlockSpec((bq, LANES), lambda b, h, i: (b * nq + i, h)),
        out_shape=jax.ShapeDtypeStruct((bp * lp, V_COLS), BF16),
        scratch_shapes=[
            pltpu.VMEM((2, 1, bq), F32),
            pltpu.VMEM((2, vrows, bq), F32),
            pltpu.VMEM((2, 2, bq, bq), F32),
        ],
        compiler_params=_cparams("parallel", "parallel", "arbitrary"),
        name="attn_prompt",
    )(q_bf, k_bf, v_t, lamp, subln_w.reshape(D_VA, 1))


def _attn_sample_body(lam_init, q_ref, kn_ref, vn_ref, kc_ref, vc_ref, lamp_ref, sw_ref, o_ref,
                      m_scr, l_scr, acc_scr):
    pj = pl.program_id(1)
    npos = kc_ref.shape[0] // H_A
    ls = q_ref.shape[0]

    @pl.when(pj == 0)
    def _():
        m_scr[...] = jnp.full(m_scr.shape, -jnp.inf, F32)
        l_scr[...] = jnp.zeros(l_scr.shape, F32)
        acc_scr[...] = jnp.zeros(acc_scr.shape, F32)

    lane = lax.broadcasted_iota(I32, (ls, D_VA), 1)

    def q_maps(h):
        q = q_ref[:, h * D_VA:(h + 1) * D_VA]
        return jnp.concatenate([jnp.where(lane < D_A, q, jnp.zeros_like(q)),
                                jnp.where(lane >= D_A, q, jnp.zeros_like(q))], axis=0)

    def update(h, qz, keys, vals):
        s = lax.dot_general(qz, keys, NT_DIMS, preferred_element_type=F32)
        m_prev = m_scr[h]
        m_new = jnp.maximum(m_prev, jnp.max(s, axis=1, keepdims=True))
        alpha = jnp.exp2(m_prev - m_new)
        p = jnp.exp2(s - m_new)
        l_scr[h] = alpha * l_scr[h] + jnp.sum(p, axis=1, keepdims=True)
        acc_scr[h] = alpha * acc_scr[h] + jnp.dot(p.astype(BF16), vals, preferred_element_type=F32)
        m_scr[h] = m_new

    for h in range(H_A):
        kc = kc_ref[pl.ds(h, npos, stride=H_A), :].astype(BF16)
        vc = vc_ref[pl.ds(h, npos, stride=H_A), :].astype(BF16)
        update(h, q_maps(h), kc, vc)

    @pl.when(pj == pl.num_programs(1) - 1)
    def _():
        lam = _diff_lambda(lamp_ref[...], lam_init)
        sw = sw_ref[...]
        for h in range(H_A):
            kn = kn_ref[:, h * D_VA:(h + 1) * D_VA]
            vn = vn_ref[pl.ds(h, ls, stride=H_A), :].astype(BF16)
            update(h, q_maps(h), kn, vn)
            both = acc_scr[h] / l_scr[h]
            o = both[0:ls] - lam * both[ls:2 * ls]
            o_ref[:, h * D_VA:(h + 1) * D_VA] = _subln(o, sw, lam_init).astype(o_ref.dtype)


def _attn_sample(q_bf, k_bf, v_new, cache_k, cache_v, lamp, subln_w, lam_init, bs, ls):
    past = cache_k.shape[1] // H_A
    pb = _pick(past, (1024, 512, 256, 128, 64))
    assert ls % (2 * SUBLANES) == 0
    return pl.pallas_call(
        functools.partial(_attn_sample_body, lam_init),
        grid=(bs, past // pb),
        in_specs=[
            pl.BlockSpec((ls, A_COLS), lambda b, j: (b, 0)),
            pl.BlockSpec((ls, A_COLS), lambda b, j: (b, 0)),
            pl.BlockSpec((ls * H_A, D_VA), lambda b, j: (b, 0)),
            pl.BlockSpec((None, pb * H_A, D_VA), lambda b, j: (b, j, 0)),
            pl.BlockSpec((None, pb * H_A, D_VA), lambda b, j: (b, j, 0)),
            pl.BlockSpec((4, D_A), lambda b, j: (0, 0)),
            pl.BlockSpec((1, D_VA), lambda b, j: (0, 0)),
        ],
        out_specs=pl.BlockSpec((ls, V_COLS), lambda b, j: (b, 0)),
        out_shape=jax.ShapeDtypeStruct((bs * ls, V_COLS), BF16),
        scratch_shapes=[
            pltpu.VMEM((H_A, 2 * ls, 1), F32),
            pltpu.VMEM((H_A, 2 * ls, 1), F32),
            pltpu.VMEM((H_A, 2 * ls, D_VA), F32),
        ],
        compiler_params=_cparams("parallel", "arbitrary"),
        name="attn_sample",
    )(q_bf, k_bf, v_new, cache_k, cache_v, lamp, subln_w)


def _gdn_body(chunk, rows, pq_ref, pk_ref, pv_ref, pz_ref, sm_ref, smt_ref, cw_ref, alr_ref, dtr_ref,
              alc_ref, dtc_ref, nw_ref, s0_ref, c0_ref, o_ref, s_ref, cn_ref, ext_scr, act_scr):
    i = pl.program_id(1)
    nblk = pl.num_programs(1)
    hk = H_B * D_K
    hv = H_B * D_V
    n_chunks = rows // chunk
    n_steps = int(math.log2(chunk))
    assert 2 ** n_steps == chunk

    @pl.when(i == 0)
    def _():
        s_ref[...] = s0_ref[...]
        ext_scr[0:SUBLANES, :] = jnp.zeros((SUBLANES, ext_scr.shape[1]), F32)
        ext_scr[pl.ds(SUBLANES - (CONV_W - 1), CONV_W - 1), :] = c0_ref[...]

    ext_scr[pl.ds(SUBLANES, rows), 0:hk] = pq_ref[...]
    ext_scr[pl.ds(SUBLANES, rows), hk:2 * hk] = pk_ref[...]
    ext_scr[pl.ds(SUBLANES, rows), 2 * hk:2 * hk + hv] = pv_ref[...]

    @pl.when(i == nblk - 1)
    def _():
        cn_ref[...] = ext_scr[pl.ds(SUBLANES + rows - (CONV_W - 1), CONV_W - 1), :]

    conv = jnp.zeros((rows, ext_scr.shape[1]), F32)
    for w in range(CONV_W):
        conv = conv + ext_scr[pl.ds(SUBLANES - (CONV_W - 1) + w, rows), :] * cw_ref[w:w + 1, :]
    act_scr[...] = _silu(conv)
    ext_scr[0:SUBLANES, :] = ext_scr[pl.ds(rows, SUBLANES), :]

    sm = sm_ref[...]
    beta_all = _sigmoid(sm[:, 0:H_B])
    g_all = -jnp.exp(alr_ref[...]) * _softplus(sm[:, H_B:2 * H_B] + dtr_ref[...])
    smt = smt_ref[...]
    g_all_t = -jnp.exp(alc_ref[...]) * _softplus(smt[H_B:2 * H_B, :] + dtc_ref[...])

    ri = lax.broadcasted_iota(I32, (chunk, chunk), 0)
    ci = lax.broadcasted_iota(I32, (chunk, chunk), 1)
    tri = ri >= ci
    strict = ri > ci
    eye = (ri == ci).astype(F32)
    ltri = tri.astype(F32)
    utri = (ri <= ci).astype(F32)
    nw = nw_ref[...]

    units = [(c, h) for c in range(n_chunks) for h in range(H_B)]
    gcs = [_dot_exact_mask(ltri, g_all[c * chunk:(c + 1) * chunk, :], True) for c in range(n_chunks)]
    grs = [_dot_exact_mask(utri, g_all_t[:, c * chunk:(c + 1) * chunk], False) for c in range(n_chunks)]
    pre = []
    for c, h in units:
        r0 = c * chunk
        q = act_scr[r0:r0 + chunk, h * D_K:(h + 1) * D_K]
        k = act_scr[r0:r0 + chunk, hk + h * D_K:hk + (h + 1) * D_K]
        v = act_scr[r0:r0 + chunk, 2 * hk + h * D_V:2 * hk + (h + 1) * D_V]
        q = q * lax.rsqrt(jnp.sum(q * q, axis=-1, keepdims=True) + EPS) * (D_K ** -0.5)
        k = k * lax.rsqrt(jnp.sum(k * k, axis=-1, keepdims=True) + EPS)
        beta = beta_all[r0:r0 + chunk, h:h + 1]
        gc = gcs[c][:, h:h + 1]
        gr = grs[c][h:h + 1, :]
        g_last = gc[chunk - 1:chunk, :]
        decay = jnp.where(tri, jnp.exp(jnp.where(tri, gc - gr, 0.0)), 0.0)
        kb = k * beta
        eg = jnp.exp(gc)
        kbf = k.astype(BF16)
        kk = lax.dot_general(kb.astype(BF16), kbf, NT_DIMS, preferred_element_type=F32)
        qk = lax.dot_general(q.astype(BF16), kbf, NT_DIMS, preferred_element_type=F32)
        pre.append(dict(
            nm=-jnp.where(strict, kk * decay, 0.0),
            rhs=jnp.concatenate([v * beta, kb * eg], axis=1),
            qk=jnp.where(tri, qk * decay, 0.0).astype(BF16),
            qg=(q * eg).astype(BF16),
            kg=(k * jnp.exp(g_last - gc)).astype(BF16),
            gl=jnp.exp(g_last)))
    invs = [eye + p["nm"] for p in pre]
    pws = [p["nm"] for p in pre]
    for _ in range(n_steps - 1):
        pws = [_bdot(pw, pw) for pw in pws]
        invs = [inv + _bdot(inv, pw) for inv, pw in zip(invs, pws)]
    resid = [eye - _dot3(eye - p["nm"], inv) for p, inv in zip(pre, invs)]
    invs = [inv + _bdot(inv, r) for inv, r in zip(invs, resid)]
    uws = [_bdot(inv, p["rhs"]) for p, inv in zip(pre, invs)]

    for (c, h), p, uw in zip(units, pre, uws):
        r0 = c * chunk
        u = uw[:, 0:D_V]
        wmat = uw[:, D_V:D_V + D_K]
        s = s_ref[h]
        sb = s.astype(BF16)
        v_new = u - jnp.dot(wmat.astype(BF16), sb, preferred_element_type=F32)
        v_new_b = v_new.astype(BF16)
        o = (jnp.dot(p["qg"], sb, preferred_element_type=F32)
             + jnp.dot(p["qk"], v_new_b, preferred_element_type=F32))
        s_ref[h] = s * p["gl"] + lax.dot_general(p["kg"], v_new_b, TN_DIMS, preferred_element_type=F32)
        z = pz_ref[r0:r0 + chunk, h * D_V:(h + 1) * D_V].astype(F32)
        o = o * lax.rsqrt(jnp.mean(o * o, axis=-1, keepdims=True) + EPS) * nw * _silu(z)
        o_ref[r0:r0 + chunk, h * D_V:(h + 1) * D_V] = o.astype(o_ref.dtype)


def _gdn(gq, zg, small, s0, c0, conv_w, a_log, dt_bias, norm_w, bn, ln, chunk):
    hk = H_B * D_K
    assert H_B * D_V == hk and Z_COLS == hk
    rows = ln
    for cand in (2 * chunk, chunk):
        if cand % LANES == 0 and ln % cand == 0:
            rows = cand
            break
    assert rows % chunk == 0 and rows % SUBLANES == 0
    nblk = ln // rows
    small_t = jnp.swapaxes(small[:, 0:2 * H_B].reshape(bn, ln, 2 * H_B), 1, 2)
    alr = a_log.reshape(1, H_B)
    dtr = dt_bias.reshape(1, H_B)
    alc = a_log.reshape(H_B, 1)
    dtc = dt_bias.reshape(H_B, 1)
    row_blk = lambda b, i: b * nblk + i
    return pl.pallas_call(
        functools.partial(_gdn_body, chunk, rows),
        grid=(bn, nblk),
        in_specs=[
            pl.BlockSpec((rows, hk), lambda b, i: (row_blk(b, i), 0)),
            pl.BlockSpec((rows, hk), lambda b, i: (row_blk(b, i), 1)),
            pl.BlockSpec((rows, hk), lambda b, i: (row_blk(b, i), 2)),
            pl.BlockSpec((rows, hk), lambda b, i: (row_blk(b, i), 0)),
            pl.BlockSpec((rows, LANES), lambda b, i: (row_blk(b, i), 0)),
            pl.BlockSpec((None, 2 * H_B, rows), lambda b, i: (b, 0, i)),
            pl.BlockSpec((CONV_W, G_COLS), lambda b, i: (0, 0)),
            pl.BlockSpec((1, H_B), lambda b, i: (0, 0)),
            pl.BlockSpec((1, H_B), lambda b, i: (0, 0)),
            pl.BlockSpec((H_B, 1), lambda b, i: (0, 0)),
            pl.BlockSpec((H_B, 1), lambda b, i: (0, 0)),
            pl.BlockSpec((1, D_V), lambda b, i: (0, 0)),
            pl.BlockSpec((None, H_B, D_K, D_V), lambda b, i: (b, 0, 0, 0)),
            pl.BlockSpec((None, CONV_W - 1, G_COLS), lambda b, i: (b, 0, 0)),
        ],
        out_specs=[
            pl.BlockSpec((rows, hk), lambda b, i: (row_blk(b, i), 0)),
            pl.BlockSpec((None, H_B, D_K, D_V), lambda b, i: (b, 0, 0, 0)),
            pl.BlockSpec((None, CONV_W - 1, G_COLS), lambda b, i: (b, 0, 0)),
        ],
        out_shape=[
            jax.ShapeDtypeStruct((bn * ln, hk), BF16),
            jax.ShapeDtypeStruct(s0.shape, F32),
            jax.ShapeDtypeStruct(c0.shape, F32),
        ],
        scratch_shapes=[
            pltpu.VMEM((rows + SUBLANES, G_COLS), F32),
            pltpu.VMEM((rows, G_COLS), F32),
        ],
        compiler_params=_cparams("parallel", "arbitrary"),
        name="gdn",
    )(gq, gq, gq, zg, small, small_t, conv_w, alr, dtr, alc, dtc, norm_w, s0, c0)


def _merge_body(x_ref, oa_ref, ob_ref, ga0_ref, ga1_ref, gb0_ref, gb1_ref, wa_ref, wb_ref, wo_ref,
                nf_ref, wr_ref, br_ref, x2_ref, t_ref, lg_ref):
    ya = jnp.dot(oa_ref[...], wa_ref[...], preferred_element_type=F32)
    yb = jnp.dot(ob_ref[...], wb_ref[...], preferred_element_type=F32)
    half = ga0_ref.shape[1]
    gate = lambda ref: _sigmoid(ref[...].astype(F32))
    m0 = gate(ga0_ref) * ya[:, :half] + gate(gb0_ref) * yb[:, :half]
    m1 = gate(ga1_ref) * ya[:, half:] + gate(gb1_ref) * yb[:, half:]
    merged = jnp.concatenate([m0, m1], axis=1).astype(BF16)
    x2 = x_ref[...] + jnp.dot(merged, wo_ref[...], preferred_element_type=F32)
    x2_ref[...] = x2
    ms = jnp.mean(x2 * x2, axis=-1, keepdims=True)
    t = x2 * lax.rsqrt(ms + EPS) * nf_ref[...]
    t_ref[...] = t
    lg_ref[...] = _dot3(t, wr_ref[...]) + br_ref[...]


def _merge(x, o_a, o_b, zg, wa, wb, wo, norm_ffn, w_router, b_router):
    T, D = x.shape
    half = D // 2
    assert Z_COLS % half == 0
    gcol = Z_COLS // half
    bm = _pick(T, (256, 128, 64, 32, 16, 8))
    const = dict(pipeline_mode=pl.Buffered(1))
    return pl.pallas_call(
        _merge_body,
        grid=(T // bm,),
        in_specs=[
            pl.BlockSpec((bm, D), lambda i: (i, 0)),
            pl.BlockSpec((bm, o_a.shape[1]), lambda i: (i, 0)),
            pl.BlockSpec((bm, o_b.shape[1]), lambda i: (i, 0)),
            pl.BlockSpec((bm, half), lambda i: (i, gcol)),
            pl.BlockSpec((bm, half), lambda i: (i, gcol + 1)),
            pl.BlockSpec((bm, half), lambda i: (i, gcol + 2)),
            pl.BlockSpec((bm, half), lambda i: (i, gcol + 3)),
            pl.BlockSpec(wa.shape, lambda i: (0, 0), **const),
            pl.BlockSpec(wb.shape, lambda i: (0, 0), **const),
            pl.BlockSpec(wo.shape, lambda i: (0, 0), **const),
            pl.BlockSpec((1, D), lambda i: (0, 0)),
            pl.BlockSpec((D, LANES), lambda i: (0, 0), **const),
            pl.BlockSpec((1, LANES), lambda i: (0, 0)),
        ],
        out_specs=[
            pl.BlockSpec((bm, D), lambda i: (i, 0)),
            pl.BlockSpec((bm, D), lambda i: (i, 0)),
            pl.BlockSpec((bm, LANES), lambda i: (i, 0)),
        ],
        out_shape=[
            jax.ShapeDtypeStruct((T, D), F32),
            jax.ShapeDtypeStruct((T, D), F32),
            jax.ShapeDtypeStruct((T, LANES), F32),
        ],
        compiler_params=_cparams("parallel"),
        name="merge",
    )(x, o_a, o_b, zg, zg, zg, zg, wa, wb, wo, norm_ffn, w_router, b_router)


def _route_body(lg_ref, mi_ref, mf_ref, cnt_ref):
    i = pl.program_id(0)

    @pl.when(i == 0)
    def _():
        cnt_ref[...] = jnp.zeros(cnt_ref.shape, F32)

    lg = lg_ref[...]
    bm = lg.shape[0]
    lane = lax.broadcasted_iota(I32, lg.shape, 1).astype(F32)
    big = jnp.float32(LANES)
    neg = -jnp.inf
    gl = jnp.where(lane < N_GROUPS, lg, neg)
    gmax = jnp.max(gl, axis=1, keepdims=True)
    gidx = jnp.min(jnp.where(gl == gmax, lane, big), axis=1, keepdims=True)
    gw = 1.0 / jnp.sum(jnp.exp(gl - gmax), axis=1, keepdims=True)
    e_lo = N_GROUPS + gidx * EXPERTS_PER_GROUP
    valid = (lane >= e_lo) & (lane < e_lo + EXPERTS_PER_GROUP)
    el = jnp.where(valid, lg, neg)
    v1 = jnp.max(el, axis=1, keepdims=True)
    i1 = jnp.min(jnp.where(el == v1, lane, big), axis=1, keepdims=True)
    el2 = jnp.where(lane == i1, neg, el)
    v2 = jnp.max(el2, axis=1, keepdims=True)
    i2 = jnp.min(jnp.where(el2 == v2, lane, big), axis=1, keepdims=True)
    e21 = jnp.exp(v2 - v1)
    w1 = gw / (1.0 + e21)
    w2 = gw * e21 / (1.0 + e21)
    oh1 = (lane == i1).astype(F32)
    oh2 = (lane == i2).astype(F32)
    oh = oh1 + oh2
    rr = lax.broadcasted_iota(I32, (bm, bm), 0)
    cc = lax.broadcasted_iota(I32, (bm, bm), 1)
    before = (cc < rr).astype(BF16)
    cum = jnp.dot(before, oh.astype(BF16), preferred_element_type=F32) + cnt_ref[...]
    rank1 = jnp.sum(cum * oh1, axis=1, keepdims=True)
    rank2 = jnp.sum(cum * oh2, axis=1, keepdims=True)
    cnt_ref[...] = cnt_ref[...] + jnp.sum(oh, axis=0, keepdims=True)
    mi = jnp.where(lane == 0, i1 - N_GROUPS, 0.0)
    mi = jnp.where(lane == 1, i2 - N_GROUPS, mi)
    mi = jnp.where(lane == 2, rank1, mi)
    mi = jnp.where(lane == 3, rank2, mi)
    mi_ref[...] = mi.T[0:SUBLANES].astype(I32)
    mf_ref[...] = jnp.where(lane == 0, w1, jnp.where(lane == 1, w2, 0.0))


def _route(logits):
    T = logits.shape[0]
    bm = _pick(T, (512, 256, 128, 64, 32, 16, 8))
    return pl.pallas_call(
        _route_body,
        grid=(T // bm,),
        in_specs=[pl.BlockSpec((bm, LANES), lambda i: (i, 0))],
        out_specs=[
            pl.BlockSpec((SUBLANES, bm), lambda i: (0, i)),
            pl.BlockSpec((bm, LANES), lambda i: (i, 0)),
            pl.BlockSpec((1, LANES), lambda i: (0, 0)),
        ],
        out_shape=[
            jax.ShapeDtypeStruct((SUBLANES, T), I32),
            jax.ShapeDtypeStruct((T, LANES), F32),
            jax.ShapeDtypeStruct((1, LANES), F32),
        ],
        compiler_params=_cparams("arbitrary"),
        name="route",
    )(logits)


def _dispatch_body(bm, blk, n_experts, nbp, ps_ref, pn_ref, nu_ref, dest_ref, tp_ref, ts_ref, xb_ref,
                   zero_scr, sem, zsem):
    i = pl.program_id(0)
    n_blocks = xb_ref.shape[0] // blk

    def pad_rows(act):
        def per_expert(e, carry):
            def per_row(r, c2):
                act(pltpu.make_async_copy(zero_scr.at[pl.ds(0, 1)], xb_ref.at[pl.ds(ps_ref[e] + r, 1)], zsem))
                return c2
            return lax.fori_loop(0, pn_ref[e], per_row, carry)
        lax.fori_loop(0, n_experts, per_expert, 0)

    def tail_blocks(act):
        def per_block(b, carry):
            act(pltpu.make_async_copy(zero_scr, xb_ref.at[pl.ds(pl.multiple_of(b * blk, blk), blk)], zsem))
            return carry
        lax.fori_loop(nu_ref[0], n_blocks, per_block, 0)

    @pl.when(i == 0)
    def _():
        zero_scr[...] = jnp.zeros(zero_scr.shape, zero_scr.dtype)
        pad_rows(lambda cp: cp.start())
        tail_blocks(lambda cp: cp.start())

    def send_rows(t_ref):
        def row_copy(r, k):
            d = dest_ref[0, k * bm + r]
            return pltpu.make_async_copy(t_ref.at[pl.ds(r, 1)], xb_ref.at[pl.ds(d, 1)], sem)

        def issue(r, carry):
            for k in range(TOP_K_INNER):
                row_copy(r, k).start()
            return carry

        def drain(r, carry):
            for k in range(TOP_K_INNER):
                row_copy(r, k).wait()
            return carry

        lax.fori_loop(0, bm, issue, 0, unroll=DMA_UNROLL)
        lax.fori_loop(0, bm, drain, 0, unroll=DMA_UNROLL)

    @pl.when(i < nbp)
    def _():
        send_rows(tp_ref)

    @pl.when(i >= nbp)
    def _():
        send_rows(ts_ref)

    @pl.when(i == 0)
    def _():
        pad_rows(lambda cp: cp.wait())
        tail_blocks(lambda cp: cp.wait())


def _dispatch(t_p, t_s, dest, pad_start, pad_len, n_used, n_rows, blk):
    (tp, D), ts = t_p.shape, t_s.shape[0]
    bm = _pick(math.gcd(tp, ts), (256, 128, 64, 32, 16, 8))
    nbp, nbs = tp // bm, ts // bm
    dest3 = _dest_blocks(dest, bm)
    grid_spec = pltpu.PrefetchScalarGridSpec(
        num_scalar_prefetch=3,
        grid=(nbp + nbs,),
        in_specs=[
            pl.BlockSpec((None, 1, TOP_K_INNER * bm), lambda i, ps, pn, nu: (i, 0, 0),
                         memory_space=pltpu.SMEM),
            pl.BlockSpec((bm, D), lambda i, ps, pn, nu: (jnp.minimum(i, nbp - 1), 0)),
            pl.BlockSpec((bm, D), lambda i, ps, pn, nu: (jnp.maximum(i - nbp, 0), 0)),
        ],
        out_specs=pl.BlockSpec(memory_space=pl.ANY),
        scratch_shapes=[
            pltpu.VMEM((blk, D), t_p.dtype),
            pltpu.SemaphoreType.DMA(()),
            pltpu.SemaphoreType.DMA(()),
        ],
    )
    return pl.pallas_call(
        functools.partial(_dispatch_body, bm, blk, pad_start.shape[0], nbp),
        grid_spec=grid_spec,
        out_shape=jax.ShapeDtypeStruct((n_rows, D), t_p.dtype),
        compiler_params=_cparams("arbitrary"),
        name="dispatch",
    )(pad_start, pad_len, n_used, dest3, t_p, t_s)


def _expert_body(be_ref, nu_ref, x_ref, wg_ref, wu_ref, wd_ref, y_ref, wg_scr, wu_scr, wd_scr):
    i = pl.program_id(0)

    @pl.when(i < nu_ref[0])
    def _():
        @pl.when(jnp.logical_or(i == 0, be_ref[i] != be_ref[jnp.maximum(i - 1, 0)]))
        def _():
            wg_scr[...] = wg_ref[...].astype(BF16)
            wu_scr[...] = wu_ref[...].astype(BF16)
            wd_scr[...] = wd_ref[...].astype(BF16)

        x = x_ref[...].astype(BF16)
        g = jnp.dot(x, wg_scr[...], preferred_element_type=F32)
        u = jnp.dot(x, wu_scr[...], preferred_element_type=F32)
        hmid = (_silu(g) * u).astype(BF16)
        y_ref[...] = jnp.dot(hmid, wd_scr[...], preferred_element_type=F32)

    @pl.when(i >= nu_ref[0])
    def _():
        y_ref[...] = jnp.zeros(y_ref.shape, F32)


def _experts(xb, blk_exp, n_used, w_gate, w_up, w_down, blk):
    P, D = xb.shape
    de = w_gate.shape[2]
    n_blocks = P // blk
    last = lambda i, nu: jnp.minimum(i, nu[0] - 1)
    grid_spec = pltpu.PrefetchScalarGridSpec(
        num_scalar_prefetch=2,
        grid=(n_blocks,),
        in_specs=[
            pl.BlockSpec((blk, D), lambda i, be, nu: (last(i, nu), 0)),
            pl.BlockSpec((None, D, de), lambda i, be, nu: (be[last(i, nu)], 0, 0)),
            pl.BlockSpec((None, D, de), lambda i, be, nu: (be[last(i, nu)], 0, 0)),
            pl.BlockSpec((None, de, D), lambda i, be, nu: (be[last(i, nu)], 0, 0)),
        ],
        out_specs=pl.BlockSpec((blk, D), lambda i, be, nu: (i, 0)),
        scratch_shapes=[
            pltpu.VMEM((D, de), BF16),
            pltpu.VMEM((D, de), BF16),
            pltpu.VMEM((de, D), BF16),
        ],
    )
    return pl.pallas_call(
        _expert_body,
        grid_spec=grid_spec,
        out_shape=jax.ShapeDtypeStruct((P, D), F32),
        compiler_params=_cparams("arbitrary"),
        name="experts",
    )(blk_exp, n_used, xb, w_gate, w_up, w_down)


def _combine_body(bm, final, dest_ref, x2_ref, mf_ref, nw_ref, yb_ref, o_ref, y0_scr, y1_scr, sem):
    bufs = (y0_scr, y1_scr)

    def row_copy(r, k):
        d = dest_ref[0, k * bm + r]
        return pltpu.make_async_copy(yb_ref.at[pl.ds(d, 1)], bufs[k].at[pl.ds(r, 1)], sem)

    def issue(r, carry):
        for k in range(TOP_K_INNER):
            row_copy(r, k).start()
        return carry

    lax.fori_loop(0, bm, issue, 0, unroll=DMA_UNROLL)

    def drain(r, carry):
        for k in range(TOP_K_INNER):
            row_copy(r, k).wait()
        return carry

    lax.fori_loop(0, bm, drain, 0, unroll=DMA_UNROLL)

    mf = mf_ref[...]
    x3 = x2_ref[...] + y0_scr[...] * mf[:, 0:1] + y1_scr[...] * mf[:, 1:2]
    if final:
        ms = jnp.mean(x3 * x3, axis=-1, keepdims=True)
        x3 = x3 * lax.rsqrt(ms + EPS) * nw_ref[...]
    o_ref[...] = x3


def _combine(x2, yb, dest, mf, norm_final, final):
    T, D = x2.shape
    bm = _pick(T, (256, 128, 64, 32, 16, 8))
    nblk = T // bm
    dest3 = _dest_blocks(dest, bm)
    return pl.pallas_call(
        functools.partial(_combine_body, bm, final),
        grid=(nblk,),
        in_specs=[
            pl.BlockSpec((None, 1, TOP_K_INNER * bm), lambda i: (i, 0, 0), memory_space=pltpu.SMEM),
            pl.BlockSpec((bm, D), lambda i: (i, 0)),
            pl.BlockSpec((bm, LANES), lambda i: (i, 0)),
            pl.BlockSpec((1, D), lambda i: (0, 0)),
            pl.BlockSpec(memory_space=pl.ANY),
        ],
        out_specs=pl.BlockSpec((bm, D), lambda i: (i, 0)),
        out_shape=jax.ShapeDtypeStruct((T, D), F32),
        scratch_shapes=[
            pltpu.VMEM((bm, D), F32),
            pltpu.VMEM((bm, D), F32),
            pltpu.SemaphoreType.DMA(()),
        ],
        compiler_params=_cparams("arbitrary"),
        name="combine",
    )(dest3, x2, mf, norm_final, yb)


def _rotary_tables(pos):
    half = ROT_DIM // 2
    inv = ROPE_THETA ** (-jnp.arange(0, ROT_DIM, 2, dtype=F32) / ROT_DIM)
    ang = pos.astype(F32)[:, None] * inv[None, :]
    cos = jnp.cos(ang)
    sin = jnp.sin(ang)
    n = pos.shape[0]
    ones = jnp.ones((n, D_A - ROT_DIM), F32)
    zeros = jnp.zeros((n, D_A - ROT_DIM), F32)
    zh = jnp.zeros((n, half), F32)
    cos64 = jnp.concatenate([cos, cos, ones], axis=1)
    sa64 = jnp.concatenate([zh, sin, zeros], axis=1)
    sb64 = jnp.concatenate([-sin, zh, zeros], axis=1)
    reps = LANES // D_A
    return jnp.tile(cos64, (1, reps)), jnp.tile(sa64, (1, reps)), jnp.tile(sb64, (1, reps))


def _dest_blocks(dest, bm):
    nblk = dest.shape[1] // bm
    return jnp.swapaxes(dest.reshape(TOP_K_INNER, nblk, bm), 0, 1).reshape(nblk, 1, TOP_K_INNER * bm)


def _moe_slots(mi, counts, n_experts, blk, n_assign):
    eid = mi[0:TOP_K_INNER]
    rank = mi[TOP_K_INNER:2 * TOP_K_INNER]
    cnt = counts[0, N_GROUPS:N_GROUPS + n_experts].astype(I32)
    pc = (cnt + blk - 1) // blk * blk
    pend = jnp.cumsum(pc)
    pstart = pend - pc
    experts = jnp.arange(n_experts, dtype=I32)[:, None, None]
    first = jnp.sum(jnp.where(eid[None] == experts, pstart[:, None, None], 0), axis=0)
    dest = (first + rank).astype(I32)
    n_blocks = -(-n_assign // blk) + n_experts
    blk_start = jnp.arange(n_blocks, dtype=I32) * blk
    blk_exp = jnp.minimum(jnp.sum((pend[None, :] <= blk_start[:, None]).astype(I32), axis=1), n_experts - 1)
    n_used = jnp.maximum(pend[-1] // blk, 1).astype(I32).reshape(1)
    return dest, blk_exp.astype(I32), n_used, n_blocks, (pstart + cnt).astype(I32), (pc - cnt).astype(I32)


def _layer(xp, xs, layer, dims, cache_k, cache_v, state_delta, state_conv, prm, norm_final, final):
    (norm_mix, w_in, lq1, lk1, lq2, lk2, subln_w, conv_w, a_log, dt_bias, gdn_norm_w,
     w_proj_a, w_proj_b, w_out, norm_ffn, w_gr, b_gr, w_er, b_er, w_eg, w_eu, w_ed) = prm
    bp, lp, bs, ls, past = dims
    D = xp.shape[1]
    tp = bp * lp
    n_qkv = 2 * A_COLS + V_COLS + G_COLS + Z_COLS
    n_small = 2 * H_B
    lam_init = 0.8 - 0.6 * math.exp(-0.3 * layer)

    w_a = w_in[:, :n_qkv].astype(BF16)
    w_b = w_in[:, n_qkv + n_small:].astype(BF16)
    w_small = jnp.pad(w_in[:, n_qkv:n_qkv + n_small], ((0, 0), (0, LANES - n_small))).astype(BF16)
    nm = norm_mix.reshape(1, D)
    pp = _inproj(xp, nm, _rotary_tables(jnp.tile(jnp.arange(lp), bp)), w_a, w_b, w_small)
    ps = _inproj(xs, nm, _rotary_tables(jnp.tile(past + jnp.arange(ls), bs)), w_a, w_b, w_small)

    lamp = jnp.stack([lq1, lk1, lq2, lk2]).astype(F32)
    sw = subln_w.reshape(1, D_VA)
    oa_p = _attn_prompt(pp.q_bf, pp.k_bf, pp.v_t, lamp, sw, lam_init, bp, lp)
    oa_s = _attn_sample(ps.q_bf, ps.k_bf, ps.v_new, cache_k.reshape(bs, past * H_A, 2 * D_A),
                        cache_v.reshape(bs, past * H_A, D_VA), lamp, sw, lam_init, bs, ls)

    nw = gdn_norm_w.reshape(1, D_V)
    ob_p, s_p, c_p = _gdn(pp.gq, pp.zg, pp.small, jnp.zeros((bp, H_B, D_K, D_V), F32),
                          jnp.zeros((bp, CONV_W - 1, G_COLS), F32), conv_w, a_log, dt_bias, nw,
                          bp, lp, _pick(lp, (GDN_CHUNK, CHUNK)))
    ob_s, s_s, c_s = _gdn(ps.gq, ps.zg, ps.small, state_delta, state_conv, conv_w, a_log, dt_bias, nw,
                          bs, ls, ls)

    n_experts = w_er.shape[1]
    w_router = jnp.pad(jnp.concatenate([w_gr, w_er], axis=1), ((0, 0), (0, LANES - N_GROUPS - n_experts)))
    b_router = jnp.pad(jnp.concatenate([b_gr, b_er]), (0, LANES - N_GROUPS - n_experts)).reshape(1, LANES)
    mw = (w_proj_a.astype(BF16), w_proj_b.astype(BF16), w_out.astype(BF16), norm_ffn.reshape(1, D),
          w_router, b_router)
    x2_p, t_p, lg_p = _merge(xp, oa_p, ob_p, pp.zg, *mw)
    x2_s, t_s, lg_s = _merge(xs, oa_s, ob_s, ps.zg, *mw)

    blk = MOE_ROWS
    mi, mf, counts = _route(jnp.concatenate([lg_p, lg_s], axis=0))
    n_tok = tp + bs * ls
    dest, blk_exp, n_used, n_blocks, pad_start, pad_len = _moe_slots(
        mi, counts, n_experts, blk, n_tok * TOP_K_INNER)
    xb = _dispatch(t_p, t_s, dest, pad_start, pad_len, n_used, n_blocks * blk, blk)
    yb = _experts(xb, blk_exp, n_used, w_eg, w_eu, w_ed, blk)
    nf = norm_final.reshape(1, D)
    y_p = _combine(x2_p, yb, dest[:, :tp], mf[:tp], nf, final)
    y_s = _combine(x2_s, yb, dest[:, tp:], mf[tp:], nf, final)
    return y_p, y_s, pp.k_new, pp.v_new, s_p, c_p, ps.k_new, ps.v_new, s_s, c_s


def kernel(x_prompt, x_sample, cache_k, cache_v, state_delta, state_conv, norm_mix, w_in, lambda_q1, lambda_k1, lambda_q2, lambda_k2, subln_w, conv_w, a_log, dt_bias, gdn_norm_w, w_proj_a, w_proj_b, w_out, norm_ffn, w_group_router, b_group_router, w_expert_router, b_expert_router, w_exp_gate, w_exp_up, w_exp_down, norm_final):
    bp, lp, D = x_prompt.shape
    bs, ls, _ = x_sample.shape
    depth = cache_k.shape[0]
    past = cache_k.shape[2]
    dims = (bp, lp, bs, ls, past)
    xp = x_prompt.reshape(bp * lp, D)
    xs = x_sample.reshape(bs * ls, D)
    kp, vp, sp, cp, ksm, vsm, ssm, csm = [], [], [], [], [], [], [], []
    for l in range(depth):
        prm = (norm_mix[l], w_in[l], lambda_q1[l], lambda_k1[l], lambda_q2[l], lambda_k2[l], subln_w[l],
               conv_w[l], a_log[l], dt_bias[l], gdn_norm_w[l], w_proj_a[l], w_proj_b[l], w_out[l],
               norm_ffn[l], w_group_router[l], b_group_router[l], w_expert_router[l], b_expert_router[l],
               w_exp_gate[l], w_exp_up[l], w_exp_down[l])
        xp, xs, k_p, v_p, s_p, c_p, k_s, v_s, s_s, c_s = _layer(
            xp, xs, l, dims, cache_k[l], cache_v[l], state_delta[l], state_conv[l], prm,
            norm_final, l == depth - 1)
        kp.append(k_p.reshape(bp, lp, H_A, 2 * D_A))
        vp.append(v_p.reshape(bp, lp, H_A, D_VA))
        ksm.append(k_s.reshape(bs, ls, H_A, 2 * D_A))
        vsm.append(v_s.reshape(bs, ls, H_A, D_VA))
        sp.append(s_p)
        cp.append(c_p)
        ssm.append(s_s)
        csm.append(c_s)
    return (xp.reshape(bp, lp, D), xs.reshape(bs, ls, D), jnp.stack(kp), jnp.stack(vp), jnp.stack(sp),
            jnp.stack(cp), jnp.stack(ksm), jnp.stack(vsm), jnp.stack(ssm), jnp.stack(csm))
```

```python
import collections
import functools
import math

import jax
import jax.numpy as jnp
from jax import lax
from jax.experimental import pallas as pl
from jax.experimental.pallas import tpu as pltpu

F32 = jnp.float32
BF16 = jnp.bfloat16
I32 = jnp.int32
U32 = jnp.uint32

CHUNK = 64
GDN_CHUNK = 128
H_A = 8
D_A = 64
D_VA = 2 * D_A
ROT_DIM = D_A // 4
ROPE_THETA = 500000.0
H_B = 8
D_K = 128
D_V = 128
CONV_W = 4
N_GROUPS = 4
EXPERTS_PER_GROUP = 8
TOP_K_INNER = 2
EPS = 1e-6

LANES = 128
SUBLANES = 8
V7X_VMEM_BYTES = 64 * 1024 * 1024
VMEM_LIMIT_BYTES = V7X_VMEM_BYTES - 8 * 1024 * 1024

Q_SCALE = (D_A ** -0.5) * math.log2(math.e)
ONES_ROWS = 16
INPROJ_BN = 512
MOE_ROWS = 256
DMA_UNROLL = 8

NT_DIMS = (((1,), (1,)), ((), ()))
TN_DIMS = (((0,), (0,)), ((), ()))

A_COLS = H_A * 2 * D_A
V_COLS = H_A * D_VA
G_COLS = 2 * H_B * D_K + H_B * D_V
Z_COLS = H_B * D_V


def _pick(n, cands):
    for c in cands:
        if n % c == 0:
            return c
    raise ValueError(f"no block size in {cands} divides {n}")


def _cparams(*sem):
    return pltpu.CompilerParams(dimension_semantics=sem, vmem_limit_bytes=VMEM_LIMIT_BYTES)


def _sigmoid(x):
    return 1.0 / (1.0 + jnp.exp(-x))


def _silu(x):
    return x * _sigmoid(x)


def _softplus(x):
    return jnp.maximum(x, 0.0) + jnp.log1p(jnp.exp(-jnp.abs(x)))


def _pack_bf16_pair(x):
    bits = lax.bitcast_convert_type(x.astype(BF16).astype(F32), U32)
    half = x.shape[1] // 2
    return (bits[:, :half] >> 16) | (bits[:, half:] & jnp.uint32(0xFFFF0000))


def _unpack_bf16_pair(w):
    lo = lax.bitcast_convert_type(w << 16, F32)
    hi = lax.bitcast_convert_type(w & jnp.uint32(0xFFFF0000), F32)
    return jnp.concatenate([lo, hi], axis=1).astype(BF16)


def _bdot(a, b):
    return jnp.dot(a.astype(BF16), b.astype(BF16), preferred_element_type=F32)


def _split2(a):
    hi = a.astype(BF16)
    lo = (a - hi.astype(F32)).astype(BF16)
    return hi, lo


def _dot3(a, b):
    ah, al = _split2(a)
    bh, bl = _split2(b)
    d = functools.partial(jnp.dot, preferred_element_type=F32)
    return d(ah, bh) + d(ah, bl) + d(al, bh)


def _dot_exact_mask(mask, x, mask_is_lhs):
    hi = x.astype(BF16)
    r1 = x - hi.astype(F32)
    mid = r1.astype(BF16)
    lo = (r1 - mid.astype(F32)).astype(BF16)
    mb = mask.astype(BF16)
    d = functools.partial(jnp.dot, preferred_element_type=F32)
    if mask_is_lhs:
        return d(mb, hi) + d(mb, mid) + d(mb, lo)
    return d(hi, mb) + d(mid, mb) + d(lo, mb)


InProj = collections.namedtuple("InProj", "small k_new v_new q_bf k_bf v_t gq zg")


def _split_w_in_body(n_a, n_small, w_ref, wa_ref, wb_ref, ws_ref):
    w = w_ref[...]
    wa_ref[...] = w[:, :n_a].astype(BF16)
    wb_ref[...] = w[:, n_a + n_small:].astype(BF16)
    lane = lax.broadcasted_iota(I32, (w.shape[0], LANES), 1)
    ws_ref[...] = jnp.where(lane < n_small, w[:, n_a:n_a + LANES], 0.0).astype(BF16)


def _split_w_in(w_in, n_a, n_small):
    D, n = w_in.shape
    n_b = n - n_a - n_small
    assert n_a % LANES == 0 and n_b % LANES == 0 and n_small <= LANES and n_b >= LANES
    rows = _pick(D, (256, 128, 64, 32, 16))
    return pl.pallas_call(
        functools.partial(_split_w_in_body, n_a, n_small),
        grid=(D // rows,),
        in_specs=[pl.BlockSpec((rows, n), lambda i: (i, 0))],
        out_specs=[
            pl.BlockSpec((rows, n_a), lambda i: (i, 0)),
            pl.BlockSpec((rows, n_b), lambda i: (i, 0)),
            pl.BlockSpec((rows, LANES), lambda i: (i, 0)),
        ],
        out_shape=[
            jax.ShapeDtypeStruct((D, n_a), BF16),
            jax.ShapeDtypeStruct((D, n_b), BF16),
            jax.ShapeDtypeStruct((D, LANES), BF16),
        ],
        compiler_params=_cparams("parallel"),
        name="split_w_in",
    )(w_in)


def _inproj_body(o_k, o_v, o_g, o_z, nb_wa, x_ref, nw_ref, cos_ref, sa_ref, sb_ref, wa_ref, wb_ref, ws_ref,
                 os_ref, k_ref, v_ref, qb_ref, kb_ref, vt_ref, gq_ref, zg_ref, h_scr):
    j = pl.program_id(1)
    bm = h_scr.shape[0]
    half = wa_ref.shape[1] // 2
    lpb = half // LANES

    @pl.when(j == 0)
    def _():
        x = x_ref[...]
        ms = jnp.mean(x * x, axis=-1, keepdims=True)
        h = (x * lax.rsqrt(ms + EPS) * nw_ref[...]).astype(BF16)
        h_scr[...] = h
        os_ref[...] = jnp.dot(h, ws_ref[...], preferred_element_type=F32)

    def halves(w_ref):
        return [jnp.dot(h_scr[...], w_ref[:, c * half:(c + 1) * half], preferred_element_type=F32)
                for c in range(2)]

    def lane_blocks(w_ref):
        for c2, acc in enumerate(halves(w_ref)):
            for c in range(lpb):
                yield c2 * lpb + c, acc[:, c * LANES:(c + 1) * LANES]

    def rotated(blk):
        return (blk * cos_ref[...] + pltpu.roll(blk, ROT_DIM // 2, 1) * sa_ref[...]
                + pltpu.roll(blk, LANES - ROT_DIM // 2, 1) * sb_ref[...])

    @pl.when(j < o_k)
    def _():
        for lb, blk in lane_blocks(wa_ref):
            qb_ref[:, lb * LANES:(lb + 1) * LANES] = (rotated(blk) * Q_SCALE).astype(BF16)

    hpb = 2 * lpb
    for jj in range(o_v - o_k):
        @pl.when(j == o_k + jj)
        def _(jj=jj):
            for lb, blk in lane_blocks(wa_ref):
                rot = rotated(blk)
                k_ref[pl.ds(jj * hpb + lb, bm, stride=H_A), :] = rot
                kb_ref[:, lb * LANES:(lb + 1) * LANES] = rot.astype(BF16)

    for jj in range(o_g - o_v):
        @pl.when(j == o_v + jj)
        def _(jj=jj):
            for lb, blk in lane_blocks(wa_ref):
                v_ref[pl.ds(jj * hpb + lb, bm, stride=H_A), :] = blk
                vt_ref[lb, 0:D_VA, :] = blk.T.astype(BF16)
                vt_ref[lb, D_VA:D_VA + ONES_ROWS, :] = jnp.ones((ONES_ROWS, bm), BF16)

    @pl.when(jnp.logical_and(j >= o_g, j < o_z))
    def _():
        for c, acc in enumerate(halves(wa_ref)):
            gq_ref[:, c * half:(c + 1) * half] = acc

    @pl.when(jnp.logical_and(j >= o_z, j < nb_wa))
    def _():
        for c, acc in enumerate(halves(wa_ref)):
            zg_ref[:, c * half:(c + 1) * half] = acc.astype(BF16)

    @pl.when(j >= nb_wa)
    def _():
        for c, acc in enumerate(halves(wb_ref)):
            zg_ref[:, c * half:(c + 1) * half] = acc.astype(BF16)


def _inproj(x, norm_w, tables, w_a, w_b, w_small):
    T, D = x.shape
    bn = INPROJ_BN
    zg_cols = Z_COLS + 2 * D
    assert A_COLS % bn == 0 and V_COLS % bn == 0 and G_COLS % bn == 0 and Z_COLS % bn == 0
    assert (bn // 2) % D_VA == 0 and D_VA == LANES and (2 * D) % bn == 0
    assert w_a.shape[1] == 2 * A_COLS + V_COLS + G_COLS + Z_COLS and w_b.shape[1] == 2 * D
    nb_a, nb_v, nb_g, nb_z = A_COLS // bn, V_COLS // bn, G_COLS // bn, zg_cols // bn
    o_k, o_v = nb_a, 2 * nb_a
    o_g = o_v + nb_v
    o_z = o_g + nb_g
    nb_wa, nb_wb = w_a.shape[1] // bn, w_b.shape[1] // bn
    bm = _pick(T, (1024, 512, 256, 128))
    hpb = bn // D_VA

    def col(lo, n):
        return lambda i, j: (i, jnp.clip(j - lo, 0, n - 1))

    cos_t, sa_t, sb_t = tables
    outs = pl.pallas_call(
        functools.partial(_inproj_body, o_k, o_v, o_g, o_z, nb_wa),
        grid=(T // bm, nb_wa + nb_wb),
        in_specs=[
            pl.BlockSpec((bm, D), lambda i, j: (i, 0), pipeline_mode=pl.Buffered(1)),
            pl.BlockSpec((1, D), lambda i, j: (0, 0)),
            pl.BlockSpec((bm, LANES), lambda i, j: (i, 0)),
            pl.BlockSpec((bm, LANES), lambda i, j: (i, 0)),
            pl.BlockSpec((bm, LANES), lambda i, j: (i, 0)),
            pl.BlockSpec((D, bn), lambda i, j: (0, jnp.minimum(j, nb_wa - 1))),
            pl.BlockSpec((D, bn), lambda i, j: (0, jnp.maximum(j - nb_wa, 0))),
            pl.BlockSpec((D, LANES), lambda i, j: (0, 0)),
        ],
        out_specs=[
            pl.BlockSpec((bm, LANES), lambda i, j: (i, 0)),
            pl.BlockSpec((bm * H_A, D_VA), lambda i, j: (i, 0)),
            pl.BlockSpec((bm * H_A, D_VA), lambda i, j: (i, 0)),
            pl.BlockSpec((bm, bn), col(0, nb_a)),
            pl.BlockSpec((bm, bn), col(o_k, nb_a)),
            pl.BlockSpec((hpb, D_VA + ONES_ROWS, bm), lambda i, j: (jnp.clip(j - o_v, 0, nb_v - 1), 0, i)),
            pl.BlockSpec((bm, bn), col(o_g, nb_g)),
            pl.BlockSpec((bm, bn), col(o_z, nb_z)),
        ],
        out_shape=[
            jax.ShapeDtypeStruct((T, LANES), F32),
            jax.ShapeDtypeStruct((T * H_A, 2 * D_A), F32),
            jax.ShapeDtypeStruct((T * H_A, D_VA), F32),
            jax.ShapeDtypeStruct((T, A_COLS), BF16),
            jax.ShapeDtypeStruct((T, A_COLS), BF16),
            jax.ShapeDtypeStruct((H_A, D_VA + ONES_ROWS, T), BF16),
            jax.ShapeDtypeStruct((T, G_COLS), F32),
            jax.ShapeDtypeStruct((T, zg_cols), BF16),
        ],
        scratch_shapes=[pltpu.VMEM((bm, D), BF16)],
        compiler_params=_cparams("parallel", "arbitrary"),
        name="inproj",
    )(x, norm_w, cos_t, sa_t, sb_t, w_a, w_b, w_small)
    return InProj(*outs)


def _diff_lambda(lamp, lam_init):
    s1 = jnp.sum(lamp[0:1] * lamp[1:2], axis=1, keepdims=True)
    s2 = jnp.sum(lamp[2:3] * lamp[3:4], axis=1, keepdims=True)
    return jnp.exp(s1) - jnp.exp(s2) + lam_init


def _subln(o, w, lam_init):
    ms = jnp.mean(o * o, axis=-1, keepdims=True)
    return o * lax.rsqrt(ms + EPS) * w * (1.0 - lam_init)


def _attn_prompt_body(lam_init, bq, q_ref, k_ref, vt_ref, lamp_ref, swc_ref, o_ref,
                      m_scr, acc_scr, s_scr):
    qi = pl.program_id(2)
    q = q_ref[...]
    lane = lax.broadcasted_iota(I32, q.shape, 1)
    qz = [jnp.where(lane < D_A, q, jnp.zeros_like(q)), jnp.where(lane >= D_A, q, jnp.zeros_like(q))]
    m_scr[...] = jnp.full(m_scr.shape, -jnp.inf, F32)
    acc_scr[...] = jnp.zeros(acc_scr.shape, F32)
    qry_pos = lax.broadcasted_iota(I32, (1, bq), 1)

    def scores(j, slot):
        kb = k_ref[pl.ds(pl.multiple_of(j * bq, bq), bq), :]
        for m in range(2):
            s_scr[slot, m] = lax.dot_general(kb, qz[m], NT_DIMS, preferred_element_type=F32)

    def accumulate(j, slot, masked):
        vb = vt_ref[:, pl.ds(pl.multiple_of(j * bq, bq), bq)]
        for m in range(2):
            if masked:
                s = jnp.concatenate(
                    [jnp.where(qry_pos >= c * CHUNK, s_scr[slot, m, c * CHUNK:(c + 1) * CHUNK, :], -jnp.inf)
                     for c in range(bq // CHUNK)], axis=0)
            else:
                s = s_scr[slot, m]
            m_prev = m_scr[m]
            m_new = jnp.maximum(m_prev, jnp.max(s, axis=0, keepdims=True))
            alpha = jnp.exp2(m_prev - m_new)
            p = jnp.exp2(s - m_new).astype(BF16)
            acc_scr[m] = alpha * acc_scr[m] + jnp.dot(vb, p, preferred_element_type=F32)
            m_scr[m] = m_new

    scores(0, 0)

    def two_full_blocks(t, carry):
        j = 2 * t
        scores(j + 1, 1)
        accumulate(j, 0, False)
        scores(j + 2, 0)
        accumulate(j + 1, 1, False)
        return carry

    lax.fori_loop(0, qi // 2, two_full_blocks, 0)
    j0 = 2 * (qi // 2)

    @pl.when(j0 == qi)
    def _():
        accumulate(qi, 0, True)

    @pl.when(j0 != qi)
    def _():
        scores(qi, 1)
        accumulate(j0, 0, False)
        accumulate(qi, 1, True)

    lam = _diff_lambda(lamp_ref[...], lam_init)
    a0 = acc_scr[0]
    a1 = acc_scr[1]
    o_t = a0[0:D_VA] / a0[D_VA:D_VA + 1] - lam * (a1[0:D_VA] / a1[D_VA:D_VA + 1])
    ms = jnp.mean(o_t * o_t, axis=0, keepdims=True)
    o_t = o_t * lax.rsqrt(ms + EPS) * swc_ref[...] * (1.0 - lam_init)
    o_ref[...] = o_t.T.astype(o_ref.dtype)


def _attn_prompt(q_bf, k_bf, v_t, lamp, subln_w, lam_init, bp, lp):
    bq = _pick(lp, (512, 256, 128))
    nq = lp // bq
    vrows = v_t.shape[1]
    return pl.pallas_call(
        functools.partial(_attn_prompt_body, lam_init, bq),
        grid=(bp, H_A, nq),
        in_specs=[
            pl.BlockSpec((bq, LANES), lambda b, h, i: (b * nq + i, h)),
            pl.BlockSpec((lp, LANES), lambda b, h, i: (b, h)),
            pl.BlockSpec((None, vrows, lp), lambda b, h, i: (h, 0, b)),
            pl.BlockSpec((4, D_A), lambda b, h, i: (0, 0)),
            pl.BlockSpec((D_VA, 1), lambda b, h, i: (0, 0)),
        ],
        out_specs=pl.BlockSpec((bq, LANES), lambda b, h, i: (b * nq + i, h)),
        out_shape=jax.ShapeDtypeStruct((bp * lp, V_COLS), BF16),
        scratch_shapes=[
            pltpu.VMEM((2, 1, bq), F32),
            pltpu.VMEM((2, vrows, bq), F32),
            pltpu.VMEM((2, 2, bq, bq), F32),
        ],
        compiler_params=_cparams("parallel", "parallel", "arbitrary"),
        name="attn_prompt",
    )(q_bf, k_bf, v_t, lamp, subln_w.reshape(D_VA, 1))


def _attn_sample_body(lam_init, q_ref, kn_ref, vn_ref, kc_ref, vc_ref, lamp_ref, sw_ref, o_ref,
                      m_scr, l_scr, acc_scr):
    pj = pl.program_id(1)
    npos = kc_ref.shape[0] // H_A
    ls = q_ref.shape[0]

    @pl.when(pj == 0)
    def _():
        m_scr[...] = jnp.full(m_scr.shape, -jnp.inf, F32)
        l_scr[...] = jnp.zeros(l_scr.shape, F32)
        acc_scr[...] = jnp.zeros(acc_scr.shape, F32)

    lane = lax.broadcasted_iota(I32, (ls, D_VA), 1)

    def q_maps(h):
        q = q_ref[:, h * D_VA:(h + 1) * D_VA]
        return jnp.concatenate([jnp.where(lane < D_A, q, jnp.zeros_like(q)),
                                jnp.where(lane >= D_A, q, jnp.zeros_like(q))], axis=0)

    def update(h, qz, keys, vals):
        s = lax.dot_general(qz, keys, NT_DIMS, preferred_element_type=F32)
        m_prev = m_scr[h]
        m_new = jnp.maximum(m_prev, jnp.max(s, axis=1, keepdims=True))
        alpha = jnp.exp2(m_prev - m_new)
        p = jnp.exp2(s - m_new)
        l_scr[h] = alpha * l_scr[h] + jnp.sum(p, axis=1, keepdims=True)
        acc_scr[h] = alpha * acc_scr[h] + jnp.dot(p.astype(BF16), vals, preferred_element_type=F32)
        m_scr[h] = m_new

    for h in range(H_A):
        kc = kc_ref[pl.ds(h, npos, stride=H_A), :].astype(BF16)
        vc = vc_ref[pl.ds(h, npos, stride=H_A), :].astype(BF16)
        update(h, q_maps(h), kc, vc)

    @pl.when(pj == pl.num_programs(1) - 1)
    def _():
        lam = _diff_lambda(lamp_ref[...], lam_init)
        sw = sw_ref[...]
        for h in range(H_A):
            kn = kn_ref[:, h * D_VA:(h + 1) * D_VA]
            vn = vn_ref[pl.ds(h, ls, stride=H_A), :].astype(BF16)
            update(h, q_maps(h), kn, vn)
            both = acc_scr[h] / l_scr[h]
            o = both[0:ls] - lam * both[ls:2 * ls]
            o_ref[:, h * D_VA:(h + 1) * D_VA] = _subln(o, sw, lam_init).astype(o_ref.dtype)


def _attn_sample(q_bf, k_bf, v_new, cache_k, cache_v, lamp, subln_w, lam_init, bs, ls):
    past = cache_k.shape[1] // H_A
    pb = _pick(past, (1024, 512, 256, 128, 64))
    assert ls % (2 * SUBLANES) == 0
    return pl.pallas_call(
        functools.partial(_attn_sample_body, lam_init),
        grid=(bs, past // pb),
        in_specs=[
            pl.BlockSpec((ls, A_COLS), lambda b, j: (b, 0)),
            pl.BlockSpec((ls, A_COLS), lambda b, j: (b, 0)),
            pl.BlockSpec((ls * H_A, D_VA), lambda b, j: (b, 0)),
            pl.BlockSpec((None, pb * H_A, D_VA), lambda b, j: (b, j, 0)),
            pl.BlockSpec((None, pb * H_A, D_VA), lambda b, j: (b, j, 0)),
            pl.BlockSpec((4, D_A), lambda b, j: (0, 0)),
            pl.BlockSpec((1, D_VA), lambda b, j: (0, 0)),
        ],
        out_specs=pl.BlockSpec((ls, V_COLS), lambda b, j: (b, 0)),
        out_shape=jax.ShapeDtypeStruct((bs * ls, V_COLS), BF16),
        scratch_shapes=[
            pltpu.VMEM((H_A, 2 * ls, 1), F32),
            pltpu.VMEM((H_A, 2 * ls, 1), F32),
            pltpu.VMEM((H_A, 2 * ls, D_VA), F32),
        ],
        compiler_params=_cparams("parallel", "arbitrary"),
        name="attn_sample",
    )(q_bf, k_bf, v_new, cache_k, cache_v, lamp, subln_w)


def _gdn_body(chunk, rows, pq_ref, pk_ref, pv_ref, pz_ref, sm_ref, smt_ref, cw_ref, alr_ref, dtr_ref,
              alc_ref, dtc_ref, nw_ref, s0_ref, c0_ref, o_ref, s_ref, cn_ref, ext_scr, act_scr):
    i = pl.program_id(1)
    nblk = pl.num_programs(1)
    hk = H_B * D_K
    hv = H_B * D_V
    n_chunks = rows // chunk
    n_steps = int(math.log2(chunk))
    assert 2 ** n_steps == chunk

    @pl.when(i == 0)
    def _():
        s_ref[...] = s0_ref[...]
        ext_scr[0:SUBLANES, :] = jnp.zeros((SUBLANES, ext_scr.shape[1]), F32)
        ext_scr[pl.ds(SUBLANES - (CONV_W - 1), CONV_W - 1), :] = c0_ref[...]

    ext_scr[pl.ds(SUBLANES, rows), 0:hk] = pq_ref[...]
    ext_scr[pl.ds(SUBLANES, rows), hk:2 * hk] = pk_ref[...]
    ext_scr[pl.ds(SUBLANES, rows), 2 * hk:2 * hk + hv] = pv_ref[...]

    @pl.when(i == nblk - 1)
    def _():
        cn_ref[...] = ext_scr[pl.ds(SUBLANES + rows - (CONV_W - 1), CONV_W - 1), :]

    conv = jnp.zeros((rows, ext_scr.shape[1]), F32)
    for w in range(CONV_W):
        conv = conv + ext_scr[pl.ds(SUBLANES - (CONV_W - 1) + w, rows), :] * cw_ref[w:w + 1, :]
    act_scr[...] = _silu(conv)
    ext_scr[0:SUBLANES, :] = ext_scr[pl.ds(rows, SUBLANES), :]

    sm = sm_ref[...]
    beta_all = _sigmoid(sm[:, 0:H_B])
    g_all = -jnp.exp(alr_ref[...]) * _softplus(sm[:, H_B:2 * H_B] + dtr_ref[...])
    smt = smt_ref[...]
    g_all_t = -jnp.exp(alc_ref[...]) * _softplus(smt[H_B:2 * H_B, :] + dtc_ref[...])

    ri = lax.broadcasted_iota(I32, (chunk, chunk), 0)
    ci = lax.broadcasted_iota(I32, (chunk, chunk), 1)
    tri = ri >= ci
    strict = ri > ci
    eye = (ri == ci).astype(F32)
    ltri = tri.astype(F32)
    utri = (ri <= ci).astype(F32)
    nw = nw_ref[...]

    units = [(c, h) for c in range(n_chunks) for h in range(H_B)]
    gcs = [_dot_exact_mask(ltri, g_all[c * chunk:(c + 1) * chunk, :], True) for c in range(n_chunks)]
    grs = [_dot_exact_mask(utri, g_all_t[:, c * chunk:(c + 1) * chunk], False) for c in range(n_chunks)]
    pre = []
    for c, h in units:
        r0 = c * chunk
        q = act_scr[r0:r0 + chunk, h * D_K:(h + 1) * D_K]
        k = act_scr[r0:r0 + chunk, hk + h * D_K:hk + (h + 1) * D_K]
        v = act_scr[r0:r0 + chunk, 2 * hk + h * D_V:2 * hk + (h + 1) * D_V]
        q = q * lax.rsqrt(jnp.sum(q * q, axis=-1, keepdims=True) + EPS) * (D_K ** -0.5)
        k = k * lax.rsqrt(jnp.sum(k * k, axis=-1, keepdims=True) + EPS)
        beta = beta_all[r0:r0 + chunk, h:h + 1]
        gc = gcs[c][:, h:h + 1]
        gr = grs[c][h:h + 1, :]
        g_last = gc[chunk - 1:chunk, :]
        decay = jnp.where(tri, jnp.exp(jnp.where(tri, gc - gr, 0.0)), 0.0)
        kb = k * beta
        eg = jnp.exp(gc)
        kbf = k.astype(BF16)
        kk = lax.dot_general(kb.astype(BF16), kbf, NT_DIMS, preferred_element_type=F32)
        qk = lax.dot_general(q.astype(BF16), kbf, NT_DIMS, preferred_element_type=F32)
        pre.append(dict(
            nm=-jnp.where(strict, kk * decay, 0.0),
            rhs=jnp.concatenate([v * beta, kb * eg], axis=1),
            qk=jnp.where(tri, qk * decay, 0.0).astype(BF16),
            qg=(q * eg).astype(BF16),
            kg=(k * jnp.exp(g_last - gc)).astype(BF16),
            gl=jnp.exp(g_last)))
    invs = [eye + p["nm"] for p in pre]
    pws = [p["nm"] for p in pre]
    for _ in range(n_steps - 1):
        pws = [_bdot(pw, pw) for pw in pws]
        invs = [inv + _bdot(inv, pw) for inv, pw in zip(invs, pws)]
    resid = [eye - _dot3(eye - p["nm"], inv) for p, inv in zip(pre, invs)]
    invs = [inv + _bdot(inv, r) for inv, r in zip(invs, resid)]
    uws = [_bdot(inv, p["rhs"]) for p, inv in zip(pre, invs)]

    for (c, h), p, uw in zip(units, pre, uws):
        r0 = c * chunk
        u = uw[:, 0:D_V]
        wmat = uw[:, D_V:D_V + D_K]
        s = s_ref[h]
        sb = s.astype(BF16)
        v_new = u - jnp.dot(wmat.astype(BF16), sb, preferred_element_type=F32)
        v_new_b = v_new.astype(BF16)
        o = (jnp.dot(p["qg"], sb, preferred_element_type=F32)
             + jnp.dot(p["qk"], v_new_b, preferred_element_type=F32))
        s_ref[h] = s * p["gl"] + lax.dot_general(p["kg"], v_new_b, TN_DIMS, preferred_element_type=F32)
        z = pz_ref[r0:r0 + chunk, h * D_V:(h + 1) * D_V].astype(F32)
        o = o * lax.rsqrt(jnp.mean(o * o, axis=-1, keepdims=True) + EPS) * nw * _silu(z)
        o_ref[r0:r0 + chunk, h * D_V:(h + 1) * D_V] = o.astype(o_ref.dtype)


def _gdn(gq, zg, small, s0, c0, conv_w, a_log, dt_bias, norm_w, bn, ln, chunk):
    hk = H_B * D_K
    assert H_B * D_V == hk and Z_COLS == hk
    rows = ln
    for cand in (2 * chunk, chunk):
        if cand % LANES == 0 and ln % cand == 0:
            rows = cand
            break
    assert rows % chunk == 0 and rows % SUBLANES == 0
    nblk = ln // rows
    small_t = jnp.swapaxes(small[:, 0:2 * H_B].reshape(bn, ln, 2 * H_B), 1, 2)
    alr = a_log.reshape(1, H_B)
    dtr = dt_bias.reshape(1, H_B)
    alc = a_log.reshape(H_B, 1)
    dtc = dt_bias.reshape(H_B, 1)
    row_blk = lambda b, i: b * nblk + i
    return pl.pallas_call(
        functools.partial(_gdn_body, chunk, rows),
        grid=(bn, nblk),
        in_specs=[
            pl.BlockSpec((rows, hk), lambda b, i: (row_blk(b, i), 0)),
            pl.BlockSpec((rows, hk), lambda b, i: (row_blk(b, i), 1)),
            pl.BlockSpec((rows, hk), lambda b, i: (row_blk(b, i), 2)),
            pl.BlockSpec((rows, hk), lambda b, i: (row_blk(b, i), 0)),
            pl.BlockSpec((rows, LANES), lambda b, i: (row_blk(b, i), 0)),
            pl.BlockSpec((None, 2 * H_B, rows), lambda b, i: (b, 0, i)),
            pl.BlockSpec((CONV_W, G_COLS), lambda b, i: (0, 0)),
            pl.BlockSpec((1, H_B), lambda b, i: (0, 0)),
            pl.BlockSpec((1, H_B), lambda b, i: (0, 0)),
            pl.BlockSpec((H_B, 1), lambda b, i: (0, 0)),
            pl.BlockSpec((H_B, 1), lambda b, i: (0, 0)),
            pl.BlockSpec((1, D_V), lambda b, i: (0, 0)),
            pl.BlockSpec((None, H_B, D_K, D_V), lambda b, i: (b, 0, 0, 0)),
            pl.BlockSpec((None, CONV_W - 1, G_COLS), lambda b, i: (b, 0, 0)),
        ],
        out_specs=[
            pl.BlockSpec((rows, hk), lambda b, i: (row_blk(b, i), 0)),
            pl.BlockSpec((None, H_B, D_K, D_V), lambda b, i: (b, 0, 0, 0)),
            pl.BlockSpec((None, CONV_W - 1, G_COLS), lambda b, i: (b, 0, 0)),
        ],
        out_shape=[
            jax.ShapeDtypeStruct((bn * ln, hk), BF16),
            jax.ShapeDtypeStruct(s0.shape, F32),
            jax.ShapeDtypeStruct(c0.shape, F32),
        ],
        scratch_shapes=[
            pltpu.VMEM((rows + SUBLANES, G_COLS), F32),
            pltpu.VMEM((rows, G_COLS), F32),
        ],
        compiler_params=_cparams("parallel", "arbitrary"),
        name="gdn",
    )(gq, gq, gq, zg, small, small_t, conv_w, alr, dtr, alc, dtc, norm_w, s0, c0)


def _merge_body(x_ref, oa_ref, ob_ref, ga0_ref, ga1_ref, gb0_ref, gb1_ref, wa_ref, wb_ref, wo_ref,
                nf_ref, wr_ref, br_ref, x2_ref, t_ref, lg_ref):
    ya = jnp.dot(oa_ref[...], wa_ref[...], preferred_element_type=F32)
    yb = jnp.dot(ob_ref[...], wb_ref[...], preferred_element_type=F32)
    half = ga0_ref.shape[1]
    gate = lambda ref: _sigmoid(ref[...].astype(F32))
    m0 = gate(ga0_ref) * ya[:, :half] + gate(gb0_ref) * yb[:, :half]
    m1 = gate(ga1_ref) * ya[:, half:] + gate(gb1_ref) * yb[:, half:]
    merged = jnp.concatenate([m0, m1], axis=1).astype(BF16)
    x2 = x_ref[...] + jnp.dot(merged, wo_ref[...], preferred_element_type=F32)
    x2_ref[...] = x2
    ms = jnp.mean(x2 * x2, axis=-1, keepdims=True)
    t = x2 * lax.rsqrt(ms + EPS) * nf_ref[...]
    t_ref[...] = _pack_bf16_pair(t)
    lg_ref[...] = _dot3(t, wr_ref[...]) + br_ref[...]


def _merge(x, o_a, o_b, zg, wa, wb, wo, norm_ffn, w_router, b_router):
    T, D = x.shape
    half = D // 2
    assert Z_COLS % half == 0
    gcol = Z_COLS // half
    bm = _pick(T, (256, 128, 64, 32, 16, 8))
    const = dict(pipeline_mode=pl.Buffered(1))
    return pl.pallas_call(
        _merge_body,
        grid=(T // bm,),
        in_specs=[
            pl.BlockSpec((bm, D), lambda i: (i, 0)),
            pl.BlockSpec((bm, o_a.shape[1]), lambda i: (i, 0)),
            pl.BlockSpec((bm, o_b.shape[1]), lambda i: (i, 0)),
            pl.BlockSpec((bm, half), lambda i: (i, gcol)),
            pl.BlockSpec((bm, half), lambda i: (i, gcol + 1)),
            pl.BlockSpec((bm, half), lambda i: (i, gcol + 2)),
            pl.BlockSpec((bm, half), lambda i: (i, gcol + 3)),
            pl.BlockSpec(wa.shape, lambda i: (0, 0), **const),
            pl.BlockSpec(wb.shape, lambda i: (0, 0), **const),
            pl.BlockSpec(wo.shape, lambda i: (0, 0), **const),
            pl.BlockSpec((1, D), lambda i: (0, 0)),
            pl.BlockSpec((D, LANES), lambda i: (0, 0), **const),
            pl.BlockSpec((1, LANES), lambda i: (0, 0)),
        ],
        out_specs=[
            pl.BlockSpec((bm, D), lambda i: (i, 0)),
            pl.BlockSpec((bm, half), lambda i: (i, 0)),
            pl.BlockSpec((bm, LANES), lambda i: (i, 0)),
        ],
        out_shape=[
            jax.ShapeDtypeStruct((T, D), F32),
            jax.ShapeDtypeStruct((T, half), U32),
            jax.ShapeDtypeStruct((T, LANES), F32),
        ],
        compiler_params=_cparams("parallel"),
        name="merge",
    )(x, o_a, o_b, zg, zg, zg, zg, wa, wb, wo, norm_ffn, w_router, b_router)


def _route_body(lg_ref, mi_ref, mf_ref, cnt_ref):
    i = pl.program_id(0)

    @pl.when(i == 0)
    def _():
        cnt_ref[...] = jnp.zeros(cnt_ref.shape, F32)

    lg = lg_ref[...]
    bm = lg.shape[0]
    lane = lax.broadcasted_iota(I32, lg.shape, 1).astype(F32)
    big = jnp.float32(LANES)
    neg = -jnp.inf
    gl = jnp.where(lane < N_GROUPS, lg, neg)
    gmax = jnp.max(gl, axis=1, keepdims=True)
    gidx = jnp.min(jnp.where(gl == gmax, lane, big), axis=1, keepdims=True)
    gw = 1.0 / jnp.sum(jnp.exp(gl - gmax), axis=1, keepdims=True)
    e_lo = N_GROUPS + gidx * EXPERTS_PER_GROUP
    valid = (lane >= e_lo) & (lane < e_lo + EXPERTS_PER_GROUP)
    el = jnp.where(valid, lg, neg)
    v1 = jnp.max(el, axis=1, keepdims=True)
    i1 = jnp.min(jnp.where(el == v1, lane, big), axis=1, keepdims=True)
    el2 = jnp.where(lane == i1, neg, el)
    v2 = jnp.max(el2, axis=1, keepdims=True)
    i2 = jnp.min(jnp.where(el2 == v2, lane, big), axis=1, keepdims=True)
    e21 = jnp.exp(v2 - v1)
    w1 = gw / (1.0 + e21)
    w2 = gw * e21 / (1.0 + e21)
    oh1 = (lane == i1).astype(F32)
    oh2 = (lane == i2).astype(F32)
    oh = oh1 + oh2
    rr = lax.broadcasted_iota(I32, (bm, bm), 0)
    cc = lax.broadcasted_iota(I32, (bm, bm), 1)
    before = (cc < rr).astype(BF16)
    cum = jnp.dot(before, oh.astype(BF16), preferred_element_type=F32) + cnt_ref[...]
    rank1 = jnp.sum(cum * oh1, axis=1, keepdims=True)
    rank2 = jnp.sum(cum * oh2, axis=1, keepdims=True)
    cnt_ref[...] = cnt_ref[...] + jnp.sum(oh, axis=0, keepdims=True)
    mi = jnp.where(lane == 0, i1 - N_GROUPS, 0.0)
    mi = jnp.where(lane == 1, i2 - N_GROUPS, mi)
    mi = jnp.where(lane == 2, rank1, mi)
    mi = jnp.where(lane == 3, rank2, mi)
    mi_ref[...] = mi.T[0:SUBLANES].astype(I32)
    mf_ref[...] = jnp.where(lane == 0, w1, jnp.where(lane == 1, w2, 0.0))


def _route(logits):
    T = logits.shape[0]
    bm = _pick(T, (512, 256, 128, 64, 32, 16, 8))
    return pl.pallas_call(
        _route_body,
        grid=(T // bm,),
        in_specs=[pl.BlockSpec((bm, LANES), lambda i: (i, 0))],
        out_specs=[
            pl.BlockSpec((SUBLANES, bm), lambda i: (0, i)),
            pl.BlockSpec((bm, LANES), lambda i: (i, 0)),
            pl.BlockSpec((1, LANES), lambda i: (0, 0)),
        ],
        out_shape=[
            jax.ShapeDtypeStruct((SUBLANES, T), I32),
            jax.ShapeDtypeStruct((T, LANES), F32),
            jax.ShapeDtypeStruct((1, LANES), F32),
        ],
        compiler_params=_cparams("arbitrary"),
        name="route",
    )(logits)


def _dispatch_body(bm, blk, n_experts, nbp, ps_ref, pn_ref, nu_ref, dest_ref, tp_ref, ts_ref, xb_ref,
                   zero_scr, sem, zsem):
    i = pl.program_id(0)
    n_blocks = xb_ref.shape[0] // blk

    def pad_rows(act):
        def per_expert(e, carry):
            def per_row(r, c2):
                act(pltpu.make_async_copy(zero_scr.at[pl.ds(0, 1)], xb_ref.at[pl.ds(ps_ref[e] + r, 1)], zsem))
                return c2
            return lax.fori_loop(0, pn_ref[e], per_row, carry)
        lax.fori_loop(0, n_experts, per_expert, 0)

    def tail_blocks(act):
        def per_block(b, carry):
            act(pltpu.make_async_copy(zero_scr, xb_ref.at[pl.ds(pl.multiple_of(b * blk, blk), blk)], zsem))
            return carry
        lax.fori_loop(nu_ref[0], n_blocks, per_block, 0)

    @pl.when(i == 0)
    def _():
        zero_scr[...] = jnp.zeros(zero_scr.shape, zero_scr.dtype)
        pad_rows(lambda cp: cp.start())
        tail_blocks(lambda cp: cp.start())

    def send_rows(t_ref):
        def row_copy(r, k):
            d = dest_ref[0, k * bm + r]
            return pltpu.make_async_copy(t_ref.at[pl.ds(r, 1)], xb_ref.at[pl.ds(d, 1)], sem)

        def issue(r, carry):
            for k in range(TOP_K_INNER):
                row_copy(r, k).start()
            return carry

        def drain(r, carry):
            for k in range(TOP_K_INNER):
                row_copy(r, k).wait()
            return carry

        lax.fori_loop(0, bm, issue, 0, unroll=DMA_UNROLL)
        lax.fori_loop(0, bm, drain, 0, unroll=DMA_UNROLL)

    @pl.when(i < nbp)
    def _():
        send_rows(tp_ref)

    @pl.when(i >= nbp)
    def _():
        send_rows(ts_ref)

    @pl.when(i == 0)
    def _():
        pad_rows(lambda cp: cp.wait())
        tail_blocks(lambda cp: cp.wait())


def _dispatch(t_p, t_s, dest, pad_start, pad_len, n_used, n_rows, blk):
    (tp, D), ts = t_p.shape, t_s.shape[0]
    bm = _pick(math.gcd(tp, ts), (256, 128, 64, 32, 16, 8))
    nbp, nbs = tp // bm, ts // bm
    dest3 = _dest_blocks(dest, bm)
    grid_spec = pltpu.PrefetchScalarGridSpec(
        num_scalar_prefetch=3,
        grid=(nbp + nbs,),
        in_specs=[
            pl.BlockSpec((None, 1, TOP_K_INNER * bm), lambda i, ps, pn, nu: (i, 0, 0),
                         memory_space=pltpu.SMEM),
            pl.BlockSpec((bm, D), lambda i, ps, pn, nu: (jnp.minimum(i, nbp - 1), 0)),
            pl.BlockSpec((bm, D), lambda i, ps, pn, nu: (jnp.maximum(i - nbp, 0), 0)),
        ],
        out_specs=pl.BlockSpec(memory_space=pl.ANY),
        scratch_shapes=[
            pltpu.VMEM((blk, D), t_p.dtype),
            pltpu.SemaphoreType.DMA(()),
            pltpu.SemaphoreType.DMA(()),
        ],
    )
    return pl.pallas_call(
        functools.partial(_dispatch_body, bm, blk, pad_start.shape[0], nbp),
        grid_spec=grid_spec,
        out_shape=jax.ShapeDtypeStruct((n_rows, D), t_p.dtype),
        compiler_params=_cparams("arbitrary"),
        name="dispatch",
    )(pad_start, pad_len, n_used, dest3, t_p, t_s)


def _expert_body(be_ref, nu_ref, x_ref, wg_ref, wu_ref, wd_ref, y_ref, wg_scr, wu_scr, wd_scr):
    i = pl.program_id(0)

    @pl.when(i < nu_ref[0])
    def _():
        @pl.when(jnp.logical_or(i == 0, be_ref[i] != be_ref[jnp.maximum(i - 1, 0)]))
        def _():
            wg_scr[...] = wg_ref[...].astype(BF16)
            wu_scr[...] = wu_ref[...].astype(BF16)
            wd_scr[...] = wd_ref[...].astype(BF16)

        x = _unpack_bf16_pair(x_ref[...])
        g = jnp.dot(x, wg_scr[...], preferred_element_type=F32)
        u = jnp.dot(x, wu_scr[...], preferred_element_type=F32)
        hmid = (_silu(g) * u).astype(BF16)
        y_ref[...] = jnp.dot(hmid, wd_scr[...], preferred_element_type=F32)

    @pl.when(i >= nu_ref[0])
    def _():
        y_ref[...] = jnp.zeros(y_ref.shape, F32)


def _experts(xb, blk_exp, n_used, w_gate, w_up, w_down, blk):
    P = xb.shape[0]
    D, de = w_gate.shape[1:]
    assert xb.shape[1] * 2 == D
    n_blocks = P // blk
    last = lambda i, nu: jnp.minimum(i, nu[0] - 1)
    grid_spec = pltpu.PrefetchScalarGridSpec(
        num_scalar_prefetch=2,
        grid=(n_blocks,),
        in_specs=[
            pl.BlockSpec((blk, D // 2), lambda i, be, nu: (last(i, nu), 0)),
            pl.BlockSpec((None, D, de), lambda i, be, nu: (be[last(i, nu)], 0, 0)),
            pl.BlockSpec((None, D, de), lambda i, be, nu: (be[last(i, nu)], 0, 0)),
            pl.BlockSpec((None, de, D), lambda i, be, nu: (be[last(i, nu)], 0, 0)),
        ],
        out_specs=pl.BlockSpec((blk, D), lambda i, be, nu: (i, 0)),
        scratch_shapes=[
            pltpu.VMEM((D, de), BF16),
            pltpu.VMEM((D, de), BF16),
            pltpu.VMEM((de, D), BF16),
        ],
    )
    return pl.pallas_call(
        _expert_body,
        grid_spec=grid_spec,
        out_shape=jax.ShapeDtypeStruct((P, D), F32),
        compiler_params=_cparams("arbitrary"),
        name="experts",
    )(blk_exp, n_used, xb, w_gate, w_up, w_down)


def _combine_body(bm, final, dest_ref, x2_ref, mf_ref, nw_ref, yb_ref, o_ref, y0_scr, y1_scr, sem):
    bufs = (y0_scr, y1_scr)

    def row_copy(r, k):
        d = dest_ref[0, k * bm + r]
        return pltpu.make_async_copy(yb_ref.at[pl.ds(d, 1)], bufs[k].at[pl.ds(r, 1)], sem)

    def issue(r, carry):
        for k in range(TOP_K_INNER):
            row_copy(r, k).start()
        return carry

    lax.fori_loop(0, bm, issue, 0, unroll=DMA_UNROLL)

    def drain(r, carry):
        for k in range(TOP_K_INNER):
            row_copy(r, k).wait()
        return carry

    lax.fori_loop(0, bm, drain, 0, unroll=DMA_UNROLL)

    mf = mf_ref[...]
    x3 = x2_ref[...] + y0_scr[...] * mf[:, 0:1] + y1_scr[...] * mf[:, 1:2]
    if final:
        ms = jnp.mean(x3 * x3, axis=-1, keepdims=True)
        x3 = x3 * lax.rsqrt(ms + EPS) * nw_ref[...]
    o_ref[...] = x3


def _combine(x2, yb, dest, mf, norm_final, final):
    T, D = x2.shape
    bm = _pick(T, (256, 128, 64, 32, 16, 8))
    nblk = T // bm
    dest3 = _dest_blocks(dest, bm)
    return pl.pallas_call(
        functools.partial(_combine_body, bm, final),
        grid=(nblk,),
        in_specs=[
            pl.BlockSpec((None, 1, TOP_K_INNER * bm), lambda i: (i, 0, 0), memory_space=pltpu.SMEM),
            pl.BlockSpec((bm, D), lambda i: (i, 0)),
            pl.BlockSpec((bm, LANES), lambda i: (i, 0)),
            pl.BlockSpec((1, D), lambda i: (0, 0)),
            pl.BlockSpec(memory_space=pl.ANY),
        ],
        out_specs=pl.BlockSpec((bm, D), lambda i: (i, 0)),
        out_shape=jax.ShapeDtypeStruct((T, D), F32),
        scratch_shapes=[
            pltpu.VMEM((bm, D), F32),
            pltpu.VMEM((bm, D), F32),
            pltpu.SemaphoreType.DMA(()),
        ],
        compiler_params=_cparams("arbitrary"),
        name="combine",
    )(dest3, x2, mf, norm_final, yb)


def _rotary_tables(pos):
    half = ROT_DIM // 2
    inv = ROPE_THETA ** (-jnp.arange(0, ROT_DIM, 2, dtype=F32) / ROT_DIM)
    ang = pos.astype(F32)[:, None] * inv[None, :]
    cos = jnp.cos(ang)
    sin = jnp.sin(ang)
    n = pos.shape[0]
    ones = jnp.ones((n, D_A - ROT_DIM), F32)
    zeros = jnp.zeros((n, D_A - ROT_DIM), F32)
    zh = jnp.zeros((n, half), F32)
    cos64 = jnp.concatenate([cos, cos, ones], axis=1)
    sa64 = jnp.concatenate([zh, sin, zeros], axis=1)
    sb64 = jnp.concatenate([-sin, zh, zeros], axis=1)
    reps = LANES // D_A
    return jnp.tile(cos64, (1, reps)), jnp.tile(sa64, (1, reps)), jnp.tile(sb64, (1, reps))


def _dest_blocks(dest, bm):
    nblk = dest.shape[1] // bm
    return jnp.swapaxes(dest.reshape(TOP_K_INNER, nblk, bm), 0, 1).reshape(nblk, 1, TOP_K_INNER * bm)


def _moe_slots(mi, counts, n_experts, blk, n_assign):
    eid = mi[0:TOP_K_INNER]
    rank = mi[TOP_K_INNER:2 * TOP_K_INNER]
    cnt = counts[0, N_GROUPS:N_GROUPS + n_experts].astype(I32)
    pc = (cnt + blk - 1) // blk * blk
    pend = jnp.cumsum(pc)
    pstart = pend - pc
    experts = jnp.arange(n_experts, dtype=I32)[:, None, None]
    first = jnp.sum(jnp.where(eid[None] == experts, pstart[:, None, None], 0), axis=0)
    dest = (first + rank).astype(I32)
    n_blocks = -(-n_assign // blk) + n_experts
    blk_start = jnp.arange(n_blocks, dtype=I32) * blk
    blk_exp = jnp.minimum(jnp.sum((pend[None, :] <= blk_start[:, None]).astype(I32), axis=1), n_experts - 1)
    n_used = jnp.maximum(pend[-1] // blk, 1).astype(I32).reshape(1)
    return dest, blk_exp.astype(I32), n_used, n_blocks, (pstart + cnt).astype(I32), (pc - cnt).astype(I32)


def _layer(xp, xs, layer, dims, cache_k, cache_v, state_delta, state_conv, prm, norm_final, final):
    (norm_mix, w_in, lq1, lk1, lq2, lk2, subln_w, conv_w, a_log, dt_bias, gdn_norm_w,
     w_proj_a, w_proj_b, w_out, norm_ffn, w_gr, b_gr, w_er, b_er, w_eg, w_eu, w_ed) = prm
    bp, lp, bs, ls, past = dims
    D = xp.shape[1]
    tp = bp * lp
    n_qkv = 2 * A_COLS + V_COLS + G_COLS + Z_COLS
    n_small = 2 * H_B
    lam_init = 0.8 - 0.6 * math.exp(-0.3 * layer)

    w_a, w_b, w_small = _split_w_in(w_in, n_qkv, n_small)
    nm = norm_mix.reshape(1, D)
    pp = _inproj(xp, nm, _rotary_tables(jnp.tile(jnp.arange(lp), bp)), w_a, w_b, w_small)
    ps = _inproj(xs, nm, _rotary_tables(jnp.tile(past + jnp.arange(ls), bs)), w_a, w_b, w_small)

    lamp = jnp.stack([lq1, lk1, lq2, lk2]).astype(F32)
    sw = subln_w.reshape(1, D_VA)
    oa_p = _attn_prompt(pp.q_bf, pp.k_bf, pp.v_t, lamp, sw, lam_init, bp, lp)
    oa_s = _attn_sample(ps.q_bf, ps.k_bf, ps.v_new, cache_k.reshape(bs, past * H_A, 2 * D_A),
                        cache_v.reshape(bs, past * H_A, D_VA), lamp, sw, lam_init, bs, ls)

    nw = gdn_norm_w.reshape(1, D_V)
    ob_p, s_p, c_p = _gdn(pp.gq, pp.zg, pp.small, jnp.zeros((bp, H_B, D_K, D_V), F32),
                          jnp.zeros((bp, CONV_W - 1, G_COLS), F32), conv_w, a_log, dt_bias, nw,
                          bp, lp, _pick(lp, (GDN_CHUNK, CHUNK)))
    ob_s, s_s, c_s = _gdn(ps.gq, ps.zg, ps.small, state_delta, state_conv, conv_w, a_log, dt_bias, nw,
                          bs, ls, ls)

    n_experts = w_er.shape[1]
    w_router = jnp.pad(jnp.concatenate([w_gr, w_er], axis=1), ((0, 0), (0, LANES - N_GROUPS - n_experts)))
    b_router = jnp.pad(jnp.concatenate([b_gr, b_er]), (0, LANES - N_GROUPS - n_experts)).reshape(1, LANES)
    mw = (w_proj_a.astype(BF16), w_proj_b.astype(BF16), w_out.astype(BF16), norm_ffn.reshape(1, D),
          w_router, b_router)
    x2_p, t_p, lg_p = _merge(xp, oa_p, ob_p, pp.zg, *mw)
    x2_s, t_s, lg_s = _merge(xs, oa_s, ob_s, ps.zg, *mw)

    blk = MOE_ROWS
    mi, mf, counts = _route(jnp.concatenate([lg_p, lg_s], axis=0))
    n_tok = tp + bs * ls
    dest, blk_exp, n_used, n_blocks, pad_start, pad_len = _moe_slots(
        mi, counts, n_experts, blk, n_tok * TOP_K_INNER)
    xb = _dispatch(t_p, t_s, dest, pad_start, pad_len, n_used, n_blocks * blk, blk)
    yb = _experts(xb, blk_exp, n_used, w_eg, w_eu, w_ed, blk)
    nf = norm_final.reshape(1, D)
    y_p = _combine(x2_p, yb, dest[:, :tp], mf[:tp], nf, final)
    y_s = _combine(x2_s, yb, dest[:, tp:], mf[tp:], nf, final)
    return y_p, y_s, pp.k_new, pp.v_new, s_p, c_p, ps.k_new, ps.v_new, s_s, c_s


def kernel(x_prompt, x_sample, cache_k, cache_v, state_delta, state_conv, norm_mix, w_in, lambda_q1, lambda_k1, lambda_q2, lambda_k2, subln_w, conv_w, a_log, dt_bias, gdn_norm_w, w_proj_a, w_proj_b, w_out, norm_ffn, w_group_router, b_group_router, w_expert_router, b_expert_router, w_exp_gate, w_exp_up, w_exp_down, norm_final):
    bp, lp, D = x_prompt.shape
    bs, ls, _ = x_sample.shape
    depth = cache_k.shape[0]
    past = cache_k.shape[2]
    dims = (bp, lp, bs, ls, past)
    xp = x_prompt.reshape(bp * lp, D)
    xs = x_sample.reshape(bs * ls, D)
    kp, vp, sp, cp, ksm, vsm, ssm, csm = [], [], [], [], [], [], [], []
    for l in range(depth):
        prm = (norm_mix[l], w_in[l], lambda_q1[l], lambda_k1[l], lambda_q2[l], lambda_k2[l], subln_w[l],
               conv_w[l], a_log[l], dt_bias[l], gdn_norm_w[l], w_proj_a[l], w_proj_b[l], w_out[l],
               norm_ffn[l], w_group_router[l], b_group_router[l], w_expert_router[l], b_expert_router[l],
               w_exp_gate[l], w_exp_up[l], w_exp_down[l])
        xp, xs, k_p, v_p, s_p, c_p, k_s, v_s, s_s, c_s = _layer(
            xp, xs, l, dims, cache_k[l], cache_v[l], state_delta[l], state_conv[l], prm,
            norm_final, l == depth - 1)
        kp.append(k_p.reshape(bp, lp, H_A, 2 * D_A))
        vp.append(v_p.reshape(bp, lp, H_A, D_VA))
        ksm.append(k_s.reshape(bs, ls, H_A, 2 * D_A))
        vsm.append(v_s.reshape(bs, ls, H_A, D_VA))
        sp.append(s_p)
        cp.append(c_p)
        ssm.append(s_s)
        csm.append(c_s)
    return (xp.reshape(bp, lp, D), xs.reshape(bs, ls, D), jnp.stack(kp), jnp.stack(vp), jnp.stack(sp),
            jnp.stack(cp), jnp.stack(ksm), jnp.stack(vsm), jnp.stack(ssm), jnp.stack(csm))
```

```python
import collections
import functools
import math

import jax
import jax.numpy as jnp
from jax import lax
from jax.experimental import pallas as pl
from jax.experimental.pallas import tpu as pltpu

F32 = jnp.float32
BF16 = jnp.bfloat16
I32 = jnp.int32
U32 = jnp.uint32

CHUNK = 64
GDN_CHUNK = 128
H_A = 8
D_A = 64
D_VA = 2 * D_A
ROT_DIM = D_A // 4
ROPE_THETA = 500000.0
H_B = 8
D_K = 128
D_V = 128
CONV_W = 4
N_GROUPS = 4
EXPERTS_PER_GROUP = 8
TOP_K_INNER = 2
EPS = 1e-6

LANES = 128
SUBLANES = 8
V7X_VMEM_BYTES = 64 * 1024 * 1024
VMEM_LIMIT_BYTES = V7X_VMEM_BYTES - 8 * 1024 * 1024

Q_SCALE = (D_A ** -0.5) * math.log2(math.e)
ONES_ROWS = 16
INPROJ_BN = 512
MOE_ROWS = 256
DMA_UNROLL = 8

NT_DIMS = (((1,), (1,)), ((), ()))
TN_DIMS = (((0,), (0,)), ((), ()))

A_COLS = H_A * 2 * D_A
V_COLS = H_A * D_VA
G_COLS = 2 * H_B * D_K + H_B * D_V
Z_COLS = H_B * D_V


def _pick(n, cands):
    for c in cands:
        if n % c == 0:
            return c
    raise ValueError(f"no block size in {cands} divides {n}")


def _cparams(*sem):
    return pltpu.CompilerParams(dimension_semantics=sem, vmem_limit_bytes=VMEM_LIMIT_BYTES)


def _sigmoid(x):
    return 1.0 / (1.0 + jnp.exp(-x))


def _silu(x):
    return x * _sigmoid(x)


def _softplus(x):
    return jnp.maximum(x, 0.0) + jnp.log1p(jnp.exp(-jnp.abs(x)))


def _pack_bf16_pair(x):
    bits = lax.bitcast_convert_type(x.astype(BF16).astype(F32), U32)
    half = x.shape[1] // 2
    return (bits[:, :half] >> 16) | (bits[:, half:] & jnp.uint32(0xFFFF0000))


def _unpack_bf16_pair(w):
    lo = lax.bitcast_convert_type(w << 16, F32)
    hi = lax.bitcast_convert_type(w & jnp.uint32(0xFFFF0000), F32)
    return jnp.concatenate([lo, hi], axis=1).astype(BF16)


def _bdot(a, b):
    return jnp.dot(a.astype(BF16), b.astype(BF16), preferred_element_type=F32)


def _split2(a):
    hi = a.astype(BF16)
    lo = (a - hi.astype(F32)).astype(BF16)
    return hi, lo


def _dot3(a, b):
    ah, al = _split2(a)
    bh, bl = _split2(b)
    d = functools.partial(jnp.dot, preferred_element_type=F32)
    return d(ah, bh) + d(ah, bl) + d(al, bh)


def _dot_exact_mask(mask, x, mask_is_lhs):
    hi = x.astype(BF16)
    r1 = x - hi.astype(F32)
    mid = r1.astype(BF16)
    lo = (r1 - mid.astype(F32)).astype(BF16)
    mb = mask.astype(BF16)
    d = functools.partial(jnp.dot, preferred_element_type=F32)
    if mask_is_lhs:
        return d(mb, hi) + d(mb, mid) + d(mb, lo)
    return d(hi, mb) + d(mid, mb) + d(lo, mb)


InProj = collections.namedtuple("InProj", "small k_new v_new q_bf k_bf v_t gq zg")


def _inproj_body(o_k, o_v, o_g, o_z, nb_wa, x_ref, nw_ref, cos_ref, sa_ref, sb_ref, wa_ref, wb_ref, ws_ref,
                 os_ref, k_ref, v_ref, qb_ref, kb_ref, vt_ref, gq_ref, zg_ref, h_scr):
    j = pl.program_id(1)
    bm = h_scr.shape[0]
    half = wa_ref.shape[0] // 2
    lpb = half // LANES

    @pl.when(j == 0)
    def _():
        x = x_ref[...]
        ms = jnp.mean(x * x, axis=-1, keepdims=True)
        h = (x * lax.rsqrt(ms + EPS) * nw_ref[...]).astype(BF16)
        h_scr[...] = h
        os_ref[...] = lax.dot_general(h, ws_ref[...], NT_DIMS, preferred_element_type=F32)

    def halves(w_ref):
        return [lax.dot_general(h_scr[...], w_ref[c * half:(c + 1) * half, :], NT_DIMS,
                                preferred_element_type=F32) for c in range(2)]

    def lane_blocks(w_ref):
        for c2, acc in enumerate(halves(w_ref)):
            for c in range(lpb):
                yield c2 * lpb + c, acc[:, c * LANES:(c + 1) * LANES]

    def rotated(blk):
        return (blk * cos_ref[...] + pltpu.roll(blk, ROT_DIM // 2, 1) * sa_ref[...]
                + pltpu.roll(blk, LANES - ROT_DIM // 2, 1) * sb_ref[...])

    @pl.when(j < o_k)
    def _():
        for lb, blk in lane_blocks(wa_ref):
            qb_ref[:, lb * LANES:(lb + 1) * LANES] = (rotated(blk) * Q_SCALE).astype(BF16)

    hpb = 2 * lpb
    for jj in range(o_v - o_k):
        @pl.when(j == o_k + jj)
        def _(jj=jj):
            for lb, blk in lane_blocks(wa_ref):
                rot = rotated(blk)
                k_ref[pl.ds(jj * hpb + lb, bm, stride=H_A), :] = rot
                kb_ref[:, lb * LANES:(lb + 1) * LANES] = rot.astype(BF16)

    for jj in range(o_g - o_v):
        @pl.when(j == o_v + jj)
        def _(jj=jj):
            for lb, blk in lane_blocks(wa_ref):
                v_ref[pl.ds(jj * hpb + lb, bm, stride=H_A), :] = blk
                vt_ref[lb, 0:D_VA, :] = blk.T.astype(BF16)
                vt_ref[lb, D_VA:D_VA + ONES_ROWS, :] = jnp.ones((ONES_ROWS, bm), BF16)

    @pl.when(jnp.logical_and(j >= o_g, j < o_z))
    def _():
        for c, acc in enumerate(halves(wa_ref)):
            gq_ref[:, c * half:(c + 1) * half] = acc

    @pl.when(jnp.logical_and(j >= o_z, j < nb_wa))
    def _():
        for c, acc in enumerate(halves(wa_ref)):
            zg_ref[:, c * half:(c + 1) * half] = acc.astype(BF16)

    @pl.when(j >= nb_wa)
    def _():
        for c, acc in enumerate(halves(wb_ref)):
            zg_ref[:, c * half:(c + 1) * half] = acc.astype(BF16)


def _inproj(x, norm_w, tables, w_a, w_b, w_small):
    T, D = x.shape
    bn = INPROJ_BN
    zg_cols = Z_COLS + 2 * D
    assert A_COLS % bn == 0 and V_COLS % bn == 0 and G_COLS % bn == 0 and Z_COLS % bn == 0
    assert (bn // 2) % D_VA == 0 and D_VA == LANES and (2 * D) % bn == 0
    assert w_a.shape[0] == 2 * A_COLS + V_COLS + G_COLS + Z_COLS and w_b.shape[0] == 2 * D
    nb_a, nb_v, nb_g, nb_z = A_COLS // bn, V_COLS // bn, G_COLS // bn, zg_cols // bn
    o_k, o_v = nb_a, 2 * nb_a
    o_g = o_v + nb_v
    o_z = o_g + nb_g
    nb_wa, nb_wb = w_a.shape[0] // bn, w_b.shape[0] // bn
    bm = _pick(T, (1024, 512, 256, 128))
    hpb = bn // D_VA

    def col(lo, n):
        return lambda i, j: (i, jnp.clip(j - lo, 0, n - 1))

    cos_t, sa_t, sb_t = tables
    outs = pl.pallas_call(
        functools.partial(_inproj_body, o_k, o_v, o_g, o_z, nb_wa),
        grid=(T // bm, nb_wa + nb_wb),
        in_specs=[
            pl.BlockSpec((bm, D), lambda i, j: (i, 0), pipeline_mode=pl.Buffered(1)),
            pl.BlockSpec((1, D), lambda i, j: (0, 0)),
            pl.BlockSpec((bm, LANES), lambda i, j: (i, 0)),
            pl.BlockSpec((bm, LANES), lambda i, j: (i, 0)),
            pl.BlockSpec((bm, LANES), lambda i, j: (i, 0)),
            pl.BlockSpec((bn, D), lambda i, j: (jnp.minimum(j, nb_wa - 1), 0)),
            pl.BlockSpec((bn, D), lambda i, j: (jnp.maximum(j - nb_wa, 0), 0)),
            pl.BlockSpec((LANES, D), lambda i, j: (0, 0)),
        ],
        out_specs=[
            pl.BlockSpec((bm, LANES), lambda i, j: (i, 0)),
            pl.BlockSpec((bm * H_A, D_VA), lambda i, j: (i, 0)),
            pl.BlockSpec((bm * H_A, D_VA), lambda i, j: (i, 0)),
            pl.BlockSpec((bm, bn), col(0, nb_a)),
            pl.BlockSpec((bm, bn), col(o_k, nb_a)),
            pl.BlockSpec((hpb, D_VA + ONES_ROWS, bm), lambda i, j: (jnp.clip(j - o_v, 0, nb_v - 1), 0, i)),
            pl.BlockSpec((bm, bn), col(o_g, nb_g)),
            pl.BlockSpec((bm, bn), col(o_z, nb_z)),
        ],
        out_shape=[
            jax.ShapeDtypeStruct((T, LANES), F32),
            jax.ShapeDtypeStruct((T * H_A, 2 * D_A), F32),
            jax.ShapeDtypeStruct((T * H_A, D_VA), F32),
            jax.ShapeDtypeStruct((T, A_COLS), BF16),
            jax.ShapeDtypeStruct((T, A_COLS), BF16),
            jax.ShapeDtypeStruct((H_A, D_VA + ONES_ROWS, T), BF16),
            jax.ShapeDtypeStruct((T, G_COLS), F32),
            jax.ShapeDtypeStruct((T, zg_cols), BF16),
        ],
        scratch_shapes=[pltpu.VMEM((bm, D), BF16)],
        compiler_params=_cparams("parallel", "arbitrary"),
        name="inproj",
    )(x, norm_w, cos_t, sa_t, sb_t, w_a, w_b, w_small)
    return InProj(*outs)


def _diff_lambda(lamp, lam_init):
    s1 = jnp.sum(lamp[0:1] * lamp[1:2], axis=1, keepdims=True)
    s2 = jnp.sum(lamp[2:3] * lamp[3:4], axis=1, keepdims=True)
    return jnp.exp(s1) - jnp.exp(s2) + lam_init


def _subln(o, w, lam_init):
    ms = jnp.mean(o * o, axis=-1, keepdims=True)
    return o * lax.rsqrt(ms + EPS) * w * (1.0 - lam_init)


def _attn_prompt_body(lam_init, bq, q_ref, k_ref, vt_ref, lamp_ref, swc_ref, o_ref,
                      m_scr, acc_scr, s_scr):
    qi = pl.program_id(2)
    q = q_ref[...]
    lane = lax.broadcasted_iota(I32, q.shape, 1)
    qz = [jnp.where(lane < D_A, q, jnp.zeros_like(q)), jnp.where(lane >= D_A, q, jnp.zeros_like(q))]
    m_scr[...] = jnp.full(m_scr.shape, -jnp.inf, F32)
    acc_scr[...] = jnp.zeros(acc_scr.shape, F32)
    qry_pos = lax.broadcasted_iota(I32, (1, bq), 1)

    def scores(j, slot):
        kb = k_ref[pl.ds(pl.multiple_of(j * bq, bq), bq), :]
        for m in range(2):
            s_scr[slot, m] = lax.dot_general(kb, qz[m], NT_DIMS, preferred_element_type=F32)

    def accumulate(j, slot, masked):
        vb = vt_ref[:, pl.ds(pl.multiple_of(j * bq, bq), bq)]
        for m in range(2):
            if masked:
                s = jnp.concatenate(
                    [jnp.where(qry_pos >= c * CHUNK, s_scr[slot, m, c * CHUNK:(c + 1) * CHUNK, :], -jnp.inf)
                     for c in range(bq // CHUNK)], axis=0)
            else:
                s = s_scr[slot, m]
            m_prev = m_scr[m]
            m_new = jnp.maximum(m_prev, jnp.max(s, axis=0, keepdims=True))
            alpha = jnp.exp2(m_prev - m_new)
            p = jnp.exp2(s - m_new).astype(BF16)
            acc_scr[m] = alpha * acc_scr[m] + jnp.dot(vb, p, preferred_element_type=F32)
            m_scr[m] = m_new

    scores(0, 0)

    def two_full_blocks(t, carry):
        j = 2 * t
        scores(j + 1, 1)
        accumulate(j, 0, False)
        scores(j + 2, 0)
        accumulate(j + 1, 1, False)
        return carry

    lax.fori_loop(0, qi // 2, two_full_blocks, 0)
    j0 = 2 * (qi // 2)

    @pl.when(j0 == qi)
    def _():
        accumulate(qi, 0, True)

    @pl.when(j0 != qi)
    def _():
        scores(qi, 1)
        accumulate(j0, 0, False)
        accumulate(qi, 1, True)

    lam = _diff_lambda(lamp_ref[...], lam_init)
    a0 = acc_scr[0]
    a1 = acc_scr[1]
    o_t = a0[0:D_VA] / a0[D_VA:D_VA + 1] - lam * (a1[0:D_VA] / a1[D_VA:D_VA + 1])
    ms = jnp.mean(o_t * o_t, axis=0, keepdims=True)
    o_t = o_t * lax.rsqrt(ms + EPS) * swc_ref[...] * (1.0 - lam_init)
    o_ref[...] = o_t.T.astype(o_ref.dtype)


def _attn_prompt(q_bf, k_bf, v_t, lamp, subln_w, lam_init, bp, lp):
    bq = _pick(lp, (512, 256, 128))
    nq = lp // bq
    vrows = v_t.shape[1]
    return pl.pallas_call(
        functools.partial(_attn_prompt_body, lam_init, bq),
        grid=(bp, H_A, nq),
        in_specs=[
            pl.BlockSpec((bq, LANES), lambda b, h, i: (b * nq + i, h)),
            pl.BlockSpec((lp, LANES), lambda b, h, i: (b, h)),
            pl.BlockSpec((None, vrows, lp), lambda b, h, i: (h, 0, b)),
            pl.BlockSpec((4, D_A), lambda b, h, i: (0, 0)),
            pl.BlockSpec((D_VA, 1), lambda b, h, i: (0, 0)),
        ],
        out_specs=pl.BlockSpec((bq, LANES), lambda b, h, i: (b * nq + i, h)),
        out_shape=jax.ShapeDtypeStruct((bp * lp, V_COLS), BF16),
        scratch_shapes=[
            pltpu.VMEM((2, 1, bq), F32),
            pltpu.VMEM((2, vrows, bq), F32),
            pltpu.VMEM((2, 2, bq, bq), F32),
        ],
        compiler_params=_cparams("parallel", "parallel", "arbitrary"),
        name="attn_prompt",
    )(q_bf, k_bf, v_t, lamp, subln_w.reshape(D_VA, 1))


def _attn_sample_body(lam_init, q_ref, kn_ref, vn_ref, kc_ref, vc_ref, lamp_ref, sw_ref, o_ref,
                      m_scr, l_scr, acc_scr):
    pj = pl.program_id(1)
    npos = kc_ref.shape[0] // H_A
    ls = q_ref.shape[0]

    @pl.when(pj == 0)
    def _():
        m_scr[...] = jnp.full(m_scr.shape, -jnp.inf, F32)
        l_scr[...] = jnp.zeros(l_scr.shape, F32)
        acc_scr[...] = jnp.zeros(acc_scr.shape, F32)

    lane = lax.broadcasted_iota(I32, (ls, D_VA), 1)

    def q_maps(h):
        q = q_ref[:, h * D_VA:(h + 1) * D_VA]
        return jnp.concatenate([jnp.where(lane < D_A, q, jnp.zeros_like(q)),
                                jnp.where(lane >= D_A, q, jnp.zeros_like(q))], axis=0)

    def update(h, qz, keys, vals):
        s = lax.dot_general(qz, keys, NT_DIMS, preferred_element_type=F32)
        m_prev = m_scr[h]
        m_new = jnp.maximum(m_prev, jnp.max(s, axis=1, keepdims=True))
        alpha = jnp.exp2(m_prev - m_new)
        p = jnp.exp2(s - m_new)
        l_scr[h] = alpha * l_scr[h] + jnp.sum(p, axis=1, keepdims=True)
        acc_scr[h] = alpha * acc_scr[h] + jnp.dot(p.astype(BF16), vals, preferred_element_type=F32)
        m_scr[h] = m_new

    for h in range(H_A):
        kc = kc_ref[pl.ds(h, npos, stride=H_A), :].astype(BF16)
        vc = vc_ref[pl.ds(h, npos, stride=H_A), :].astype(BF16)
        update(h, q_maps(h), kc, vc)

    @pl.when(pj == pl.num_programs(1) - 1)
    def _():
        lam = _diff_lambda(lamp_ref[...], lam_init)
        sw = sw_ref[...]
        for h in range(H_A):
            kn = kn_ref[:, h * D_VA:(h + 1) * D_VA]
            vn = vn_ref[pl.ds(h, ls, stride=H_A), :].astype(BF16)
            update(h, q_maps(h), kn, vn)
            both = acc_scr[h] / l_scr[h]
            o = both[0:ls] - lam * both[ls:2 * ls]
            o_ref[:, h * D_VA:(h + 1) * D_VA] = _subln(o, sw, lam_init).astype(o_ref.dtype)


def _attn_sample(q_bf, k_bf, v_new, cache_k, cache_v, lamp, subln_w, lam_init, bs, ls):
    past = cache_k.shape[1] // H_A
    pb = _pick(past, (1024, 512, 256, 128, 64))
    assert ls % (2 * SUBLANES) == 0
    return pl.pallas_call(
        functools.partial(_attn_sample_body, lam_init),
        grid=(bs, past // pb),
        in_specs=[
            pl.BlockSpec((ls, A_COLS), lambda b, j: (b, 0)),
            pl.BlockSpec((ls, A_COLS), lambda b, j: (b, 0)),
            pl.BlockSpec((ls * H_A, D_VA), lambda b, j: (b, 0)),
            pl.BlockSpec((None, pb * H_A, D_VA), lambda b, j: (b, j, 0)),
            pl.BlockSpec((None, pb * H_A, D_VA), lambda b, j: (b, j, 0)),
            pl.BlockSpec((4, D_A), lambda b, j: (0, 0)),
            pl.BlockSpec((1, D_VA), lambda b, j: (0, 0)),
        ],
        out_specs=pl.BlockSpec((ls, V_COLS), lambda b, j: (b, 0)),
        out_shape=jax.ShapeDtypeStruct((bs * ls, V_COLS), BF16),
        scratch_shapes=[
            pltpu.VMEM((H_A, 2 * ls, 1), F32),
            pltpu.VMEM((H_A, 2 * ls, 1), F32),
            pltpu.VMEM((H_A, 2 * ls, D_VA), F32),
        ],
        compiler_params=_cparams("parallel", "arbitrary"),
        name="attn_sample",
    )(q_bf, k_bf, v_new, cache_k, cache_v, lamp, subln_w)


def _gdn_body(chunk, rows, pq_ref, pk_ref, pv_ref, pz_ref, sm_ref, smt_ref, cw_ref, alr_ref, dtr_ref,
              alc_ref, dtc_ref, nw_ref, s0_ref, c0_ref, o_ref, s_ref, cn_ref, ext_scr, act_scr):
    i = pl.program_id(1)
    nblk = pl.num_programs(1)
    hk = H_B * D_K
    hv = H_B * D_V
    n_chunks = rows // chunk
    n_steps = int(math.log2(chunk))
    assert 2 ** n_steps == chunk

    @pl.when(i == 0)
    def _():
        s_ref[...] = s0_ref[...]
        ext_scr[0:SUBLANES, :] = jnp.zeros((SUBLANES, ext_scr.shape[1]), F32)
        ext_scr[pl.ds(SUBLANES - (CONV_W - 1), CONV_W - 1), :] = c0_ref[...]

    ext_scr[pl.ds(SUBLANES, rows), 0:hk] = pq_ref[...]
    ext_scr[pl.ds(SUBLANES, rows), hk:2 * hk] = pk_ref[...]
    ext_scr[pl.ds(SUBLANES, rows), 2 * hk:2 * hk + hv] = pv_ref[...]

    @pl.when(i == nblk - 1)
    def _():
        cn_ref[...] = ext_scr[pl.ds(SUBLANES + rows - (CONV_W - 1), CONV_W - 1), :]

    conv = jnp.zeros((rows, ext_scr.shape[1]), F32)
    for w in range(CONV_W):
        conv = conv + ext_scr[pl.ds(SUBLANES - (CONV_W - 1) + w, rows), :] * cw_ref[w:w + 1, :]
    act_scr[...] = _silu(conv)
    ext_scr[0:SUBLANES, :] = ext_scr[pl.ds(rows, SUBLANES), :]

    sm = sm_ref[...]
    beta_all = _sigmoid(sm[:, 0:H_B])
    g_all = -jnp.exp(alr_ref[...]) * _softplus(sm[:, H_B:2 * H_B] + dtr_ref[...])
    smt = smt_ref[...]
    g_all_t = -jnp.exp(alc_ref[...]) * _softplus(smt[H_B:2 * H_B, :] + dtc_ref[...])

    ri = lax.broadcasted_iota(I32, (chunk, chunk), 0)
    ci = lax.broadcasted_iota(I32, (chunk, chunk), 1)
    tri = ri >= ci
    strict = ri > ci
    eye = (ri == ci).astype(F32)
    ltri = tri.astype(F32)
    utri = (ri <= ci).astype(F32)
    nw = nw_ref[...]

    units = [(c, h) for c in range(n_chunks) for h in range(H_B)]
    gcs = [_dot_exact_mask(ltri, g_all[c * chunk:(c + 1) * chunk, :], True) for c in range(n_chunks)]
    grs = [_dot_exact_mask(utri, g_all_t[:, c * chunk:(c + 1) * chunk], False) for c in range(n_chunks)]
    pre = []
    for c, h in units:
        r0 = c * chunk
        q = act_scr[r0:r0 + chunk, h * D_K:(h + 1) * D_K]
        k = act_scr[r0:r0 + chunk, hk + h * D_K:hk + (h + 1) * D_K]
        v = act_scr[r0:r0 + chunk, 2 * hk + h * D_V:2 * hk + (h + 1) * D_V]
        q = q * lax.rsqrt(jnp.sum(q * q, axis=-1, keepdims=True) + EPS) * (D_K ** -0.5)
        k = k * lax.rsqrt(jnp.sum(k * k, axis=-1, keepdims=True) + EPS)
        beta = beta_all[r0:r0 + chunk, h:h + 1]
        gc = gcs[c][:, h:h + 1]
        gr = grs[c][h:h + 1, :]
        g_last = gc[chunk - 1:chunk, :]
        decay = jnp.where(tri, jnp.exp(jnp.where(tri, gc - gr, 0.0)), 0.0)
        kb = k * beta
        eg = jnp.exp(gc)
        kbf = k.astype(BF16)
        kk = lax.dot_general(kb.astype(BF16), kbf, NT_DIMS, preferred_element_type=F32)
        qk = lax.dot_general(q.astype(BF16), kbf, NT_DIMS, preferred_element_type=F32)
        pre.append(dict(
            nm=-jnp.where(strict, kk * decay, 0.0),
            rhs=jnp.concatenate([v * beta, kb * eg], axis=1),
            qk=jnp.where(tri, qk * decay, 0.0).astype(BF16),
            qg=(q * eg).astype(BF16),
            kg=(k * jnp.exp(g_last - gc)).astype(BF16),
            gl=jnp.exp(g_last)))
    invs = [eye + p["nm"] for p in pre]
    pws = [p["nm"] for p in pre]
    for _ in range(n_steps - 1):
        pws = [_bdot(pw, pw) for pw in pws]
        invs = [inv + _bdot(inv, pw) for inv, pw in zip(invs, pws)]
    resid = [eye - _dot3(eye - p["nm"], inv) for p, inv in zip(pre, invs)]
    invs = [inv + _bdot(inv, r) for inv, r in zip(invs, resid)]
    uws = [_bdot(inv, p["rhs"]) for p, inv in zip(pre, invs)]

    for (c, h), p, uw in zip(units, pre, uws):
        r0 = c * chunk
        u = uw[:, 0:D_V]
        wmat = uw[:, D_V:D_V + D_K]
        s = s_ref[h]
        sb = s.astype(BF16)
        v_new = u - jnp.dot(wmat.astype(BF16), sb, preferred_element_type=F32)
        v_new_b = v_new.astype(BF16)
        o = (jnp.dot(p["qg"], sb, preferred_element_type=F32)
             + jnp.dot(p["qk"], v_new_b, preferred_element_type=F32))
        s_ref[h] = s * p["gl"] + lax.dot_general(p["kg"], v_new_b, TN_DIMS, preferred_element_type=F32)
        z = pz_ref[r0:r0 + chunk, h * D_V:(h + 1) * D_V].astype(F32)
        o = o * lax.rsqrt(jnp.mean(o * o, axis=-1, keepdims=True) + EPS) * nw * _silu(z)
        o_ref[r0:r0 + chunk, h * D_V:(h + 1) * D_V] = o.astype(o_ref.dtype)


def _gdn(gq, zg, small, s0, c0, conv_w, a_log, dt_bias, norm_w, bn, ln, chunk):
    hk = H_B * D_K
    assert H_B * D_V == hk and Z_COLS == hk
    rows = ln
    for cand in (2 * chunk, chunk):
        if cand % LANES == 0 and ln % cand == 0:
            rows = cand
            break
    assert rows % chunk == 0 and rows % SUBLANES == 0
    nblk = ln // rows
    small_t = jnp.swapaxes(small[:, 0:2 * H_B].reshape(bn, ln, 2 * H_B), 1, 2)
    alr = a_log.reshape(1, H_B)
    dtr = dt_bias.reshape(1, H_B)
    alc = a_log.reshape(H_B, 1)
    dtc = dt_bias.reshape(H_B, 1)
    row_blk = lambda b, i: b * nblk + i
    return pl.pallas_call(
        functools.partial(_gdn_body, chunk, rows),
        grid=(bn, nblk),
        in_specs=[
            pl.BlockSpec((rows, hk), lambda b, i: (row_blk(b, i), 0)),
            pl.BlockSpec((rows, hk), lambda b, i: (row_blk(b, i), 1)),
            pl.BlockSpec((rows, hk), lambda b, i: (row_blk(b, i), 2)),
            pl.BlockSpec((rows, hk), lambda b, i: (row_blk(b, i), 0)),
            pl.BlockSpec((rows, LANES), lambda b, i: (row_blk(b, i), 0)),
            pl.BlockSpec((None, 2 * H_B, rows), lambda b, i: (b, 0, i)),
            pl.BlockSpec((CONV_W, G_COLS), lambda b, i: (0, 0)),
            pl.BlockSpec((1, H_B), lambda b, i: (0, 0)),
            pl.BlockSpec((1, H_B), lambda b, i: (0, 0)),
            pl.BlockSpec((H_B, 1), lambda b, i: (0, 0)),
            pl.BlockSpec((H_B, 1), lambda b, i: (0, 0)),
            pl.BlockSpec((1, D_V), lambda b, i: (0, 0)),
            pl.BlockSpec((None, H_B, D_K, D_V), lambda b, i: (b, 0, 0, 0)),
            pl.BlockSpec((None, CONV_W - 1, G_COLS), lambda b, i: (b, 0, 0)),
        ],
        out_specs=[
            pl.BlockSpec((rows, hk), lambda b, i: (row_blk(b, i), 0)),
            pl.BlockSpec((None, H_B, D_K, D_V), lambda b, i: (b, 0, 0, 0)),
            pl.BlockSpec((None, CONV_W - 1, G_COLS), lambda b, i: (b, 0, 0)),
        ],
        out_shape=[
            jax.ShapeDtypeStruct((bn * ln, hk), BF16),
            jax.ShapeDtypeStruct(s0.shape, F32),
            jax.ShapeDtypeStruct(c0.shape, F32),
        ],
        scratch_shapes=[
            pltpu.VMEM((rows + SUBLANES, G_COLS), F32),
            pltpu.VMEM((rows, G_COLS), F32),
        ],
        compiler_params=_cparams("parallel", "arbitrary"),
        name="gdn",
    )(gq, gq, gq, zg, small, small_t, conv_w, alr, dtr, alc, dtc, norm_w, s0, c0)


def _merge_body(x_ref, oa_ref, ob_ref, ga0_ref, ga1_ref, gb0_ref, gb1_ref, wa_ref, wb_ref, wo_ref,
                nf_ref, wr_ref, br_ref, x2_ref, t_ref, lg_ref):
    ya = jnp.dot(oa_ref[...], wa_ref[...], preferred_element_type=F32)
    yb = jnp.dot(ob_ref[...], wb_ref[...], preferred_element_type=F32)
    half = ga0_ref.shape[1]
    gate = lambda ref: _sigmoid(ref[...].astype(F32))
    m0 = gate(ga0_ref) * ya[:, :half] + gate(gb0_ref) * yb[:, :half]
    m1 = gate(ga1_ref) * ya[:, half:] + gate(gb1_ref) * yb[:, half:]
    merged = jnp.concatenate([m0, m1], axis=1).astype(BF16)
    x2 = x_ref[...] + jnp.dot(merged, wo_ref[...], preferred_element_type=F32)
    x2_ref[...] = x2
    ms = jnp.mean(x2 * x2, axis=-1, keepdims=True)
    t = x2 * lax.rsqrt(ms + EPS) * nf_ref[...]
    t_ref[...] = _pack_bf16_pair(t)
    lg_ref[...] = _dot3(t, wr_ref[...]) + br_ref[...]


def _merge(x, o_a, o_b, zg, wa, wb, wo, norm_ffn, w_router, b_router):
    T, D = x.shape
    half = D // 2
    assert Z_COLS % half == 0
    gcol = Z_COLS // half
    bm = _pick(T, (256, 128, 64, 32, 16, 8))
    const = dict(pipeline_mode=pl.Buffered(1))
    return pl.pallas_call(
        _merge_body,
        grid=(T // bm,),
        in_specs=[
            pl.BlockSpec((bm, D), lambda i: (i, 0)),
            pl.BlockSpec((bm, o_a.shape[1]), lambda i: (i, 0)),
            pl.BlockSpec((bm, o_b.shape[1]), lambda i: (i, 0)),
            pl.BlockSpec((bm, half), lambda i: (i, gcol)),
            pl.BlockSpec((bm, half), lambda i: (i, gcol + 1)),
            pl.BlockSpec((bm, half), lambda i: (i, gcol + 2)),
            pl.BlockSpec((bm, half), lambda i: (i, gcol + 3)),
            pl.BlockSpec(wa.shape, lambda i: (0, 0), **const),
            pl.BlockSpec(wb.shape, lambda i: (0, 0), **const),
            pl.BlockSpec(wo.shape, lambda i: (0, 0), **const),
            pl.BlockSpec((1, D), lambda i: (0, 0)),
            pl.BlockSpec((D, LANES), lambda i: (0, 0), **const),
            pl.BlockSpec((1, LANES), lambda i: (0, 0)),
        ],
        out_specs=[
            pl.BlockSpec((bm, D), lambda i: (i, 0)),
            pl.BlockSpec((bm, half), lambda i: (i, 0)),
            pl.BlockSpec((bm, LANES), lambda i: (i, 0)),
        ],
        out_shape=[
            jax.ShapeDtypeStruct((T, D), F32),
            jax.ShapeDtypeStruct((T, half), U32),
            jax.ShapeDtypeStruct((T, LANES), F32),
        ],
        compiler_params=_cparams("parallel"),
        name="merge",
    )(x, o_a, o_b, zg, zg, zg, zg, wa, wb, wo, norm_ffn, w_router, b_router)


def _route_body(lg_ref, mi_ref, mf_ref, cnt_ref):
    i = pl.program_id(0)

    @pl.when(i == 0)
    def _():
        cnt_ref[...] = jnp.zeros(cnt_ref.shape, F32)

    lg = lg_ref[...]
    bm = lg.shape[0]
    lane = lax.broadcasted_iota(I32, lg.shape, 1).astype(F32)
    big = jnp.float32(LANES)
    neg = -jnp.inf
    gl = jnp.where(lane < N_GROUPS, lg, neg)
    gmax = jnp.max(gl, axis=1, keepdims=True)
    gidx = jnp.min(jnp.where(gl == gmax, lane, big), axis=1, keepdims=True)
    gw = 1.0 / jnp.sum(jnp.exp(gl - gmax), axis=1, keepdims=True)
    e_lo = N_GROUPS + gidx * EXPERTS_PER_GROUP
    valid = (lane >= e_lo) & (lane < e_lo + EXPERTS_PER_GROUP)
    el = jnp.where(valid, lg, neg)
    v1 = jnp.max(el, axis=1, keepdims=True)
    i1 = jnp.min(jnp.where(el == v1, lane, big), axis=1, keepdims=True)
    el2 = jnp.where(lane == i1, neg, el)
    v2 = jnp.max(el2, axis=1, keepdims=True)
    i2 = jnp.min(jnp.where(el2 == v2, lane, big), axis=1, keepdims=True)
    e21 = jnp.exp(v2 - v1)
    w1 = gw / (1.0 + e21)
    w2 = gw * e21 / (1.0 + e21)
    oh1 = (lane == i1).astype(F32)
    oh2 = (lane == i2).astype(F32)
    oh = oh1 + oh2
    rr = lax.broadcasted_iota(I32, (bm, bm), 0)
    cc = lax.broadcasted_iota(I32, (bm, bm), 1)
    before = (cc < rr).astype(BF16)
    cum = jnp.dot(before, oh.astype(BF16), preferred_element_type=F32) + cnt_ref[...]
    rank1 = jnp.sum(cum * oh1, axis=1, keepdims=True)
    rank2 = jnp.sum(cum * oh2, axis=1, keepdims=True)
    cnt_ref[...] = cnt_ref[...] + jnp.sum(oh, axis=0, keepdims=True)
    mi = jnp.where(lane == 0, i1 - N_GROUPS, 0.0)
    mi = jnp.where(lane == 1, i2 - N_GROUPS, mi)
    mi = jnp.where(lane == 2, rank1, mi)
    mi = jnp.where(lane == 3, rank2, mi)
    mi_ref[...] = mi.T[0:SUBLANES].astype(I32)
    mf_ref[...] = jnp.where(lane == 0, w1, jnp.where(lane == 1, w2, 0.0))


def _route(logits):
    T = logits.shape[0]
    bm = _pick(T, (512, 256, 128, 64, 32, 16, 8))
    return pl.pallas_call(
        _route_body,
        grid=(T // bm,),
        in_specs=[pl.BlockSpec((bm, LANES), lambda i: (i, 0))],
        out_specs=[
            pl.BlockSpec((SUBLANES, bm), lambda i: (0, i)),
            pl.BlockSpec((bm, LANES), lambda i: (i, 0)),
            pl.BlockSpec((1, LANES), lambda i: (0, 0)),
        ],
        out_shape=[
            jax.ShapeDtypeStruct((SUBLANES, T), I32),
            jax.ShapeDtypeStruct((T, LANES), F32),
            jax.ShapeDtypeStruct((1, LANES), F32),
        ],
        compiler_params=_cparams("arbitrary"),
        name="route",
    )(logits)


def _dispatch_body(bm, blk, n_experts, nbp, ps_ref, pn_ref, nu_ref, dest_ref, tp_ref, ts_ref, xb_ref,
                   zero_scr, sem, zsem):
    i = pl.program_id(0)
    n_blocks = xb_ref.shape[0] // blk

    def pad_rows(act):
        def per_expert(e, carry):
            def per_row(r, c2):
                act(pltpu.make_async_copy(zero_scr.at[pl.ds(0, 1)], xb_ref.at[pl.ds(ps_ref[e] + r, 1)], zsem))
                return c2
            return lax.fori_loop(0, pn_ref[e], per_row, carry)
        lax.fori_loop(0, n_experts, per_expert, 0)

    def tail_blocks(act):
        def per_block(b, carry):
            act(pltpu.make_async_copy(zero_scr, xb_ref.at[pl.ds(pl.multiple_of(b * blk, blk), blk)], zsem))
            return carry
        lax.fori_loop(nu_ref[0], n_blocks, per_block, 0)

    @pl.when(i == 0)
    def _():
        zero_scr[...] = jnp.zeros(zero_scr.shape, zero_scr.dtype)
        pad_rows(lambda cp: cp.start())
        tail_blocks(lambda cp: cp.start())

    def send_rows(t_ref):
        def row_copy(r, k):
            d = dest_ref[0, k * bm + r]
            return pltpu.make_async_copy(t_ref.at[pl.ds(r, 1)], xb_ref.at[pl.ds(d, 1)], sem)

        def issue(r, carry):
            for k in range(TOP_K_INNER):
                row_copy(r, k).start()
            return carry

        def drain(r, carry):
            for k in range(TOP_K_INNER):
                row_copy(r, k).wait()
            return carry

        lax.fori_loop(0, bm, issue, 0, unroll=DMA_UNROLL)
        lax.fori_loop(0, bm, drain, 0, unroll=DMA_UNROLL)

    @pl.when(i < nbp)
    def _():
        send_rows(tp_ref)

    @pl.when(i >= nbp)
    def _():
        send_rows(ts_ref)

    @pl.when(i == 0)
    def _():
        pad_rows(lambda cp: cp.wait())
        tail_blocks(lambda cp: cp.wait())


def _dispatch(t_p, t_s, dest, pad_start, pad_len, n_used, n_rows, blk):
    (tp, D), ts = t_p.shape, t_s.shape[0]
    bm = _pick(math.gcd(tp, ts), (256, 128, 64, 32, 16, 8))
    nbp, nbs = tp // bm, ts // bm
    dest3 = _dest_blocks(dest, bm)
    grid_spec = pltpu.PrefetchScalarGridSpec(
        num_scalar_prefetch=3,
        grid=(nbp + nbs,),
        in_specs=[
            pl.BlockSpec((None, 1, TOP_K_INNER * bm), lambda i, ps, pn, nu: (i, 0, 0),
                         memory_space=pltpu.SMEM),
            pl.BlockSpec((bm, D), lambda i, ps, pn, nu: (jnp.minimum(i, nbp - 1), 0)),
            pl.BlockSpec((bm, D), lambda i, ps, pn, nu: (jnp.maximum(i - nbp, 0), 0)),
        ],
        out_specs=pl.BlockSpec(memory_space=pl.ANY),
        scratch_shapes=[
            pltpu.VMEM((blk, D), t_p.dtype),
            pltpu.SemaphoreType.DMA(()),
            pltpu.SemaphoreType.DMA(()),
        ],
    )
    return pl.pallas_call(
        functools.partial(_dispatch_body, bm, blk, pad_start.shape[0], nbp),
        grid_spec=grid_spec,
        out_shape=jax.ShapeDtypeStruct((n_rows, D), t_p.dtype),
        compiler_params=_cparams("arbitrary"),
        name="dispatch",
    )(pad_start, pad_len, n_used, dest3, t_p, t_s)


def _expert_body(be_ref, nu_ref, x_ref, wg_ref, wu_ref, wd_ref, y_ref, wg_scr, wu_scr, wd_scr):
    i = pl.program_id(0)

    @pl.when(i < nu_ref[0])
    def _():
        @pl.when(jnp.logical_or(i == 0, be_ref[i] != be_ref[jnp.maximum(i - 1, 0)]))
        def _():
            wg_scr[...] = wg_ref[...].astype(BF16)
            wu_scr[...] = wu_ref[...].astype(BF16)
            wd_scr[...] = wd_ref[...].astype(BF16)

        x = _unpack_bf16_pair(x_ref[...])
        g = jnp.dot(x, wg_scr[...], preferred_element_type=F32)
        u = jnp.dot(x, wu_scr[...], preferred_element_type=F32)
        hmid = (_silu(g) * u).astype(BF16)
        y_ref[...] = jnp.dot(hmid, wd_scr[...], preferred_element_type=F32)

    @pl.when(i >= nu_ref[0])
    def _():
        y_ref[...] = jnp.zeros(y_ref.shape, F32)


def _experts(xb, blk_exp, n_used, w_gate, w_up, w_down, blk):
    P = xb.shape[0]
    D, de = w_gate.shape[1:]
    assert xb.shape[1] * 2 == D
    n_blocks = P // blk
    last = lambda i, nu: jnp.minimum(i, nu[0] - 1)
    grid_spec = pltpu.PrefetchScalarGridSpec(
        num_scalar_prefetch=2,
        grid=(n_blocks,),
        in_specs=[
            pl.BlockSpec((blk, D // 2), lambda i, be, nu: (last(i, nu), 0)),
            pl.BlockSpec((None, D, de), lambda i, be, nu: (be[last(i, nu)], 0, 0)),
            pl.BlockSpec((None, D, de), lambda i, be, nu: (be[last(i, nu)], 0, 0)),
            pl.BlockSpec((None, de, D), lambda i, be, nu: (be[last(i, nu)], 0, 0)),
        ],
        out_specs=pl.BlockSpec((blk, D), lambda i, be, nu: (i, 0)),
        scratch_shapes=[
            pltpu.VMEM((D, de), BF16),
            pltpu.VMEM((D, de), BF16),
            pltpu.VMEM((de, D), BF16),
        ],
    )
    return pl.pallas_call(
        _expert_body,
        grid_spec=grid_spec,
        out_shape=jax.ShapeDtypeStruct((P, D), F32),
        compiler_params=_cparams("arbitrary"),
        name="experts",
    )(blk_exp, n_used, xb, w_gate, w_up, w_down)


def _combine_body(bm, final, dest_ref, x2_ref, mf_ref, nw_ref, yb_ref, o_ref, y0_scr, y1_scr, sem):
    bufs = (y0_scr, y1_scr)

    def row_copy(r, k):
        d = dest_ref[0, k * bm + r]
        return pltpu.make_async_copy(yb_ref.at[pl.ds(d, 1)], bufs[k].at[pl.ds(r, 1)], sem)

    def issue(r, carry):
        for k in range(TOP_K_INNER):
            row_copy(r, k).start()
        return carry

    lax.fori_loop(0, bm, issue, 0, unroll=DMA_UNROLL)

    def drain(r, carry):
        for k in range(TOP_K_INNER):
            row_copy(r, k).wait()
        return carry

    lax.fori_loop(0, bm, drain, 0, unroll=DMA_UNROLL)

    mf = mf_ref[...]
    x3 = x2_ref[...] + y0_scr[...] * mf[:, 0:1] + y1_scr[...] * mf[:, 1:2]
    if final:
        ms = jnp.mean(x3 * x3, axis=-1, keepdims=True)
        x3 = x3 * lax.rsqrt(ms + EPS) * nw_ref[...]
    o_ref[...] = x3


def _combine(x2, yb, dest, mf, norm_final, final):
    T, D = x2.shape
    bm = _pick(T, (256, 128, 64, 32, 16, 8))
    nblk = T // bm
    dest3 = _dest_blocks(dest, bm)
    return pl.pallas_call(
        functools.partial(_combine_body, bm, final),
        grid=(nblk,),
        in_specs=[
            pl.BlockSpec((None, 1, TOP_K_INNER * bm), lambda i: (i, 0, 0), memory_space=pltpu.SMEM),
            pl.BlockSpec((bm, D), lambda i: (i, 0)),
            pl.BlockSpec((bm, LANES), lambda i: (i, 0)),
            pl.BlockSpec((1, D), lambda i: (0, 0)),
            pl.BlockSpec(memory_space=pl.ANY),
        ],
        out_specs=pl.BlockSpec((bm, D), lambda i: (i, 0)),
        out_shape=jax.ShapeDtypeStruct((T, D), F32),
        scratch_shapes=[
            pltpu.VMEM((bm, D), F32),
            pltpu.VMEM((bm, D), F32),
            pltpu.SemaphoreType.DMA(()),
        ],
        compiler_params=_cparams("arbitrary"),
        name="combine",
    )(dest3, x2, mf, norm_final, yb)


def _rotary_tables(pos):
    half = ROT_DIM // 2
    inv = ROPE_THETA ** (-jnp.arange(0, ROT_DIM, 2, dtype=F32) / ROT_DIM)
    ang = pos.astype(F32)[:, None] * inv[None, :]
    cos = jnp.cos(ang)
    sin = jnp.sin(ang)
    n = pos.shape[0]
    ones = jnp.ones((n, D_A - ROT_DIM), F32)
    zeros = jnp.zeros((n, D_A - ROT_DIM), F32)
    zh = jnp.zeros((n, half), F32)
    cos64 = jnp.concatenate([cos, cos, ones], axis=1)
    sa64 = jnp.concatenate([zh, sin, zeros], axis=1)
    sb64 = jnp.concatenate([-sin, zh, zeros], axis=1)
    reps = LANES // D_A
    return jnp.tile(cos64, (1, reps)), jnp.tile(sa64, (1, reps)), jnp.tile(sb64, (1, reps))


def _dest_blocks(dest, bm):
    nblk = dest.shape[1] // bm
    return jnp.swapaxes(dest.reshape(TOP_K_INNER, nblk, bm), 0, 1).reshape(nblk, 1, TOP_K_INNER * bm)


def _moe_slots(mi, counts, n_experts, blk, n_assign):
    eid = mi[0:TOP_K_INNER]
    rank = mi[TOP_K_INNER:2 * TOP_K_INNER]
    cnt = counts[0, N_GROUPS:N_GROUPS + n_experts].astype(I32)
    pc = (cnt + blk - 1) // blk * blk
    pend = jnp.cumsum(pc)
    pstart = pend - pc
    experts = jnp.arange(n_experts, dtype=I32)[:, None, None]
    first = jnp.sum(jnp.where(eid[None] == experts, pstart[:, None, None], 0), axis=0)
    dest = (first + rank).astype(I32)
    n_blocks = -(-n_assign // blk) + n_experts
    blk_start = jnp.arange(n_blocks, dtype=I32) * blk
    blk_exp = jnp.minimum(jnp.sum((pend[None, :] <= blk_start[:, None]).astype(I32), axis=1), n_experts - 1)
    n_used = jnp.maximum(pend[-1] // blk, 1).astype(I32).reshape(1)
    return dest, blk_exp.astype(I32), n_used, n_blocks, (pstart + cnt).astype(I32), (pc - cnt).astype(I32)


def _layer(xp, xs, layer, dims, cache_k, cache_v, state_delta, state_conv, prm, norm_final, final):
    (norm_mix, w_in, lq1, lk1, lq2, lk2, subln_w, conv_w, a_log, dt_bias, gdn_norm_w,
     w_proj_a, w_proj_b, w_out, norm_ffn, w_gr, b_gr, w_er, b_er, w_eg, w_eu, w_ed) = prm
    bp, lp, bs, ls, past = dims
    D = xp.shape[1]
    tp = bp * lp
    n_qkv = 2 * A_COLS + V_COLS + G_COLS + Z_COLS
    n_small = 2 * H_B
    lam_init = 0.8 - 0.6 * math.exp(-0.3 * layer)

    w_t = jnp.swapaxes(w_in, 0, 1)
    w_a = w_t[:n_qkv].astype(BF16)
    w_b = w_t[n_qkv + n_small:].astype(BF16)
    w_small = jnp.pad(w_t[n_qkv:n_qkv + n_small], ((0, LANES - n_small), (0, 0))).astype(BF16)
    nm = norm_mix.reshape(1, D)
    pp = _inproj(xp, nm, _rotary_tables(jnp.tile(jnp.arange(lp), bp)), w_a, w_b, w_small)
    ps = _inproj(xs, nm, _rotary_tables(jnp.tile(past + jnp.arange(ls), bs)), w_a, w_b, w_small)

    lamp = jnp.stack([lq1, lk1, lq2, lk2]).astype(F32)
    sw = subln_w.reshape(1, D_VA)
    oa_p = _attn_prompt(pp.q_bf, pp.k_bf, pp.v_t, lamp, sw, lam_init, bp, lp)
    oa_s = _attn_sample(ps.q_bf, ps.k_bf, ps.v_new, cache_k.reshape(bs, past * H_A, 2 * D_A),
                        cache_v.reshape(bs, past * H_A, D_VA), lamp, sw, lam_init, bs, ls)

    nw = gdn_norm_w.reshape(1, D_V)
    ob_p, s_p, c_p = _gdn(pp.gq, pp.zg, pp.small, jnp.zeros((bp, H_B, D_K, D_V), F32),
                          jnp.zeros((bp, CONV_W - 1, G_COLS), F32), conv_w, a_log, dt_bias, nw,
                          bp, lp, _pick(lp, (GDN_CHUNK, CHUNK)))
    ob_s, s_s, c_s = _gdn(ps.gq, ps.zg, ps.small, state_delta, state_conv, conv_w, a_log, dt_bias, nw,
                          bs, ls, ls)

    n_experts = w_er.shape[1]
    w_router = jnp.pad(jnp.concatenate([w_gr, w_er], axis=1), ((0, 0), (0, LANES - N_GROUPS - n_experts)))
    b_router = jnp.pad(jnp.concatenate([b_gr, b_er]), (0, LANES - N_GROUPS - n_experts)).reshape(1, LANES)
    mw = (w_proj_a.astype(BF16), w_proj_b.astype(BF16), w_out.astype(BF16), norm_ffn.reshape(1, D),
          w_router, b_router)
    x2_p, t_p, lg_p = _merge(xp, oa_p, ob_p, pp.zg, *mw)
    x2_s, t_s, lg_s = _merge(xs, oa_s, ob_s, ps.zg, *mw)

    blk = MOE_ROWS
    mi, mf, counts = _route(jnp.concatenate([lg_p, lg_s], axis=0))
    n_tok = tp + bs * ls
    dest, blk_exp, n_used, n_blocks, pad_start, pad_len = _moe_slots(
        mi, counts, n_experts, blk, n_tok * TOP_K_INNER)
    xb = _dispatch(t_p, t_s, dest, pad_start, pad_len, n_used, n_blocks * blk, blk)
    yb = _experts(xb, blk_exp, n_used, w_eg, w_eu, w_ed, blk)
    nf = norm_final.reshape(1, D)
    y_p = _combine(x2_p, yb, dest[:, :tp], mf[:tp], nf, final)
    y_s = _combine(x2_s, yb, dest[:, tp:], mf[tp:], nf, final)
    return y_p, y_s, pp.k_new, pp.v_new, s_p, c_p, ps.k_new, ps.v_new, s_s, c_s


def kernel(x_prompt, x_sample, cache_k, cache_v, state_delta, state_conv, norm_mix, w_in, lambda_q1, lambda_k1, lambda_q2, lambda_k2, subln_w, conv_w, a_log, dt_bias, gdn_norm_w, w_proj_a, w_proj_b, w_out, norm_ffn, w_group_router, b_group_router, w_expert_router, b_expert_router, w_exp_gate, w_exp_up, w_exp_down, norm_final):
    bp, lp, D = x_prompt.shape
    bs, ls, _ = x_sample.shape
    depth = cache_k.shape[0]
    past = cache_k.shape[2]
    dims = (bp, lp, bs, ls, past)
    xp = x_prompt.reshape(bp * lp, D)
    xs = x_sample.reshape(bs * ls, D)
    kp, vp, sp, cp, ksm, vsm, ssm, csm = [], [], [], [], [], [], [], []
    for l in range(depth):
        prm = (norm_mix[l], w_in[l], lambda_q1[l], lambda_k1[l], lambda_q2[l], lambda_k2[l], subln_w[l],
               conv_w[l], a_log[l], dt_bias[l], gdn_norm_w[l], w_proj_a[l], w_proj_b[l], w_out[l],
               norm_ffn[l], w_group_router[l], b_group_router[l], w_expert_router[l], b_expert_router[l],
               w_exp_gate[l], w_exp_up[l], w_exp_down[l])
        xp, xs, k_p, v_p, s_p, c_p, k_s, v_s, s_s, c_s = _layer(
            xp, xs, l, dims, cache_k[l], cache_v[l], state_delta[l], state_conv[l], prm,
            norm_final, l == depth - 1)
        kp.append(k_p.reshape(bp, lp, H_A, 2 * D_A))
        vp.append(v_p.reshape(bp, lp, H_A, D_VA))
        ksm.append(k_s.reshape(bs, ls, H_A, 2 * D_A))
        vsm.append(v_s.reshape(bs, ls, H_A, D_VA))
        sp.append(s_p)
        cp.append(c_p)
        ssm.append(s_s)
        csm.append(c_s)
    return (xp.reshape(bp, lp, D), xs.reshape(bs, ls, D), jnp.stack(kp), jnp.stack(vp), jnp.stack(sp),
            jnp.stack(cp), jnp.stack(ksm), jnp.stack(vsm), jnp.stack(ssm), jnp.stack(csm))
```

```python
import collections
import functools
import math

import jax
import jax.numpy as jnp
from jax import lax
from jax.experimental import pallas as pl
from jax.experimental.pallas import tpu as pltpu

F32 = jnp.float32
BF16 = jnp.bfloat16
I32 = jnp.int32
U32 = jnp.uint32

CHUNK = 64
GDN_CHUNK = 128
H_A = 8
D_A = 64
D_VA = 2 * D_A
ROT_DIM = D_A // 4
ROPE_THETA = 500000.0
H_B = 8
D_K = 128
D_V = 128
CONV_W = 4
N_GROUPS = 4
EXPERTS_PER_GROUP = 8
TOP_K_INNER = 2
EPS = 1e-6

LANES = 128
SUBLANES = 8
V7X_VMEM_BYTES = 64 * 1024 * 1024
VMEM_LIMIT_BYTES = V7X_VMEM_BYTES - 8 * 1024 * 1024

Q_SCALE = (D_A ** -0.5) * math.log2(math.e)
ONES_ROWS = 16
INPROJ_BN = 512
MOE_ROWS = 256
DMA_UNROLL = 8

NT_DIMS = (((1,), (1,)), ((), ()))
TN_DIMS = (((0,), (0,)), ((), ()))

A_COLS = H_A * 2 * D_A
V_COLS = H_A * D_VA
G_COLS = 2 * H_B * D_K + H_B * D_V
Z_COLS = H_B * D_V


def _pick(n, cands):
    for c in cands:
        if n % c == 0:
            return c
    raise ValueError(f"no block size in {cands} divides {n}")


def _cparams(*sem):
    return pltpu.CompilerParams(dimension_semantics=sem, vmem_limit_bytes=VMEM_LIMIT_BYTES)


def _sigmoid(x):
    return 1.0 / (1.0 + jnp.exp(-x))


def _silu(x):
    return x * _sigmoid(x)


def _softplus(x):
    return jnp.maximum(x, 0.0) + jnp.log1p(jnp.exp(-jnp.abs(x)))


def _pack_bf16_pair(x):
    bits = lax.bitcast_convert_type(x.astype(BF16).astype(F32), U32)
    half = x.shape[1] // 2
    return (bits[:, :half] >> 16) | (bits[:, half:] & jnp.uint32(0xFFFF0000))


def _unpack_bf16_pair(w):
    lo = lax.bitcast_convert_type(w << 16, F32)
    hi = lax.bitcast_convert_type(w & jnp.uint32(0xFFFF0000), F32)
    return jnp.concatenate([lo, hi], axis=1).astype(BF16)


def _bdot(a, b):
    return jnp.dot(a.astype(BF16), b.astype(BF16), preferred_element_type=F32)


def _split2(a):
    hi = a.astype(BF16)
    lo = (a - hi.astype(F32)).astype(BF16)
    return hi, lo


def _dot3(a, b):
    ah, al = _split2(a)
    bh, bl = _split2(b)
    d = functools.partial(jnp.dot, preferred_element_type=F32)
    return d(ah, bh) + d(ah, bl) + d(al, bh)


def _dot_exact_mask(mask, x, mask_is_lhs):
    hi = x.astype(BF16)
    r1 = x - hi.astype(F32)
    mid = r1.astype(BF16)
    lo = (r1 - mid.astype(F32)).astype(BF16)
    mb = mask.astype(BF16)
    d = functools.partial(jnp.dot, preferred_element_type=F32)
    if mask_is_lhs:
        return d(mb, hi) + d(mb, mid) + d(mb, lo)
    return d(hi, mb) + d(mid, mb) + d(lo, mb)


InProj = collections.namedtuple("InProj", "small k_new v_new q_bf k_bf v_t gq zg")


def _inproj_body(o_k, o_v, o_g, o_z, nb_wa, x_ref, nw_ref, cos_ref, sa_ref, sb_ref, wa_ref, wb_ref, ws_ref,
                 os_ref, k_ref, v_ref, qb_ref, kb_ref, vt_ref, gq_ref, zg_ref, h_scr):
    j = pl.program_id(1)
    bm = h_scr.shape[0]
    half = wa_ref.shape[0] // 2
    lpb = half // LANES

    @pl.when(j == 0)
    def _():
        x = x_ref[...]
        ms = jnp.mean(x * x, axis=-1, keepdims=True)
        h = (x * lax.rsqrt(ms + EPS) * nw_ref[...]).astype(BF16)
        h_scr[...] = h
        os_ref[...] = lax.dot_general(h, ws_ref[...], NT_DIMS, preferred_element_type=F32)

    def halves(w_ref):
        return [lax.dot_general(h_scr[...], w_ref[c * half:(c + 1) * half, :], NT_DIMS,
                                preferred_element_type=F32) for c in range(2)]

    def lane_blocks(w_ref):
        for c2, acc in enumerate(halves(w_ref)):
            for c in range(lpb):
                yield c2 * lpb + c, acc[:, c * LANES:(c + 1) * LANES]

    def rotated(blk):
        return (blk * cos_ref[...] + pltpu.roll(blk, ROT_DIM // 2, 1) * sa_ref[...]
                + pltpu.roll(blk, LANES - ROT_DIM // 2, 1) * sb_ref[...])

    @pl.when(j < o_k)
    def _():
        for lb, blk in lane_blocks(wa_ref):
            qb_ref[:, lb * LANES:(lb + 1) * LANES] = (rotated(blk) * Q_SCALE).astype(BF16)

    hpb = 2 * lpb
    for jj in range(o_v - o_k):
        @pl.when(j == o_k + jj)
        def _(jj=jj):
            for lb, blk in lane_blocks(wa_ref):
                rot = rotated(blk)
                k_ref[pl.ds(jj * hpb + lb, bm, stride=H_A), :] = rot
                kb_ref[:, lb * LANES:(lb + 1) * LANES] = rot.astype(BF16)

    for jj in range(o_g - o_v):
        @pl.when(j == o_v + jj)
        def _(jj=jj):
            for lb, blk in lane_blocks(wa_ref):
                v_ref[pl.ds(jj * hpb + lb, bm, stride=H_A), :] = blk
                vt_ref[lb, 0:D_VA, :] = blk.T.astype(BF16)
                vt_ref[lb, D_VA:D_VA + ONES_ROWS, :] = jnp.ones((ONES_ROWS, bm), BF16)

    @pl.when(jnp.logical_and(j >= o_g, j < o_z))
    def _():
        for c, acc in enumerate(halves(wa_ref)):
            gq_ref[:, c * half:(c + 1) * half] = acc

    @pl.when(jnp.logical_and(j >= o_z, j < nb_wa))
    def _():
        for c, acc in enumerate(halves(wa_ref)):
            zg_ref[:, c * half:(c + 1) * half] = acc.astype(BF16)

    @pl.when(j >= nb_wa)
    def _():
        for c, acc in enumerate(halves(wb_ref)):
            zg_ref[:, c * half:(c + 1) * half] = acc.astype(BF16)


def _inproj(x, norm_w, tables, w_a, w_b, w_small):
    T, D = x.shape
    bn = INPROJ_BN
    zg_cols = Z_COLS + 2 * D
    assert A_COLS % bn == 0 and V_COLS % bn == 0 and G_COLS % bn == 0 and Z_COLS % bn == 0
    assert (bn // 2) % D_VA == 0 and D_VA == LANES and (2 * D) % bn == 0
    assert w_a.shape[0] == 2 * A_COLS + V_COLS + G_COLS + Z_COLS and w_b.shape[0] == 2 * D
    nb_a, nb_v, nb_g, nb_z = A_COLS // bn, V_COLS // bn, G_COLS // bn, zg_cols // bn
    o_k, o_v = nb_a, 2 * nb_a
    o_g = o_v + nb_v
    o_z = o_g + nb_g
    nb_wa, nb_wb = w_a.shape[0] // bn, w_b.shape[0] // bn
    bm = _pick(T, (1024, 512, 256, 128))
    hpb = bn // D_VA

    def col(lo, n):
        return lambda i, j: (i, jnp.clip(j - lo, 0, n - 1))

    cos_t, sa_t, sb_t = tables
    outs = pl.pallas_call(
        functools.partial(_inproj_body, o_k, o_v, o_g, o_z, nb_wa),
        grid=(T // bm, nb_wa + nb_wb),
        in_specs=[
            pl.BlockSpec((bm, D), lambda i, j: (i, 0), pipeline_mode=pl.Buffered(1)),
            pl.BlockSpec((1, D), lambda i, j: (0, 0)),
            pl.BlockSpec((bm, LANES), lambda i, j: (i, 0)),
            pl.BlockSpec((bm, LANES), lambda i, j: (i, 0)),
            pl.BlockSpec((bm, LANES), lambda i, j: (i, 0)),
            pl.BlockSpec((bn, D), lambda i, j: (jnp.minimum(j, nb_wa - 1), 0)),
            pl.BlockSpec((bn, D), lambda i, j: (jnp.maximum(j - nb_wa, 0), 0)),
            pl.BlockSpec((LANES, D), lambda i, j: (0, 0)),
        ],
        out_specs=[
            pl.BlockSpec((bm, LANES), lambda i, j: (i, 0)),
            pl.BlockSpec((bm * H_A, D_VA), lambda i, j: (i, 0)),
            pl.BlockSpec((bm * H_A, D_VA), lambda i, j: (i, 0)),
            pl.BlockSpec((bm, bn), col(0, nb_a)),
            pl.BlockSpec((bm, bn), col(o_k, nb_a)),
            pl.BlockSpec((hpb, D_VA + ONES_ROWS, bm), lambda i, j: (jnp.clip(j - o_v, 0, nb_v - 1), 0, i)),
            pl.BlockSpec((bm, bn), col(o_g, nb_g)),
            pl.BlockSpec((bm, bn), col(o_z, nb_z)),
        ],
        out_shape=[
            jax.ShapeDtypeStruct((T, LANES), F32),
            jax.ShapeDtypeStruct((T * H_A, 2 * D_A), F32),
            jax.ShapeDtypeStruct((T * H_A, D_VA), F32),
            jax.ShapeDtypeStruct((T, A_COLS), BF16),
            jax.ShapeDtypeStruct((T, A_COLS), BF16),
            jax.ShapeDtypeStruct((H_A, D_VA + ONES_ROWS, T), BF16),
            jax.ShapeDtypeStruct((T, G_COLS), F32),
            jax.ShapeDtypeStruct((T, zg_cols), BF16),
        ],
        scratch_shapes=[pltpu.VMEM((bm, D), BF16)],
        compiler_params=_cparams("parallel", "arbitrary"),
        name="inproj",
    )(x, norm_w, cos_t, sa_t, sb_t, w_a, w_b, w_small)
    return InProj(*outs)


def _diff_lambda(lamp, lam_init):
    s1 = jnp.sum(lamp[0:1] * lamp[1:2], axis=1, keepdims=True)
    s2 = jnp.sum(lamp[2:3] * lamp[3:4], axis=1, keepdims=True)
    return jnp.exp(s1) - jnp.exp(s2) + lam_init


def _subln(o, w, lam_init):
    ms = jnp.mean(o * o, axis=-1, keepdims=True)
    return o * lax.rsqrt(ms + EPS) * w * (1.0 - lam_init)


def _attn_prompt_body(lam_init, bq, q_ref, k_ref, vt_ref, lamp_ref, swc_ref, o_ref,
                      m_scr, acc_scr, s_scr):
    def query_block(qi, carry):
        _attn_query_block(lam_init, bq, qi, q_ref, k_ref, vt_ref, lamp_ref, swc_ref, o_ref,
                          m_scr, acc_scr, s_scr)
        return carry

    lax.fori_loop(0, q_ref.shape[0] // bq, query_block, 0)


def _attn_query_block(lam_init, bq, qi, q_ref, k_ref, vt_ref, lamp_ref, swc_ref, o_ref,
                      m_scr, acc_scr, s_scr):
    q_rows = pl.ds(pl.multiple_of(qi * bq, bq), bq)
    q = q_ref[q_rows, :]
    lane = lax.broadcasted_iota(I32, q.shape, 1)
    qz = [jnp.where(lane < D_A, q, jnp.zeros_like(q)), jnp.where(lane >= D_A, q, jnp.zeros_like(q))]
    m_scr[...] = jnp.full(m_scr.shape, -jnp.inf, F32)
    acc_scr[...] = jnp.zeros(acc_scr.shape, F32)
    qry_pos = lax.broadcasted_iota(I32, (1, bq), 1)

    def scores(j, slot):
        kb = k_ref[pl.ds(pl.multiple_of(j * bq, bq), bq), :]
        for m in range(2):
            s_scr[slot, m] = lax.dot_general(kb, qz[m], NT_DIMS, preferred_element_type=F32)

    def accumulate(j, slot, masked):
        vb = vt_ref[:, pl.ds(pl.multiple_of(j * bq, bq), bq)]
        for m in range(2):
            if masked:
                s = jnp.concatenate(
                    [jnp.where(qry_pos >= c * CHUNK, s_scr[slot, m, c * CHUNK:(c + 1) * CHUNK, :], -jnp.inf)
                     for c in range(bq // CHUNK)], axis=0)
            else:
                s = s_scr[slot, m]
            m_prev = m_scr[m]
            m_new = jnp.maximum(m_prev, jnp.max(s, axis=0, keepdims=True))
            alpha = jnp.exp2(m_prev - m_new)
            p = jnp.exp2(s - m_new).astype(BF16)
            acc_scr[m] = alpha * acc_scr[m] + jnp.dot(vb, p, preferred_element_type=F32)
            m_scr[m] = m_new

    scores(0, 0)

    def two_full_blocks(t, carry):
        j = 2 * t
        scores(j + 1, 1)
        accumulate(j, 0, False)
        scores(j + 2, 0)
        accumulate(j + 1, 1, False)
        return carry

    lax.fori_loop(0, qi // 2, two_full_blocks, 0)
    j0 = 2 * (qi // 2)

    @pl.when(j0 == qi)
    def _():
        accumulate(qi, 0, True)

    @pl.when(j0 != qi)
    def _():
        scores(qi, 1)
        accumulate(j0, 0, False)
        accumulate(qi, 1, True)

    lam = _diff_lambda(lamp_ref[...], lam_init)
    a0 = acc_scr[0]
    a1 = acc_scr[1]
    o_t = a0[0:D_VA] / a0[D_VA:D_VA + 1] - lam * (a1[0:D_VA] / a1[D_VA:D_VA + 1])
    ms = jnp.mean(o_t * o_t, axis=0, keepdims=True)
    o_t = o_t * lax.rsqrt(ms + EPS) * swc_ref[...] * (1.0 - lam_init)
    o_ref[q_rows, :] = o_t.T.astype(o_ref.dtype)


def _attn_prompt(q_bf, k_bf, v_t, lamp, subln_w, lam_init, bp, lp):
    bq = _pick(lp, (512, 256, 128))
    vrows = v_t.shape[1]
    return pl.pallas_call(
        functools.partial(_attn_prompt_body, lam_init, bq),
        grid=(bp, H_A),
        in_specs=[
            pl.BlockSpec((lp, LANES), lambda b, h: (b, h)),
            pl.BlockSpec((lp, LANES), lambda b, h: (b, h)),
            pl.BlockSpec((None, vrows, lp), lambda b, h: (h, 0, b)),
            pl.BlockSpec((4, D_A), lambda b, h: (0, 0)),
            pl.BlockSpec((D_VA, 1), lambda b, h: (0, 0)),
        ],
        out_specs=pl.BlockSpec((lp, LANES), lambda b, h: (b, h)),
        out_shape=jax.ShapeDtypeStruct((bp * lp, V_COLS), BF16),
        scratch_shapes=[
            pltpu.VMEM((2, 1, bq), F32),
            pltpu.VMEM((2, vrows, bq), F32),
            pltpu.VMEM((2, 2, bq, bq), F32),
        ],
        compiler_params=_cparams("parallel", "parallel"),
        name="attn_prompt",
    )(q_bf, k_bf, v_t, lamp, subln_w.reshape(D_VA, 1))


def _attn_sample_body(lam_init, q_ref, kn_ref, vn_ref, kc_ref, vc_ref, lamp_ref, sw_ref, o_ref,
                      m_scr, l_scr, acc_scr):
    pj = pl.program_id(1)
    npos = kc_ref.shape[0] // H_A
    ls = q_ref.shape[0]

    @pl.when(pj == 0)
    def _():
        m_scr[...] = jnp.full(m_scr.shape, -jnp.inf, F32)
        l_scr[...] = jnp.zeros(l_scr.shape, F32)
        acc_scr[...] = jnp.zeros(acc_scr.shape, F32)

    lane = lax.broadcasted_iota(I32, (ls, D_VA), 1)

    def q_maps(h):
        q = q_ref[:, h * D_VA:(h + 1) * D_VA]
        return jnp.concatenate([jnp.where(lane < D_A, q, jnp.zeros_like(q)),
                                jnp.where(lane >= D_A, q, jnp.zeros_like(q))], axis=0)

    def update_all(keys_of, vals_of):
        heads = range(H_A)
        ss = [lax.dot_general(q_maps(h), keys_of(h), NT_DIMS, preferred_element_type=F32) for h in heads]
        m_new = [jnp.maximum(m_scr[h], jnp.max(ss[h], axis=1, keepdims=True)) for h in heads]
        ps = [jnp.exp2(ss[h] - m_new[h]) for h in heads]
        pv = [jnp.dot(ps[h].astype(BF16), vals_of(h), preferred_element_type=F32) for h in heads]
        for h in heads:
            alpha = jnp.exp2(m_scr[h] - m_new[h])
            l_scr[h] = alpha * l_scr[h] + jnp.sum(ps[h], axis=1, keepdims=True)
            acc_scr[h] = alpha * acc_scr[h] + pv[h]
            m_scr[h] = m_new[h]

    update_all(lambda h: kc_ref[pl.ds(h, npos, stride=H_A), :].astype(BF16),
               lambda h: vc_ref[pl.ds(h, npos, stride=H_A), :].astype(BF16))

    @pl.when(pj == pl.num_programs(1) - 1)
    def _():
        lam = _diff_lambda(lamp_ref[...], lam_init)
        sw = sw_ref[...]
        update_all(lambda h: kn_ref[:, h * D_VA:(h + 1) * D_VA],
                   lambda h: vn_ref[pl.ds(h, ls, stride=H_A), :].astype(BF16))
        for h in range(H_A):
            both = acc_scr[h] / l_scr[h]
            o = both[0:ls] - lam * both[ls:2 * ls]
            o_ref[:, h * D_VA:(h + 1) * D_VA] = _subln(o, sw, lam_init).astype(o_ref.dtype)


def _attn_sample(q_bf, k_bf, v_new, cache_k, cache_v, lamp, subln_w, lam_init, bs, ls):
    past = cache_k.shape[1] // H_A
    pb = _pick(past, (1024, 512, 256, 128, 64))
    assert ls % (2 * SUBLANES) == 0
    return pl.pallas_call(
        functools.partial(_attn_sample_body, lam_init),
        grid=(bs, past // pb),
        in_specs=[
            pl.BlockSpec((ls, A_COLS), lambda b, j: (b, 0)),
            pl.BlockSpec((ls, A_COLS), lambda b, j: (b, 0)),
            pl.BlockSpec((ls * H_A, D_VA), lambda b, j: (b, 0)),
            pl.BlockSpec((None, pb * H_A, D_VA), lambda b, j: (b, j, 0)),
            pl.BlockSpec((None, pb * H_A, D_VA), lambda b, j: (b, j, 0)),
            pl.BlockSpec((4, D_A), lambda b, j: (0, 0)),
            pl.BlockSpec((1, D_VA), lambda b, j: (0, 0)),
        ],
        out_specs=pl.BlockSpec((ls, V_COLS), lambda b, j: (b, 0)),
        out_shape=jax.ShapeDtypeStruct((bs * ls, V_COLS), BF16),
        scratch_shapes=[
            pltpu.VMEM((H_A, 2 * ls, 1), F32),
            pltpu.VMEM((H_A, 2 * ls, 1), F32),
            pltpu.VMEM((H_A, 2 * ls, D_VA), F32),
        ],
        compiler_params=_cparams("parallel", "arbitrary"),
        name="attn_sample",
    )(q_bf, k_bf, v_new, cache_k, cache_v, lamp, subln_w)


def _gdn_body(chunk, rows, pq_ref, pk_ref, pv_ref, pz_ref, sm_ref, smt_ref, cw_ref, alr_ref, dtr_ref,
              alc_ref, dtc_ref, nw_ref, s0_ref, c0_ref, o_ref, s_ref, cn_ref, ext_scr, act_scr):
    i = pl.program_id(1)
    nblk = pl.num_programs(1)
    hk = H_B * D_K
    hv = H_B * D_V
    n_chunks = rows // chunk
    n_steps = int(math.log2(chunk))
    assert 2 ** n_steps == chunk

    @pl.when(i == 0)
    def _():
        s_ref[...] = s0_ref[...]
        ext_scr[0:SUBLANES, :] = jnp.zeros((SUBLANES, ext_scr.shape[1]), F32)
        ext_scr[pl.ds(SUBLANES - (CONV_W - 1), CONV_W - 1), :] = c0_ref[...]

    ext_scr[pl.ds(SUBLANES, rows), 0:hk] = pq_ref[...]
    ext_scr[pl.ds(SUBLANES, rows), hk:2 * hk] = pk_ref[...]
    ext_scr[pl.ds(SUBLANES, rows), 2 * hk:2 * hk + hv] = pv_ref[...]

    @pl.when(i == nblk - 1)
    def _():
        cn_ref[...] = ext_scr[pl.ds(SUBLANES + rows - (CONV_W - 1), CONV_W - 1), :]

    conv = jnp.zeros((rows, ext_scr.shape[1]), F32)
    for w in range(CONV_W):
        conv = conv + ext_scr[pl.ds(SUBLANES - (CONV_W - 1) + w, rows), :] * cw_ref[w:w + 1, :]
    act_scr[...] = _silu(conv)
    ext_scr[0:SUBLANES, :] = ext_scr[pl.ds(rows, SUBLANES), :]

    sm = sm_ref[...]
    beta_all = _sigmoid(sm[:, 0:H_B])
    g_all = -jnp.exp(alr_ref[...]) * _softplus(sm[:, H_B:2 * H_B] + dtr_ref[...])
    smt = smt_ref[...]
    g_all_t = -jnp.exp(alc_ref[...]) * _softplus(smt[H_B:2 * H_B, :] + dtc_ref[...])

    ri = lax.broadcasted_iota(I32, (chunk, chunk), 0)
    ci = lax.broadcasted_iota(I32, (chunk, chunk), 1)
    tri = ri >= ci
    strict = ri > ci
    eye = (ri == ci).astype(F32)
    ltri = tri.astype(F32)
    utri = (ri <= ci).astype(F32)
    nw = nw_ref[...]

    units = [(c, h) for c in range(n_chunks) for h in range(H_B)]
    gcs = [_dot_exact_mask(ltri, g_all[c * chunk:(c + 1) * chunk, :], True) for c in range(n_chunks)]
    grs = [_dot_exact_mask(utri, g_all_t[:, c * chunk:(c + 1) * chunk], False) for c in range(n_chunks)]
    pre = []
    for c, h in units:
        r0 = c * chunk
        q = act_scr[r0:r0 + chunk, h * D_K:(h + 1) * D_K]
        k = act_scr[r0:r0 + chunk, hk + h * D_K:hk + (h + 1) * D_K]
        v = act_scr[r0:r0 + chunk, 2 * hk + h * D_V:2 * hk + (h + 1) * D_V]
        q = q * lax.rsqrt(jnp.sum(q * q, axis=-1, keepdims=True) + EPS) * (D_K ** -0.5)
        k = k * lax.rsqrt(jnp.sum(k * k, axis=-1, keepdims=True) + EPS)
        beta = beta_all[r0:r0 + chunk, h:h + 1]
        gc = gcs[c][:, h:h + 1]
        gr = grs[c][h:h + 1, :]
        g_last = gc[chunk - 1:chunk, :]
        decay = jnp.where(tri, jnp.exp(jnp.where(tri, gc - gr, 0.0)), 0.0)
        kb = k * beta
        eg = jnp.exp(gc)
        kbf = k.astype(BF16)
        kk = lax.dot_general(kb.astype(BF16), kbf, NT_DIMS, preferred_element_type=F32)
        qk = lax.dot_general(q.astype(BF16), kbf, NT_DIMS, preferred_element_type=F32)
        pre.append(dict(
            nm=-jnp.where(strict, kk * decay, 0.0),
            rhs=jnp.concatenate([v * beta, kb * eg], axis=1),
            qk=jnp.where(tri, qk * decay, 0.0).astype(BF16),
            qg=(q * eg).astype(BF16),
            kg=(k * jnp.exp(g_last - gc)).astype(BF16),
            gl=jnp.exp(g_last)))
    invs = [eye + p["nm"] for p in pre]
    pws = [p["nm"] for p in pre]
    for _ in range(n_steps - 1):
        pws = [_bdot(pw, pw) for pw in pws]
        invs = [inv + _bdot(inv, pw) for inv, pw in zip(invs, pws)]
    resid = [eye - _dot3(eye - p["nm"], inv) for p, inv in zip(pre, invs)]
    invs = [inv + _bdot(inv, r) for inv, r in zip(invs, resid)]
    uws = [_bdot(inv, p["rhs"]) for p, inv in zip(pre, invs)]

    for (c, h), p, uw in zip(units, pre, uws):
        r0 = c * chunk
        u = uw[:, 0:D_V]
        wmat = uw[:, D_V:D_V + D_K]
        s = s_ref[h]
        sb = s.astype(BF16)
        v_new = u - jnp.dot(wmat.astype(BF16), sb, preferred_element_type=F32)
        v_new_b = v_new.astype(BF16)
        o = (jnp.dot(p["qg"], sb, preferred_element_type=F32)
             + jnp.dot(p["qk"], v_new_b, preferred_element_type=F32))
        s_ref[h] = s * p["gl"] + lax.dot_general(p["kg"], v_new_b, TN_DIMS, preferred_element_type=F32)
        z = pz_ref[r0:r0 + chunk, h * D_V:(h + 1) * D_V].astype(F32)
        o = o * lax.rsqrt(jnp.mean(o * o, axis=-1, keepdims=True) + EPS) * nw * _silu(z)
        o_ref[r0:r0 + chunk, h * D_V:(h + 1) * D_V] = o.astype(o_ref.dtype)


def _gdn(gq, zg, small, s0, c0, conv_w, a_log, dt_bias, norm_w, bn, ln, chunk):
    hk = H_B * D_K
    assert H_B * D_V == hk and Z_COLS == hk
    rows = ln
    for cand in (2 * chunk, chunk):
        if cand % LANES == 0 and ln % cand == 0:
            rows = cand
            break
    assert rows % chunk == 0 and rows % SUBLANES == 0
    nblk = ln // rows
    small_t = jnp.swapaxes(small[:, 0:2 * H_B].reshape(bn, ln, 2 * H_B), 1, 2)
    alr = a_log.reshape(1, H_B)
    dtr = dt_bias.reshape(1, H_B)
    alc = a_log.reshape(H_B, 1)
    dtc = dt_bias.reshape(H_B, 1)
    row_blk = lambda b, i: b * nblk + i
    return pl.pallas_call(
        functools.partial(_gdn_body, chunk, rows),
        grid=(bn, nblk),
        in_specs=[
            pl.BlockSpec((rows, hk), lambda b, i: (row_blk(b, i), 0)),
            pl.BlockSpec((rows, hk), lambda b, i: (row_blk(b, i), 1)),
            pl.BlockSpec((rows, hk), lambda b, i: (row_blk(b, i), 2)),
            pl.BlockSpec((rows, hk), lambda b, i: (row_blk(b, i), 0)),
            pl.BlockSpec((rows, LANES), lambda b, i: (row_blk(b, i), 0)),
            pl.BlockSpec((None, 2 * H_B, rows), lambda b, i: (b, 0, i)),
            pl.BlockSpec((CONV_W, G_COLS), lambda b, i: (0, 0)),
            pl.BlockSpec((1, H_B), lambda b, i: (0, 0)),
            pl.BlockSpec((1, H_B), lambda b, i: (0, 0)),
            pl.BlockSpec((H_B, 1), lambda b, i: (0, 0)),
            pl.BlockSpec((H_B, 1), lambda b, i: (0, 0)),
            pl.BlockSpec((1, D_V), lambda b, i: (0, 0)),
            pl.BlockSpec((None, H_B, D_K, D_V), lambda b, i: (b, 0, 0, 0)),
            pl.BlockSpec((None, CONV_W - 1, G_COLS), lambda b, i: (b, 0, 0)),
        ],
        out_specs=[
            pl.BlockSpec((rows, hk), lambda b, i: (row_blk(b, i), 0)),
            pl.BlockSpec((None, H_B, D_K, D_V), lambda b, i: (b, 0, 0, 0)),
            pl.BlockSpec((None, CONV_W - 1, G_COLS), lambda b, i: (b, 0, 0)),
        ],
        out_shape=[
            jax.ShapeDtypeStruct((bn * ln, hk), BF16),
            jax.ShapeDtypeStruct(s0.shape, F32),
            jax.ShapeDtypeStruct(c0.shape, F32),
        ],
        scratch_shapes=[
            pltpu.VMEM((rows + SUBLANES, G_COLS), F32),
            pltpu.VMEM((rows, G_COLS), F32),
        ],
        compiler_params=_cparams("parallel", "arbitrary"),
        name="gdn",
    )(gq, gq, gq, zg, small, small_t, conv_w, alr, dtr, alc, dtc, norm_w, s0, c0)


def _merge_body(x_ref, oa_ref, ob_ref, ga0_ref, ga1_ref, gb0_ref, gb1_ref, wa_ref, wb_ref, wo_ref,
                nf_ref, wr_ref, br_ref, x2_ref, t_ref, lg_ref):
    ya = jnp.dot(oa_ref[...], wa_ref[...], preferred_element_type=F32)
    yb = jnp.dot(ob_ref[...], wb_ref[...], preferred_element_type=F32)
    half = ga0_ref.shape[1]
    gate = lambda ref: _sigmoid(ref[...].astype(F32))
    m0 = gate(ga0_ref) * ya[:, :half] + gate(gb0_ref) * yb[:, :half]
    m1 = gate(ga1_ref) * ya[:, half:] + gate(gb1_ref) * yb[:, half:]
    merged = jnp.concatenate([m0, m1], axis=1).astype(BF16)
    x2 = x_ref[...] + jnp.dot(merged, wo_ref[...], preferred_element_type=F32)
    x2_ref[...] = x2
    ms = jnp.mean(x2 * x2, axis=-1, keepdims=True)
    t = x2 * lax.rsqrt(ms + EPS) * nf_ref[...]
    t_ref[...] = _pack_bf16_pair(t)
    lg_ref[...] = _dot3(t, wr_ref[...]) + br_ref[...]


def _merge(x, o_a, o_b, zg, wa, wb, wo, norm_ffn, w_router, b_router):
    T, D = x.shape
    half = D // 2
    assert Z_COLS % half == 0
    gcol = Z_COLS // half
    bm = _pick(T, (256, 128, 64, 32, 16, 8))
    const = dict(pipeline_mode=pl.Buffered(1))
    return pl.pallas_call(
        _merge_body,
        grid=(T // bm,),
        in_specs=[
            pl.BlockSpec((bm, D), lambda i: (i, 0)),
            pl.BlockSpec((bm, o_a.shape[1]), lambda i: (i, 0)),
            pl.BlockSpec((bm, o_b.shape[1]), lambda i: (i, 0)),
            pl.BlockSpec((bm, half), lambda i: (i, gcol)),
            pl.BlockSpec((bm, half), lambda i: (i, gcol + 1)),
            pl.BlockSpec((bm, half), lambda i: (i, gcol + 2)),
            pl.BlockSpec((bm, half), lambda i: (i, gcol + 3)),
            pl.BlockSpec(wa.shape, lambda i: (0, 0), **const),
            pl.BlockSpec(wb.shape, lambda i: (0, 0), **const),
            pl.BlockSpec(wo.shape, lambda i: (0, 0), **const),
            pl.BlockSpec((1, D), lambda i: (0, 0)),
            pl.BlockSpec((D, LANES), lambda i: (0, 0), **const),
            pl.BlockSpec((1, LANES), lambda i: (0, 0)),
        ],
        out_specs=[
            pl.BlockSpec((bm, D), lambda i: (i, 0)),
            pl.BlockSpec((bm, half), lambda i: (i, 0)),
            pl.BlockSpec((bm, LANES), lambda i: (i, 0)),
        ],
        out_shape=[
            jax.ShapeDtypeStruct((T, D), F32),
            jax.ShapeDtypeStruct((T, half), U32),
            jax.ShapeDtypeStruct((T, LANES), F32),
        ],
        compiler_params=_cparams("parallel"),
        name="merge",
    )(x, o_a, o_b, zg, zg, zg, zg, wa, wb, wo, norm_ffn, w_router, b_router)


def _route_body(lg_ref, mi_ref, mf_ref, cnt_ref):
    i = pl.program_id(0)

    @pl.when(i == 0)
    def _():
        cnt_ref[...] = jnp.zeros(cnt_ref.shape, F32)

    lg = lg_ref[...]
    bm = lg.shape[0]
    lane = lax.broadcasted_iota(I32, lg.shape, 1).astype(F32)
    big = jnp.float32(LANES)
    neg = -jnp.inf
    gl = jnp.where(lane < N_GROUPS, lg, neg)
    gmax = jnp.max(gl, axis=1, keepdims=True)
    gidx = jnp.min(jnp.where(gl == gmax, lane, big), axis=1, keepdims=True)
    gw = 1.0 / jnp.sum(jnp.exp(gl - gmax), axis=1, keepdims=True)
    e_lo = N_GROUPS + gidx * EXPERTS_PER_GROUP
    valid = (lane >= e_lo) & (lane < e_lo + EXPERTS_PER_GROUP)
    el = jnp.where(valid, lg, neg)
    v1 = jnp.max(el, axis=1, keepdims=True)
    i1 = jnp.min(jnp.where(el == v1, lane, big), axis=1, keepdims=True)
    el2 = jnp.where(lane == i1, neg, el)
    v2 = jnp.max(el2, axis=1, keepdims=True)
    i2 = jnp.min(jnp.where(el2 == v2, lane, big), axis=1, keepdims=True)
    e21 = jnp.exp(v2 - v1)
    w1 = gw / (1.0 + e21)
    w2 = gw * e21 / (1.0 + e21)
    oh1 = (lane == i1).astype(F32)
    oh2 = (lane == i2).astype(F32)
    oh = oh1 + oh2
    rr = lax.broadcasted_iota(I32, (bm, bm), 0)
    cc = lax.broadcasted_iota(I32, (bm, bm), 1)
    before = (cc < rr).astype(BF16)
    cum = jnp.dot(before, oh.astype(BF16), preferred_element_type=F32) + cnt_ref[...]
    rank1 = jnp.sum(cum * oh1, axis=1, keepdims=True)
    rank2 = jnp.sum(cum * oh2, axis=1, keepdims=True)
    cnt_ref[...] = cnt_ref[...] + jnp.sum(oh, axis=0, keepdims=True)
    mi = jnp.where(lane == 0, i1 - N_GROUPS, 0.0)
    mi = jnp.where(lane == 1, i2 - N_GROUPS, mi)
    mi = jnp.where(lane == 2, rank1, mi)
    mi = jnp.where(lane == 3, rank2, mi)
    mi_ref[...] = mi.T[0:SUBLANES].astype(I32)
    mf_ref[...] = jnp.where(lane == 0, w1, jnp.where(lane == 1, w2, 0.0))


def _route(logits):
    T = logits.shape[0]
    bm = _pick(T, (512, 256, 128, 64, 32, 16, 8))
    return pl.pallas_call(
        _route_body,
        grid=(T // bm,),
        in_specs=[pl.BlockSpec((bm, LANES), lambda i: (i, 0))],
        out_specs=[
            pl.BlockSpec((SUBLANES, bm), lambda i: (0, i)),
            pl.BlockSpec((bm, LANES), lambda i: (i, 0)),
            pl.BlockSpec((1, LANES), lambda i: (0, 0)),
        ],
        out_shape=[
            jax.ShapeDtypeStruct((SUBLANES, T), I32),
            jax.ShapeDtypeStruct((T, LANES), F32),
            jax.ShapeDtypeStruct((1, LANES), F32),
        ],
        compiler_params=_cparams("arbitrary"),
        name="route",
    )(logits)


def _dispatch_body(bm, blk, n_experts, nbp, ps_ref, pn_ref, nu_ref, dest_ref, tp_ref, ts_ref, xb_ref,
                   zero_scr, sem, zsem):
    i = pl.program_id(0)
    n_blocks = xb_ref.shape[0] // blk

    def pad_rows(act):
        def per_expert(e, carry):
            def per_row(r, c2):
                act(pltpu.make_async_copy(zero_scr.at[pl.ds(0, 1)], xb_ref.at[pl.ds(ps_ref[e] + r, 1)], zsem))
                return c2
            return lax.fori_loop(0, pn_ref[e], per_row, carry)
        lax.fori_loop(0, n_experts, per_expert, 0)

    def tail_blocks(act):
        def per_block(b, carry):
            act(pltpu.make_async_copy(zero_scr, xb_ref.at[pl.ds(pl.multiple_of(b * blk, blk), blk)], zsem))
            return carry
        lax.fori_loop(nu_ref[0], n_blocks, per_block, 0)

    @pl.when(i == 0)
    def _():
        zero_scr[...] = jnp.zeros(zero_scr.shape, zero_scr.dtype)
        pad_rows(lambda cp: cp.start())
        tail_blocks(lambda cp: cp.start())

    def send_rows(t_ref):
        def row_copy(r, k):
            d = dest_ref[0, k * bm + r]
            return pltpu.make_async_copy(t_ref.at[pl.ds(r, 1)], xb_ref.at[pl.ds(d, 1)], sem)

        def issue(r, carry):
            for k in range(TOP_K_INNER):
                row_copy(r, k).start()
            return carry

        def drain(r, carry):
            for k in range(TOP_K_INNER):
                row_copy(r, k).wait()
            return carry

        lax.fori_loop(0, bm, issue, 0, unroll=DMA_UNROLL)
        lax.fori_loop(0, bm, drain, 0, unroll=DMA_UNROLL)

    @pl.when(i < nbp)
    def _():
        send_rows(tp_ref)

    @pl.when(i >= nbp)
    def _():
        send_rows(ts_ref)

    @pl.when(i == 0)
    def _():
        pad_rows(lambda cp: cp.wait())
        tail_blocks(lambda cp: cp.wait())


def _dispatch(t_p, t_s, dest, pad_start, pad_len, n_used, n_rows, blk):
    (tp, D), ts = t_p.shape, t_s.shape[0]
    bm = _pick(math.gcd(tp, ts), (256, 128, 64, 32, 16, 8))
    nbp, nbs = tp // bm, ts // bm
    dest3 = _dest_blocks(dest, bm)
    grid_spec = pltpu.PrefetchScalarGridSpec(
        num_scalar_prefetch=3,
        grid=(nbp + nbs,),
        in_specs=[
            pl.BlockSpec((None, 1, TOP_K_INNER * bm), lambda i, ps, pn, nu: (i, 0, 0),
                         memory_space=pltpu.SMEM),
            pl.BlockSpec((bm, D), lambda i, ps, pn, nu: (jnp.minimum(i, nbp - 1), 0)),
            pl.BlockSpec((bm, D), lambda i, ps, pn, nu: (jnp.maximum(i - nbp, 0), 0)),
        ],
        out_specs=pl.BlockSpec(memory_space=pl.ANY),
        scratch_shapes=[
            pltpu.VMEM((blk, D), t_p.dtype),
            pltpu.SemaphoreType.DMA(()),
            pltpu.SemaphoreType.DMA(()),
        ],
    )
    return pl.pallas_call(
        functools.partial(_dispatch_body, bm, blk, pad_start.shape[0], nbp),
        grid_spec=grid_spec,
        out_shape=jax.ShapeDtypeStruct((n_rows, D), t_p.dtype),
        compiler_params=_cparams("arbitrary"),
        name="dispatch",
    )(pad_start, pad_len, n_used, dest3, t_p, t_s)


def _expert_body(be_ref, nu_ref, x_ref, wg_ref, wu_ref, wd_ref, y_ref, wg_scr, wu_scr, wd_scr):
    i = pl.program_id(0)

    @pl.when(i < nu_ref[0])
    def _():
        @pl.when(jnp.logical_or(i == 0, be_ref[i] != be_ref[jnp.maximum(i - 1, 0)]))
        def _():
            wg_scr[...] = wg_ref[...].astype(BF16)
            wu_scr[...] = wu_ref[...].astype(BF16)
            wd_scr[...] = wd_ref[...].astype(BF16)

        x = _unpack_bf16_pair(x_ref[...])
        g = jnp.dot(x, wg_scr[...], preferred_element_type=F32)
        u = jnp.dot(x, wu_scr[...], preferred_element_type=F32)
        hmid = (_silu(g) * u).astype(BF16)
        y_ref[...] = jnp.dot(hmid, wd_scr[...], preferred_element_type=F32)

    @pl.when(i >= nu_ref[0])
    def _():
        y_ref[...] = jnp.zeros(y_ref.shape, F32)


def _experts(xb, blk_exp, n_used, w_gate, w_up, w_down, blk):
    P = xb.shape[0]
    D, de = w_gate.shape[1:]
    assert xb.shape[1] * 2 == D
    n_blocks = P // blk
    last = lambda i, nu: jnp.minimum(i, nu[0] - 1)
    grid_spec = pltpu.PrefetchScalarGridSpec(
        num_scalar_prefetch=2,
        grid=(n_blocks,),
        in_specs=[
            pl.BlockSpec((blk, D // 2), lambda i, be, nu: (last(i, nu), 0)),
            pl.BlockSpec((None, D, de), lambda i, be, nu: (be[last(i, nu)], 0, 0)),
            pl.BlockSpec((None, D, de), lambda i, be, nu: (be[last(i, nu)], 0, 0)),
            pl.BlockSpec((None, de, D), lambda i, be, nu: (be[last(i, nu)], 0, 0)),
        ],
        out_specs=pl.BlockSpec((blk, D), lambda i, be, nu: (i, 0)),
        scratch_shapes=[
            pltpu.VMEM((D, de), BF16),
            pltpu.VMEM((D, de), BF16),
            pltpu.VMEM((de, D), BF16),
        ],
    )
    return pl.pallas_call(
        _expert_body,
        grid_spec=grid_spec,
        out_shape=jax.ShapeDtypeStruct((P, D), F32),
        compiler_params=_cparams("arbitrary"),
        name="experts",
    )(blk_exp, n_used, xb, w_gate, w_up, w_down)


def _combine_body(bm, final, dest_ref, x2_ref, mf_ref, nw_ref, yb_ref, o_ref, y0_scr, y1_scr, sem):
    bufs = (y0_scr, y1_scr)

    def row_copy(r, k):
        d = dest_ref[0, k * bm + r]
        return pltpu.make_async_copy(yb_ref.at[pl.ds(d, 1)], bufs[k].at[pl.ds(r, 1)], sem)

    def issue(r, carry):
        for k in range(TOP_K_INNER):
            row_copy(r, k).start()
        return carry

    lax.fori_loop(0, bm, issue, 0, unroll=DMA_UNROLL)

    def drain(r, carry):
        for k in range(TOP_K_INNER):
            row_copy(r, k).wait()
        return carry

    lax.fori_loop(0, bm, drain, 0, unroll=DMA_UNROLL)

    mf = mf_ref[...]
    x3 = x2_ref[...] + y0_scr[...] * mf[:, 0:1] + y1_scr[...] * mf[:, 1:2]
    if final:
        ms = jnp.mean(x3 * x3, axis=-1, keepdims=True)
        x3 = x3 * lax.rsqrt(ms + EPS) * nw_ref[...]
    o_ref[...] = x3


def _combine(x2, yb, dest, mf, norm_final, final):
    T, D = x2.shape
    bm = _pick(T, (256, 128, 64, 32, 16, 8))
    nblk = T // bm
    dest3 = _dest_blocks(dest, bm)
    return pl.pallas_call(
        functools.partial(_combine_body, bm, final),
        grid=(nblk,),
        in_specs=[
            pl.BlockSpec((None, 1, TOP_K_INNER * bm), lambda i: (i, 0, 0), memory_space=pltpu.SMEM),
            pl.BlockSpec((bm, D), lambda i: (i, 0)),
            pl.BlockSpec((bm, LANES), lambda i: (i, 0)),
            pl.BlockSpec((1, D), lambda i: (0, 0)),
            pl.BlockSpec(memory_space=pl.ANY),
        ],
        out_specs=pl.BlockSpec((bm, D), lambda i: (i, 0)),
        out_shape=jax.ShapeDtypeStruct((T, D), F32),
        scratch_shapes=[
            pltpu.VMEM((bm, D), F32),
            pltpu.VMEM((bm, D), F32),
            pltpu.SemaphoreType.DMA(()),
        ],
        compiler_params=_cparams("arbitrary"),
        name="combine",
    )(dest3, x2, mf, norm_final, yb)


def _rotary_tables(pos):
    half = ROT_DIM // 2
    inv = ROPE_THETA ** (-jnp.arange(0, ROT_DIM, 2, dtype=F32) / ROT_DIM)
    ang = pos.astype(F32)[:, None] * inv[None, :]
    cos = jnp.cos(ang)
    sin = jnp.sin(ang)
    n = pos.shape[0]
    ones = jnp.ones((n, D_A - ROT_DIM), F32)
    zeros = jnp.zeros((n, D_A - ROT_DIM), F32)
    zh = jnp.zeros((n, half), F32)
    cos64 = jnp.concatenate([cos, cos, ones], axis=1)
    sa64 = jnp.concatenate([zh, sin, zeros], axis=1)
    sb64 = jnp.concatenate([-sin, zh, zeros], axis=1)
    reps = LANES // D_A
    return jnp.tile(cos64, (1, reps)), jnp.tile(sa64, (1, reps)), jnp.tile(sb64, (1, reps))


def _dest_blocks(dest, bm):
    nblk = dest.shape[1] // bm
    return jnp.swapaxes(dest.reshape(TOP_K_INNER, nblk, bm), 0, 1).reshape(nblk, 1, TOP_K_INNER * bm)


def _moe_slots(mi, counts, n_experts, blk, n_assign):
    eid = mi[0:TOP_K_INNER]
    rank = mi[TOP_K_INNER:2 * TOP_K_INNER]
    cnt = counts[0, N_GROUPS:N_GROUPS + n_experts].astype(I32)
    pc = (cnt + blk - 1) // blk * blk
    pend = jnp.cumsum(pc)
    pstart = pend - pc
    experts = jnp.arange(n_experts, dtype=I32)[:, None, None]
    first = jnp.sum(jnp.where(eid[None] == experts, pstart[:, None, None], 0), axis=0)
    dest = (first + rank).astype(I32)
    n_blocks = -(-n_assign // blk) + n_experts
    blk_start = jnp.arange(n_blocks, dtype=I32) * blk
    blk_exp = jnp.minimum(jnp.sum((pend[None, :] <= blk_start[:, None]).astype(I32), axis=1), n_experts - 1)
    n_used = jnp.maximum(pend[-1] // blk, 1).astype(I32).reshape(1)
    return dest, blk_exp.astype(I32), n_used, n_blocks, (pstart + cnt).astype(I32), (pc - cnt).astype(I32)


def _layer(xp, xs, layer, dims, cache_k, cache_v, state_delta, state_conv, prm, norm_final, final):
    (norm_mix, w_in, lq1, lk1, lq2, lk2, subln_w, conv_w, a_log, dt_bias, gdn_norm_w,
     w_proj_a, w_proj_b, w_out, norm_ffn, w_gr, b_gr, w_er, b_er, w_eg, w_eu, w_ed) = prm
    bp, lp, bs, ls, past = dims
    D = xp.shape[1]
    tp = bp * lp
    n_qkv = 2 * A_COLS + V_COLS + G_COLS + Z_COLS
    n_small = 2 * H_B
    lam_init = 0.8 - 0.6 * math.exp(-0.3 * layer)

    w_t = jnp.swapaxes(w_in, 0, 1)
    w_a = w_t[:n_qkv].astype(BF16)
    w_b = w_t[n_qkv + n_small:].astype(BF16)
    w_small = jnp.pad(w_t[n_qkv:n_qkv + n_small], ((0, LANES - n_small), (0, 0))).astype(BF16)
    nm = norm_mix.reshape(1, D)
    pp = _inproj(xp, nm, _rotary_tables(jnp.tile(jnp.arange(lp), bp)), w_a, w_b, w_small)
    ps = _inproj(xs, nm, _rotary_tables(jnp.tile(past + jnp.arange(ls), bs)), w_a, w_b, w_small)

    lamp = jnp.stack([lq1, lk1, lq2, lk2]).astype(F32)
    sw = subln_w.reshape(1, D_VA)
    oa_p = _attn_prompt(pp.q_bf, pp.k_bf, pp.v_t, lamp, sw, lam_init, bp, lp)
    oa_s = _attn_sample(ps.q_bf, ps.k_bf, ps.v_new, cache_k.reshape(bs, past * H_A, 2 * D_A),
                        cache_v.reshape(bs, past * H_A, D_VA), lamp, sw, lam_init, bs, ls)

    nw = gdn_norm_w.reshape(1, D_V)
    ob_p, s_p, c_p = _gdn(pp.gq, pp.zg, pp.small, jnp.zeros((bp, H_B, D_K, D_V), F32),
                          jnp.zeros((bp, CONV_W - 1, G_COLS), F32), conv_w, a_log, dt_bias, nw,
                          bp, lp, _pick(lp, (GDN_CHUNK, CHUNK)))
    ob_s, s_s, c_s = _gdn(ps.gq, ps.zg, ps.small, state_delta, state_conv, conv_w, a_log, dt_bias, nw,
                          bs, ls, ls)

    n_experts = w_er.shape[1]
    w_router = jnp.pad(jnp.concatenate([w_gr, w_er], axis=1), ((0, 0), (0, LANES - N_GROUPS - n_experts)))
    b_router = jnp.pad(jnp.concatenate([b_gr, b_er]), (0, LANES - N_GROUPS - n_experts)).reshape(1, LANES)
    mw = (w_proj_a.astype(BF16), w_proj_b.astype(BF16), w_out.astype(BF16), norm_ffn.reshape(1, D),
          w_router, b_router)
    x2_p, t_p, lg_p = _merge(xp, oa_p, ob_p, pp.zg, *mw)
    x2_s, t_s, lg_s = _merge(xs, oa_s, ob_s, ps.zg, *mw)

    blk = MOE_ROWS
    mi, mf, counts = _route(jnp.concatenate([lg_p, lg_s], axis=0))
    n_tok = tp + bs * ls
    dest, blk_exp, n_used, n_blocks, pad_start, pad_len = _moe_slots(
        mi, counts, n_experts, blk, n_tok * TOP_K_INNER)
    xb = _dispatch(t_p, t_s, dest, pad_start, pad_len, n_used, n_blocks * blk, blk)
    yb = _experts(xb, blk_exp, n_used, w_eg, w_eu, w_ed, blk)
    nf = norm_final.reshape(1, D)
    y_p = _combine(x2_p, yb, dest[:, :tp], mf[:tp], nf, final)
    y_s = _combine(x2_s, yb, dest[:, tp:], mf[tp:], nf, final)
    return y_p, y_s, pp.k_new, pp.v_new, s_p, c_p, ps.k_new, ps.v_new, s_s, c_s


def kernel(x_prompt, x_sample, cache_k, cache_v, state_delta, state_conv, norm_mix, w_in, lambda_q1, lambda_k1, lambda_q2, lambda_k2, subln_w, conv_w, a_log, dt_bias, gdn_norm_w, w_proj_a, w_proj_b, w_out, norm_ffn, w_group_router, b_group_router, w_expert_router, b_expert_router, w_exp_gate, w_exp_up, w_exp_down, norm_final):
    bp, lp, D = x_prompt.shape
    bs, ls, _ = x_sample.shape
    depth = cache_k.shape[0]
    past = cache_k.shape[2]
    dims = (bp, lp, bs, ls, past)
    xp = x_prompt.reshape(bp * lp, D)
    xs = x_sample.reshape(bs * ls, D)
    kp, vp, sp, cp, ksm, vsm, ssm, csm = [], [], [], [], [], [], [], []
    for l in range(depth):
        prm = (norm_mix[l], w_in[l], lambda_q1[l], lambda_k1[l], lambda_q2[l], lambda_k2[l], subln_w[l],
               conv_w[l], a_log[l], dt_bias[l], gdn_norm_w[l], w_proj_a[l], w_proj_b[l], w_out[l],
               norm_ffn[l], w_group_router[l], b_group_router[l], w_expert_router[l], b_expert_router[l],
               w_exp_gate[l], w_exp_up[l], w_exp_down[l])
        xp, xs, k_p, v_p, s_p, c_p, k_s, v_s, s_s, c_s = _layer(
            xp, xs, l, dims, cache_k[l], cache_v[l], state_delta[l], state_conv[l], prm,
            norm_final, l == depth - 1)
        kp.append(k_p.reshape(bp, lp, H_A, 2 * D_A))
        vp.append(v_p.reshape(bp, lp, H_A, D_VA))
        ksm.append(k_s.reshape(bs, ls, H_A, 2 * D_A))
        vsm.append(v_s.reshape(bs, ls, H_A, D_VA))
        sp.append(s_p)
        cp.append(c_p)
        ssm.append(s_s)
        csm.append(c_s)
    return (xp.reshape(bp, lp, D), xs.reshape(bs, ls, D), jnp.stack(kp), jnp.stack(vp), jnp.stack(sp),
            jnp.stack(cp), jnp.stack(ksm), jnp.stack(vsm), jnp.stack(ssm), jnp.stack(csm))
```

```python
import collections
import functools
import math

import jax
import jax.numpy as jnp
from jax import lax
from jax.experimental import pallas as pl
from jax.experimental.pallas import tpu as pltpu

F32 = jnp.float32
BF16 = jnp.bfloat16
I32 = jnp.int32
U32 = jnp.uint32

CHUNK = 64
GDN_CHUNK = 128
H_A = 8
D_A = 64
D_VA = 2 * D_A
ROT_DIM = D_A // 4
ROPE_THETA = 500000.0
H_B = 8
D_K = 128
D_V = 128
CONV_W = 4
N_GROUPS = 4
EXPERTS_PER_GROUP = 8
TOP_K_INNER = 2
EPS = 1e-6

LANES = 128
SUBLANES = 8
V7X_VMEM_BYTES = 64 * 1024 * 1024
VMEM_LIMIT_BYTES = V7X_VMEM_BYTES - 8 * 1024 * 1024

Q_SCALE = (D_A ** -0.5) * math.log2(math.e)
ONES_ROWS = 16
INPROJ_BN = 512
MOE_ROWS = 256
DMA_UNROLL = 8

NT_DIMS = (((1,), (1,)), ((), ()))
TN_DIMS = (((0,), (0,)), ((), ()))

A_COLS = H_A * 2 * D_A
V_COLS = H_A * D_VA
G_COLS = 2 * H_B * D_K + H_B * D_V
Z_COLS = H_B * D_V


def _pick(n, cands):
    for c in cands:
        if n % c == 0:
            return c
    raise ValueError(f"no block size in {cands} divides {n}")


def _cparams(*sem):
    return pltpu.CompilerParams(dimension_semantics=sem, vmem_limit_bytes=VMEM_LIMIT_BYTES)


def _sigmoid(x):
    return 1.0 / (1.0 + jnp.exp(-x))


def _silu(x):
    return x * _sigmoid(x)


def _softplus(x):
    return jnp.maximum(x, 0.0) + jnp.log1p(jnp.exp(-jnp.abs(x)))


def _pack_bf16_pair(x):
    bits = lax.bitcast_convert_type(x.astype(BF16).astype(F32), U32)
    half = x.shape[1] // 2
    return (bits[:, :half] >> 16) | (bits[:, half:] & jnp.uint32(0xFFFF0000))


def _unpack_bf16_pair(w):
    lo = lax.bitcast_convert_type(w << 16, F32)
    hi = lax.bitcast_convert_type(w & jnp.uint32(0xFFFF0000), F32)
    return jnp.concatenate([lo, hi], axis=1).astype(BF16)


def _bdot(a, b):
    return jnp.dot(a.astype(BF16), b.astype(BF16), preferred_element_type=F32)


def _split2(a):
    hi = a.astype(BF16)
    lo = (a - hi.astype(F32)).astype(BF16)
    return hi, lo


def _dot3(a, b):
    ah, al = _split2(a)
    bh, bl = _split2(b)
    d = functools.partial(jnp.dot, preferred_element_type=F32)
    return d(ah, bh) + d(ah, bl) + d(al, bh)


def _dot_exact_mask(mask, x, mask_is_lhs):
    hi = x.astype(BF16)
    r1 = x - hi.astype(F32)
    mid = r1.astype(BF16)
    lo = (r1 - mid.astype(F32)).astype(BF16)
    mb = mask.astype(BF16)
    d = functools.partial(jnp.dot, preferred_element_type=F32)
    if mask_is_lhs:
        return d(mb, hi) + d(mb, mid) + d(mb, lo)
    return d(hi, mb) + d(mid, mb) + d(lo, mb)


InProj = collections.namedtuple("InProj", "small k_new v_new q_bf k_bf v_t gq zg")


def _inproj_body(o_k, o_v, o_g, o_z, nb_wa, x_ref, nw_ref, cos_ref, sa_ref, sb_ref, wa_ref, wb_ref, ws_ref,
                 os_ref, k_ref, v_ref, qb_ref, kb_ref, vt_ref, gq_ref, zg_ref, h_scr):
    j = pl.program_id(1)
    bm = h_scr.shape[0]
    half = wa_ref.shape[0] // 2
    lpb = half // LANES

    @pl.when(j == 0)
    def _():
        x = x_ref[...]
        ms = jnp.mean(x * x, axis=-1, keepdims=True)
        h = (x * lax.rsqrt(ms + EPS) * nw_ref[...]).astype(BF16)
        h_scr[...] = h
        os_ref[...] = lax.dot_general(h, ws_ref[...], NT_DIMS, preferred_element_type=F32)

    def halves(w_ref):
        return [lax.dot_general(h_scr[...], w_ref[c * half:(c + 1) * half, :], NT_DIMS,
                                preferred_element_type=F32) for c in range(2)]

    def lane_blocks(w_ref):
        for c2, acc in enumerate(halves(w_ref)):
            for c in range(lpb):
                yield c2 * lpb + c, acc[:, c * LANES:(c + 1) * LANES]

    def rotated(blk):
        return (blk * cos_ref[...] + pltpu.roll(blk, ROT_DIM // 2, 1) * sa_ref[...]
                + pltpu.roll(blk, LANES - ROT_DIM // 2, 1) * sb_ref[...])

    @pl.when(j < o_k)
    def _():
        for lb, blk in lane_blocks(wa_ref):
            qb_ref[:, lb * LANES:(lb + 1) * LANES] = (rotated(blk) * Q_SCALE).astype(BF16)

    hpb = 2 * lpb
    for jj in range(o_v - o_k):
        @pl.when(j == o_k + jj)
        def _(jj=jj):
            for lb, blk in lane_blocks(wa_ref):
                rot = rotated(blk)
                k_ref[pl.ds(jj * hpb + lb, bm, stride=H_A), :] = rot
                kb_ref[:, lb * LANES:(lb + 1) * LANES] = rot.astype(BF16)

    for jj in range(o_g - o_v):
        @pl.when(j == o_v + jj)
        def _(jj=jj):
            for lb, blk in lane_blocks(wa_ref):
                v_ref[pl.ds(jj * hpb + lb, bm, stride=H_A), :] = blk
                vt_ref[lb, 0:D_VA, :] = blk.T.astype(BF16)
                vt_ref[lb, D_VA:D_VA + ONES_ROWS, :] = jnp.ones((ONES_ROWS, bm), BF16)

    @pl.when(jnp.logical_and(j >= o_g, j < o_z))
    def _():
        for c, acc in enumerate(halves(wa_ref)):
            gq_ref[:, c * half:(c + 1) * half] = acc

    @pl.when(jnp.logical_and(j >= o_z, j < nb_wa))
    def _():
        for c, acc in enumerate(halves(wa_ref)):
            zg_ref[:, c * half:(c + 1) * half] = acc.astype(BF16)

    @pl.when(j >= nb_wa)
    def _():
        for c, acc in enumerate(halves(wb_ref)):
            zg_ref[:, c * half:(c + 1) * half] = acc.astype(BF16)


def _inproj(x, norm_w, tables, w_a, w_b, w_small):
    T, D = x.shape
    bn = INPROJ_BN
    zg_cols = Z_COLS + 2 * D
    assert A_COLS % bn == 0 and V_COLS % bn == 0 and G_COLS % bn == 0 and Z_COLS % bn == 0
    assert (bn // 2) % D_VA == 0 and D_VA == LANES and (2 * D) % bn == 0
    assert w_a.shape[0] == 2 * A_COLS + V_COLS + G_COLS + Z_COLS and w_b.shape[0] == 2 * D
    nb_a, nb_v, nb_g, nb_z = A_COLS // bn, V_COLS // bn, G_COLS // bn, zg_cols // bn
    o_k, o_v = nb_a, 2 * nb_a
    o_g = o_v + nb_v
    o_z = o_g + nb_g
    nb_wa, nb_wb = w_a.shape[0] // bn, w_b.shape[0] // bn
    bm = _pick(T, (1024, 512, 256, 128))
    hpb = bn // D_VA

    def col(lo, n):
        return lambda i, j: (i, jnp.clip(j - lo, 0, n - 1))

    cos_t, sa_t, sb_t = tables
    outs = pl.pallas_call(
        functools.partial(_inproj_body, o_k, o_v, o_g, o_z, nb_wa),
        grid=(T // bm, nb_wa + nb_wb),
        in_specs=[
            pl.BlockSpec((bm, D), lambda i, j: (i, 0), pipeline_mode=pl.Buffered(1)),
            pl.BlockSpec((1, D), lambda i, j: (0, 0)),
            pl.BlockSpec((bm, LANES), lambda i, j: (i, 0)),
            pl.BlockSpec((bm, LANES), lambda i, j: (i, 0)),
            pl.BlockSpec((bm, LANES), lambda i, j: (i, 0)),
            pl.BlockSpec((bn, D), lambda i, j: (jnp.minimum(j, nb_wa - 1), 0)),
            pl.BlockSpec((bn, D), lambda i, j: (jnp.maximum(j - nb_wa, 0), 0)),
            pl.BlockSpec((LANES, D), lambda i, j: (0, 0)),
        ],
        out_specs=[
            pl.BlockSpec((bm, LANES), lambda i, j: (i, 0)),
            pl.BlockSpec((bm * H_A, D_VA), lambda i, j: (i, 0)),
            pl.BlockSpec((bm * H_A, D_VA), lambda i, j: (i, 0)),
            pl.BlockSpec((bm, bn), col(0, nb_a)),
            pl.BlockSpec((bm, bn), col(o_k, nb_a)),
            pl.BlockSpec((hpb, D_VA + ONES_ROWS, bm), lambda i, j: (jnp.clip(j - o_v, 0, nb_v - 1), 0, i)),
            pl.BlockSpec((bm, bn), col(o_g, nb_g)),
            pl.BlockSpec((bm, bn), col(o_z, nb_z)),
        ],
        out_shape=[
            jax.ShapeDtypeStruct((T, LANES), F32),
            jax.ShapeDtypeStruct((T * H_A, 2 * D_A), F32),
            jax.ShapeDtypeStruct((T * H_A, D_VA), F32),
            jax.ShapeDtypeStruct((T, A_COLS), BF16),
            jax.ShapeDtypeStruct((T, A_COLS), BF16),
            jax.ShapeDtypeStruct((H_A, D_VA + ONES_ROWS, T), BF16),
            jax.ShapeDtypeStruct((T, G_COLS), F32),
            jax.ShapeDtypeStruct((T, zg_cols), BF16),
        ],
        scratch_shapes=[pltpu.VMEM((bm, D), BF16)],
        compiler_params=_cparams("parallel", "arbitrary"),
        name="inproj",
    )(x, norm_w, cos_t, sa_t, sb_t, w_a, w_b, w_small)
    return InProj(*outs)


def _diff_lambda(lamp, lam_init):
    s1 = jnp.sum(lamp[0:1] * lamp[1:2], axis=1, keepdims=True)
    s2 = jnp.sum(lamp[2:3] * lamp[3:4], axis=1, keepdims=True)
    return jnp.exp(s1) - jnp.exp(s2) + lam_init


def _subln(o, w, lam_init):
    ms = jnp.mean(o * o, axis=-1, keepdims=True)
    return o * lax.rsqrt(ms + EPS) * w * (1.0 - lam_init)


def _attn_prompt_body(lam_init, bq, q_ref, k_ref, vt_ref, lamp_ref, swc_ref, o_ref,
                      m_scr, acc_scr, s_scr):
    qi = pl.program_id(2)
    q = q_ref[...]
    lane = lax.broadcasted_iota(I32, q.shape, 1)
    qz = [jnp.where(lane < D_A, q, jnp.zeros_like(q)), jnp.where(lane >= D_A, q, jnp.zeros_like(q))]
    m_scr[...] = jnp.full(m_scr.shape, -jnp.inf, F32)
    acc_scr[...] = jnp.zeros(acc_scr.shape, F32)
    qry_pos = lax.broadcasted_iota(I32, (1, bq), 1)

    def scores(j, slot):
        kb = k_ref[pl.ds(pl.multiple_of(j * bq, bq), bq), :]
        for m in range(2):
            s_scr[slot, m] = lax.dot_general(kb, qz[m], NT_DIMS, preferred_element_type=F32)

    def accumulate(j, slot, masked):
        vb = vt_ref[:, pl.ds(pl.multiple_of(j * bq, bq), bq)]
        for m in range(2):
            if masked:
                s = jnp.concatenate(
                    [jnp.where(qry_pos >= c * CHUNK, s_scr[slot, m, c * CHUNK:(c + 1) * CHUNK, :], -jnp.inf)
                     for c in range(bq // CHUNK)], axis=0)
            else:
                s = s_scr[slot, m]
            m_prev = m_scr[m]
            m_new = jnp.maximum(m_prev, jnp.max(s, axis=0, keepdims=True))
            alpha = jnp.exp2(m_prev - m_new)
            p = jnp.exp2(s - m_new).astype(BF16)
            acc_scr[m] = alpha * acc_scr[m] + jnp.dot(vb, p, preferred_element_type=F32)
            m_scr[m] = m_new

    scores(0, 0)

    def two_full_blocks(t, carry):
        j = 2 * t
        scores(j + 1, 1)
        accumulate(j, 0, False)
        scores(j + 2, 0)
        accumulate(j + 1, 1, False)
        return carry

    lax.fori_loop(0, qi // 2, two_full_blocks, 0)
    j0 = 2 * (qi // 2)

    @pl.when(j0 == qi)
    def _():
        accumulate(qi, 0, True)

    @pl.when(j0 != qi)
    def _():
        scores(qi, 1)
        accumulate(j0, 0, False)
        accumulate(qi, 1, True)

    lam = _diff_lambda(lamp_ref[...], lam_init)
    a0 = acc_scr[0]
    a1 = acc_scr[1]
    o_t = a0[0:D_VA] / a0[D_VA:D_VA + 1] - lam * (a1[0:D_VA] / a1[D_VA:D_VA + 1])
    ms = jnp.mean(o_t * o_t, axis=0, keepdims=True)
    o_t = o_t * lax.rsqrt(ms + EPS) * swc_ref[...] * (1.0 - lam_init)
    o_ref[...] = o_t.T.astype(o_ref.dtype)


def _attn_prompt(q_bf, k_bf, v_t, lamp, subln_w, lam_init, bp, lp):
    bq = _pick(lp, (512, 256, 128))
    nq = lp // bq
    vrows = v_t.shape[1]
    return pl.pallas_call(
        functools.partial(_attn_prompt_body, lam_init, bq),
        grid=(bp, H_A, nq),
        in_specs=[
            pl.BlockSpec((bq, LANES), lambda b, h, i: (b * nq + i, h)),
            pl.BlockSpec((lp, LANES), lambda b, h, i: (b, h)),
            pl.BlockSpec((None, vrows, lp), lambda b, h, i: (h, 0, b)),
            pl.BlockSpec((4, D_A), lambda b, h, i: (0, 0)),
            pl.BlockSpec((D_VA, 1), lambda b, h, i: (0, 0)),
        ],
        out_specs=pl.BlockSpec((bq, LANES), lambda b, h, i: (b * nq + i, h)),
        out_shape=jax.ShapeDtypeStruct((bp * lp, V_COLS), BF16),
        scratch_shapes=[
            pltpu.VMEM((2, 1, bq), F32),
            pltpu.VMEM((2, vrows, bq), F32),
            pltpu.VMEM((2, 2, bq, bq), F32),
        ],
        compiler_params=_cparams("parallel", "parallel", "arbitrary"),
        name="attn_prompt",
    )(q_bf, k_bf, v_t, lamp, subln_w.reshape(D_VA, 1))


def _attn_sample_body(lam_init, q_ref, kn_ref, vn_ref, kc_ref, vc_ref, lamp_ref, sw_ref, o_ref,
                      m_scr, l_scr, acc_scr):
    pj = pl.program_id(1)
    npos = kc_ref.shape[0] // H_A
    ls = q_ref.shape[0]

    @pl.when(pj == 0)
    def _():
        m_scr[...] = jnp.full(m_scr.shape, -jnp.inf, F32)
        l_scr[...] = jnp.zeros(l_scr.shape, F32)
        acc_scr[...] = jnp.zeros(acc_scr.shape, F32)

    lane = lax.broadcasted_iota(I32, (ls, D_VA), 1)

    def q_maps(h):
        q = q_ref[:, h * D_VA:(h + 1) * D_VA]
        return jnp.concatenate([jnp.where(lane < D_A, q, jnp.zeros_like(q)),
                                jnp.where(lane >= D_A, q, jnp.zeros_like(q))], axis=0)

    def update_all(keys_of, vals_of):
        heads = range(H_A)
        ss = [lax.dot_general(q_maps(h), keys_of(h), NT_DIMS, preferred_element_type=F32) for h in heads]
        m_new = [jnp.maximum(m_scr[h], jnp.max(ss[h], axis=1, keepdims=True)) for h in heads]
        ps = [jnp.exp2(ss[h] - m_new[h]) for h in heads]
        pv = [jnp.dot(ps[h].astype(BF16), vals_of(h), preferred_element_type=F32) for h in heads]
        for h in heads:
            alpha = jnp.exp2(m_scr[h] - m_new[h])
            l_scr[h] = alpha * l_scr[h] + jnp.sum(ps[h], axis=1, keepdims=True)
            acc_scr[h] = alpha * acc_scr[h] + pv[h]
            m_scr[h] = m_new[h]

    update_all(lambda h: kc_ref[pl.ds(h, npos, stride=H_A), :].astype(BF16),
               lambda h: vc_ref[pl.ds(h, npos, stride=H_A), :].astype(BF16))

    @pl.when(pj == pl.num_programs(1) - 1)
    def _():
        lam = _diff_lambda(lamp_ref[...], lam_init)
        sw = sw_ref[...]
        update_all(lambda h: kn_ref[:, h * D_VA:(h + 1) * D_VA],
                   lambda h: vn_ref[pl.ds(h, ls, stride=H_A), :].astype(BF16))
        for h in range(H_A):
            both = acc_scr[h] / l_scr[h]
            o = both[0:ls] - lam * both[ls:2 * ls]
            o_ref[:, h * D_VA:(h + 1) * D_VA] = _subln(o, sw, lam_init).astype(o_ref.dtype)


def _attn_sample(q_bf, k_bf, v_new, cache_k, cache_v, lamp, subln_w, lam_init, bs, ls):
    past = cache_k.shape[1] // H_A
    pb = _pick(past, (1024, 512, 256, 128, 64))
    assert ls % (2 * SUBLANES) == 0
    return pl.pallas_call(
        functools.partial(_attn_sample_body, lam_init),
        grid=(bs, past // pb),
        in_specs=[
            pl.BlockSpec((ls, A_COLS), lambda b, j: (b, 0)),
            pl.BlockSpec((ls, A_COLS), lambda b, j: (b, 0)),
            pl.BlockSpec((ls * H_A, D_VA), lambda b, j: (b, 0)),
            pl.BlockSpec((None, pb * H_A, D_VA), lambda b, j: (b, j, 0)),
            pl.BlockSpec((None, pb * H_A, D_VA), lambda b, j: (b, j, 0)),
            pl.BlockSpec((4, D_A), lambda b, j: (0, 0)),
            pl.BlockSpec((1, D_VA), lambda b, j: (0, 0)),
        ],
        out_specs=pl.BlockSpec((ls, V_COLS), lambda b, j: (b, 0)),
        out_shape=jax.ShapeDtypeStruct((bs * ls, V_COLS), BF16),
        scratch_shapes=[
            pltpu.VMEM((H_A, 2 * ls, 1), F32),
            pltpu.VMEM((H_A, 2 * ls, 1), F32),
            pltpu.VMEM((H_A, 2 * ls, D_VA), F32),
        ],
        compiler_params=_cparams("parallel", "arbitrary"),
        name="attn_sample",
    )(q_bf, k_bf, v_new, cache_k, cache_v, lamp, subln_w)


def _gdn_body(chunk, rows, pq_ref, pk_ref, pv_ref, pz_ref, sm_ref, smt_ref, cw_ref, alr_ref, dtr_ref,
              alc_ref, dtc_ref, nw_ref, s0_ref, c0_ref, o_ref, s_ref, cn_ref, ext_scr, act_scr):
    i = pl.program_id(1)
    nblk = pl.num_programs(1)
    hk = H_B * D_K
    hv = H_B * D_V
    n_chunks = rows // chunk
    n_steps = int(math.log2(chunk))
    assert 2 ** n_steps == chunk

    @pl.when(i == 0)
    def _():
        s_ref[...] = s0_ref[...]
        ext_scr[0:SUBLANES, :] = jnp.zeros((SUBLANES, ext_scr.shape[1]), F32)
        ext_scr[pl.ds(SUBLANES - (CONV_W - 1), CONV_W - 1), :] = c0_ref[...]

    ext_scr[pl.ds(SUBLANES, rows), 0:hk] = pq_ref[...]
    ext_scr[pl.ds(SUBLANES, rows), hk:2 * hk] = pk_ref[...]
    ext_scr[pl.ds(SUBLANES, rows), 2 * hk:2 * hk + hv] = pv_ref[...]

    @pl.when(i == nblk - 1)
    def _():
        cn_ref[...] = ext_scr[pl.ds(SUBLANES + rows - (CONV_W - 1), CONV_W - 1), :]

    conv = jnp.zeros((rows, ext_scr.shape[1]), F32)
    for w in range(CONV_W):
        conv = conv + ext_scr[pl.ds(SUBLANES - (CONV_W - 1) + w, rows), :] * cw_ref[w:w + 1, :]
    act_scr[...] = _silu(conv)
    ext_scr[0:SUBLANES, :] = ext_scr[pl.ds(rows, SUBLANES), :]

    sm = sm_ref[...]
    beta_all = _sigmoid(sm[:, 0:H_B])
    g_all = -jnp.exp(alr_ref[...]) * _softplus(sm[:, H_B:2 * H_B] + dtr_ref[...])
    smt = smt_ref[...]
    g_all_t = -jnp.exp(alc_ref[...]) * _softplus(smt[H_B:2 * H_B, :] + dtc_ref[...])

    ri = lax.broadcasted_iota(I32, (chunk, chunk), 0)
    ci = lax.broadcasted_iota(I32, (chunk, chunk), 1)
    tri = ri >= ci
    strict = ri > ci
    eye = (ri == ci).astype(F32)
    ltri = tri.astype(F32)
    utri = (ri <= ci).astype(F32)
    nw = nw_ref[...]

    units = [(c, h) for c in range(n_chunks) for h in range(H_B)]
    gcs = [_dot_exact_mask(ltri, g_all[c * chunk:(c + 1) * chunk, :], True) for c in range(n_chunks)]
    grs = [_dot_exact_mask(utri, g_all_t[:, c * chunk:(c + 1) * chunk], False) for c in range(n_chunks)]
    pre = []
    for c, h in units:
        r0 = c * chunk
        q = act_scr[r0:r0 + chunk, h * D_K:(h + 1) * D_K]
        k = act_scr[r0:r0 + chunk, hk + h * D_K:hk + (h + 1) * D_K]
        v = act_scr[r0:r0 + chunk, 2 * hk + h * D_V:2 * hk + (h + 1) * D_V]
        q = q * lax.rsqrt(jnp.sum(q * q, axis=-1, keepdims=True) + EPS) * (D_K ** -0.5)
        k = k * lax.rsqrt(jnp.sum(k * k, axis=-1, keepdims=True) + EPS)
        beta = beta_all[r0:r0 + chunk, h:h + 1]
        gc = gcs[c][:, h:h + 1]
        gr = grs[c][h:h + 1, :]
        g_last = gc[chunk - 1:chunk, :]
        decay = jnp.where(tri, jnp.exp(jnp.where(tri, gc - gr, 0.0)), 0.0)
        kb = k * beta
        eg = jnp.exp(gc)
        kbf = k.astype(BF16)
        kk = lax.dot_general(kb.astype(BF16), kbf, NT_DIMS, preferred_element_type=F32)
        qk = lax.dot_general(q.astype(BF16), kbf, NT_DIMS, preferred_element_type=F32)
        pre.append(dict(
            nm=-jnp.where(strict, kk * decay, 0.0),
            rhs=jnp.concatenate([v * beta, kb * eg], axis=1),
            qk=jnp.where(tri, qk * decay, 0.0).astype(BF16),
            qg=(q * eg).astype(BF16),
            kg=(k * jnp.exp(g_last - gc)).astype(BF16),
            gl=jnp.exp(g_last)))
    invs = [eye + p["nm"] for p in pre]
    pws = [p["nm"] for p in pre]
    for _ in range(n_steps - 1):
        pws = [_bdot(pw, pw) for pw in pws]
        invs = [inv + _bdot(inv, pw) for inv, pw in zip(invs, pws)]
    resid = [eye - _dot3(eye - p["nm"], inv) for p, inv in zip(pre, invs)]
    invs = [inv + _bdot(inv, r) for inv, r in zip(invs, resid)]
    uws = [_bdot(inv, p["rhs"]) for p, inv in zip(pre, invs)]

    for (c, h), p, uw in zip(units, pre, uws):
        r0 = c * chunk
        u = uw[:, 0:D_V]
        wmat = uw[:, D_V:D_V + D_K]
        s = s_ref[h]
        sb = s.astype(BF16)
        v_new = u - jnp.dot(wmat.astype(BF16), sb, preferred_element_type=F32)
        v_new_b = v_new.astype(BF16)
        o = (jnp.dot(p["qg"], sb, preferred_element_type=F32)
             + jnp.dot(p["qk"], v_new_b, preferred_element_type=F32))
        s_ref[h] = s * p["gl"] + lax.dot_general(p["kg"], v_new_b, TN_DIMS, preferred_element_type=F32)
        z = pz_ref[r0:r0 + chunk, h * D_V:(h + 1) * D_V].astype(F32)
        o = o * lax.rsqrt(jnp.mean(o * o, axis=-1, keepdims=True) + EPS) * nw * _silu(z)
        o_ref[r0:r0 + chunk, h * D_V:(h + 1) * D_V] = o.astype(o_ref.dtype)


def _gdn(gq, zg, small, s0, c0, conv_w, a_log, dt_bias, norm_w, bn, ln, chunk):
    hk = H_B * D_K
    assert H_B * D_V == hk and Z_COLS == hk
    rows = ln
    for cand in (2 * chunk, chunk):
        if cand % LANES == 0 and ln % cand == 0:
            rows = cand
            break
    assert rows % chunk == 0 and rows % SUBLANES == 0
    nblk = ln // rows
    small_t = jnp.swapaxes(small[:, 0:2 * H_B].reshape(bn, ln, 2 * H_B), 1, 2)
    alr = a_log.reshape(1, H_B)
    dtr = dt_bias.reshape(1, H_B)
    alc = a_log.reshape(H_B, 1)
    dtc = dt_bias.reshape(H_B, 1)
    row_blk = lambda b, i: b * nblk + i
    return pl.pallas_call(
        functools.partial(_gdn_body, chunk, rows),
        grid=(bn, nblk),
        in_specs=[
            pl.BlockSpec((rows, hk), lambda b, i: (row_blk(b, i), 0)),
            pl.BlockSpec((rows, hk), lambda b, i: (row_blk(b, i), 1)),
            pl.BlockSpec((rows, hk), lambda b, i: (row_blk(b, i), 2)),
            pl.BlockSpec((rows, hk), lambda b, i: (row_blk(b, i), 0)),
            pl.BlockSpec((rows, LANES), lambda b, i: (row_blk(b, i), 0)),
            pl.BlockSpec((None, 2 * H_B, rows), lambda b, i: (b, 0, i)),
            pl.BlockSpec((CONV_W, G_COLS), lambda b, i: (0, 0)),
            pl.BlockSpec((1, H_B), lambda b, i: (0, 0)),
            pl.BlockSpec((1, H_B), lambda b, i: (0, 0)),
            pl.BlockSpec((H_B, 1), lambda b, i: (0, 0)),
            pl.BlockSpec((H_B, 1), lambda b, i: (0, 0)),
            pl.BlockSpec((1, D_V), lambda b, i: (0, 0)),
            pl.BlockSpec((None, H_B, D_K, D_V), lambda b, i: (b, 0, 0, 0)),
            pl.BlockSpec((None, CONV_W - 1, G_COLS), lambda b, i: (b, 0, 0)),
        ],
        out_specs=[
            pl.BlockSpec((rows, hk), lambda b, i: (row_blk(b, i), 0)),
            pl.BlockSpec((None, H_B, D_K, D_V), lambda b, i: (b, 0, 0, 0)),
            pl.BlockSpec((None, CONV_W - 1, G_COLS), lambda b, i: (b, 0, 0)),
        ],
        out_shape=[
            jax.ShapeDtypeStruct((bn * ln, hk), BF16),
            jax.ShapeDtypeStruct(s0.shape, F32),
            jax.ShapeDtypeStruct(c0.shape, F32),
        ],
        scratch_shapes=[
            pltpu.VMEM((rows + SUBLANES, G_COLS), F32),
            pltpu.VMEM((rows, G_COLS), F32),
        ],
        compiler_params=_cparams("parallel", "arbitrary"),
        name="gdn",
    )(gq, gq, gq, zg, small, small_t, conv_w, alr, dtr, alc, dtc, norm_w, s0, c0)


def _merge_body(x_ref, oa_ref, ob_ref, ga0_ref, ga1_ref, gb0_ref, gb1_ref, wa_ref, wb_ref, wo_ref,
                nf_ref, wr_ref, br_ref, x2_ref, t_ref, lg_ref):
    ya = jnp.dot(oa_ref[...], wa_ref[...], preferred_element_type=F32)
    yb = jnp.dot(ob_ref[...], wb_ref[...], preferred_element_type=F32)
    half = ga0_ref.shape[1]
    gate = lambda ref: _sigmoid(ref[...].astype(F32))
    m0 = gate(ga0_ref) * ya[:, :half] + gate(gb0_ref) * yb[:, :half]
    m1 = gate(ga1_ref) * ya[:, half:] + gate(gb1_ref) * yb[:, half:]
    merged = jnp.concatenate([m0, m1], axis=1).astype(BF16)
    x2 = x_ref[...] + jnp.dot(merged, wo_ref[...], preferred_element_type=F32)
    x2_ref[...] = x2
    ms = jnp.mean(x2 * x2, axis=-1, keepdims=True)
    t = x2 * lax.rsqrt(ms + EPS) * nf_ref[...]
    t_ref[...] = _pack_bf16_pair(t)
    lg_ref[...] = _bdot(t, wr_ref[...]) + br_ref[...]


def _merge(x, o_a, o_b, zg, wa, wb, wo, norm_ffn, w_router, b_router):
    T, D = x.shape
    half = D // 2
    assert Z_COLS % half == 0
    gcol = Z_COLS // half
    bm = _pick(T, (256, 128, 64, 32, 16, 8))
    const = dict(pipeline_mode=pl.Buffered(1))
    return pl.pallas_call(
        _merge_body,
        grid=(T // bm,),
        in_specs=[
            pl.BlockSpec((bm, D), lambda i: (i, 0)),
            pl.BlockSpec((bm, o_a.shape[1]), lambda i: (i, 0)),
            pl.BlockSpec((bm, o_b.shape[1]), lambda i: (i, 0)),
            pl.BlockSpec((bm, half), lambda i: (i, gcol)),
            pl.BlockSpec((bm, half), lambda i: (i, gcol + 1)),
            pl.BlockSpec((bm, half), lambda i: (i, gcol + 2)),
            pl.BlockSpec((bm, half), lambda i: (i, gcol + 3)),
            pl.BlockSpec(wa.shape, lambda i: (0, 0), **const),
            pl.BlockSpec(wb.shape, lambda i: (0, 0), **const),
            pl.BlockSpec(wo.shape, lambda i: (0, 0), **const),
            pl.BlockSpec((1, D), lambda i: (0, 0)),
            pl.BlockSpec((D, LANES), lambda i: (0, 0), **const),
            pl.BlockSpec((1, LANES), lambda i: (0, 0)),
        ],
        out_specs=[
            pl.BlockSpec((bm, D), lambda i: (i, 0)),
            pl.BlockSpec((bm, half), lambda i: (i, 0)),
            pl.BlockSpec((bm, LANES), lambda i: (i, 0)),
        ],
        out_shape=[
            jax.ShapeDtypeStruct((T, D), F32),
            jax.ShapeDtypeStruct((T, half), U32),
            jax.ShapeDtypeStruct((T, LANES), F32),
        ],
        compiler_params=_cparams("parallel"),
        name="merge",
    )(x, o_a, o_b, zg, zg, zg, zg, wa, wb, wo, norm_ffn, w_router, b_router)


def _route_body(lg_ref, mi_ref, mf_ref, cnt_ref):
    i = pl.program_id(0)

    @pl.when(i == 0)
    def _():
        cnt_ref[...] = jnp.zeros(cnt_ref.shape, F32)

    lg = lg_ref[...]
    bm = lg.shape[0]
    lane = lax.broadcasted_iota(I32, lg.shape, 1).astype(F32)
    big = jnp.float32(LANES)
    neg = -jnp.inf
    gl = jnp.where(lane < N_GROUPS, lg, neg)
    gmax = jnp.max(gl, axis=1, keepdims=True)
    gidx = jnp.min(jnp.where(gl == gmax, lane, big), axis=1, keepdims=True)
    gw = 1.0 / jnp.sum(jnp.exp(gl - gmax), axis=1, keepdims=True)
    e_lo = N_GROUPS + gidx * EXPERTS_PER_GROUP
    valid = (lane >= e_lo) & (lane < e_lo + EXPERTS_PER_GROUP)
    el = jnp.where(valid, lg, neg)
    v1 = jnp.max(el, axis=1, keepdims=True)
    i1 = jnp.min(jnp.where(el == v1, lane, big), axis=1, keepdims=True)
    el2 = jnp.where(lane == i1, neg, el)
    v2 = jnp.max(el2, axis=1, keepdims=True)
    i2 = jnp.min(jnp.where(el2 == v2, lane, big), axis=1, keepdims=True)
    e21 = jnp.exp(v2 - v1)
    w1 = gw / (1.0 + e21)
    w2 = gw * e21 / (1.0 + e21)
    oh1 = (lane == i1).astype(F32)
    oh2 = (lane == i2).astype(F32)
    oh = oh1 + oh2
    rr = lax.broadcasted_iota(I32, (bm, bm), 0)
    cc = lax.broadcasted_iota(I32, (bm, bm), 1)
    before = (cc < rr).astype(BF16)
    cum = jnp.dot(before, oh.astype(BF16), preferred_element_type=F32) + cnt_ref[...]
    rank1 = jnp.sum(cum * oh1, axis=1, keepdims=True)
    rank2 = jnp.sum(cum * oh2, axis=1, keepdims=True)
    cnt_ref[...] = cnt_ref[...] + jnp.sum(oh, axis=0, keepdims=True)
    mi = jnp.where(lane == 0, i1 - N_GROUPS, 0.0)
    mi = jnp.where(lane == 1, i2 - N_GROUPS, mi)
    mi = jnp.where(lane == 2, rank1, mi)
    mi = jnp.where(lane == 3, rank2, mi)
    mi_ref[...] = mi.T[0:SUBLANES].astype(I32)
    mf_ref[...] = jnp.where(lane == 0, w1, jnp.where(lane == 1, w2, 0.0))


def _route(logits):
    T = logits.shape[0]
    bm = _pick(T, (512, 256, 128, 64, 32, 16, 8))
    return pl.pallas_call(
        _route_body,
        grid=(T // bm,),
        in_specs=[pl.BlockSpec((bm, LANES), lambda i: (i, 0))],
        out_specs=[
            pl.BlockSpec((SUBLANES, bm), lambda i: (0, i)),
            pl.BlockSpec((bm, LANES), lambda i: (i, 0)),
            pl.BlockSpec((1, LANES), lambda i: (0, 0)),
        ],
        out_shape=[
            jax.ShapeDtypeStruct((SUBLANES, T), I32),
            jax.ShapeDtypeStruct((T, LANES), F32),
            jax.ShapeDtypeStruct((1, LANES), F32),
        ],
        compiler_params=_cparams("arbitrary"),
        name="route",
    )(logits)


def _dispatch_body(bm, blk, n_experts, nbp, ps_ref, pn_ref, nu_ref, dest_ref, tp_ref, ts_ref, xb_ref,
                   zero_scr, sem, zsem):
    i = pl.program_id(0)
    n_blocks = xb_ref.shape[0] // blk

    def pad_rows(act):
        def per_expert(e, carry):
            def per_row(r, c2):
                act(pltpu.make_async_copy(zero_scr.at[pl.ds(0, 1)], xb_ref.at[pl.ds(ps_ref[e] + r, 1)], zsem))
                return c2
            return lax.fori_loop(0, pn_ref[e], per_row, carry)
        lax.fori_loop(0, n_experts, per_expert, 0)

    def tail_blocks(act):
        def per_block(b, carry):
            act(pltpu.make_async_copy(zero_scr, xb_ref.at[pl.ds(pl.multiple_of(b * blk, blk), blk)], zsem))
            return carry
        lax.fori_loop(nu_ref[0], n_blocks, per_block, 0)

    @pl.when(i == 0)
    def _():
        zero_scr[...] = jnp.zeros(zero_scr.shape, zero_scr.dtype)
        pad_rows(lambda cp: cp.start())
        tail_blocks(lambda cp: cp.start())

    def send_rows(t_ref):
        def row_copy(r, k):
            d = dest_ref[0, k * bm + r]
            return pltpu.make_async_copy(t_ref.at[pl.ds(r, 1)], xb_ref.at[pl.ds(d, 1)], sem)

        def issue(r, carry):
            for k in range(TOP_K_INNER):
                row_copy(r, k).start()
            return carry

        def drain(r, carry):
            for k in range(TOP_K_INNER):
                row_copy(r, k).wait()
            return carry

        lax.fori_loop(0, bm, issue, 0, unroll=DMA_UNROLL)
        lax.fori_loop(0, bm, drain, 0, unroll=DMA_UNROLL)

    @pl.when(i < nbp)
    def _():
        send_rows(tp_ref)

    @pl.when(i >= nbp)
    def _():
        send_rows(ts_ref)

    @pl.when(i == 0)
    def _():
        pad_rows(lambda cp: cp.wait())
        tail_blocks(lambda cp: cp.wait())


def _dispatch(t_p, t_s, dest, pad_start, pad_len, n_used, n_rows, blk):
    (tp, D), ts = t_p.shape, t_s.shape[0]
    bm = _pick(math.gcd(tp, ts), (256, 128, 64, 32, 16, 8))
    nbp, nbs = tp // bm, ts // bm
    dest3 = _dest_blocks(dest, bm)
    grid_spec = pltpu.PrefetchScalarGridSpec(
        num_scalar_prefetch=3,
        grid=(nbp + nbs,),
        in_specs=[
            pl.BlockSpec((None, 1, TOP_K_INNER * bm), lambda i, ps, pn, nu: (i, 0, 0),
                         memory_space=pltpu.SMEM),
            pl.BlockSpec((bm, D), lambda i, ps, pn, nu: (jnp.minimum(i, nbp - 1), 0)),
            pl.BlockSpec((bm, D), lambda i, ps, pn, nu: (jnp.maximum(i - nbp, 0), 0)),
        ],
        out_specs=pl.BlockSpec(memory_space=pl.ANY),
        scratch_shapes=[
            pltpu.VMEM((blk, D), t_p.dtype),
            pltpu.SemaphoreType.DMA(()),
            pltpu.SemaphoreType.DMA(()),
        ],
    )
    return pl.pallas_call(
        functools.partial(_dispatch_body, bm, blk, pad_start.shape[0], nbp),
        grid_spec=grid_spec,
        out_shape=jax.ShapeDtypeStruct((n_rows, D), t_p.dtype),
        compiler_params=_cparams("arbitrary"),
        name="dispatch",
    )(pad_start, pad_len, n_used, dest3, t_p, t_s)


def _expert_body(be_ref, nu_ref, x_ref, wg_ref, wu_ref, wd_ref, y_ref, wg_scr, wu_scr, wd_scr):
    i = pl.program_id(0)

    @pl.when(i < nu_ref[0])
    def _():
        @pl.when(jnp.logical_or(i == 0, be_ref[i] != be_ref[jnp.maximum(i - 1, 0)]))
        def _():
            wg_scr[...] = wg_ref[...].astype(BF16)
            wu_scr[...] = wu_ref[...].astype(BF16)
            wd_scr[...] = wd_ref[...].astype(BF16)

        x = _unpack_bf16_pair(x_ref[...])
        g = jnp.dot(x, wg_scr[...], preferred_element_type=F32)
        u = jnp.dot(x, wu_scr[...], preferred_element_type=F32)
        hmid = (_silu(g) * u).astype(BF16)
        y_ref[...] = jnp.dot(hmid, wd_scr[...], preferred_element_type=F32)

    @pl.when(i >= nu_ref[0])
    def _():
        y_ref[...] = jnp.zeros(y_ref.shape, F32)


def _experts(xb, blk_exp, n_used, w_gate, w_up, w_down, blk):
    P = xb.shape[0]
    D, de = w_gate.shape[1:]
    assert xb.shape[1] * 2 == D
    n_blocks = P // blk
    last = lambda i, nu: jnp.minimum(i, nu[0] - 1)
    grid_spec = pltpu.PrefetchScalarGridSpec(
        num_scalar_prefetch=2,
        grid=(n_blocks,),
        in_specs=[
            pl.BlockSpec((blk, D // 2), lambda i, be, nu: (last(i, nu), 0)),
            pl.BlockSpec((None, D, de), lambda i, be, nu: (be[last(i, nu)], 0, 0)),
            pl.BlockSpec((None, D, de), lambda i, be, nu: (be[last(i, nu)], 0, 0)),
            pl.BlockSpec((None, de, D), lambda i, be, nu: (be[last(i, nu)], 0, 0)),
        ],
        out_specs=pl.BlockSpec((blk, D), lambda i, be, nu: (i, 0)),
        scratch_shapes=[
            pltpu.VMEM((D, de), BF16),
            pltpu.VMEM((D, de), BF16),
            pltpu.VMEM((de, D), BF16),
        ],
    )
    return pl.pallas_call(
        _expert_body,
        grid_spec=grid_spec,
        out_shape=jax.ShapeDtypeStruct((P, D), F32),
        compiler_params=_cparams("arbitrary"),
        name="experts",
    )(blk_exp, n_used, xb, w_gate, w_up, w_down)


def _combine_body(bm, final, dcur_ref, dnxt_ref, x2_ref, mf_ref, nw_ref, yb_ref, o_ref, y_scr, sems):
    i = pl.program_id(0)
    n = pl.num_programs(0)

    def gather(dest_ref, slot, act):
        def row_copy(r, k):
            d = dest_ref[0, k * bm + r]
            return pltpu.make_async_copy(yb_ref.at[pl.ds(d, 1)], y_scr.at[slot, k, pl.ds(r, 1)],
                                         sems.at[slot])

        def body(r, carry):
            for k in range(TOP_K_INNER):
                act(row_copy(r, k))
            return carry

        lax.fori_loop(0, bm, body, 0, unroll=DMA_UNROLL)

    def step(slot):
        if slot == 0:
            @pl.when(i == 0)
            def _():
                gather(dcur_ref, 0, lambda cp: cp.start())

        @pl.when(i + 1 < n)
        def _():
            gather(dnxt_ref, 1 - slot, lambda cp: cp.start())

        gather(dcur_ref, slot, lambda cp: cp.wait())
        mf = mf_ref[...]
        x3 = x2_ref[...] + y_scr[slot, 0] * mf[:, 0:1] + y_scr[slot, 1] * mf[:, 1:2]
        if final:
            ms = jnp.mean(x3 * x3, axis=-1, keepdims=True)
            x3 = x3 * lax.rsqrt(ms + EPS) * nw_ref[...]
        o_ref[...] = x3

    for slot in range(2):
        @pl.when(lax.rem(i, 2) == slot)
        def _(slot=slot):
            step(slot)


def _combine(x2, yb, dest, mf, norm_final, final):
    T, D = x2.shape
    bm = _pick(T, (256, 128, 64, 32, 16, 8))
    nblk = T // bm
    dest3 = _dest_blocks(dest, bm)
    return pl.pallas_call(
        functools.partial(_combine_body, bm, final),
        grid=(nblk,),
        in_specs=[
            pl.BlockSpec((None, 1, TOP_K_INNER * bm), lambda i: (i, 0, 0), memory_space=pltpu.SMEM),
            pl.BlockSpec((None, 1, TOP_K_INNER * bm), lambda i: (jnp.minimum(i + 1, nblk - 1), 0, 0),
                         memory_space=pltpu.SMEM),
            pl.BlockSpec((bm, D), lambda i: (i, 0)),
            pl.BlockSpec((bm, LANES), lambda i: (i, 0)),
            pl.BlockSpec((1, D), lambda i: (0, 0)),
            pl.BlockSpec(memory_space=pl.ANY),
        ],
        out_specs=pl.BlockSpec((bm, D), lambda i: (i, 0)),
        out_shape=jax.ShapeDtypeStruct((T, D), F32),
        scratch_shapes=[
            pltpu.VMEM((2, TOP_K_INNER, bm, D), F32),
            pltpu.SemaphoreType.DMA((2,)),
        ],
        compiler_params=_cparams("arbitrary"),
        name="combine",
    )(dest3, dest3, x2, mf, norm_final, yb)


def _rotary_tables(pos):
    half = ROT_DIM // 2
    inv = ROPE_THETA ** (-jnp.arange(0, ROT_DIM, 2, dtype=F32) / ROT_DIM)
    ang = pos.astype(F32)[:, None] * inv[None, :]
    cos = jnp.cos(ang)
    sin = jnp.sin(ang)
    n = pos.shape[0]
    ones = jnp.ones((n, D_A - ROT_DIM), F32)
    zeros = jnp.zeros((n, D_A - ROT_DIM), F32)
    zh = jnp.zeros((n, half), F32)
    cos64 = jnp.concatenate([cos, cos, ones], axis=1)
    sa64 = jnp.concatenate([zh, sin, zeros], axis=1)
    sb64 = jnp.concatenate([-sin, zh, zeros], axis=1)
    reps = LANES // D_A
    return jnp.tile(cos64, (1, reps)), jnp.tile(sa64, (1, reps)), jnp.tile(sb64, (1, reps))


def _dest_blocks(dest, bm):
    nblk = dest.shape[1] // bm
    return jnp.swapaxes(dest.reshape(TOP_K_INNER, nblk, bm), 0, 1).reshape(nblk, 1, TOP_K_INNER * bm)


def _moe_slots(mi, counts, n_experts, blk, n_assign):
    eid = mi[0:TOP_K_INNER]
    rank = mi[TOP_K_INNER:2 * TOP_K_INNER]
    cnt = counts[0, N_GROUPS:N_GROUPS + n_experts].astype(I32)
    pc = (cnt + blk - 1) // blk * blk
    pend = jnp.cumsum(pc)
    pstart = pend - pc
    experts = jnp.arange(n_experts, dtype=I32)[:, None, None]
    first = jnp.sum(jnp.where(eid[None] == experts, pstart[:, None, None], 0), axis=0)
    dest = (first + rank).astype(I32)
    n_blocks = -(-n_assign // blk) + n_experts
    blk_start = jnp.arange(n_blocks, dtype=I32) * blk
    blk_exp = jnp.minimum(jnp.sum((pend[None, :] <= blk_start[:, None]).astype(I32), axis=1), n_experts - 1)
    n_used = jnp.maximum(pend[-1] // blk, 1).astype(I32).reshape(1)
    return dest, blk_exp.astype(I32), n_used, n_blocks, (pstart + cnt).astype(I32), (pc - cnt).astype(I32)


def _layer(xp, xs, layer, dims, cache_k, cache_v, state_delta, state_conv, prm, norm_final, final):
    (norm_mix, w_in, lq1, lk1, lq2, lk2, subln_w, conv_w, a_log, dt_bias, gdn_norm_w,
     w_proj_a, w_proj_b, w_out, norm_ffn, w_gr, b_gr, w_er, b_er, w_eg, w_eu, w_ed) = prm
    bp, lp, bs, ls, past = dims
    D = xp.shape[1]
    tp = bp * lp
    n_qkv = 2 * A_COLS + V_COLS + G_COLS + Z_COLS
    n_small = 2 * H_B
    lam_init = 0.8 - 0.6 * math.exp(-0.3 * layer)

    w_t = jnp.swapaxes(w_in, 0, 1)
    w_a = w_t[:n_qkv].astype(BF16)
    w_b = w_t[n_qkv + n_small:].astype(BF16)
    w_small = jnp.pad(w_t[n_qkv:n_qkv + n_small], ((0, LANES - n_small), (0, 0))).astype(BF16)
    nm = norm_mix.reshape(1, D)
    pp = _inproj(xp, nm, _rotary_tables(jnp.tile(jnp.arange(lp), bp)), w_a, w_b, w_small)
    ps = _inproj(xs, nm, _rotary_tables(jnp.tile(past + jnp.arange(ls), bs)), w_a, w_b, w_small)

    lamp = jnp.stack([lq1, lk1, lq2, lk2]).astype(F32)
    sw = subln_w.reshape(1, D_VA)
    oa_p = _attn_prompt(pp.q_bf, pp.k_bf, pp.v_t, lamp, sw, lam_init, bp, lp)
    oa_s = _attn_sample(ps.q_bf, ps.k_bf, ps.v_new, cache_k.reshape(bs, past * H_A, 2 * D_A),
                        cache_v.reshape(bs, past * H_A, D_VA), lamp, sw, lam_init, bs, ls)

    nw = gdn_norm_w.reshape(1, D_V)
    ob_p, s_p, c_p = _gdn(pp.gq, pp.zg, pp.small, jnp.zeros((bp, H_B, D_K, D_V), F32),
                          jnp.zeros((bp, CONV_W - 1, G_COLS), F32), conv_w, a_log, dt_bias, nw,
                          bp, lp, _pick(lp, (GDN_CHUNK, CHUNK)))
    ob_s, s_s, c_s = _gdn(ps.gq, ps.zg, ps.small, state_delta, state_conv, conv_w, a_log, dt_bias, nw,
                          bs, ls, ls)

    n_experts = w_er.shape[1]
    w_router = jnp.pad(jnp.concatenate([w_gr, w_er], axis=1), ((0, 0), (0, LANES - N_GROUPS - n_experts)))
    b_router = jnp.pad(jnp.concatenate([b_gr, b_er]), (0, LANES - N_GROUPS - n_experts)).reshape(1, LANES)
    mw = (w_proj_a.astype(BF16), w_proj_b.astype(BF16), w_out.astype(BF16), norm_ffn.reshape(1, D),
          w_router, b_router)
    x2_p, t_p, lg_p = _merge(xp, oa_p, ob_p, pp.zg, *mw)
    x2_s, t_s, lg_s = _merge(xs, oa_s, ob_s, ps.zg, *mw)

    blk = MOE_ROWS
    mi, mf, counts = _route(jnp.concatenate([lg_p, lg_s], axis=0))
    n_tok = tp + bs * ls
    dest, blk_exp, n_used, n_blocks, pad_start, pad_len = _moe_slots(
        mi, counts, n_experts, blk, n_tok * TOP_K_INNER)
    xb = _dispatch(t_p, t_s, dest, pad_start, pad_len, n_used, n_blocks * blk, blk)
    yb = _experts(xb, blk_exp, n_used, w_eg, w_eu, w_ed, blk)
    nf = norm_final.reshape(1, D)
    y_p = _combine(x2_p, yb, dest[:, :tp], mf[:tp], nf, final)
    y_s = _combine(x2_s, yb, dest[:, tp:], mf[tp:], nf, final)
    return y_p, y_s, pp.k_new, pp.v_new, s_p, c_p, ps.k_new, ps.v_new, s_s, c_s


def kernel(x_prompt, x_sample, cache_k, cache_v, state_delta, state_conv, norm_mix, w_in, lambda_q1, lambda_k1, lambda_q2, lambda_k2, subln_w, conv_w, a_log, dt_bias, gdn_norm_w, w_proj_a, w_proj_b, w_out, norm_ffn, w_group_router, b_group_router, w_expert_router, b_expert_router, w_exp_gate, w_exp_up, w_exp_down, norm_final):
    bp, lp, D = x_prompt.shape
    bs, ls, _ = x_sample.shape
    depth = cache_k.shape[0]
    past = cache_k.shape[2]
    dims = (bp, lp, bs, ls, past)
    xp = x_prompt.reshape(bp * lp, D)
    xs = x_sample.reshape(bs * ls, D)
    kp, vp, sp, cp, ksm, vsm, ssm, csm = [], [], [], [], [], [], [], []
    for l in range(depth):
        prm = (norm_mix[l], w_in[l], lambda_q1[l], lambda_k1[l], lambda_q2[l], lambda_k2[l], subln_w[l],
               conv_w[l], a_log[l], dt_bias[l], gdn_norm_w[l], w_proj_a[l], w_proj_b[l], w_out[l],
               norm_ffn[l], w_group_router[l], b_group_router[l], w_expert_router[l], b_expert_router[l],
               w_exp_gate[l], w_exp_up[l], w_exp_down[l])
        xp, xs, k_p, v_p, s_p, c_p, k_s, v_s, s_s, c_s = _layer(
            xp, xs, l, dims, cache_k[l], cache_v[l], state_delta[l], state_conv[l], prm,
            norm_final, l == depth - 1)
        kp.append(k_p.reshape(bp, lp, H_A, 2 * D_A))
        vp.append(v_p.reshape(bp, lp, H_A, D_VA))
        ksm.append(k_s.reshape(bs, ls, H_A, 2 * D_A))
        vsm.append(v_s.reshape(bs, ls, H_A, D_VA))
        sp.append(s_p)
        cp.append(c_p)
        ssm.append(s_s)
        csm.append(c_s)
    return (xp.reshape(bp, lp, D), xs.reshape(bs, ls, D), jnp.stack(kp), jnp.stack(vp), jnp.stack(sp),
            jnp.stack(cp), jnp.stack(ksm), jnp.stack(vsm), jnp.stack(ssm), jnp.stack(csm))
```

```python
import collections
import functools
import math

import jax
import jax.numpy as jnp
from jax import lax
from jax.experimental import pallas as pl
from jax.experimental.pallas import tpu as pltpu

F32 = jnp.float32
BF16 = jnp.bfloat16
I32 = jnp.int32
U32 = jnp.uint32

CHUNK = 64
GDN_CHUNK = 128
H_A = 8
D_A = 64
D_VA = 2 * D_A
ROT_DIM = D_A // 4
ROPE_THETA = 500000.0
H_B = 8
D_K = 128
D_V = 128
CONV_W = 4
N_GROUPS = 4
EXPERTS_PER_GROUP = 8
TOP_K_INNER = 2
EPS = 1e-6

LANES = 128
SUBLANES = 8
V7X_VMEM_BYTES = 64 * 1024 * 1024
VMEM_LIMIT_BYTES = V7X_VMEM_BYTES - 8 * 1024 * 1024

Q_SCALE = (D_A ** -0.5) * math.log2(math.e)
ONES_ROWS = 16
INPROJ_BN = 512
MOE_ROWS = 256
DMA_UNROLL = 8

NT_DIMS = (((1,), (1,)), ((), ()))
TN_DIMS = (((0,), (0,)), ((), ()))

A_COLS = H_A * 2 * D_A
V_COLS = H_A * D_VA
G_COLS = 2 * H_B * D_K + H_B * D_V
Z_COLS = H_B * D_V


def _pick(n, cands):
    for c in cands:
        if n % c == 0:
            return c
    raise ValueError(f"no block size in {cands} divides {n}")


def _cparams(*sem):
    return pltpu.CompilerParams(dimension_semantics=sem, vmem_limit_bytes=VMEM_LIMIT_BYTES)


def _sigmoid(x):
    return 1.0 / (1.0 + jnp.exp(-x))


def _silu(x):
    return x * _sigmoid(x)


def _softplus(x):
    return jnp.maximum(x, 0.0) + jnp.log1p(jnp.exp(-jnp.abs(x)))


def _pack_bf16_pair(x):
    bits = lax.bitcast_convert_type(x.astype(BF16).astype(F32), U32)
    half = x.shape[1] // 2
    return (bits[:, :half] >> 16) | (bits[:, half:] & jnp.uint32(0xFFFF0000))


def _unpack_bf16_pair(w):
    lo = lax.bitcast_convert_type(w << 16, F32)
    hi = lax.bitcast_convert_type(w & jnp.uint32(0xFFFF0000), F32)
    return jnp.concatenate([lo, hi], axis=1).astype(BF16)


def _bdot(a, b):
    return jnp.dot(a.astype(BF16), b.astype(BF16), preferred_element_type=F32)


def _split2(a):
    hi = a.astype(BF16)
    lo = (a - hi.astype(F32)).astype(BF16)
    return hi, lo


def _dot3(a, b):
    ah, al = _split2(a)
    bh, bl = _split2(b)
    d = functools.partial(jnp.dot, preferred_element_type=F32)
    return d(ah, bh) + d(ah, bl) + d(al, bh)


def _dot_exact_mask(mask, x, mask_is_lhs):
    hi = x.astype(BF16)
    r1 = x - hi.astype(F32)
    mid = r1.astype(BF16)
    lo = (r1 - mid.astype(F32)).astype(BF16)
    mb = mask.astype(BF16)
    d = functools.partial(jnp.dot, preferred_element_type=F32)
    if mask_is_lhs:
        return d(mb, hi) + d(mb, mid) + d(mb, lo)
    return d(hi, mb) + d(mid, mb) + d(lo, mb)


InProj = collections.namedtuple("InProj", "small k_new v_new q_bf k_bf v_t gq zg")


def _inproj_body(o_k, o_v, o_g, o_z, nb_wa, x_ref, nw_ref, cos_ref, sa_ref, sb_ref, wa_ref, wb_ref, ws_ref,
                 os_ref, k_ref, v_ref, qb_ref, kb_ref, vt_ref, gq_ref, zg_ref, h_scr):
    j = pl.program_id(1)
    bm = h_scr.shape[0]
    half = wa_ref.shape[0] // 2
    lpb = half // LANES

    @pl.when(j == 0)
    def _():
        x = x_ref[...]
        ms = jnp.mean(x * x, axis=-1, keepdims=True)
        h = (x * lax.rsqrt(ms + EPS) * nw_ref[...]).astype(BF16)
        h_scr[...] = h
        os_ref[...] = lax.dot_general(h, ws_ref[...], NT_DIMS, preferred_element_type=F32)

    def halves(w_ref):
        return [lax.dot_general(h_scr[...], w_ref[c * half:(c + 1) * half, :], NT_DIMS,
                                preferred_element_type=F32) for c in range(2)]

    def lane_blocks(w_ref):
        for c2, acc in enumerate(halves(w_ref)):
            for c in range(lpb):
                yield c2 * lpb + c, acc[:, c * LANES:(c + 1) * LANES]

    def rotated(blk):
        return (blk * cos_ref[...] + pltpu.roll(blk, ROT_DIM // 2, 1) * sa_ref[...]
                + pltpu.roll(blk, LANES - ROT_DIM // 2, 1) * sb_ref[...])

    @pl.when(j < o_k)
    def _():
        for lb, blk in lane_blocks(wa_ref):
            qb_ref[:, lb * LANES:(lb + 1) * LANES] = (rotated(blk) * Q_SCALE).astype(BF16)

    hpb = 2 * lpb
    for jj in range(o_v - o_k):
        @pl.when(j == o_k + jj)
        def _(jj=jj):
            for lb, blk in lane_blocks(wa_ref):
                rot = rotated(blk)
                k_ref[pl.ds(jj * hpb + lb, bm, stride=H_A), :] = rot
                kb_ref[:, lb * LANES:(lb + 1) * LANES] = rot.astype(BF16)

    for jj in range(o_g - o_v):
        @pl.when(j == o_v + jj)
        def _(jj=jj):
            for lb, blk in lane_blocks(wa_ref):
                v_ref[pl.ds(jj * hpb + lb, bm, stride=H_A), :] = blk
                vt_ref[lb, 0:D_VA, :] = blk.T.astype(BF16)
                vt_ref[lb, D_VA:D_VA + ONES_ROWS, :] = jnp.ones((ONES_ROWS, bm), BF16)

    @pl.when(jnp.logical_and(j >= o_g, j < o_z))
    def _():
        for c, acc in enumerate(halves(wa_ref)):
            gq_ref[:, c * half:(c + 1) * half] = acc

    @pl.when(jnp.logical_and(j >= o_z, j < nb_wa))
    def _():
        for c, acc in enumerate(halves(wa_ref)):
            zg_ref[:, c * half:(c + 1) * half] = acc.astype(BF16)

    @pl.when(j >= nb_wa)
    def _():
        for c, acc in enumerate(halves(wb_ref)):
            zg_ref[:, c * half:(c + 1) * half] = acc.astype(BF16)


def _inproj(x, norm_w, tables, w_a, w_b, w_small):
    T, D = x.shape
    bn = INPROJ_BN
    zg_cols = Z_COLS + 2 * D
    assert A_COLS % bn == 0 and V_COLS % bn == 0 and G_COLS % bn == 0 and Z_COLS % bn == 0
    assert (bn // 2) % D_VA == 0 and D_VA == LANES and (2 * D) % bn == 0
    assert w_a.shape[0] == 2 * A_COLS + V_COLS + G_COLS + Z_COLS and w_b.shape[0] == 2 * D
    nb_a, nb_v, nb_g, nb_z = A_COLS // bn, V_COLS // bn, G_COLS // bn, zg_cols // bn
    o_k, o_v = nb_a, 2 * nb_a
    o_g = o_v + nb_v
    o_z = o_g + nb_g
    nb_wa, nb_wb = w_a.shape[0] // bn, w_b.shape[0] // bn
    bm = _pick(T, (1024, 512, 256, 128))
    hpb = bn // D_VA

    def col(lo, n):
        return lambda i, j: (i, jnp.clip(j - lo, 0, n - 1))

    cos_t, sa_t, sb_t = tables
    outs = pl.pallas_call(
        functools.partial(_inproj_body, o_k, o_v, o_g, o_z, nb_wa),
        grid=(T // bm, nb_wa + nb_wb),
        in_specs=[
            pl.BlockSpec((bm, D), lambda i, j: (i, 0), pipeline_mode=pl.Buffered(1)),
            pl.BlockSpec((1, D), lambda i, j: (0, 0)),
            pl.BlockSpec((bm, LANES), lambda i, j: (i, 0)),
            pl.BlockSpec((bm, LANES), lambda i, j: (i, 0)),
            pl.BlockSpec((bm, LANES), lambda i, j: (i, 0)),
            pl.BlockSpec((bn, D), lambda i, j: (jnp.minimum(j, nb_wa - 1), 0)),
            pl.BlockSpec((bn, D), lambda i, j: (jnp.maximum(j - nb_wa, 0), 0)),
            pl.BlockSpec((LANES, D), lambda i, j: (0, 0)),
        ],
        out_specs=[
            pl.BlockSpec((bm, LANES), lambda i, j: (i, 0)),
            pl.BlockSpec((bm * H_A, D_VA), lambda i, j: (i, 0)),
            pl.BlockSpec((bm * H_A, D_VA), lambda i, j: (i, 0)),
            pl.BlockSpec((bm, bn), col(0, nb_a)),
            pl.BlockSpec((bm, bn), col(o_k, nb_a)),
            pl.BlockSpec((hpb, D_VA + ONES_ROWS, bm), lambda i, j: (jnp.clip(j - o_v, 0, nb_v - 1), 0, i)),
            pl.BlockSpec((bm, bn), col(o_g, nb_g)),
            pl.BlockSpec((bm, bn), col(o_z, nb_z)),
        ],
        out_shape=[
            jax.ShapeDtypeStruct((T, LANES), F32),
            jax.ShapeDtypeStruct((T * H_A, 2 * D_A), F32),
            jax.ShapeDtypeStruct((T * H_A, D_VA), F32),
            jax.ShapeDtypeStruct((T, A_COLS), BF16),
            jax.ShapeDtypeStruct((T, A_COLS), BF16),
            jax.ShapeDtypeStruct((H_A, D_VA + ONES_ROWS, T), BF16),
            jax.ShapeDtypeStruct((T, G_COLS), F32),
            jax.ShapeDtypeStruct((T, zg_cols), BF16),
        ],
        scratch_shapes=[pltpu.VMEM((bm, D), BF16)],
        compiler_params=_cparams("parallel", "arbitrary"),
        name="inproj",
    )(x, norm_w, cos_t, sa_t, sb_t, w_a, w_b, w_small)
    return InProj(*outs)


def _diff_lambda(lamp, lam_init):
    s1 = jnp.sum(lamp[0:1] * lamp[1:2], axis=1, keepdims=True)
    s2 = jnp.sum(lamp[2:3] * lamp[3:4], axis=1, keepdims=True)
    return jnp.exp(s1) - jnp.exp(s2) + lam_init


def _subln(o, w, lam_init):
    ms = jnp.mean(o * o, axis=-1, keepdims=True)
    return o * lax.rsqrt(ms + EPS) * w * (1.0 - lam_init)


def _attn_prompt_body(lam_init, bq, q_ref, k_ref, vt_ref, lamp_ref, swc_ref, o_ref,
                      m_scr, acc_scr, s_scr):
    qi = pl.program_id(2)
    q = q_ref[...]
    lane = lax.broadcasted_iota(I32, q.shape, 1)
    qz = [jnp.where(lane < D_A, q, jnp.zeros_like(q)), jnp.where(lane >= D_A, q, jnp.zeros_like(q))]
    m_scr[...] = jnp.full(m_scr.shape, -jnp.inf, F32)
    acc_scr[...] = jnp.zeros(acc_scr.shape, F32)
    qry_pos = lax.broadcasted_iota(I32, (1, bq), 1)

    def scores(j, slot):
        kb = k_ref[pl.ds(pl.multiple_of(j * bq, bq), bq), :]
        for m in range(2):
            s_scr[slot, m] = lax.dot_general(kb, qz[m], NT_DIMS, preferred_element_type=F32)

    def accumulate(j, slot, masked):
        vb = vt_ref[:, pl.ds(pl.multiple_of(j * bq, bq), bq)]
        for m in range(2):
            if masked:
                s = jnp.concatenate(
                    [jnp.where(qry_pos >= c * CHUNK, s_scr[slot, m, c * CHUNK:(c + 1) * CHUNK, :], -jnp.inf)
                     for c in range(bq // CHUNK)], axis=0)
            else:
                s = s_scr[slot, m]
            m_prev = m_scr[m]
            m_new = jnp.maximum(m_prev, jnp.max(s, axis=0, keepdims=True))
            alpha = jnp.exp2(m_prev - m_new)
            p = jnp.exp2(s - m_new).astype(BF16)
            acc_scr[m] = alpha * acc_scr[m] + jnp.dot(vb, p, preferred_element_type=F32)
            m_scr[m] = m_new

    scores(0, 0)

    def two_full_blocks(t, carry):
        j = 2 * t
        scores(j + 1, 1)
        accumulate(j, 0, False)
        scores(j + 2, 0)
        accumulate(j + 1, 1, False)
        return carry

    lax.fori_loop(0, qi // 2, two_full_blocks, 0)
    j0 = 2 * (qi // 2)

    @pl.when(j0 == qi)
    def _():
        accumulate(qi, 0, True)

    @pl.when(j0 != qi)
    def _():
        scores(qi, 1)
        accumulate(j0, 0, False)
        accumulate(qi, 1, True)

    lam = _diff_lambda(lamp_ref[...], lam_init)
    a0 = acc_scr[0]
    a1 = acc_scr[1]
    o_t = a0[0:D_VA] / a0[D_VA:D_VA + 1] - lam * (a1[0:D_VA] / a1[D_VA:D_VA + 1])
    ms = jnp.mean(o_t * o_t, axis=0, keepdims=True)
    o_t = o_t * lax.rsqrt(ms + EPS) * swc_ref[...] * (1.0 - lam_init)
    o_ref[...] = o_t.T.astype(o_ref.dtype)


def _attn_prompt(q_bf, k_bf, v_t, lamp, subln_w, lam_init, bp, lp):
    bq = _pick(lp, (512, 256, 128))
    nq = lp // bq
    vrows = v_t.shape[1]
    return pl.pallas_call(
        functools.partial(_attn_prompt_body, lam_init, bq),
        grid=(bp, H_A, nq),
        in_specs=[
            pl.BlockSpec((bq, LANES), lambda b, h, i: (b * nq + i, h)),
            pl.BlockSpec((lp, LANES), lambda b, h, i: (b, h)),
            pl.BlockSpec((None, vrows, lp), lambda b, h, i: (h, 0, b)),
            pl.BlockSpec((4, D_A), lambda b, h, i: (0, 0)),
            pl.BlockSpec((D_VA, 1), lambda b, h, i: (0, 0)),
        ],
        out_specs=pl.BlockSpec((bq, LANES), lambda b, h, i: (b * nq + i, h)),
        out_shape=jax.ShapeDtypeStruct((bp * lp, V_COLS), BF16),
        scratch_shapes=[
            pltpu.VMEM((2, 1, bq), F32),
            pltpu.VMEM((2, vrows, bq), F32),
            pltpu.VMEM((2, 2, bq, bq), F32),
        ],
        compiler_params=_cparams("parallel", "parallel", "arbitrary"),
        name="attn_prompt",
    )(q_bf, k_bf, v_t, lamp, subln_w.reshape(D_VA, 1))


def _attn_sample_body(lam_init, q_ref, kn_ref, vn_ref, kc_ref, vc_ref, lamp_ref, sw_ref, o_ref,
                      m_scr, l_scr, acc_scr):
    pj = pl.program_id(1)
    npos = kc_ref.shape[0] // H_A
    ls = q_ref.shape[0]

    @pl.when(pj == 0)
    def _():
        m_scr[...] = jnp.full(m_scr.shape, -jnp.inf, F32)
        l_scr[...] = jnp.zeros(l_scr.shape, F32)
        acc_scr[...] = jnp.zeros(acc_scr.shape, F32)

    lane = lax.broadcasted_iota(I32, (ls, D_VA), 1)

    def q_maps(h):
        q = q_ref[:, h * D_VA:(h + 1) * D_VA]
        return jnp.concatenate([jnp.where(lane < D_A, q, jnp.zeros_like(q)),
                                jnp.where(lane >= D_A, q, jnp.zeros_like(q))], axis=0)

    def update_all(keys_of, vals_of):
        heads = range(H_A)
        ss = [lax.dot_general(q_maps(h), keys_of(h), NT_DIMS, preferred_element_type=F32) for h in heads]
        m_new = [jnp.maximum(m_scr[h], jnp.max(ss[h], axis=1, keepdims=True)) for h in heads]
        ps = [jnp.exp2(ss[h] - m_new[h]) for h in heads]
        pv = [jnp.dot(ps[h].astype(BF16), vals_of(h), preferred_element_type=F32) for h in heads]
        for h in heads:
            alpha = jnp.exp2(m_scr[h] - m_new[h])
            l_scr[h] = alpha * l_scr[h] + jnp.sum(ps[h], axis=1, keepdims=True)
            acc_scr[h] = alpha * acc_scr[h] + pv[h]
            m_scr[h] = m_new[h]

    update_all(lambda h: kc_ref[pl.ds(h, npos, stride=H_A), :].astype(BF16),
               lambda h: vc_ref[pl.ds(h, npos, stride=H_A), :].astype(BF16))

    @pl.when(pj == pl.num_programs(1) - 1)
    def _():
        lam = _diff_lambda(lamp_ref[...], lam_init)
        sw = sw_ref[...]
        update_all(lambda h: kn_ref[:, h * D_VA:(h + 1) * D_VA],
                   lambda h: vn_ref[pl.ds(h, ls, stride=H_A), :].astype(BF16))
        for h in range(H_A):
            both = acc_scr[h] / l_scr[h]
            o = both[0:ls] - lam * both[ls:2 * ls]
            o_ref[:, h * D_VA:(h + 1) * D_VA] = _subln(o, sw, lam_init).astype(o_ref.dtype)


def _attn_sample(q_bf, k_bf, v_new, cache_k, cache_v, lamp, subln_w, lam_init, bs, ls):
    past = cache_k.shape[1] // H_A
    pb = _pick(past, (1024, 512, 256, 128, 64))
    assert ls % (2 * SUBLANES) == 0
    return pl.pallas_call(
        functools.partial(_attn_sample_body, lam_init),
        grid=(bs, past // pb),
        in_specs=[
            pl.BlockSpec((ls, A_COLS), lambda b, j: (b, 0)),
            pl.BlockSpec((ls, A_COLS), lambda b, j: (b, 0)),
            pl.BlockSpec((ls * H_A, D_VA), lambda b, j: (b, 0)),
            pl.BlockSpec((None, pb * H_A, D_VA), lambda b, j: (b, j, 0)),
            pl.BlockSpec((None, pb * H_A, D_VA), lambda b, j: (b, j, 0)),
            pl.BlockSpec((4, D_A), lambda b, j: (0, 0)),
            pl.BlockSpec((1, D_VA), lambda b, j: (0, 0)),
        ],
        out_specs=pl.BlockSpec((ls, V_COLS), lambda b, j: (b, 0)),
        out_shape=jax.ShapeDtypeStruct((bs * ls, V_COLS), BF16),
        scratch_shapes=[
            pltpu.VMEM((H_A, 2 * ls, 1), F32),
            pltpu.VMEM((H_A, 2 * ls, 1), F32),
            pltpu.VMEM((H_A, 2 * ls, D_VA), F32),
        ],
        compiler_params=_cparams("parallel", "arbitrary"),
        name="attn_sample",
    )(q_bf, k_bf, v_new, cache_k, cache_v, lamp, subln_w)


def _gdn_body(chunk, rows, pq_ref, pk_ref, pv_ref, pz_ref, sm_ref, smt_ref, cw_ref, alr_ref, dtr_ref,
              alc_ref, dtc_ref, nw_ref, s0_ref, c0_ref, o_ref, s_ref, cn_ref, ext_scr, act_scr):
    i = pl.program_id(1)
    nblk = pl.num_programs(1)
    hk = H_B * D_K
    hv = H_B * D_V
    n_chunks = rows // chunk
    n_steps = int(math.log2(chunk))
    assert 2 ** n_steps == chunk

    @pl.when(i == 0)
    def _():
        s_ref[...] = s0_ref[...]
        ext_scr[0:SUBLANES, :] = jnp.zeros((SUBLANES, ext_scr.shape[1]), F32)
        ext_scr[pl.ds(SUBLANES - (CONV_W - 1), CONV_W - 1), :] = c0_ref[...]

    ext_scr[pl.ds(SUBLANES, rows), 0:hk] = pq_ref[...]
    ext_scr[pl.ds(SUBLANES, rows), hk:2 * hk] = pk_ref[...]
    ext_scr[pl.ds(SUBLANES, rows), 2 * hk:2 * hk + hv] = pv_ref[...]

    @pl.when(i == nblk - 1)
    def _():
        cn_ref[...] = ext_scr[pl.ds(SUBLANES + rows - (CONV_W - 1), CONV_W - 1), :]

    conv = jnp.zeros((rows, ext_scr.shape[1]), F32)
    for w in range(CONV_W):
        conv = conv + ext_scr[pl.ds(SUBLANES - (CONV_W - 1) + w, rows), :] * cw_ref[w:w + 1, :]
    act_scr[...] = _silu(conv)
    ext_scr[0:SUBLANES, :] = ext_scr[pl.ds(rows, SUBLANES), :]

    sm = sm_ref[...]
    beta_all = _sigmoid(sm[:, 0:H_B])
    g_all = -jnp.exp(alr_ref[...]) * _softplus(sm[:, H_B:2 * H_B] + dtr_ref[...])
    smt = smt_ref[...]
    g_all_t = -jnp.exp(alc_ref[...]) * _softplus(smt[H_B:2 * H_B, :] + dtc_ref[...])

    ri = lax.broadcasted_iota(I32, (chunk, chunk), 0)
    ci = lax.broadcasted_iota(I32, (chunk, chunk), 1)
    tri = ri >= ci
    strict = ri > ci
    eye = (ri == ci).astype(F32)
    ltri = tri.astype(F32)
    utri = (ri <= ci).astype(F32)
    nw = nw_ref[...]

    units = [(c, h) for c in range(n_chunks) for h in range(H_B)]
    gcs = [_dot_exact_mask(ltri, g_all[c * chunk:(c + 1) * chunk, :], True) for c in range(n_chunks)]
    grs = [_dot_exact_mask(utri, g_all_t[:, c * chunk:(c + 1) * chunk], False) for c in range(n_chunks)]
    pre = []
    for c, h in units:
        r0 = c * chunk
        q = act_scr[r0:r0 + chunk, h * D_K:(h + 1) * D_K]
        k = act_scr[r0:r0 + chunk, hk + h * D_K:hk + (h + 1) * D_K]
        v = act_scr[r0:r0 + chunk, 2 * hk + h * D_V:2 * hk + (h + 1) * D_V]
        q = q * lax.rsqrt(jnp.sum(q * q, axis=-1, keepdims=True) + EPS) * (D_K ** -0.5)
        k = k * lax.rsqrt(jnp.sum(k * k, axis=-1, keepdims=True) + EPS)
        beta = beta_all[r0:r0 + chunk, h:h + 1]
        gc = gcs[c][:, h:h + 1]
        gr = grs[c][h:h + 1, :]
        g_last = gc[chunk - 1:chunk, :]
        decay = jnp.where(tri, jnp.exp(jnp.where(tri, gc - gr, 0.0)), 0.0)
        kb = k * beta
        eg = jnp.exp(gc)
        kbf = k.astype(BF16)
        kk = lax.dot_general(kb.astype(BF16), kbf, NT_DIMS, preferred_element_type=F32)
        qk = lax.dot_general(q.astype(BF16), kbf, NT_DIMS, preferred_element_type=F32)
        pre.append(dict(
            nm=-jnp.where(strict, kk * decay, 0.0),
            rhs=jnp.concatenate([v * beta, kb * eg], axis=1),
            qk=jnp.where(tri, qk * decay, 0.0).astype(BF16),
            qg=(q * eg).astype(BF16),
            kg=(k * jnp.exp(g_last - gc)).astype(BF16),
            gl=jnp.exp(g_last)))
    invs = [eye + p["nm"] for p in pre]
    pws = [p["nm"] for p in pre]
    for _ in range(n_steps - 1):
        pws = [_bdot(pw, pw) for pw in pws]
        invs = [inv + _bdot(inv, pw) for inv, pw in zip(invs, pws)]
    resid = [eye - _dot3(eye - p["nm"], inv) for p, inv in zip(pre, invs)]
    invs = [inv + _bdot(inv, r) for inv, r in zip(invs, resid)]
    uws = [_bdot(inv, p["rhs"]) for p, inv in zip(pre, invs)]

    for (c, h), p, uw in zip(units, pre, uws):
        r0 = c * chunk
        u = uw[:, 0:D_V]
        wmat = uw[:, D_V:D_V + D_K]
        s = s_ref[h]
        sb = s.astype(BF16)
        v_new = u - jnp.dot(wmat.astype(BF16), sb, preferred_element_type=F32)
        v_new_b = v_new.astype(BF16)
        o = (jnp.dot(p["qg"], sb, preferred_element_type=F32)
             + jnp.dot(p["qk"], v_new_b, preferred_element_type=F32))
        s_ref[h] = s * p["gl"] + lax.dot_general(p["kg"], v_new_b, TN_DIMS, preferred_element_type=F32)
        z = pz_ref[r0:r0 + chunk, h * D_V:(h + 1) * D_V].astype(F32)
        o = o * lax.rsqrt(jnp.mean(o * o, axis=-1, keepdims=True) + EPS) * nw * _silu(z)
        o_ref[r0:r0 + chunk, h * D_V:(h + 1) * D_V] = o.astype(o_ref.dtype)


def _gdn(gq, zg, small, s0, c0, conv_w, a_log, dt_bias, norm_w, bn, ln, chunk):
    hk = H_B * D_K
    assert H_B * D_V == hk and Z_COLS == hk
    rows = ln
    for cand in (2 * chunk, chunk):
        if cand % LANES == 0 and ln % cand == 0:
            rows = cand
            break
    assert rows % chunk == 0 and rows % SUBLANES == 0
    nblk = ln // rows
    small_t = jnp.swapaxes(small[:, 0:2 * H_B].reshape(bn, ln, 2 * H_B), 1, 2)
    alr = a_log.reshape(1, H_B)
    dtr = dt_bias.reshape(1, H_B)
    alc = a_log.reshape(H_B, 1)
    dtc = dt_bias.reshape(H_B, 1)
    row_blk = lambda b, i: b * nblk + i
    return pl.pallas_call(
        functools.partial(_gdn_body, chunk, rows),
        grid=(bn, nblk),
        in_specs=[
            pl.BlockSpec((rows, hk), lambda b, i: (row_blk(b, i), 0)),
            pl.BlockSpec((rows, hk), lambda b, i: (row_blk(b, i), 1)),
            pl.BlockSpec((rows, hk), lambda b, i: (row_blk(b, i), 2)),
            pl.BlockSpec((rows, hk), lambda b, i: (row_blk(b, i), 0)),
            pl.BlockSpec((rows, LANES), lambda b, i: (row_blk(b, i), 0)),
            pl.BlockSpec((None, 2 * H_B, rows), lambda b, i: (b, 0, i)),
            pl.BlockSpec((CONV_W, G_COLS), lambda b, i: (0, 0)),
            pl.BlockSpec((1, H_B), lambda b, i: (0, 0)),
            pl.BlockSpec((1, H_B), lambda b, i: (0, 0)),
            pl.BlockSpec((H_B, 1), lambda b, i: (0, 0)),
            pl.BlockSpec((H_B, 1), lambda b, i: (0, 0)),
            pl.BlockSpec((1, D_V), lambda b, i: (0, 0)),
            pl.BlockSpec((None, H_B, D_K, D_V), lambda b, i: (b, 0, 0, 0)),
            pl.BlockSpec((None, CONV_W - 1, G_COLS), lambda b, i: (b, 0, 0)),
        ],
        out_specs=[
            pl.BlockSpec((rows, hk), lambda b, i: (row_blk(b, i), 0)),
            pl.BlockSpec((None, H_B, D_K, D_V), lambda b, i: (b, 0, 0, 0)),
            pl.BlockSpec((None, CONV_W - 1, G_COLS), lambda b, i: (b, 0, 0)),
        ],
        out_shape=[
            jax.ShapeDtypeStruct((bn * ln, hk), BF16),
            jax.ShapeDtypeStruct(s0.shape, F32),
            jax.ShapeDtypeStruct(c0.shape, F32),
        ],
        scratch_shapes=[
            pltpu.VMEM((rows + SUBLANES, G_COLS), F32),
            pltpu.VMEM((rows, G_COLS), F32),
        ],
        compiler_params=_cparams("parallel", "arbitrary"),
        name="gdn",
    )(gq, gq, gq, zg, small, small_t, conv_w, alr, dtr, alc, dtc, norm_w, s0, c0)


def _merge_body(x_ref, oa_ref, ob_ref, ga0_ref, ga1_ref, gb0_ref, gb1_ref, wa_ref, wb_ref, wo_ref,
                nf_ref, wr_ref, br_ref, x2_ref, t_ref, lg_ref):
    ya = jnp.dot(oa_ref[...], wa_ref[...], preferred_element_type=F32)
    yb = jnp.dot(ob_ref[...], wb_ref[...], preferred_element_type=F32)
    half = ga0_ref.shape[1]
    gate = lambda ref: _sigmoid(ref[...].astype(F32))
    m0 = gate(ga0_ref) * ya[:, :half] + gate(gb0_ref) * yb[:, :half]
    m1 = gate(ga1_ref) * ya[:, half:] + gate(gb1_ref) * yb[:, half:]
    merged = jnp.concatenate([m0, m1], axis=1).astype(BF16)
    x2 = x_ref[...] + jnp.dot(merged, wo_ref[...], preferred_element_type=F32)
    x2_ref[...] = x2
    ms = jnp.mean(x2 * x2, axis=-1, keepdims=True)
    t = x2 * lax.rsqrt(ms + EPS) * nf_ref[...]
    t_ref[...] = _pack_bf16_pair(t)
    lg_ref[...] = _bdot(t, wr_ref[...]) + br_ref[...]


def _merge(x, o_a, o_b, zg, wa, wb, wo, norm_ffn, w_router, b_router):
    T, D = x.shape
    half = D // 2
    assert Z_COLS % half == 0
    gcol = Z_COLS // half
    bm = _pick(T, (256, 128, 64, 32, 16, 8))
    const = dict(pipeline_mode=pl.Buffered(1))
    return pl.pallas_call(
        _merge_body,
        grid=(T // bm,),
        in_specs=[
            pl.BlockSpec((bm, D), lambda i: (i, 0)),
            pl.BlockSpec((bm, o_a.shape[1]), lambda i: (i, 0)),
            pl.BlockSpec((bm, o_b.shape[1]), lambda i: (i, 0)),
            pl.BlockSpec((bm, half), lambda i: (i, gcol)),
            pl.BlockSpec((bm, half), lambda i: (i, gcol + 1)),
            pl.BlockSpec((bm, half), lambda i: (i, gcol + 2)),
            pl.BlockSpec((bm, half), lambda i: (i, gcol + 3)),
            pl.BlockSpec(wa.shape, lambda i: (0, 0), **const),
            pl.BlockSpec(wb.shape, lambda i: (0, 0), **const),
            pl.BlockSpec(wo.shape, lambda i: (0, 0), **const),
            pl.BlockSpec((1, D), lambda i: (0, 0)),
            pl.BlockSpec((D, LANES), lambda i: (0, 0), **const),
            pl.BlockSpec((1, LANES), lambda i: (0, 0)),
        ],
        out_specs=[
            pl.BlockSpec((bm, D), lambda i: (i, 0)),
            pl.BlockSpec((bm, half), lambda i: (i, 0)),
            pl.BlockSpec((bm, LANES), lambda i: (i, 0)),
        ],
        out_shape=[
            jax.ShapeDtypeStruct((T, D), F32),
            jax.ShapeDtypeStruct((T, half), U32),
            jax.ShapeDtypeStruct((T, LANES), F32),
        ],
        compiler_params=_cparams("parallel"),
        name="merge",
    )(x, o_a, o_b, zg, zg, zg, zg, wa, wb, wo, norm_ffn, w_router, b_router)


def _route_body(lg_ref, mi_ref, mf_ref, cnt_ref):
    i = pl.program_id(0)

    @pl.when(i == 0)
    def _():
        cnt_ref[...] = jnp.zeros(cnt_ref.shape, F32)

    lg = lg_ref[...]
    bm = lg.shape[0]
    lane = lax.broadcasted_iota(I32, lg.shape, 1).astype(F32)
    big = jnp.float32(LANES)
    neg = -jnp.inf
    gl = jnp.where(lane < N_GROUPS, lg, neg)
    gmax = jnp.max(gl, axis=1, keepdims=True)
    gidx = jnp.min(jnp.where(gl == gmax, lane, big), axis=1, keepdims=True)
    gw = 1.0 / jnp.sum(jnp.exp(gl - gmax), axis=1, keepdims=True)
    e_lo = N_GROUPS + gidx * EXPERTS_PER_GROUP
    valid = (lane >= e_lo) & (lane < e_lo + EXPERTS_PER_GROUP)
    el = jnp.where(valid, lg, neg)
    v1 = jnp.max(el, axis=1, keepdims=True)
    i1 = jnp.min(jnp.where(el == v1, lane, big), axis=1, keepdims=True)
    el2 = jnp.where(lane == i1, neg, el)
    v2 = jnp.max(el2, axis=1, keepdims=True)
    i2 = jnp.min(jnp.where(el2 == v2, lane, big), axis=1, keepdims=True)
    e21 = jnp.exp(v2 - v1)
    w1 = gw / (1.0 + e21)
    w2 = gw * e21 / (1.0 + e21)
    oh1 = (lane == i1).astype(F32)
    oh2 = (lane == i2).astype(F32)
    oh = oh1 + oh2
    rr = lax.broadcasted_iota(I32, (bm, bm), 0)
    cc = lax.broadcasted_iota(I32, (bm, bm), 1)
    before = (cc < rr).astype(BF16)
    cum = jnp.dot(before, oh.astype(BF16), preferred_element_type=F32) + cnt_ref[...]
    rank1 = jnp.sum(cum * oh1, axis=1, keepdims=True)
    rank2 = jnp.sum(cum * oh2, axis=1, keepdims=True)
    cnt_ref[...] = cnt_ref[...] + jnp.sum(oh, axis=0, keepdims=True)
    mi = jnp.where(lane == 0, i1 - N_GROUPS, 0.0)
    mi = jnp.where(lane == 1, i2 - N_GROUPS, mi)
    mi = jnp.where(lane == 2, rank1, mi)
    mi = jnp.where(lane == 3, rank2, mi)
    mi_ref[...] = mi.T[0:SUBLANES].astype(I32)
    mf_ref[...] = jnp.where(lane == 0, w1, jnp.where(lane == 1, w2, 0.0))


def _route(logits):
    T = logits.shape[0]
    bm = _pick(T, (512, 256, 128, 64, 32, 16, 8))
    return pl.pallas_call(
        _route_body,
        grid=(T // bm,),
        in_specs=[pl.BlockSpec((bm, LANES), lambda i: (i, 0))],
        out_specs=[
            pl.BlockSpec((SUBLANES, bm), lambda i: (0, i)),
            pl.BlockSpec((bm, LANES), lambda i: (i, 0)),
            pl.BlockSpec((1, LANES), lambda i: (0, 0)),
        ],
        out_shape=[
            jax.ShapeDtypeStruct((SUBLANES, T), I32),
            jax.ShapeDtypeStruct((T, LANES), F32),
            jax.ShapeDtypeStruct((1, LANES), F32),
        ],
        compiler_params=_cparams("arbitrary"),
        name="route",
    )(logits)


def _dispatch_body(bm, blk, n_experts, nbp, ps_ref, pn_ref, nu_ref, dest_ref, dprev_ref, tp_ref, ts_ref,
                   xb_ref, zero_scr, stage_scr, sems, zsem):
    i = pl.program_id(0)
    n_blocks = xb_ref.shape[0] // blk

    def pad_rows(act):
        def per_expert(e, carry):
            def per_row(r, c2):
                act(pltpu.make_async_copy(zero_scr.at[pl.ds(0, 1)], xb_ref.at[pl.ds(ps_ref[e] + r, 1)], zsem))
                return c2
            return lax.fori_loop(0, pn_ref[e], per_row, carry)
        lax.fori_loop(0, n_experts, per_expert, 0)

    def tail_blocks(act):
        def per_block(b, carry):
            act(pltpu.make_async_copy(zero_scr, xb_ref.at[pl.ds(pl.multiple_of(b * blk, blk), blk)], zsem))
            return carry
        lax.fori_loop(nu_ref[0], n_blocks, per_block, 0)

    @pl.when(i == 0)
    def _():
        zero_scr[...] = jnp.zeros(zero_scr.shape, zero_scr.dtype)
        pad_rows(lambda cp: cp.start())
        tail_blocks(lambda cp: cp.start())

    def rows(dref, slot, act):
        def body(r, carry):
            for k in range(TOP_K_INNER):
                d = dref[0, k * bm + r]
                act(pltpu.make_async_copy(stage_scr.at[slot, pl.ds(r, 1)], xb_ref.at[pl.ds(d, 1)],
                                          sems.at[slot]))
            return carry

        lax.fori_loop(0, bm, body, 0, unroll=DMA_UNROLL)

    def step(slot):
        @pl.when(i < nbp)
        def _():
            stage_scr[slot] = tp_ref[...]

        @pl.when(i >= nbp)
        def _():
            stage_scr[slot] = ts_ref[...]

        rows(dest_ref, slot, lambda cp: cp.start())

        @pl.when(i > 0)
        def _():
            rows(dprev_ref, 1 - slot, lambda cp: cp.wait())

        @pl.when(i == pl.num_programs(0) - 1)
        def _():
            rows(dest_ref, slot, lambda cp: cp.wait())

    for slot in range(2):
        @pl.when(lax.rem(i, 2) == slot)
        def _(slot=slot):
            step(slot)

    @pl.when(i == 0)
    def _():
        pad_rows(lambda cp: cp.wait())
        tail_blocks(lambda cp: cp.wait())


def _dispatch(t_p, t_s, dest, pad_start, pad_len, n_used, n_rows, blk):
    (tp, D), ts = t_p.shape, t_s.shape[0]
    bm = _pick(math.gcd(tp, ts), (256, 128, 64, 32, 16, 8))
    nbp, nbs = tp // bm, ts // bm
    dest3 = _dest_blocks(dest, bm)
    grid_spec = pltpu.PrefetchScalarGridSpec(
        num_scalar_prefetch=3,
        grid=(nbp + nbs,),
        in_specs=[
            pl.BlockSpec((None, 1, TOP_K_INNER * bm), lambda i, ps, pn, nu: (i, 0, 0),
                         memory_space=pltpu.SMEM),
            pl.BlockSpec((None, 1, TOP_K_INNER * bm), lambda i, ps, pn, nu: (jnp.maximum(i - 1, 0), 0, 0),
                         memory_space=pltpu.SMEM),
            pl.BlockSpec((bm, D), lambda i, ps, pn, nu: (jnp.minimum(i, nbp - 1), 0)),
            pl.BlockSpec((bm, D), lambda i, ps, pn, nu: (jnp.maximum(i - nbp, 0), 0)),
        ],
        out_specs=pl.BlockSpec(memory_space=pl.ANY),
        scratch_shapes=[
            pltpu.VMEM((blk, D), t_p.dtype),
            pltpu.VMEM((2, bm, D), t_p.dtype),
            pltpu.SemaphoreType.DMA((2,)),
            pltpu.SemaphoreType.DMA(()),
        ],
    )
    return pl.pallas_call(
        functools.partial(_dispatch_body, bm, blk, pad_start.shape[0], nbp),
        grid_spec=grid_spec,
        out_shape=jax.ShapeDtypeStruct((n_rows, D), t_p.dtype),
        compiler_params=_cparams("arbitrary"),
        name="dispatch",
    )(pad_start, pad_len, n_used, dest3, dest3, t_p, t_s)


def _expert_body(be_ref, nu_ref, x_ref, wg_ref, wu_ref, wd_ref, y_ref, wg_scr, wu_scr, wd_scr):
    i = pl.program_id(0)

    @pl.when(i < nu_ref[0])
    def _():
        @pl.when(jnp.logical_or(i == 0, be_ref[i] != be_ref[jnp.maximum(i - 1, 0)]))
        def _():
            wg_scr[...] = wg_ref[...].astype(BF16)
            wu_scr[...] = wu_ref[...].astype(BF16)
            wd_scr[...] = wd_ref[...].astype(BF16)

        x = _unpack_bf16_pair(x_ref[...])
        g = jnp.dot(x, wg_scr[...], preferred_element_type=F32)
        u = jnp.dot(x, wu_scr[...], preferred_element_type=F32)
        hmid = (_silu(g) * u).astype(BF16)
        y_ref[...] = jnp.dot(hmid, wd_scr[...], preferred_element_type=F32)

    @pl.when(i >= nu_ref[0])
    def _():
        y_ref[...] = jnp.zeros(y_ref.shape, F32)


def _experts(xb, blk_exp, n_used, w_gate, w_up, w_down, blk):
    P = xb.shape[0]
    D, de = w_gate.shape[1:]
    assert xb.shape[1] * 2 == D
    n_blocks = P // blk
    last = lambda i, nu: jnp.minimum(i, nu[0] - 1)
    grid_spec = pltpu.PrefetchScalarGridSpec(
        num_scalar_prefetch=2,
        grid=(n_blocks,),
        in_specs=[
            pl.BlockSpec((blk, D // 2), lambda i, be, nu: (last(i, nu), 0)),
            pl.BlockSpec((None, D, de), lambda i, be, nu: (be[last(i, nu)], 0, 0)),
            pl.BlockSpec((None, D, de), lambda i, be, nu: (be[last(i, nu)], 0, 0)),
            pl.BlockSpec((None, de, D), lambda i, be, nu: (be[last(i, nu)], 0, 0)),
        ],
        out_specs=pl.BlockSpec((blk, D), lambda i, be, nu: (i, 0)),
        scratch_shapes=[
            pltpu.VMEM((D, de), BF16),
            pltpu.VMEM((D, de), BF16),
            pltpu.VMEM((de, D), BF16),
        ],
    )
    return pl.pallas_call(
        _expert_body,
        grid_spec=grid_spec,
        out_shape=jax.ShapeDtypeStruct((P, D), F32),
        compiler_params=_cparams("arbitrary"),
        name="experts",
    )(blk_exp, n_used, xb, w_gate, w_up, w_down)


def _combine_body(bm, final, dcur_ref, dnxt_ref, x2_ref, mf_ref, nw_ref, yb_ref, o_ref, y_scr, sems):
    i = pl.program_id(0)
    n = pl.num_programs(0)

    def gather(dest_ref, slot, act):
        def row_copy(r, k):
            d = dest_ref[0, k * bm + r]
            return pltpu.make_async_copy(yb_ref.at[pl.ds(d, 1)], y_scr.at[slot, k, pl.ds(r, 1)],
                                         sems.at[slot])

        def body(r, carry):
            for k in range(TOP_K_INNER):
                act(row_copy(r, k))
            return carry

        lax.fori_loop(0, bm, body, 0, unroll=DMA_UNROLL)

    def step(slot):
        if slot == 0:
            @pl.when(i == 0)
            def _():
                gather(dcur_ref, 0, lambda cp: cp.start())

        @pl.when(i + 1 < n)
        def _():
            gather(dnxt_ref, 1 - slot, lambda cp: cp.start())

        gather(dcur_ref, slot, lambda cp: cp.wait())
        mf = mf_ref[...]
        x3 = x2_ref[...] + y_scr[slot, 0] * mf[:, 0:1] + y_scr[slot, 1] * mf[:, 1:2]
        if final:
            ms = jnp.mean(x3 * x3, axis=-1, keepdims=True)
            x3 = x3 * lax.rsqrt(ms + EPS) * nw_ref[...]
        o_ref[...] = x3

    for slot in range(2):
        @pl.when(lax.rem(i, 2) == slot)
        def _(slot=slot):
            step(slot)


def _combine(x2, yb, dest, mf, norm_final, final):
    T, D = x2.shape
    bm = _pick(T, (256, 128, 64, 32, 16, 8))
    nblk = T // bm
    dest3 = _dest_blocks(dest, bm)
    return pl.pallas_call(
        functools.partial(_combine_body, bm, final),
        grid=(nblk,),
        in_specs=[
            pl.BlockSpec((None, 1, TOP_K_INNER * bm), lambda i: (i, 0, 0), memory_space=pltpu.SMEM),
            pl.BlockSpec((None, 1, TOP_K_INNER * bm), lambda i: (jnp.minimum(i + 1, nblk - 1), 0, 0),
                         memory_space=pltpu.SMEM),
            pl.BlockSpec((bm, D), lambda i: (i, 0)),
            pl.BlockSpec((bm, LANES), lambda i: (i, 0)),
            pl.BlockSpec((1, D), lambda i: (0, 0)),
            pl.BlockSpec(memory_space=pl.ANY),
        ],
        out_specs=pl.BlockSpec((bm, D), lambda i: (i, 0)),
        out_shape=jax.ShapeDtypeStruct((T, D), F32),
        scratch_shapes=[
            pltpu.VMEM((2, TOP_K_INNER, bm, D), F32),
            pltpu.SemaphoreType.DMA((2,)),
        ],
        compiler_params=_cparams("arbitrary"),
        name="combine",
    )(dest3, dest3, x2, mf, norm_final, yb)


def _rotary_tables(pos):
    half = ROT_DIM // 2
    inv = ROPE_THETA ** (-jnp.arange(0, ROT_DIM, 2, dtype=F32) / ROT_DIM)
    ang = pos.astype(F32)[:, None] * inv[None, :]
    cos = jnp.cos(ang)
    sin = jnp.sin(ang)
    n = pos.shape[0]
    ones = jnp.ones((n, D_A - ROT_DIM), F32)
    zeros = jnp.zeros((n, D_A - ROT_DIM), F32)
    zh = jnp.zeros((n, half), F32)
    cos64 = jnp.concatenate([cos, cos, ones], axis=1)
    sa64 = jnp.concatenate([zh, sin, zeros], axis=1)
    sb64 = jnp.concatenate([-sin, zh, zeros], axis=1)
    reps = LANES // D_A
    return jnp.tile(cos64, (1, reps)), jnp.tile(sa64, (1, reps)), jnp.tile(sb64, (1, reps))


def _dest_blocks(dest, bm):
    nblk = dest.shape[1] // bm
    return jnp.swapaxes(dest.reshape(TOP_K_INNER, nblk, bm), 0, 1).reshape(nblk, 1, TOP_K_INNER * bm)


def _moe_slots(mi, counts, n_experts, blk, n_assign):
    eid = mi[0:TOP_K_INNER]
    rank = mi[TOP_K_INNER:2 * TOP_K_INNER]
    cnt = counts[0, N_GROUPS:N_GROUPS + n_experts].astype(I32)
    pc = (cnt + blk - 1) // blk * blk
    pend = jnp.cumsum(pc)
    pstart = pend - pc
    experts = jnp.arange(n_experts, dtype=I32)[:, None, None]
    first = jnp.sum(jnp.where(eid[None] == experts, pstart[:, None, None], 0), axis=0)
    dest = (first + rank).astype(I32)
    n_blocks = -(-n_assign // blk) + n_experts
    blk_start = jnp.arange(n_blocks, dtype=I32) * blk
    blk_exp = jnp.minimum(jnp.sum((pend[None, :] <= blk_start[:, None]).astype(I32), axis=1), n_experts - 1)
    n_used = jnp.maximum(pend[-1] // blk, 1).astype(I32).reshape(1)
    return dest, blk_exp.astype(I32), n_used, n_blocks, (pstart + cnt).astype(I32), (pc - cnt).astype(I32)


def _layer(xp, xs, layer, dims, cache_k, cache_v, state_delta, state_conv, prm, norm_final, final):
    (norm_mix, w_in, lq1, lk1, lq2, lk2, subln_w, conv_w, a_log, dt_bias, gdn_norm_w,
     w_proj_a, w_proj_b, w_out, norm_ffn, w_gr, b_gr, w_er, b_er, w_eg, w_eu, w_ed) = prm
    bp, lp, bs, ls, past = dims
    D = xp.shape[1]
    tp = bp * lp
    n_qkv = 2 * A_COLS + V_COLS + G_COLS + Z_COLS
    n_small = 2 * H_B
    lam_init = 0.8 - 0.6 * math.exp(-0.3 * layer)

    w_t = jnp.swapaxes(w_in, 0, 1)
    w_a = w_t[:n_qkv].astype(BF16)
    w_b = w_t[n_qkv + n_small:].astype(BF16)
    w_small = jnp.pad(w_t[n_qkv:n_qkv + n_small], ((0, LANES - n_small), (0, 0))).astype(BF16)
    nm = norm_mix.reshape(1, D)
    pp = _inproj(xp, nm, _rotary_tables(jnp.tile(jnp.arange(lp), bp)), w_a, w_b, w_small)
    ps = _inproj(xs, nm, _rotary_tables(jnp.tile(past + jnp.arange(ls), bs)), w_a, w_b, w_small)

    lamp = jnp.stack([lq1, lk1, lq2, lk2]).astype(F32)
    sw = subln_w.reshape(1, D_VA)
    oa_p = _attn_prompt(pp.q_bf, pp.k_bf, pp.v_t, lamp, sw, lam_init, bp, lp)
    oa_s = _attn_sample(ps.q_bf, ps.k_bf, ps.v_new, cache_k.reshape(bs, past * H_A, 2 * D_A),
                        cache_v.reshape(bs, past * H_A, D_VA), lamp, sw, lam_init, bs, ls)

    nw = gdn_norm_w.reshape(1, D_V)
    ob_p, s_p, c_p = _gdn(pp.gq, pp.zg, pp.small, jnp.zeros((bp, H_B, D_K, D_V), F32),
                          jnp.zeros((bp, CONV_W - 1, G_COLS), F32), conv_w, a_log, dt_bias, nw,
                          bp, lp, _pick(lp, (GDN_CHUNK, CHUNK)))
    ob_s, s_s, c_s = _gdn(ps.gq, ps.zg, ps.small, state_delta, state_conv, conv_w, a_log, dt_bias, nw,
                          bs, ls, ls)

    n_experts = w_er.shape[1]
    w_router = jnp.pad(jnp.concatenate([w_gr, w_er], axis=1), ((0, 0), (0, LANES - N_GROUPS - n_experts)))
    b_router = jnp.pad(jnp.concatenate([b_gr, b_er]), (0, LANES - N_GROUPS - n_experts)).reshape(1, LANES)
    mw = (w_proj_a.astype(BF16), w_proj_b.astype(BF16), w_out.astype(BF16), norm_ffn.reshape(1, D),
          w_router, b_router)
    x2_p, t_p, lg_p = _merge(xp, oa_p, ob_p, pp.zg, *mw)
    x2_s, t_s, lg_s = _merge(xs, oa_s, ob_s, ps.zg, *mw)

    blk = MOE_ROWS
    mi, mf, counts = _route(jnp.concatenate([lg_p, lg_s], axis=0))
    n_tok = tp + bs * ls
    dest, blk_exp, n_used, n_blocks, pad_start, pad_len = _moe_slots(
        mi, counts, n_experts, blk, n_tok * TOP_K_INNER)
    xb = _dispatch(t_p, t_s, dest, pad_start, pad_len, n_used, n_blocks * blk, blk)
    yb = _experts(xb, blk_exp, n_used, w_eg, w_eu, w_ed, blk)
    nf = norm_final.reshape(1, D)
    y_p = _combine(x2_p, yb, dest[:, :tp], mf[:tp], nf, final)
    y_s = _combine(x2_s, yb, dest[:, tp:], mf[tp:], nf, final)
    return y_p, y_s, pp.k_new, pp.v_new, s_p, c_p, ps.k_new, ps.v_new, s_s, c_s


def kernel(x_prompt, x_sample, cache_k, cache_v, state_delta, state_conv, norm_mix, w_in, lambda_q1, lambda_k1, lambda_q2, lambda_k2, subln_w, conv_w, a_log, dt_bias, gdn_norm_w, w_proj_a, w_proj_b, w_out, norm_ffn, w_group_router, b_group_router, w_expert_router, b_expert_router, w_exp_gate, w_exp_up, w_exp_down, norm_final):
    bp, lp, D = x_prompt.shape
    bs, ls, _ = x_sample.shape
    depth = cache_k.shape[0]
    past = cache_k.shape[2]
    dims = (bp, lp, bs, ls, past)
    xp = x_prompt.reshape(bp * lp, D)
    xs = x_sample.reshape(bs * ls, D)
    kp, vp, sp, cp, ksm, vsm, ssm, csm = [], [], [], [], [], [], [], []
    for l in range(depth):
        prm = (norm_mix[l], w_in[l], lambda_q1[l], lambda_k1[l], lambda_q2[l], lambda_k2[l], subln_w[l],
               conv_w[l], a_log[l], dt_bias[l], gdn_norm_w[l], w_proj_a[l], w_proj_b[l], w_out[l],
               norm_ffn[l], w_group_router[l], b_group_router[l], w_expert_router[l], b_expert_router[l],
               w_exp_gate[l], w_exp_up[l], w_exp_down[l])
        xp, xs, k_p, v_p, s_p, c_p, k_s, v_s, s_s, c_s = _layer(
            xp, xs, l, dims, cache_k[l], cache_v[l], state_delta[l], state_conv[l], prm,
            norm_final, l == depth - 1)
        kp.append(k_p.reshape(bp, lp, H_A, 2 * D_A))
        vp.append(v_p.reshape(bp, lp, H_A, D_VA))
        ksm.append(k_s.reshape(bs, ls, H_A, 2 * D_A))
        vsm.append(v_s.reshape(bs, ls, H_A, D_VA))
        sp.append(s_p)
        cp.append(c_p)
        ssm.append(s_s)
        csm.append(c_s)
    return (xp.reshape(bp, lp, D), xs.reshape(bs, ls, D), jnp.stack(kp), jnp.stack(vp), jnp.stack(sp),
            jnp.stack(cp), jnp.stack(ksm), jnp.stack(vsm), jnp.stack(ssm), jnp.stack(csm))
```

```python
import collections
import functools
import math

import jax
import jax.numpy as jnp
from jax import lax
from jax.experimental import pallas as pl
from jax.experimental.pallas import tpu as pltpu

F32 = jnp.float32
BF16 = jnp.bfloat16
I32 = jnp.int32
U32 = jnp.uint32

CHUNK = 64
GDN_CHUNK = 128
H_A = 8
D_A = 64
D_VA = 2 * D_A
ROT_DIM = D_A // 4
ROPE_THETA = 500000.0
H_B = 8
D_K = 128
D_V = 128
CONV_W = 4
N_GROUPS = 4
EXPERTS_PER_GROUP = 8
TOP_K_INNER = 2
EPS = 1e-6

LANES = 128
SUBLANES = 8
V7X_VMEM_BYTES = 64 * 1024 * 1024
VMEM_LIMIT_BYTES = V7X_VMEM_BYTES - 8 * 1024 * 1024

Q_SCALE = (D_A ** -0.5) * math.log2(math.e)
ONES_ROWS = 16
INPROJ_BN = 512
MOE_ROWS = 256
DMA_UNROLL = 8

NT_DIMS = (((1,), (1,)), ((), ()))
TN_DIMS = (((0,), (0,)), ((), ()))

A_COLS = H_A * 2 * D_A
V_COLS = H_A * D_VA
G_COLS = 2 * H_B * D_K + H_B * D_V
Z_COLS = H_B * D_V


def _pick(n, cands):
    for c in cands:
        if n % c == 0:
            return c
    raise ValueError(f"no block size in {cands} divides {n}")


def _cparams(*sem):
    return pltpu.CompilerParams(dimension_semantics=sem, vmem_limit_bytes=VMEM_LIMIT_BYTES)


def _sigmoid(x):
    return 1.0 / (1.0 + jnp.exp(-x))


def _silu(x):
    return x * _sigmoid(x)


def _softplus(x):
    return jnp.maximum(x, 0.0) + jnp.log1p(jnp.exp(-jnp.abs(x)))


def _pack_bf16_pair(x):
    bits = lax.bitcast_convert_type(x.astype(BF16).astype(F32), U32)
    half = x.shape[1] // 2
    return (bits[:, :half] >> 16) | (bits[:, half:] & jnp.uint32(0xFFFF0000))


def _unpack_bf16_pair(w):
    lo = lax.bitcast_convert_type(w << 16, F32)
    hi = lax.bitcast_convert_type(w & jnp.uint32(0xFFFF0000), F32)
    return jnp.concatenate([lo, hi], axis=1).astype(BF16)


def _start_alternating(copy, k):
    copy.start(priority=k % 2)


def _wait(copy, k):
    del k
    copy.wait()


def _bdot(a, b):
    return jnp.dot(a.astype(BF16), b.astype(BF16), preferred_element_type=F32)


def _split2(a):
    hi = a.astype(BF16)
    lo = (a - hi.astype(F32)).astype(BF16)
    return hi, lo


def _dot3(a, b):
    ah, al = _split2(a)
    bh, bl = _split2(b)
    d = functools.partial(jnp.dot, preferred_element_type=F32)
    return d(ah, bh) + d(ah, bl) + d(al, bh)


def _dot_exact_mask(mask, x, mask_is_lhs):
    hi = x.astype(BF16)
    r1 = x - hi.astype(F32)
    mid = r1.astype(BF16)
    lo = (r1 - mid.astype(F32)).astype(BF16)
    mb = mask.astype(BF16)
    d = functools.partial(jnp.dot, preferred_element_type=F32)
    if mask_is_lhs:
        return d(mb, hi) + d(mb, mid) + d(mb, lo)
    return d(hi, mb) + d(mid, mb) + d(lo, mb)


InProj = collections.namedtuple("InProj", "small k_new v_new q_bf k_bf v_t gq zg")


def _inproj_body(o_k, o_v, o_g, o_z, nb_wa, x_ref, nw_ref, cos_ref, sa_ref, sb_ref, wa_ref, wb_ref, ws_ref,
                 os_ref, k_ref, v_ref, qb_ref, kb_ref, vt_ref, gq_ref, zg_ref, h_scr):
    j = pl.program_id(1)
    bm = h_scr.shape[0]
    half = wa_ref.shape[0] // 2
    lpb = half // LANES

    @pl.when(j == 0)
    def _():
        x = x_ref[...]
        ms = jnp.mean(x * x, axis=-1, keepdims=True)
        h = (x * lax.rsqrt(ms + EPS) * nw_ref[...]).astype(BF16)
        h_scr[...] = h
        os_ref[...] = lax.dot_general(h, ws_ref[...], NT_DIMS, preferred_element_type=F32)

    def halves(w_ref):
        return [lax.dot_general(h_scr[...], w_ref[c * half:(c + 1) * half, :], NT_DIMS,
                                preferred_element_type=F32) for c in range(2)]

    def lane_blocks(w_ref):
        for c2, acc in enumerate(halves(w_ref)):
            for c in range(lpb):
                yield c2 * lpb + c, acc[:, c * LANES:(c + 1) * LANES]

    def rotated(blk):
        return (blk * cos_ref[...] + pltpu.roll(blk, ROT_DIM // 2, 1) * sa_ref[...]
                + pltpu.roll(blk, LANES - ROT_DIM // 2, 1) * sb_ref[...])

    @pl.when(j < o_k)
    def _():
        for lb, blk in lane_blocks(wa_ref):
            qb_ref[:, lb * LANES:(lb + 1) * LANES] = (rotated(blk) * Q_SCALE).astype(BF16)

    hpb = 2 * lpb
    for jj in range(o_v - o_k):
        @pl.when(j == o_k + jj)
        def _(jj=jj):
            for lb, blk in lane_blocks(wa_ref):
                rot = rotated(blk)
                k_ref[pl.ds(jj * hpb + lb, bm, stride=H_A), :] = rot
                kb_ref[:, lb * LANES:(lb + 1) * LANES] = rot.astype(BF16)

    for jj in range(o_g - o_v):
        @pl.when(j == o_v + jj)
        def _(jj=jj):
            for lb, blk in lane_blocks(wa_ref):
                v_ref[pl.ds(jj * hpb + lb, bm, stride=H_A), :] = blk
                vt_ref[lb, 0:D_VA, :] = blk.T.astype(BF16)
                vt_ref[lb, D_VA:D_VA + ONES_ROWS, :] = jnp.ones((ONES_ROWS, bm), BF16)

    @pl.when(jnp.logical_and(j >= o_g, j < o_z))
    def _():
        for c, acc in enumerate(halves(wa_ref)):
            gq_ref[:, c * half:(c + 1) * half] = acc

    @pl.when(jnp.logical_and(j >= o_z, j < nb_wa))
    def _():
        for c, acc in enumerate(halves(wa_ref)):
            zg_ref[:, c * half:(c + 1) * half] = acc.astype(BF16)

    @pl.when(j >= nb_wa)
    def _():
        for c, acc in enumerate(halves(wb_ref)):
            zg_ref[:, c * half:(c + 1) * half] = acc.astype(BF16)


def _inproj(x, norm_w, tables, w_a, w_b, w_small):
    T, D = x.shape
    bn = INPROJ_BN
    zg_cols = Z_COLS + 2 * D
    assert A_COLS % bn == 0 and V_COLS % bn == 0 and G_COLS % bn == 0 and Z_COLS % bn == 0
    assert (bn // 2) % D_VA == 0 and D_VA == LANES and (2 * D) % bn == 0
    assert w_a.shape[0] == 2 * A_COLS + V_COLS + G_COLS + Z_COLS and w_b.shape[0] == 2 * D
    nb_a, nb_v, nb_g, nb_z = A_COLS // bn, V_COLS // bn, G_COLS // bn, zg_cols // bn
    o_k, o_v = nb_a, 2 * nb_a
    o_g = o_v + nb_v
    o_z = o_g + nb_g
    nb_wa, nb_wb = w_a.shape[0] // bn, w_b.shape[0] // bn
    bm = _pick(T, (1024, 512, 256, 128))
    hpb = bn // D_VA

    def col(lo, n):
        return lambda i, j: (i, jnp.clip(j - lo, 0, n - 1))

    cos_t, sa_t, sb_t = tables
    outs = pl.pallas_call(
        functools.partial(_inproj_body, o_k, o_v, o_g, o_z, nb_wa),
        grid=(T // bm, nb_wa + nb_wb),
        in_specs=[
            pl.BlockSpec((bm, D), lambda i, j: (i, 0), pipeline_mode=pl.Buffered(1)),
            pl.BlockSpec((1, D), lambda i, j: (0, 0)),
            pl.BlockSpec((bm, LANES), lambda i, j: (i, 0)),
            pl.BlockSpec((bm, LANES), lambda i, j: (i, 0)),
            pl.BlockSpec((bm, LANES), lambda i, j: (i, 0)),
            pl.BlockSpec((bn, D), lambda i, j: (jnp.minimum(j, nb_wa - 1), 0)),
            pl.BlockSpec((bn, D), lambda i, j: (jnp.maximum(j - nb_wa, 0), 0)),
            pl.BlockSpec((LANES, D), lambda i, j: (0, 0)),
        ],
        out_specs=[
            pl.BlockSpec((bm, LANES), lambda i, j: (i, 0)),
            pl.BlockSpec((bm * H_A, D_VA), lambda i, j: (i, 0)),
            pl.BlockSpec((bm * H_A, D_VA), lambda i, j: (i, 0)),
            pl.BlockSpec((bm, bn), col(0, nb_a)),
            pl.BlockSpec((bm, bn), col(o_k, nb_a)),
            pl.BlockSpec((hpb, D_VA + ONES_ROWS, bm), lambda i, j: (jnp.clip(j - o_v, 0, nb_v - 1), 0, i)),
            pl.BlockSpec((bm, bn), col(o_g, nb_g)),
            pl.BlockSpec((bm, bn), col(o_z, nb_z)),
        ],
        out_shape=[
            jax.ShapeDtypeStruct((T, LANES), F32),
            jax.ShapeDtypeStruct((T * H_A, 2 * D_A), F32),
            jax.ShapeDtypeStruct((T * H_A, D_VA), F32),
            jax.ShapeDtypeStruct((T, A_COLS), BF16),
            jax.ShapeDtypeStruct((T, A_COLS), BF16),
            jax.ShapeDtypeStruct((H_A, D_VA + ONES_ROWS, T), BF16),
            jax.ShapeDtypeStruct((T, G_COLS), F32),
            jax.ShapeDtypeStruct((T, zg_cols), BF16),
        ],
        scratch_shapes=[pltpu.VMEM((bm, D), BF16)],
        compiler_params=_cparams("parallel", "arbitrary"),
        name="inproj",
    )(x, norm_w, cos_t, sa_t, sb_t, w_a, w_b, w_small)
    return InProj(*outs)


def _diff_lambda(lamp, lam_init):
    s1 = jnp.sum(lamp[0:1] * lamp[1:2], axis=1, keepdims=True)
    s2 = jnp.sum(lamp[2:3] * lamp[3:4], axis=1, keepdims=True)
    return jnp.exp(s1) - jnp.exp(s2) + lam_init


def _subln(o, w, lam_init):
    ms = jnp.mean(o * o, axis=-1, keepdims=True)
    return o * lax.rsqrt(ms + EPS) * w * (1.0 - lam_init)


def _attn_prompt_body(lam_init, bq, q_ref, k_ref, vt_ref, lamp_ref, swc_ref, o_ref,
                      m_scr, acc_scr, s_scr):
    qi = pl.program_id(2)
    q = q_ref[...]
    lane = lax.broadcasted_iota(I32, q.shape, 1)
    qz = [jnp.where(lane < D_A, q, jnp.zeros_like(q)), jnp.where(lane >= D_A, q, jnp.zeros_like(q))]
    m_scr[...] = jnp.full(m_scr.shape, -jnp.inf, F32)
    acc_scr[...] = jnp.zeros(acc_scr.shape, F32)
    qry_pos = lax.broadcasted_iota(I32, (1, bq), 1)

    def scores(j, slot):
        kb = k_ref[pl.ds(pl.multiple_of(j * bq, bq), bq), :]
        for m in range(2):
            s_scr[slot, m] = lax.dot_general(kb, qz[m], NT_DIMS, preferred_element_type=F32)

    def accumulate(j, slot, masked):
        vb = vt_ref[:, pl.ds(pl.multiple_of(j * bq, bq), bq)]
        for m in range(2):
            if masked:
                s = jnp.concatenate(
                    [jnp.where(qry_pos >= c * CHUNK, s_scr[slot, m, c * CHUNK:(c + 1) * CHUNK, :], -jnp.inf)
                     for c in range(bq // CHUNK)], axis=0)
            else:
                s = s_scr[slot, m]
            m_prev = m_scr[m]
            m_new = jnp.maximum(m_prev, jnp.max(s, axis=0, keepdims=True))
            alpha = jnp.exp2(m_prev - m_new)
            p = jnp.exp2(s - m_new).astype(BF16)
            acc_scr[m] = alpha * acc_scr[m] + jnp.dot(vb, p, preferred_element_type=F32)
            m_scr[m] = m_new

    scores(0, 0)

    def two_full_blocks(t, carry):
        j = 2 * t
        scores(j + 1, 1)
        accumulate(j, 0, False)
        scores(j + 2, 0)
        accumulate(j + 1, 1, False)
        return carry

    lax.fori_loop(0, qi // 2, two_full_blocks, 0)
    j0 = 2 * (qi // 2)

    @pl.when(j0 == qi)
    def _():
        accumulate(qi, 0, True)

    @pl.when(j0 != qi)
    def _():
        scores(qi, 1)
        accumulate(j0, 0, False)
        accumulate(qi, 1, True)

    lam = _diff_lambda(lamp_ref[...], lam_init)
    a0 = acc_scr[0]
    a1 = acc_scr[1]
    o_t = a0[0:D_VA] / a0[D_VA:D_VA + 1] - lam * (a1[0:D_VA] / a1[D_VA:D_VA + 1])
    ms = jnp.mean(o_t * o_t, axis=0, keepdims=True)
    o_t = o_t * lax.rsqrt(ms + EPS) * swc_ref[...] * (1.0 - lam_init)
    o_ref[...] = o_t.T.astype(o_ref.dtype)


def _attn_prompt(q_bf, k_bf, v_t, lamp, subln_w, lam_init, bp, lp):
    bq = _pick(lp, (512, 256, 128))
    nq = lp // bq
    vrows = v_t.shape[1]
    return pl.pallas_call(
        functools.partial(_attn_prompt_body, lam_init, bq),
        grid=(bp, H_A, nq),
        in_specs=[
            pl.BlockSpec((bq, LANES), lambda b, h, i: (b * nq + i, h)),
            pl.BlockSpec((lp, LANES), lambda b, h, i: (b, h)),
            pl.BlockSpec((None, vrows, lp), lambda b, h, i: (h, 0, b)),
            pl.BlockSpec((4, D_A), lambda b, h, i: (0, 0)),
            pl.BlockSpec((D_VA, 1), lambda b, h, i: (0, 0)),
        ],
        out_specs=pl.BlockSpec((bq, LANES), lambda b, h, i: (b * nq + i, h)),
        out_shape=jax.ShapeDtypeStruct((bp * lp, V_COLS), BF16),
        scratch_shapes=[
            pltpu.VMEM((2, 1, bq), F32),
            pltpu.VMEM((2, vrows, bq), F32),
            pltpu.VMEM((2, 2, bq, bq), F32),
        ],
        compiler_params=_cparams("parallel", "parallel", "arbitrary"),
        name="attn_prompt",
    )(q_bf, k_bf, v_t, lamp, subln_w.reshape(D_VA, 1))


def _attn_sample_body(lam_init, q_ref, kn_ref, vn_ref, kc_ref, vc_ref, lamp_ref, sw_ref, o_ref,
                      m_scr, l_scr, acc_scr):
    pj = pl.program_id(1)
    npos = kc_ref.shape[0] // H_A
    ls = q_ref.shape[0]

    @pl.when(pj == 0)
    def _():
        m_scr[...] = jnp.full(m_scr.shape, -jnp.inf, F32)
        l_scr[...] = jnp.zeros(l_scr.shape, F32)
        acc_scr[...] = jnp.zeros(acc_scr.shape, F32)

    lane = lax.broadcasted_iota(I32, (ls, D_VA), 1)

    def q_maps(h):
        q = q_ref[:, h * D_VA:(h + 1) * D_VA]
        return jnp.concatenate([jnp.where(lane < D_A, q, jnp.zeros_like(q)),
                                jnp.where(lane >= D_A, q, jnp.zeros_like(q))], axis=0)

    def update_all(keys_of, vals_of):
        heads = range(H_A)
        ss = [lax.dot_general(q_maps(h), keys_of(h), NT_DIMS, preferred_element_type=F32) for h in heads]
        m_new = [jnp.maximum(m_scr[h], jnp.max(ss[h], axis=1, keepdims=True)) for h in heads]
        ps = [jnp.exp2(ss[h] - m_new[h]) for h in heads]
        pv = [jnp.dot(ps[h].astype(BF16), vals_of(h), preferred_element_type=F32) for h in heads]
        for h in heads:
            alpha = jnp.exp2(m_scr[h] - m_new[h])
            l_scr[h] = alpha * l_scr[h] + jnp.sum(ps[h], axis=1, keepdims=True)
            acc_scr[h] = alpha * acc_scr[h] + pv[h]
            m_scr[h] = m_new[h]

    update_all(lambda h: kc_ref[pl.ds(h, npos, stride=H_A), :].astype(BF16),
               lambda h: vc_ref[pl.ds(h, npos, stride=H_A), :].astype(BF16))

    @pl.when(pj == pl.num_programs(1) - 1)
    def _():
        lam = _diff_lambda(lamp_ref[...], lam_init)
        sw = sw_ref[...]
        update_all(lambda h: kn_ref[:, h * D_VA:(h + 1) * D_VA],
                   lambda h: vn_ref[pl.ds(h, ls, stride=H_A), :].astype(BF16))
        for h in range(H_A):
            both = acc_scr[h] / l_scr[h]
            o = both[0:ls] - lam * both[ls:2 * ls]
            o_ref[:, h * D_VA:(h + 1) * D_VA] = _subln(o, sw, lam_init).astype(o_ref.dtype)


def _attn_sample(q_bf, k_bf, v_new, cache_k, cache_v, lamp, subln_w, lam_init, bs, ls):
    past = cache_k.shape[1] // H_A
    pb = _pick(past, (1024, 512, 256, 128, 64))
    assert ls % (2 * SUBLANES) == 0
    return pl.pallas_call(
        functools.partial(_attn_sample_body, lam_init),
        grid=(bs, past // pb),
        in_specs=[
            pl.BlockSpec((ls, A_COLS), lambda b, j: (b, 0)),
            pl.BlockSpec((ls, A_COLS), lambda b, j: (b, 0)),
            pl.BlockSpec((ls * H_A, D_VA), lambda b, j: (b, 0)),
            pl.BlockSpec((None, pb * H_A, D_VA), lambda b, j: (b, j, 0)),
            pl.BlockSpec((None, pb * H_A, D_VA), lambda b, j: (b, j, 0)),
            pl.BlockSpec((4, D_A), lambda b, j: (0, 0)),
            pl.BlockSpec((1, D_VA), lambda b, j: (0, 0)),
        ],
        out_specs=pl.BlockSpec((ls, V_COLS), lambda b, j: (b, 0)),
        out_shape=jax.ShapeDtypeStruct((bs * ls, V_COLS), BF16),
        scratch_shapes=[
            pltpu.VMEM((H_A, 2 * ls, 1), F32),
            pltpu.VMEM((H_A, 2 * ls, 1), F32),
            pltpu.VMEM((H_A, 2 * ls, D_VA), F32),
        ],
        compiler_params=_cparams("parallel", "arbitrary"),
        name="attn_sample",
    )(q_bf, k_bf, v_new, cache_k, cache_v, lamp, subln_w)


def _gdn_body(chunk, rows, pq_ref, pk_ref, pv_ref, pz_ref, sm_ref, smt_ref, cw_ref, alr_ref, dtr_ref,
              alc_ref, dtc_ref, nw_ref, s0_ref, c0_ref, o_ref, s_ref, cn_ref, ext_scr, act_scr):
    i = pl.program_id(1)
    nblk = pl.num_programs(1)
    hk = H_B * D_K
    hv = H_B * D_V
    n_chunks = rows // chunk
    n_steps = int(math.log2(chunk))
    assert 2 ** n_steps == chunk

    @pl.when(i == 0)
    def _():
        s_ref[...] = s0_ref[...]
        ext_scr[0:SUBLANES, :] = jnp.zeros((SUBLANES, ext_scr.shape[1]), F32)
        ext_scr[pl.ds(SUBLANES - (CONV_W - 1), CONV_W - 1), :] = c0_ref[...]

    ext_scr[pl.ds(SUBLANES, rows), 0:hk] = pq_ref[...]
    ext_scr[pl.ds(SUBLANES, rows), hk:2 * hk] = pk_ref[...]
    ext_scr[pl.ds(SUBLANES, rows), 2 * hk:2 * hk + hv] = pv_ref[...]

    @pl.when(i == nblk - 1)
    def _():
        cn_ref[...] = ext_scr[pl.ds(SUBLANES + rows - (CONV_W - 1), CONV_W - 1), :]

    conv = jnp.zeros((rows, ext_scr.shape[1]), F32)
    for w in range(CONV_W):
        conv = conv + ext_scr[pl.ds(SUBLANES - (CONV_W - 1) + w, rows), :] * cw_ref[w:w + 1, :]
    act_scr[...] = _silu(conv)
    ext_scr[0:SUBLANES, :] = ext_scr[pl.ds(rows, SUBLANES), :]

    sm = sm_ref[...]
    beta_all = _sigmoid(sm[:, 0:H_B])
    g_all = -jnp.exp(alr_ref[...]) * _softplus(sm[:, H_B:2 * H_B] + dtr_ref[...])
    smt = smt_ref[...]
    g_all_t = -jnp.exp(alc_ref[...]) * _softplus(smt[H_B:2 * H_B, :] + dtc_ref[...])

    ri = lax.broadcasted_iota(I32, (chunk, chunk), 0)
    ci = lax.broadcasted_iota(I32, (chunk, chunk), 1)
    tri = ri >= ci
    strict = ri > ci
    eye = (ri == ci).astype(F32)
    ltri = tri.astype(F32)
    utri = (ri <= ci).astype(F32)
    nw = nw_ref[...]

    units = [(c, h) for c in range(n_chunks) for h in range(H_B)]
    gcs = [_dot_exact_mask(ltri, g_all[c * chunk:(c + 1) * chunk, :], True) for c in range(n_chunks)]
    grs = [_dot_exact_mask(utri, g_all_t[:, c * chunk:(c + 1) * chunk], False) for c in range(n_chunks)]
    pre = []
    for c, h in units:
        r0 = c * chunk
        q = act_scr[r0:r0 + chunk, h * D_K:(h + 1) * D_K]
        k = act_scr[r0:r0 + chunk, hk + h * D_K:hk + (h + 1) * D_K]
        v = act_scr[r0:r0 + chunk, 2 * hk + h * D_V:2 * hk + (h + 1) * D_V]
        q = q * lax.rsqrt(jnp.sum(q * q, axis=-1, keepdims=True) + EPS) * (D_K ** -0.5)
        k = k * lax.rsqrt(jnp.sum(k * k, axis=-1, keepdims=True) + EPS)
        beta = beta_all[r0:r0 + chunk, h:h + 1]
        gc = gcs[c][:, h:h + 1]
        gr = grs[c][h:h + 1, :]
        g_last = gc[chunk - 1:chunk, :]
        decay = jnp.where(tri, jnp.exp(jnp.where(tri, gc - gr, 0.0)), 0.0)
        kb = k * beta
        eg = jnp.exp(gc)
        kbf = k.astype(BF16)
        kk = lax.dot_general(kb.astype(BF16), kbf, NT_DIMS, preferred_element_type=F32)
        qk = lax.dot_general(q.astype(BF16), kbf, NT_DIMS, preferred_element_type=F32)
        pre.append(dict(
            nm=-jnp.where(strict, kk * decay, 0.0),
            rhs=jnp.concatenate([v * beta, kb * eg], axis=1),
            qk=jnp.where(tri, qk * decay, 0.0).astype(BF16),
            qg=(q * eg).astype(BF16),
            kg=(k * jnp.exp(g_last - gc)).astype(BF16),
            gl=jnp.exp(g_last)))
    invs = [eye + p["nm"] for p in pre]
    pws = [p["nm"] for p in pre]
    for _ in range(n_steps - 1):
        pws = [_bdot(pw, pw) for pw in pws]
        invs = [inv + _bdot(inv, pw) for inv, pw in zip(invs, pws)]
    resid = [eye - _dot3(eye - p["nm"], inv) for p, inv in zip(pre, invs)]
    invs = [inv + _bdot(inv, r) for inv, r in zip(invs, resid)]
    uws = [_bdot(inv, p["rhs"]) for p, inv in zip(pre, invs)]

    for (c, h), p, uw in zip(units, pre, uws):
        r0 = c * chunk
        u = uw[:, 0:D_V]
        wmat = uw[:, D_V:D_V + D_K]
        s = s_ref[h]
        sb = s.astype(BF16)
        v_new = u - jnp.dot(wmat.astype(BF16), sb, preferred_element_type=F32)
        v_new_b = v_new.astype(BF16)
        o = (jnp.dot(p["qg"], sb, preferred_element_type=F32)
             + jnp.dot(p["qk"], v_new_b, preferred_element_type=F32))
        s_ref[h] = s * p["gl"] + lax.dot_general(p["kg"], v_new_b, TN_DIMS, preferred_element_type=F32)
        z = pz_ref[r0:r0 + chunk, h * D_V:(h + 1) * D_V].astype(F32)
        o = o * lax.rsqrt(jnp.mean(o * o, axis=-1, keepdims=True) + EPS) * nw * _silu(z)
        o_ref[r0:r0 + chunk, h * D_V:(h + 1) * D_V] = o.astype(o_ref.dtype)


def _gdn(gq, zg, small, s0, c0, conv_w, a_log, dt_bias, norm_w, bn, ln, chunk):
    hk = H_B * D_K
    assert H_B * D_V == hk and Z_COLS == hk
    rows = ln
    for cand in (2 * chunk, chunk):
        if cand % LANES == 0 and ln % cand == 0:
            rows = cand
            break
    assert rows % chunk == 0 and rows % SUBLANES == 0
    nblk = ln // rows
    small_t = jnp.swapaxes(small[:, 0:2 * H_B].reshape(bn, ln, 2 * H_B), 1, 2)
    alr = a_log.reshape(1, H_B)
    dtr = dt_bias.reshape(1, H_B)
    alc = a_log.reshape(H_B, 1)
    dtc = dt_bias.reshape(H_B, 1)
    row_blk = lambda b, i: b * nblk + i
    return pl.pallas_call(
        functools.partial(_gdn_body, chunk, rows),
        grid=(bn, nblk),
        in_specs=[
            pl.BlockSpec((rows, hk), lambda b, i: (row_blk(b, i), 0)),
            pl.BlockSpec((rows, hk), lambda b, i: (row_blk(b, i), 1)),
            pl.BlockSpec((rows, hk), lambda b, i: (row_blk(b, i), 2)),
            pl.BlockSpec((rows, hk), lambda b, i: (row_blk(b, i), 0)),
            pl.BlockSpec((rows, LANES), lambda b, i: (row_blk(b, i), 0)),
            pl.BlockSpec((None, 2 * H_B, rows), lambda b, i: (b, 0, i)),
            pl.BlockSpec((CONV_W, G_COLS), lambda b, i: (0, 0)),
            pl.BlockSpec((1, H_B), lambda b, i: (0, 0)),
            pl.BlockSpec((1, H_B), lambda b, i: (0, 0)),
            pl.BlockSpec((H_B, 1), lambda b, i: (0, 0)),
            pl.BlockSpec((H_B, 1), lambda b, i: (0, 0)),
            pl.BlockSpec((1, D_V), lambda b, i: (0, 0)),
            pl.BlockSpec((None, H_B, D_K, D_V), lambda b, i: (b, 0, 0, 0)),
            pl.BlockSpec((None, CONV_W - 1, G_COLS), lambda b, i: (b, 0, 0)),
        ],
        out_specs=[
            pl.BlockSpec((rows, hk), lambda b, i: (row_blk(b, i), 0)),
            pl.BlockSpec((None, H_B, D_K, D_V), lambda b, i: (b, 0, 0, 0)),
            pl.BlockSpec((None, CONV_W - 1, G_COLS), lambda b, i: (b, 0, 0)),
        ],
        out_shape=[
            jax.ShapeDtypeStruct((bn * ln, hk), BF16),
            jax.ShapeDtypeStruct(s0.shape, F32),
            jax.ShapeDtypeStruct(c0.shape, F32),
        ],
        scratch_shapes=[
            pltpu.VMEM((rows + SUBLANES, G_COLS), F32),
            pltpu.VMEM((rows, G_COLS), F32),
        ],
        compiler_params=_cparams("parallel", "arbitrary"),
        name="gdn",
    )(gq, gq, gq, zg, small, small_t, conv_w, alr, dtr, alc, dtc, norm_w, s0, c0)


def _merge_body(x_ref, oa_ref, ob_ref, ga0_ref, ga1_ref, gb0_ref, gb1_ref, wa_ref, wb_ref, wo_ref,
                nf_ref, wr_ref, br_ref, x2_ref, t_ref, lg_ref):
    ya = jnp.dot(oa_ref[...], wa_ref[...], preferred_element_type=F32)
    yb = jnp.dot(ob_ref[...], wb_ref[...], preferred_element_type=F32)
    half = ga0_ref.shape[1]
    gate = lambda ref: _sigmoid(ref[...].astype(F32))
    m0 = gate(ga0_ref) * ya[:, :half] + gate(gb0_ref) * yb[:, :half]
    m1 = gate(ga1_ref) * ya[:, half:] + gate(gb1_ref) * yb[:, half:]
    merged = jnp.concatenate([m0, m1], axis=1).astype(BF16)
    x2 = x_ref[...] + jnp.dot(merged, wo_ref[...], preferred_element_type=F32)
    x2_ref[...] = x2
    ms = jnp.mean(x2 * x2, axis=-1, keepdims=True)
    t = x2 * lax.rsqrt(ms + EPS) * nf_ref[...]
    t_ref[...] = _pack_bf16_pair(t)
    lg_ref[...] = _bdot(t, wr_ref[...]) + br_ref[...]


def _merge(x, o_a, o_b, zg, wa, wb, wo, norm_ffn, w_router, b_router):
    T, D = x.shape
    half = D // 2
    assert Z_COLS % half == 0
    gcol = Z_COLS // half
    bm = _pick(T, (256, 128, 64, 32, 16, 8))
    const = dict(pipeline_mode=pl.Buffered(1))
    return pl.pallas_call(
        _merge_body,
        grid=(T // bm,),
        in_specs=[
            pl.BlockSpec((bm, D), lambda i: (i, 0)),
            pl.BlockSpec((bm, o_a.shape[1]), lambda i: (i, 0)),
            pl.BlockSpec((bm, o_b.shape[1]), lambda i: (i, 0)),
            pl.BlockSpec((bm, half), lambda i: (i, gcol)),
            pl.BlockSpec((bm, half), lambda i: (i, gcol + 1)),
            pl.BlockSpec((bm, half), lambda i: (i, gcol + 2)),
            pl.BlockSpec((bm, half), lambda i: (i, gcol + 3)),
            pl.BlockSpec(wa.shape, lambda i: (0, 0), **const),
            pl.BlockSpec(wb.shape, lambda i: (0, 0), **const),
            pl.BlockSpec(wo.shape, lambda i: (0, 0), **const),
            pl.BlockSpec((1, D), lambda i: (0, 0)),
            pl.BlockSpec((D, LANES), lambda i: (0, 0), **const),
            pl.BlockSpec((1, LANES), lambda i: (0, 0)),
        ],
        out_specs=[
            pl.BlockSpec((bm, D), lambda i: (i, 0)),
            pl.BlockSpec((bm, half), lambda i: (i, 0)),
            pl.BlockSpec((bm, LANES), lambda i: (i, 0)),
        ],
        out_shape=[
            jax.ShapeDtypeStruct((T, D), F32),
            jax.ShapeDtypeStruct((T, half), U32),
            jax.ShapeDtypeStruct((T, LANES), F32),
        ],
        compiler_params=_cparams("parallel"),
        name="merge",
    )(x, o_a, o_b, zg, zg, zg, zg, wa, wb, wo, norm_ffn, w_router, b_router)


def _route_body(lg_ref, mi_ref, mf_ref, cnt_ref):
    i = pl.program_id(0)

    @pl.when(i == 0)
    def _():
        cnt_ref[...] = jnp.zeros(cnt_ref.shape, F32)

    lg = lg_ref[...]
    bm = lg.shape[0]
    lane = lax.broadcasted_iota(I32, lg.shape, 1).astype(F32)
    big = jnp.float32(LANES)
    neg = -jnp.inf
    gl = jnp.where(lane < N_GROUPS, lg, neg)
    gmax = jnp.max(gl, axis=1, keepdims=True)
    gidx = jnp.min(jnp.where(gl == gmax, lane, big), axis=1, keepdims=True)
    gw = 1.0 / jnp.sum(jnp.exp(gl - gmax), axis=1, keepdims=True)
    e_lo = N_GROUPS + gidx * EXPERTS_PER_GROUP
    valid = (lane >= e_lo) & (lane < e_lo + EXPERTS_PER_GROUP)
    el = jnp.where(valid, lg, neg)
    v1 = jnp.max(el, axis=1, keepdims=True)
    i1 = jnp.min(jnp.where(el == v1, lane, big), axis=1, keepdims=True)
    el2 = jnp.where(lane == i1, neg, el)
    v2 = jnp.max(el2, axis=1, keepdims=True)
    i2 = jnp.min(jnp.where(el2 == v2, lane, big), axis=1, keepdims=True)
    e21 = jnp.exp(v2 - v1)
    w1 = gw / (1.0 + e21)
    w2 = gw * e21 / (1.0 + e21)
    oh1 = (lane == i1).astype(F32)
    oh2 = (lane == i2).astype(F32)
    oh = oh1 + oh2
    rr = lax.broadcasted_iota(I32, (bm, bm), 0)
    cc = lax.broadcasted_iota(I32, (bm, bm), 1)
    before = (cc < rr).astype(BF16)
    cum = jnp.dot(before, oh.astype(BF16), preferred_element_type=F32) + cnt_ref[...]
    rank1 = jnp.sum(cum * oh1, axis=1, keepdims=True)
    rank2 = jnp.sum(cum * oh2, axis=1, keepdims=True)
    cnt_ref[...] = cnt_ref[...] + jnp.sum(oh, axis=0, keepdims=True)
    mi = jnp.where(lane == 0, i1 - N_GROUPS, 0.0)
    mi = jnp.where(lane == 1, i2 - N_GROUPS, mi)
    mi = jnp.where(lane == 2, rank1, mi)
    mi = jnp.where(lane == 3, rank2, mi)
    mi_ref[...] = mi.T[0:SUBLANES].astype(I32)
    mf_ref[...] = jnp.where(lane == 0, w1, jnp.where(lane == 1, w2, 0.0))


def _route(logits):
    T = logits.shape[0]
    bm = _pick(T, (512, 256, 128, 64, 32, 16, 8))
    return pl.pallas_call(
        _route_body,
        grid=(T // bm,),
        in_specs=[pl.BlockSpec((bm, LANES), lambda i: (i, 0))],
        out_specs=[
            pl.BlockSpec((SUBLANES, bm), lambda i: (0, i)),
            pl.BlockSpec((bm, LANES), lambda i: (i, 0)),
            pl.BlockSpec((1, LANES), lambda i: (0, 0)),
        ],
        out_shape=[
            jax.ShapeDtypeStruct((SUBLANES, T), I32),
            jax.ShapeDtypeStruct((T, LANES), F32),
            jax.ShapeDtypeStruct((1, LANES), F32),
        ],
        compiler_params=_cparams("arbitrary"),
        name="route",
    )(logits)


def _dispatch_body(bm, blk, n_experts, nbp, ps_ref, pn_ref, nu_ref, dest_ref, dprev_ref, tp_ref, ts_ref,
                   xb_ref, zero_scr, stage_scr, sems, zsem):
    i = pl.program_id(0)
    n_blocks = xb_ref.shape[0] // blk

    def pad_rows(act):
        def per_expert(e, carry):
            def per_row(r, c2):
                act(pltpu.make_async_copy(zero_scr.at[pl.ds(0, 1)], xb_ref.at[pl.ds(ps_ref[e] + r, 1)], zsem))
                return c2
            return lax.fori_loop(0, pn_ref[e], per_row, carry)
        lax.fori_loop(0, n_experts, per_expert, 0)

    def tail_blocks(act):
        def per_block(b, carry):
            act(pltpu.make_async_copy(zero_scr, xb_ref.at[pl.ds(pl.multiple_of(b * blk, blk), blk)], zsem))
            return carry
        lax.fori_loop(nu_ref[0], n_blocks, per_block, 0)

    @pl.when(i == 0)
    def _():
        zero_scr[...] = jnp.zeros(zero_scr.shape, zero_scr.dtype)
        pad_rows(lambda cp: cp.start())
        tail_blocks(lambda cp: cp.start())

    def rows(dref, slot, act):
        def body(r, carry):
            for k in range(TOP_K_INNER):
                d = dref[0, k * bm + r]
                act(pltpu.make_async_copy(stage_scr.at[slot, pl.ds(r, 1)], xb_ref.at[pl.ds(d, 1)],
                                          sems.at[slot]), k)
            return carry

        lax.fori_loop(0, bm, body, 0, unroll=DMA_UNROLL)

    def step(slot):
        @pl.when(i < nbp)
        def _():
            stage_scr[slot] = tp_ref[...]

        @pl.when(i >= nbp)
        def _():
            stage_scr[slot] = ts_ref[...]

        rows(dest_ref, slot, _start_alternating)

        @pl.when(i > 0)
        def _():
            rows(dprev_ref, 1 - slot, _wait)

        @pl.when(i == pl.num_programs(0) - 1)
        def _():
            rows(dest_ref, slot, _wait)

    for slot in range(2):
        @pl.when(lax.rem(i, 2) == slot)
        def _(slot=slot):
            step(slot)

    @pl.when(i == 0)
    def _():
        pad_rows(lambda cp: cp.wait())
        tail_blocks(lambda cp: cp.wait())


def _dispatch(t_p, t_s, dest, pad_start, pad_len, n_used, n_rows, blk):
    (tp, D), ts = t_p.shape, t_s.shape[0]
    bm = _pick(math.gcd(tp, ts), (256, 128, 64, 32, 16, 8))
    nbp, nbs = tp // bm, ts // bm
    dest3 = _dest_blocks(dest, bm)
    grid_spec = pltpu.PrefetchScalarGridSpec(
        num_scalar_prefetch=3,
        grid=(nbp + nbs,),
        in_specs=[
            pl.BlockSpec((None, 1, TOP_K_INNER * bm), lambda i, ps, pn, nu: (i, 0, 0),
                         memory_space=pltpu.SMEM),
            pl.BlockSpec((None, 1, TOP_K_INNER * bm), lambda i, ps, pn, nu: (jnp.maximum(i - 1, 0), 0, 0),
                         memory_space=pltpu.SMEM),
            pl.BlockSpec((bm, D), lambda i, ps, pn, nu: (jnp.minimum(i, nbp - 1), 0)),
            pl.BlockSpec((bm, D), lambda i, ps, pn, nu: (jnp.maximum(i - nbp, 0), 0)),
        ],
        out_specs=pl.BlockSpec(memory_space=pl.ANY),
        scratch_shapes=[
            pltpu.VMEM((blk, D), t_p.dtype),
            pltpu.VMEM((2, bm, D), t_p.dtype),
            pltpu.SemaphoreType.DMA((2,)),
            pltpu.SemaphoreType.DMA(()),
        ],
    )
    return pl.pallas_call(
        functools.partial(_dispatch_body, bm, blk, pad_start.shape[0], nbp),
        grid_spec=grid_spec,
        out_shape=jax.ShapeDtypeStruct((n_rows, D), t_p.dtype),
        compiler_params=_cparams("arbitrary"),
        name="dispatch",
    )(pad_start, pad_len, n_used, dest3, dest3, t_p, t_s)


def _expert_body(be_ref, nu_ref, x_ref, wg_ref, wu_ref, wd_ref, y_ref, wg_scr, wu_scr, wd_scr):
    i = pl.program_id(0)

    @pl.when(i < nu_ref[0])
    def _():
        @pl.when(jnp.logical_or(i == 0, be_ref[i] != be_ref[jnp.maximum(i - 1, 0)]))
        def _():
            wg_scr[...] = wg_ref[...].astype(BF16)
            wu_scr[...] = wu_ref[...].astype(BF16)
            wd_scr[...] = wd_ref[...].astype(BF16)

        x = _unpack_bf16_pair(x_ref[...])
        g = jnp.dot(x, wg_scr[...], preferred_element_type=F32)
        u = jnp.dot(x, wu_scr[...], preferred_element_type=F32)
        hmid = (_silu(g) * u).astype(BF16)
        y_ref[...] = jnp.dot(hmid, wd_scr[...], preferred_element_type=F32)

    @pl.when(i >= nu_ref[0])
    def _():
        y_ref[...] = jnp.zeros(y_ref.shape, F32)


def _experts(xb, blk_exp, n_used, w_gate, w_up, w_down, blk):
    P = xb.shape[0]
    D, de = w_gate.shape[1:]
    assert xb.shape[1] * 2 == D
    n_blocks = P // blk
    last = lambda i, nu: jnp.minimum(i, nu[0] - 1)
    grid_spec = pltpu.PrefetchScalarGridSpec(
        num_scalar_prefetch=2,
        grid=(n_blocks,),
        in_specs=[
            pl.BlockSpec((blk, D // 2), lambda i, be, nu: (last(i, nu), 0)),
            pl.BlockSpec((None, D, de), lambda i, be, nu: (be[last(i, nu)], 0, 0)),
            pl.BlockSpec((None, D, de), lambda i, be, nu: (be[last(i, nu)], 0, 0)),
            pl.BlockSpec((None, de, D), lambda i, be, nu: (be[last(i, nu)], 0, 0)),
        ],
        out_specs=pl.BlockSpec((blk, D), lambda i, be, nu: (i, 0)),
        scratch_shapes=[
            pltpu.VMEM((D, de), BF16),
            pltpu.VMEM((D, de), BF16),
            pltpu.VMEM((de, D), BF16),
        ],
    )
    return pl.pallas_call(
        _expert_body,
        grid_spec=grid_spec,
        out_shape=jax.ShapeDtypeStruct((P, D), F32),
        compiler_params=_cparams("arbitrary"),
        name="experts",
    )(blk_exp, n_used, xb, w_gate, w_up, w_down)


def _combine_body(bm, final, dcur_ref, dnxt_ref, x2_ref, mf_ref, nw_ref, yb_ref, o_ref, y_scr, sems):
    i = pl.program_id(0)
    n = pl.num_programs(0)

    def gather(dest_ref, slot, act):
        def row_copy(r, k):
            d = dest_ref[0, k * bm + r]
            return pltpu.make_async_copy(yb_ref.at[pl.ds(d, 1)], y_scr.at[slot, k, pl.ds(r, 1)],
                                         sems.at[slot])

        def body(r, carry):
            for k in range(TOP_K_INNER):
                act(row_copy(r, k), k)
            return carry

        lax.fori_loop(0, bm, body, 0, unroll=DMA_UNROLL)

    def step(slot):
        if slot == 0:
            @pl.when(i == 0)
            def _():
                gather(dcur_ref, 0, _start_alternating)

        @pl.when(i + 1 < n)
        def _():
            gather(dnxt_ref, 1 - slot, _start_alternating)

        gather(dcur_ref, slot, _wait)
        mf = mf_ref[...]
        x3 = x2_ref[...] + y_scr[slot, 0] * mf[:, 0:1] + y_scr[slot, 1] * mf[:, 1:2]
        if final:
            ms = jnp.mean(x3 * x3, axis=-1, keepdims=True)
            x3 = x3 * lax.rsqrt(ms + EPS) * nw_ref[...]
        o_ref[...] = x3

    for slot in range(2):
        @pl.when(lax.rem(i, 2) == slot)
        def _(slot=slot):
            step(slot)


def _combine(x2, yb, dest, mf, norm_final, final):
    T, D = x2.shape
    bm = _pick(T, (256, 128, 64, 32, 16, 8))
    nblk = T // bm
    dest3 = _dest_blocks(dest, bm)
    return pl.pallas_call(
        functools.partial(_combine_body, bm, final),
        grid=(nblk,),
        in_specs=[
            pl.BlockSpec((None, 1, TOP_K_INNER * bm), lambda i: (i, 0, 0), memory_space=pltpu.SMEM),
            pl.BlockSpec((None, 1, TOP_K_INNER * bm), lambda i: (jnp.minimum(i + 1, nblk - 1), 0, 0),
                         memory_space=pltpu.SMEM),
            pl.BlockSpec((bm, D), lambda i: (i, 0)),
            pl.BlockSpec((bm, LANES), lambda i: (i, 0)),
            pl.BlockSpec((1, D), lambda i: (0, 0)),
            pl.BlockSpec(memory_space=pl.ANY),
        ],
        out_specs=pl.BlockSpec((bm, D), lambda i: (i, 0)),
        out_shape=jax.ShapeDtypeStruct((T, D), F32),
        scratch_shapes=[
            pltpu.VMEM((2, TOP_K_INNER, bm, D), F32),
            pltpu.SemaphoreType.DMA((2,)),
        ],
        compiler_params=_cparams("arbitrary"),
        name="combine",
    )(dest3, dest3, x2, mf, norm_final, yb)


def _rotary_tables(pos):
    half = ROT_DIM // 2
    inv = ROPE_THETA ** (-jnp.arange(0, ROT_DIM, 2, dtype=F32) / ROT_DIM)
    ang = pos.astype(F32)[:, None] * inv[None, :]
    cos = jnp.cos(ang)
    sin = jnp.sin(ang)
    n = pos.shape[0]
    ones = jnp.ones((n, D_A - ROT_DIM), F32)
    zeros = jnp.zeros((n, D_A - ROT_DIM), F32)
    zh = jnp.zeros((n, half), F32)
    cos64 = jnp.concatenate([cos, cos, ones], axis=1)
    sa64 = jnp.concatenate([zh, sin, zeros], axis=1)
    sb64 = jnp.concatenate([-sin, zh, zeros], axis=1)
    reps = LANES // D_A
    return jnp.tile(cos64, (1, reps)), jnp.tile(sa64, (1, reps)), jnp.tile(sb64, (1, reps))


def _dest_blocks(dest, bm):
    nblk = dest.shape[1] // bm
    return jnp.swapaxes(dest.reshape(TOP_K_INNER, nblk, bm), 0, 1).reshape(nblk, 1, TOP_K_INNER * bm)


def _moe_slots(mi, counts, n_experts, blk, n_assign):
    eid = mi[0:TOP_K_INNER]
    rank = mi[TOP_K_INNER:2 * TOP_K_INNER]
    cnt = counts[0, N_GROUPS:N_GROUPS + n_experts].astype(I32)
    pc = (cnt + blk - 1) // blk * blk
    pend = jnp.cumsum(pc)
    pstart = pend - pc
    experts = jnp.arange(n_experts, dtype=I32)[:, None, None]
    first = jnp.sum(jnp.where(eid[None] == experts, pstart[:, None, None], 0), axis=0)
    dest = (first + rank).astype(I32)
    n_blocks = -(-n_assign // blk) + n_experts
    blk_start = jnp.arange(n_blocks, dtype=I32) * blk
    blk_exp = jnp.minimum(jnp.sum((pend[None, :] <= blk_start[:, None]).astype(I32), axis=1), n_experts - 1)
    n_used = jnp.maximum(pend[-1] // blk, 1).astype(I32).reshape(1)
    return dest, blk_exp.astype(I32), n_used, n_blocks, (pstart + cnt).astype(I32), (pc - cnt).astype(I32)


def _layer(xp, xs, layer, dims, cache_k, cache_v, state_delta, state_conv, prm, norm_final, final):
    (norm_mix, w_in, lq1, lk1, lq2, lk2, subln_w, conv_w, a_log, dt_bias, gdn_norm_w,
     w_proj_a, w_proj_b, w_out, norm_ffn, w_gr, b_gr, w_er, b_er, w_eg, w_eu, w_ed) = prm
    bp, lp, bs, ls, past = dims
    D = xp.shape[1]
    tp = bp * lp
    n_qkv = 2 * A_COLS + V_COLS + G_COLS + Z_COLS
    n_small = 2 * H_B
    lam_init = 0.8 - 0.6 * math.exp(-0.3 * layer)

    w_t = jnp.swapaxes(w_in, 0, 1)
    w_a = w_t[:n_qkv].astype(BF16)
    w_b = w_t[n_qkv + n_small:].astype(BF16)
    w_small = jnp.pad(w_t[n_qkv:n_qkv + n_small], ((0, LANES - n_small), (0, 0))).astype(BF16)
    nm = norm_mix.reshape(1, D)
    pp = _inproj(xp, nm, _rotary_tables(jnp.tile(jnp.arange(lp), bp)), w_a, w_b, w_small)
    ps = _inproj(xs, nm, _rotary_tables(jnp.tile(past + jnp.arange(ls), bs)), w_a, w_b, w_small)

    lamp = jnp.stack([lq1, lk1, lq2, lk2]).astype(F32)
    sw = subln_w.reshape(1, D_VA)
    oa_p = _attn_prompt(pp.q_bf, pp.k_bf, pp.v_t, lamp, sw, lam_init, bp, lp)
    oa_s = _attn_sample(ps.q_bf, ps.k_bf, ps.v_new, cache_k.reshape(bs, past * H_A, 2 * D_A),
                        cache_v.reshape(bs, past * H_A, D_VA), lamp, sw, lam_init, bs, ls)

    nw = gdn_norm_w.reshape(1, D_V)
    ob_p, s_p, c_p = _gdn(pp.gq, pp.zg, pp.small, jnp.zeros((bp, H_B, D_K, D_V), F32),
                          jnp.zeros((bp, CONV_W - 1, G_COLS), F32), conv_w, a_log, dt_bias, nw,
                          bp, lp, _pick(lp, (GDN_CHUNK, CHUNK)))
    ob_s, s_s, c_s = _gdn(ps.gq, ps.zg, ps.small, state_delta, state_conv, conv_w, a_log, dt_bias, nw,
                          bs, ls, ls)

    n_experts = w_er.shape[1]
    w_router = jnp.pad(jnp.concatenate([w_gr, w_er], axis=1), ((0, 0), (0, LANES - N_GROUPS - n_experts)))
    b_router = jnp.pad(jnp.concatenate([b_gr, b_er]), (0, LANES - N_GROUPS - n_experts)).reshape(1, LANES)
    mw = (w_proj_a.astype(BF16), w_proj_b.astype(BF16), w_out.astype(BF16), norm_ffn.reshape(1, D),
          w_router, b_router)
    x2_p, t_p, lg_p = _merge(xp, oa_p, ob_p, pp.zg, *mw)
    x2_s, t_s, lg_s = _merge(xs, oa_s, ob_s, ps.zg, *mw)

    blk = MOE_ROWS
    mi, mf, counts = _route(jnp.concatenate([lg_p, lg_s], axis=0))
    n_tok = tp + bs * ls
    dest, blk_exp, n_used, n_blocks, pad_start, pad_len = _moe_slots(
        mi, counts, n_experts, blk, n_tok * TOP_K_INNER)
    xb = _dispatch(t_p, t_s, dest, pad_start, pad_len, n_used, n_blocks * blk, blk)
    yb = _experts(xb, blk_exp, n_used, w_eg, w_eu, w_ed, blk)
    nf = norm_final.reshape(1, D)
    y_p = _combine(x2_p, yb, dest[:, :tp], mf[:tp], nf, final)
    y_s = _combine(x2_s, yb, dest[:, tp:], mf[tp:], nf, final)
    return y_p, y_s, pp.k_new, pp.v_new, s_p, c_p, ps.k_new, ps.v_new, s_s, c_s


def kernel(x_prompt, x_sample, cache_k, cache_v, state_delta, state_conv, norm_mix, w_in, lambda_q1, lambda_k1, lambda_q2, lambda_k2, subln_w, conv_w, a_log, dt_bias, gdn_norm_w, w_proj_a, w_proj_b, w_out, norm_ffn, w_group_router, b_group_router, w_expert_router, b_expert_router, w_exp_gate, w_exp_up, w_exp_down, norm_final):
    bp, lp, D = x_prompt.shape
    bs, ls, _ = x_sample.shape
    depth = cache_k.shape[0]
    past = cache_k.shape[2]
    dims = (bp, lp, bs, ls, past)
    xp = x_prompt.reshape(bp * lp, D)
    xs = x_sample.reshape(bs * ls, D)
    kp, vp, sp, cp, ksm, vsm, ssm, csm = [], [], [], [], [], [], [], []
    for l in range(depth):
        prm = (norm_mix[l], w_in[l], lambda_q1[l], lambda_k1[l], lambda_q2[l], lambda_k2[l], subln_w[l],
               conv_w[l], a_log[l], dt_bias[l], gdn_norm_w[l], w_proj_a[l], w_proj_b[l], w_out[l],
               norm_ffn[l], w_group_router[l], b_group_router[l], w_expert_router[l], b_expert_router[l],
               w_exp_gate[l], w_exp_up[l], w_exp_down[l])
        xp, xs, k_p, v_p, s_p, c_p, k_s, v_s, s_s, c_s = _layer(
            xp, xs, l, dims, cache_k[l], cache_v[l], state_delta[l], state_conv[l], prm,
            norm_final, l == depth - 1)
        kp.append(k_p.reshape(bp, lp, H_A, 2 * D_A))
        vp.append(v_p.reshape(bp, lp, H_A, D_VA))
        ksm.append(k_s.reshape(bs, ls, H_A, 2 * D_A))
        vsm.append(v_s.reshape(bs, ls, H_A, D_VA))
        sp.append(s_p)
        cp.append(c_p)
        ssm.append(s_s)
        csm.append(c_s)
    return (xp.reshape(bp, lp, D), xs.reshape(bs, ls, D), jnp.stack(kp), jnp.stack(vp), jnp.stack(sp),
            jnp.stack(cp), jnp.stack(ksm), jnp.stack(vsm), jnp.stack(ssm), jnp.stack(csm))
```
